```python
import math
import jax, jax.numpy as jnp
from jax import lax
import numpy as np

D_MODEL = 2048
BATCH = 1
SEQ = 8192
DEPTH = 1
DEC_BATCH = 32
DEC_SEQ = 1
PAST_LEN = 8192
PAGE_SIZE = 128

HEAD_DIM = 128
ATTN_WIDTH = D_MODEL // 2
N_HEADS = ATTN_WIDTH // HEAD_DIM
N_KV_HEADS = 2
Q_PER_KV = N_HEADS // N_KV_HEADS
KV_WIDTH = N_KV_HEADS * HEAD_DIM
N_BRANCH = 3
CMP_LEN = 32
CMP_STRIDE = 16
CMP_HID = HEAD_DIM
SEL_BLOCK = 64
CMP_PER_SEL = SEL_BLOCK // CMP_STRIDE
N_SEL = 16
WINDOW = 512
REL_BUCKETS = 32
REL_MAX_DIST = 128
KV_ROWS = 4
QB_MAX = 128
SSM_WIDTH = D_MODEL - ATTN_WIDTH
SSM_GROUP_CH = 16
SSM_GROUPS = SSM_WIDTH // SSM_GROUP_CH
SSM_STATE = 64
D_FF = 128 * ((8 * D_MODEL // 3 + 127) // 128)
CONV_WIDTH = 3
IN_COLS = ATTN_WIDTH + 6 * KV_WIDTH + N_BRANCH * N_HEADS + SSM_WIDTH
NEG_INF = -1e30
EPS = 1e-6

kernel_name = 'hymba_nsa_s5_convffn_step'


def rms_norm(x, g):
    xf = x.astype(jnp.float32)
    y = xf * lax.rsqrt(jnp.mean(xf * xf, axis=-1, keepdims=True) + EPS)
    return (y * g.astype(jnp.float32)).astype(x.dtype)


def masked_softmax(s, mask):
    s = jnp.where(mask, s.astype(jnp.float32), NEG_INF)
    m = jnp.max(s, axis=-1, keepdims=True)
    p = jnp.where(mask, jnp.exp(s - m), 0.0)
    return p / jnp.maximum(jnp.sum(p, axis=-1, keepdims=True), 1e-30)


def rel_bucket(dist):
    n = jnp.maximum(dist, 0)
    exact = REL_BUCKETS // 2
    nf = jnp.maximum(n, 1).astype(jnp.float32)
    large = exact + (jnp.log(nf / exact) / math.log(REL_MAX_DIST / exact)
                     * (REL_BUCKETS - exact)).astype(jnp.int32)
    return jnp.where(n < exact, n, jnp.minimum(large, REL_BUCKETS - 1))


def compress(rows, w1, pos, w2):
    b, lk = rows.shape[:2]
    n_ch = lk // CMP_STRIDE
    ch = rows[:, :n_ch * CMP_STRIDE].reshape(b, n_ch, CMP_STRIDE, N_KV_HEADS, HEAD_DIM)
    e_lo = jnp.einsum('bcphd,pde->bche', ch, w1[:CMP_STRIDE])
    e_hi = jnp.einsum('bcphd,pde->bche', ch, w1[CMP_STRIDE:])
    c_pos = jnp.einsum('pd,pde->e', pos, w1)
    hid = jax.nn.gelu(e_lo[:, :-1] + e_hi[:, 1:] + c_pos)
    return jnp.einsum('bche,ed->bchd', hid, w2)


def nsa_attention(q, gates, k_cmp, v_cmp, k_sel, v_sel, k_win, v_win, q_pos0, w_pos0,
                  w_cmp1, pos_cmp, w_cmp2, g_kc, rel_bias):
    b, lq = q.shape[:2]
    lk = k_sel.shape[1]
    kc = rms_norm(compress(k_cmp, w_cmp1[0], pos_cmp[0], w_cmp2[0]), g_kc)
    vc = compress(v_cmp, w_cmp1[1], pos_cmp[1], w_cmp2[1])
    nc = kc.shape[1]
    c_end = jnp.arange(nc) * CMP_STRIDE + (CMP_LEN - 1)
    ncb = -(-nc // CMP_PER_SEL)
    ns = -(-lk // SEL_BLOCK)
    n_top = min(N_SEL, ns)
    pad = ns * SEL_BLOCK - lk
    ksb = jnp.pad(k_sel, ((0, 0), (0, pad), (0, 0), (0, 0))).reshape(
        b, ns, SEL_BLOCK, N_KV_HEADS, HEAD_DIM).transpose(0, 3, 1, 2, 4)
    vsb = jnp.pad(v_sel, ((0, 0), (0, pad), (0, 0), (0, 0))).reshape(
        b, ns, SEL_BLOCK, N_KV_HEADS, HEAD_DIM).transpose(0, 3, 1, 2, 4)
    kwp = jnp.pad(k_win, ((0, 0), (WINDOW, 0), (0, 0), (0, 0)))
    vwp = jnp.pad(v_win, ((0, 0), (WINDOW, 0), (0, 0), (0, 0)))
    tbl = rel_bias.reshape(REL_BUCKETS, N_KV_HEADS, Q_PER_KV).transpose(1, 0, 2)
    b_ix = jnp.arange(b)[:, None, None, None]
    h_ix = jnp.arange(N_KV_HEADS)[None, :, None, None]
    jb = jnp.arange(ns)
    qb = QB_MAX if lq % QB_MAX == 0 else lq
    nblk = lq // qb
    q_blocks = q.reshape(b, nblk, qb, N_KV_HEADS, Q_PER_KV, HEAD_DIM).transpose(1, 0, 3, 4, 2, 5)
    g_blocks = gates.reshape(b, nblk, qb, N_BRANCH, N_KV_HEADS, Q_PER_KV).transpose(1, 0, 3, 4, 5, 2)

    def one_block(args):
        qi, gi, bi = args
        t0 = q_pos0 + bi * qb
        t = t0 + jnp.arange(qb)
        d_c = t[:, None] - c_end[None, :]
        bias_c = jnp.transpose(rel_bias[rel_bucket(d_c)].reshape(qb, nc, N_KV_HEADS, Q_PER_KV), (2, 3, 0, 1))
        s_c = jnp.einsum('bhgqd,bchd->bhgqc', qi, kc) + bias_c
        p_c = masked_softmax(s_c, d_c >= 0)
        o_c = jnp.einsum('bhgqc,bchd->bhgqd', p_c, vc)
        imp = jnp.sum(p_c, axis=2)
        r = jnp.pad(imp, ((0, 0), (0, 0), (0, 0), (0, ncb * CMP_PER_SEL - nc))).reshape(
            b, N_KV_HEADS, qb, ncb, CMP_PER_SEL)
        imp_s = (jnp.pad(jnp.sum(r, axis=-1), ((0, 0), (0, 0), (0, 0), (0, ns + 1 - ncb)))
                 + jnp.pad(r[..., CMP_PER_SEL - 1], ((0, 0), (0, 0), (0, 0), (1, ns - ncb))))[..., :ns]
        cur = t // SEL_BLOCK
        forced = (jb[None] == 0) | (jb[None] == cur[:, None]) | (jb[None] == cur[:, None] - 1)
        valid = jb[None] <= cur[:, None]
        score = jnp.where(forced, 1e9, jnp.where(valid, imp_s, -1.0))
        top_v, top_i = lax.top_k(score, n_top)
        n_s = n_top * SEL_BLOCK
        kg = ksb[b_ix, h_ix, top_i].reshape(b, N_KV_HEADS, qb, n_s, HEAD_DIM)
        vg = vsb[b_ix, h_ix, top_i].reshape(b, N_KV_HEADS, qb, n_s, HEAD_DIM)
        kpos = (top_i[..., None] * SEL_BLOCK + jnp.arange(SEL_BLOCK)).reshape(b, N_KV_HEADS, qb, n_s)
        ok_s = jnp.repeat(top_v >= 0, SEL_BLOCK, axis=-1) & (kpos <= t[:, None])
        bias_s = jnp.transpose(tbl[h_ix, rel_bucket(t[:, None] - kpos)], (0, 1, 4, 2, 3))
        s_s = jnp.einsum('bhgqd,bhqsd->bhgqs', qi, kg) + bias_s
        p_s = masked_softmax(s_s, ok_s[:, :, None])
        o_s = jnp.einsum('bhgqs,bhqsd->bhgqd', p_s, vg)
        kw = lax.dynamic_slice_in_dim(kwp, t0 - w_pos0, WINDOW + qb, axis=1)
        vw = lax.dynamic_slice_in_dim(vwp, t0 - w_pos0, WINDOW + qb, axis=1)
        kpos_w = t0 - WINDOW + jnp.arange(WINDOW + qb)
        d_w = t[:, None] - kpos_w[None, :]
        ok_w = (d_w >= 0) & (d_w <= WINDOW) & (kpos_w[None, :] >= w_pos0)
        bias_w = jnp.transpose(rel_bias[rel_bucket(d_w)].reshape(qb, WINDOW + qb, N_KV_HEADS, Q_PER_KV), (2, 3, 0, 1))
        s_w = jnp.einsum('bhgqd,bkhd->bhgqk', qi, kw) + bias_w
        p_w = masked_softmax(s_w, ok_w)
        o_w = jnp.einsum('bhgqk,bkhd->bhgqd', p_w, vw)
        return gi[:, 0][..., None] * o_c + gi[:, 1][..., None] * o_s + gi[:, 2][..., None] * o_w

    outs = lax.map(one_block, (q_blocks, g_blocks, jnp.arange(nblk)))
    return outs.transpose(1, 0, 4, 2, 3, 5).reshape(b, lq, N_HEADS * HEAD_DIM)


def _ssm_combine(e1, e2):
    a1r, a1i, b1r, b1i = e1
    a2r, a2i, b2r, b2i = e2
    return (a2r * a1r - a2i * a1i, a2r * a1i + a2i * a1r,
            a2r * b1r - a2i * b1i + b2r, a2r * b1i + a2i * b1r + b2i)


def s5_scan(u, h0, lam_re, lam_im, log_dt, b_re, b_im, c_re, c_im, d_skip):
    b, l = u.shape[:2]
    uf = u.astype(jnp.float32).reshape(b, l, SSM_GROUPS, SSM_GROUP_CH)
    dt = jnp.exp(log_dt.astype(jnp.float32))[:, None]
    lr = lam_re.astype(jnp.float32)
    li = lam_im.astype(jnp.float32)
    mag = jnp.exp(lr * dt)
    ab_re, ab_im = mag * jnp.cos(li * dt), mag * jnp.sin(li * dt)
    den = lr * lr + li * li
    nr, ni = ab_re - 1.0, ab_im
    f_re, f_im = (nr * lr + ni * li) / den, (ni * lr - nr * li) / den
    br, bi = b_re.astype(jnp.float32), b_im.astype(jnp.float32)
    bb_re = f_re[..., None] * br - f_im[..., None] * bi
    bb_im = f_re[..., None] * bi + f_im[..., None] * br
    bu_re = jnp.einsum('blgp,gnp->blgn', uf, bb_re)
    bu_im = jnp.einsum('blgp,gnp->blgn', uf, bb_im)
    h0r, h0i = h0[..., 0].astype(jnp.float32), h0[..., 1].astype(jnp.float32)
    bu_re = bu_re.at[:, 0].add(ab_re * h0r - ab_im * h0i)
    bu_im = bu_im.at[:, 0].add(ab_re * h0i + ab_im * h0r)
    a_re = jnp.broadcast_to(ab_re, bu_re.shape)
    a_im = jnp.broadcast_to(ab_im, bu_im.shape)
    _, _, x_re, x_im = lax.associative_scan(_ssm_combine, (a_re, a_im, bu_re, bu_im), axis=1)
    y = (jnp.einsum('blgn,gpn->blgp', x_re, c_re.astype(jnp.float32))
         - jnp.einsum('blgn,gpn->blgp', x_im, c_im.astype(jnp.float32))
         + d_skip.astype(jnp.float32).reshape(SSM_GROUPS, SSM_GROUP_CH) * uf)
    h_last = jnp.stack([x_re[:, -1], x_im[:, -1]], axis=-1)
    return y.reshape(b, l, SSM_WIDTH), h_last


def conv_ffn(x, conv0, w_up, conv_w, conv_b, w_down):
    l = x.shape[1]
    a, g = jnp.split(x @ w_up, 2, axis=-1)
    ext = jnp.concatenate([conv0.astype(a.dtype), a], axis=1)
    c = conv_b
    for j in range(CONV_WIDTH):
        c = c + conv_w[j] * ext[:, j:j + l]
    y = (jax.nn.gelu(c) * g) @ w_down
    return y, ext[:, l:]


def trunk_layer(x, past_kv, win_past, h0, conv0, pos0, rel_bias,
                g_mix, w_in, g_q, g_k, w_cmp1, pos_cmp, w_cmp2,
                lam_re, lam_im, log_dt, b_re, b_im, c_re, c_im, d_skip, w_glu,
                g_out_attn, g_out_ssm, w_out, g_ffn, w_up, conv_w, conv_b, w_down):
    b, l, _ = x.shape
    xn = rms_norm(x, g_mix)
    z = xn @ w_in
    cuts = [int(c) for c in np.cumsum([ATTN_WIDTH] + [KV_WIDTH] * 6 + [N_BRANCH * N_HEADS])]
    q, kc, vc, ks, vs, kw, vw, gt, u = jnp.split(z, cuts, axis=-1)
    hd = lambda a: a.reshape(b, l, N_KV_HEADS, HEAD_DIM)
    q = rms_norm(q.reshape(b, l, N_HEADS, HEAD_DIM), g_q) * (HEAD_DIM ** -0.5)
    ks = rms_norm(hd(ks), g_k[1])
    kw = rms_norm(hd(kw), g_k[2])
    new_kv = jnp.stack([hd(kc), hd(vc), ks, hd(vs)], axis=2)
    all_kv = jnp.concatenate([past_kv.astype(x.dtype), new_kv], axis=1)
    new_win = jnp.stack([kw, hd(vw)], axis=2)
    all_win = jnp.concatenate([win_past.astype(x.dtype), new_win], axis=1)
    w_pos0 = pos0 - win_past.shape[1]
    gates = jax.nn.sigmoid(gt.astype(jnp.float32)).reshape(b, l, N_BRANCH, N_HEADS)
    attn = nsa_attention(q, gates, all_kv[:, :, 0], all_kv[:, :, 1], all_kv[:, :, 2], all_kv[:, :, 3],
                         all_win[:, :, 0], all_win[:, :, 1], pos0, w_pos0,
                         w_cmp1, pos_cmp, w_cmp2, g_k[0], rel_bias)
    y_ssm, h_last = s5_scan(u, h0, lam_re, lam_im, log_dt, b_re, b_im, c_re, c_im, d_skip)
    ga, gb = jnp.split(jax.nn.gelu(y_ssm).astype(x.dtype) @ w_glu, 2, axis=-1)
    ssm_out = ga * jax.nn.sigmoid(gb)
    mixed = jnp.concatenate([rms_norm(attn.astype(x.dtype), g_out_attn),
                             rms_norm(ssm_out, g_out_ssm)], axis=-1) @ w_out
    h = x + mixed
    f, conv_new = conv_ffn(rms_norm(h, g_ffn), conv0, w_up, conv_w, conv_b, w_down)
    out = (h + f).astype(x.dtype)
    keep = min(WINDOW, all_win.shape[1])
    return out, new_kv, all_win[:, all_win.shape[1] - keep:], h_last, conv_new


def setup_inputs(seed: int = 0) -> dict:
    key = jax.random.key(seed)
    ks = iter(jax.random.split(key, 48))
    nrm = lambda shape, scale: scale * jax.random.normal(next(ks), shape, jnp.float32)
    n_pages = PAST_LEN // PAGE_SIZE
    n_pool = (5 * DEC_BATCH * n_pages + 3) // 4
    win_buf = min(WINDOW, PAST_LEN)
    x_prompt = nrm((BATCH, SEQ, D_MODEL), 1.0)
    x_sample = nrm((DEC_BATCH, DEC_SEQ, D_MODEL), 1.0)
    cache_kv = nrm((DEPTH, n_pool, PAGE_SIZE, KV_ROWS, N_KV_HEADS, HEAD_DIM), 1.0)
    page_table = jax.random.permutation(next(ks), n_pool)[:DEC_BATCH * n_pages].reshape(
        DEC_BATCH, n_pages).astype(jnp.int32)
    cache_win = nrm((DEPTH, DEC_BATCH, win_buf, 2, N_KV_HEADS, HEAD_DIM), 1.0)
    state_ssm = nrm((DEPTH, DEC_BATCH, SSM_GROUPS, SSM_STATE, 2), 0.5)
    state_conv = nrm((DEPTH, DEC_BATCH, CONV_WIDTH - 1, D_FF), 1.0)
    rel_bias = nrm((REL_BUCKETS, N_HEADS), 0.1)
    g_mix = 1.0 + nrm((DEPTH, D_MODEL), 0.01)
    w_in = nrm((DEPTH, D_MODEL, IN_COLS), D_MODEL ** -0.5)
    g_q = 1.0 + nrm((DEPTH, HEAD_DIM), 0.01)
    g_k = 1.0 + nrm((DEPTH, N_BRANCH, HEAD_DIM), 0.01)
    w_cmp1 = nrm((DEPTH, 2, CMP_LEN, HEAD_DIM, CMP_HID), (CMP_LEN * HEAD_DIM) ** -0.5)
    pos_cmp = nrm((DEPTH, 2, CMP_LEN, HEAD_DIM), 0.1)
    w_cmp2 = nrm((DEPTH, 2, CMP_HID, HEAD_DIM), CMP_HID ** -0.5)
    lam_re = -0.5 + nrm((DEPTH, SSM_GROUPS, SSM_STATE), 0.01)
    lam_im = jnp.pi * jnp.arange(SSM_STATE, dtype=jnp.float32) + nrm((DEPTH, SSM_GROUPS, SSM_STATE), 0.01)
    log_dt = jax.random.uniform(next(ks), (DEPTH, SSM_GROUPS), jnp.float32,
                                math.log(1e-3), math.log(1e-1))
    b_re = nrm((DEPTH, SSM_GROUPS, SSM_STATE, SSM_GROUP_CH), SSM_GROUP_CH ** -0.5)
    b_im = nrm((DEPTH, SSM_GROUPS, SSM_STATE, SSM_GROUP_CH), SSM_GROUP_CH ** -0.5)
    c_re = nrm((DEPTH, SSM_GROUPS, SSM_GROUP_CH, SSM_STATE), (2 * SSM_STATE) ** -0.5)
    c_im = nrm((DEPTH, SSM_GROUPS, SSM_GROUP_CH, SSM_STATE), (2 * SSM_STATE) ** -0.5)
    d_skip = nrm((DEPTH, SSM_WIDTH), 1.0)
    w_glu = nrm((DEPTH, SSM_WIDTH, 2 * SSM_WIDTH), SSM_WIDTH ** -0.5)
    g_out_attn = 1.0 + nrm((DEPTH, ATTN_WIDTH), 0.01)
    g_out_ssm = 1.0 + nrm((DEPTH, SSM_WIDTH), 0.01)
    w_out = nrm((DEPTH, D_MODEL, D_MODEL), D_MODEL ** -0.5)
    g_ffn = 1.0 + nrm((DEPTH, D_MODEL), 0.01)
    w_up = nrm((DEPTH, D_MODEL, 2 * D_FF), D_MODEL ** -0.5)
    conv_w = nrm((DEPTH, CONV_WIDTH, D_FF), CONV_WIDTH ** -0.5)
    conv_b = nrm((DEPTH, D_FF), 0.01)
    w_down = nrm((DEPTH, D_FF, D_MODEL), D_FF ** -0.5)
    return {'x_prompt': x_prompt, 'x_sample': x_sample, 'cache_kv': cache_kv, 'page_table': page_table,
            'cache_win': cache_win, 'state_ssm': state_ssm, 'state_conv': state_conv,
            'rel_bias': rel_bias, 'g_mix': g_mix, 'w_in': w_in, 'g_q': g_q, 'g_k': g_k,
            'w_cmp1': w_cmp1, 'pos_cmp': pos_cmp, 'w_cmp2': w_cmp2, 'lam_re': lam_re, 'lam_im': lam_im,
            'log_dt': log_dt, 'b_re': b_re, 'b_im': b_im, 'c_re': c_re, 'c_im': c_im, 'd_skip': d_skip,
            'w_glu': w_glu, 'g_out_attn': g_out_attn, 'g_out_ssm': g_out_ssm, 'w_out': w_out,
            'g_ffn': g_ffn, 'w_up': w_up, 'conv_w': conv_w, 'conv_b': conv_b, 'w_down': w_down}


def reference(x_prompt, x_sample, cache_kv, page_table, cache_win, state_ssm, state_conv,
              rel_bias, g_mix, w_in, g_q, g_k, w_cmp1, pos_cmp, w_cmp2, lam_re, lam_im, log_dt,
              b_re, b_im, c_re, c_im, d_skip, w_glu, g_out_attn, g_out_ssm, w_out, g_ffn,
              w_up, conv_w, conv_b, w_down):
    n_pages = page_table.shape[1]
    past_len = n_pages * PAGE_SIZE
    bp, bs = x_prompt.shape[0], x_sample.shape[0]
    dt = x_prompt.dtype
    y_prompt, y_sample = x_prompt, x_sample
    kv_p, kv_s, win_p, win_s, ssm_p, ssm_s, conv_p, conv_s = [], [], [], [], [], [], [], []
    for li in range(DEPTH):
        lw = (g_mix[li], w_in[li], g_q[li], g_k[li], w_cmp1[li], pos_cmp[li], w_cmp2[li],
              lam_re[li], lam_im[li], log_dt[li], b_re[li], b_im[li], c_re[li], c_im[li], d_skip[li],
              w_glu[li], g_out_attn[li], g_out_ssm[li], w_out[li], g_ffn[li], w_up[li],
              conv_w[li], conv_b[li], w_down[li])
        y_prompt, kvp, wp, hp, cp = trunk_layer(
            y_prompt,
            jnp.zeros((bp, 0, KV_ROWS, N_KV_HEADS, HEAD_DIM), dt),
            jnp.zeros((bp, 0, 2, N_KV_HEADS, HEAD_DIM), dt),
            jnp.zeros((bp, SSM_GROUPS, SSM_STATE, 2), jnp.float32),
            jnp.zeros((bp, CONV_WIDTH - 1, D_FF), dt),
            0, rel_bias, *lw)
        past = cache_kv[li][page_table].reshape(bs, past_len, KV_ROWS, N_KV_HEADS, HEAD_DIM)
        y_sample, kvs, wsm, hs, cs = trunk_layer(
            y_sample, past, cache_win[li], state_ssm[li], state_conv[li],
            past_len, rel_bias, *lw)
        kv_p.append(kvp)
        kv_s.append(kvs)
        win_p.append(wp)
        win_s.append(wsm)
        ssm_p.append(hp)
        ssm_s.append(hs)
        conv_p.append(cp)
        conv_s.append(cs)
    kv_prompt = jnp.stack(kv_p, 0)
    kv_sample = jnp.stack(kv_s, 0)
    win_prompt = jnp.stack(win_p, 0)
    win_sample = jnp.stack(win_s, 0)
    ssm_prompt = jnp.stack(ssm_p, 0)
    ssm_sample = jnp.stack(ssm_s, 0)
    conv_prompt = jnp.stack(conv_p, 0)
    conv_sample = jnp.stack(conv_s, 0)
    return (y_prompt, y_sample, kv_prompt, kv_sample, win_prompt, win_sample,
            ssm_prompt, ssm_sample, conv_prompt, conv_sample)
```

```python
import functools
import math

import numpy as np
import jax
import jax.numpy as jnp
from jax import lax
from jax.experimental import pallas as pl
from jax.experimental.pallas import tpu as pltpu

F32 = jnp.float32
BF16 = jnp.bfloat16
I32 = jnp.int32

D_MODEL = 2048
HEAD_DIM = 128
N_HEADS = 8
N_KV = 2
Q_PER_KV = 4
ATTN_W = 1024
KV_W = 256
N_BRANCH = 3
CMP_LEN = 32
CMP_STRIDE = 16
SEL_BLOCK = 64
CMP_PER_SEL = 4
N_SEL = 16
WINDOW = 512
REL_BUCKETS = 32
REL_MAX_DIST = 128
PAGE = 128
SSM_W = 1024
SSM_G = 64
SSM_N = 64
SSM_P = 16
SSM_SG = 8
SSM_LANES = SSM_G * SSM_N
D_FF = 5504
D_FF_PAD = 5632
CONV_W = 3
EPS = 1e-6
NEG = -1e30
QB = 128
LANE = 128
VMEM_LIMIT = 56 * 1024 * 1024


def _cparams(sem):
    return pltpu.CompilerParams(dimension_semantics=sem, vmem_limit_bytes=VMEM_LIMIT)


def _rms(x, g):
    return x * lax.rsqrt(jnp.mean(x * x, axis=-1, keepdims=True) + EPS) * g


def _gelu(x):
    return jax.nn.gelu(x)


def _dot(a, b):
    return jnp.dot(a, b, preferred_element_type=F32)


def _dot_nt(a, b):
    return lax.dot_general(a, b, (((1,), (1,)), ((), ())), preferred_element_type=F32)


def _split3(x):
    hi = x.astype(BF16)
    r1 = x - hi.astype(F32)
    mid = r1.astype(BF16)
    lo = (r1 - mid.astype(F32)).astype(BF16)
    return hi, mid, lo


def _bucket_np(d):
    n = np.maximum(d, 0)
    exact = REL_BUCKETS // 2
    nf = np.maximum(n, 1).astype(np.float32)
    large = exact + (np.log(nf / np.float32(exact)) / np.float32(math.log(REL_MAX_DIST / exact))
                     * np.float32(REL_BUCKETS - exact)).astype(np.int32)
    return np.where(n < exact, n, np.minimum(large, REL_BUCKETS - 1)).astype(np.int32)


def _bias_lookup(bkt, rb_ref, head, shift=None):
    last = rb_ref[REL_BUCKETS - 1, head]
    acc = jnp.full(bkt.shape, last, F32)
    for b in range(REL_BUCKETS - 1):
        acc = jnp.where(bkt == b, rb_ref[b, head], acc)
    if shift:
        acc = acc - last
    return acc


def _prep_body(lre_ref, lim_ref, ldt_ref, bre_ref, bim_ref, pos_ref, w1_ref,
               abre_ref, abim_ref, bbre_ref, bbim_ref, cpos_ref):
    lr = lre_ref[...]
    li = lim_ref[...]
    dt = jnp.exp(ldt_ref[...])
    mag = jnp.exp(lr * dt)
    ab_re = mag * jnp.cos(li * dt)
    ab_im = mag * jnp.sin(li * dt)
    den = lr * lr + li * li
    nr = ab_re - 1.0
    ni = ab_im
    f_re = (nr * lr + ni * li) / den
    f_im = (ni * lr - nr * li) / den
    abre_ref[...] = ab_re
    abim_ref[...] = ab_im
    for p in range(SSM_P):
        br = bre_ref[p]
        bi = bim_ref[p]
        bbre_ref[p] = f_re * br - f_im * bi
        bbim_ref[p] = f_re * bi + f_im * br
    for kind in range(2):
        cpos_ref[kind] = jnp.dot(pos_ref[kind], w1_ref[kind], preferred_element_type=F32,
                                 precision=lax.Precision.HIGHEST)


def _prep(lam_re, lam_im, log_dt, b_re, b_im, pos_cmp, w_cmp1):
    bre_t = jnp.transpose(b_re, (2, 0, 1))
    bim_t = jnp.transpose(b_im, (2, 0, 1))
    pos = jnp.zeros((2, 8, CMP_LEN * HEAD_DIM), F32).at[:, 0, :].set(pos_cmp.reshape(2, CMP_LEN * HEAD_DIM))
    w1 = w_cmp1.reshape(2, CMP_LEN * HEAD_DIM, HEAD_DIM)
    return pl.pallas_call(
        _prep_body,
        out_shape=[jax.ShapeDtypeStruct((SSM_G, SSM_N), F32), jax.ShapeDtypeStruct((SSM_G, SSM_N), F32),
                   jax.ShapeDtypeStruct((SSM_P, SSM_G, SSM_N), F32), jax.ShapeDtypeStruct((SSM_P, SSM_G, SSM_N), F32),
                   jax.ShapeDtypeStruct((2, 8, HEAD_DIM), F32)],
        compiler_params=pltpu.CompilerParams(vmem_limit_bytes=VMEM_LIMIT),
        name="prep",
    )(lam_re, lam_im, log_dt.reshape(SSM_G, 1), bre_t, bim_t, pos, w1)


IN_COLS_PAD = ATTN_W + 4 * KV_W + 2 * KV_W + SSM_W + LANE


def _inproj_body(x_ref, gmix_ref, w_ref, gq_ref, gks_ref, gkw_ref,
                 q_ref, kv_ref, win_ref, u_ref, gt_ref, kvb_ref):
    xn = _rms(x_ref[...], gmix_ref[...]).astype(BF16)
    zq = _dot(xn, w_ref[:, 0:ATTN_W])
    scale = HEAD_DIM ** -0.5
    for h in range(N_HEADS):
        sl = slice(h * HEAD_DIM, (h + 1) * HEAD_DIM)
        q_ref[:, sl] = (_rms(zq[:, sl], gq_ref[...]) * scale).astype(BF16)
    zkv = _dot(xn, w_ref[:, ATTN_W:ATTN_W + 4 * KV_W])
    kv_ref[:, 0:2 * KV_W] = zkv[:, 0:2 * KV_W]
    for h in range(N_KV):
        sl = slice(2 * KV_W + h * HEAD_DIM, 2 * KV_W + (h + 1) * HEAD_DIM)
        ks = _rms(zkv[:, sl], gks_ref[...])
        kv_ref[:, sl] = ks
        kvb_ref[:, h * HEAD_DIM:(h + 1) * HEAD_DIM] = ks.astype(BF16)
    kv_ref[:, 3 * KV_W:4 * KV_W] = zkv[:, 3 * KV_W:4 * KV_W]
    kvb_ref[:, KV_W:2 * KV_W] = zkv[:, 3 * KV_W:4 * KV_W].astype(BF16)
    c0 = ATTN_W + 4 * KV_W
    zw = _dot(xn, w_ref[:, c0:c0 + 2 * KV_W])
    for h in range(N_KV):
        sl = slice(h * HEAD_DIM, (h + 1) * HEAD_DIM)
        kw = _rms(zw[:, sl], gkw_ref[...])
        win_ref[:, sl] = kw
        kvb_ref[:, 2 * KV_W + h * HEAD_DIM:2 * KV_W + (h + 1) * HEAD_DIM] = kw.astype(BF16)
    win_ref[:, KV_W:2 * KV_W] = zw[:, KV_W:2 * KV_W]
    kvb_ref[:, 3 * KV_W:4 * KV_W] = zw[:, KV_W:2 * KV_W].astype(BF16)
    c1 = c0 + 2 * KV_W
    u_ref[...] = _dot(xn, w_ref[:, c1:c1 + SSM_W])
    c2 = c1 + SSM_W
    gt_ref[...] = jax.nn.sigmoid(_dot(xn, w_ref[:, c2:c2 + LANE]))


def _inproj(x, g_mix, w, g_q, g_ks, g_kw, tm):
    rows = x.shape[0]
    row_spec = lambda n: pl.BlockSpec((tm, n), lambda i: (i, 0))
    full = lambda a: pl.BlockSpec(a.shape, lambda i: (0,) * a.ndim)
    return pl.pallas_call(
        _inproj_body,
        grid=(rows // tm,),
        in_specs=[row_spec(D_MODEL), full(g_mix), full(w), full(g_q), full(g_ks), full(g_kw)],
        out_specs=[row_spec(ATTN_W), row_spec(4 * KV_W), row_spec(2 * KV_W), row_spec(SSM_W), row_spec(LANE),
                   row_spec(4 * KV_W)],
        out_shape=[jax.ShapeDtypeStruct((rows, ATTN_W), BF16), jax.ShapeDtypeStruct((rows, 4 * KV_W), F32),
                   jax.ShapeDtypeStruct((rows, 2 * KV_W), F32), jax.ShapeDtypeStruct((rows, SSM_W), F32),
                   jax.ShapeDtypeStruct((rows, LANE), F32), jax.ShapeDtypeStruct((rows, 4 * KV_W), BF16)],
        compiler_params=_cparams(("arbitrary",)),
        name="inproj",
    )(x, g_mix, w, g_q, g_ks, g_kw)


def _compress_rows(x_ref, wcat_ref, cpos, w2):
    n_ch = x_ref.shape[0] // CMP_STRIDE
    acc = jnp.zeros((n_ch, 2 * HEAD_DIM), F32)
    for p in range(CMP_STRIDE):
        xp = x_ref[pl.ds(p, n_ch, stride=CMP_STRIDE), :].astype(BF16)
        acc = acc + _dot(xp, wcat_ref[p])
    e_lo = acc[:, 0:HEAD_DIM]
    e_hi_next = pltpu.roll(acc[:, HEAD_DIM:2 * HEAD_DIM], n_ch - 1, 0)
    hid = _gelu(e_lo + e_hi_next + cpos)
    return _dot(hid.astype(BF16), w2)


def _pcompress_body(x_ref, wcat_ref, cpos_ref, w2_ref, gkc_ref, o_ref):
    s = pl.program_id(0)
    out = _compress_rows(x_ref, wcat_ref.at[0], cpos_ref[0, 0:1, :], w2_ref[0])
    o_ref[...] = jnp.where(s < N_KV, _rms(out, gkc_ref[...]), out)


def _pcompress(kv, wcat, cpos, w2, g_kc):
    rows = kv.shape[0]
    n_ch = rows // CMP_STRIDE
    return pl.pallas_call(
        _pcompress_body,
        grid=(2 * N_KV,),
        in_specs=[pl.BlockSpec((rows, HEAD_DIM), lambda s: (0, s)),
                  pl.BlockSpec((1, CMP_STRIDE, HEAD_DIM, 2 * HEAD_DIM), lambda s: (s // N_KV, 0, 0, 0)),
                  pl.BlockSpec((1, 8, HEAD_DIM), lambda s: (s // N_KV, 0, 0)),
                  pl.BlockSpec((1, HEAD_DIM, HEAD_DIM), lambda s: (s // N_KV, 0, 0)),
                  pl.BlockSpec((1, HEAD_DIM), lambda s: (0, 0))],
        out_specs=pl.BlockSpec((n_ch, HEAD_DIM), lambda s: (0, s)),
        out_shape=jax.ShapeDtypeStruct((n_ch, 2 * N_KV * HEAD_DIM), F32),
        compiler_params=_cparams(("arbitrary",)),
        name="pcompress",
    )(kv, wcat, cpos, w2, g_kc)


def _select_rounds(score_t, on_pick=None):
    n_j = score_t.shape[0]
    jio = lax.broadcasted_iota(I32, score_t.shape, 0)
    sel = jnp.zeros(score_t.shape, F32)
    sc = score_t
    for r in range(N_SEL):
        m = jnp.max(sc, axis=0, keepdims=True)
        idx = jnp.min(jnp.where(sc == m, jio, n_j), axis=0, keepdims=True)
        pick = jio == idx
        ok = m >= 0.0
        sel = jnp.where(pick & ok, 1.0, sel)
        sc = jnp.where(pick, -jnp.inf, sc)
        if on_pick is not None:
            on_pick(r, idx, ok)
    return sel


def _softmax_update(st, s, v):
    m, l, acc = st
    m_new = jnp.maximum(m, jnp.max(s, axis=-1, keepdims=True))
    alpha = jnp.exp(m - m_new)
    p = jnp.exp(s - m_new)
    l = alpha * l + jnp.sum(p, axis=-1, keepdims=True)
    acc = alpha * acc + _dot(p.astype(BF16), v)
    return m_new, l, acc


def _pattn_body(q_ref, gt_ref, kvb_ref, kc_ref, vc_ref, rb_ref, bk0_ref, bk1_ref, bkc_ref, mft_ref, gout_ref,
                o_ref, b0_s, b1_s, bc_s, bw4_s):
    i = pl.program_id(0)
    rows = Q_PER_KV * QB

    @pl.when(i == 0)
    def _tables():
        a_io = lax.broadcasted_iota(I32, (QB, QB), 0)
        b_io = lax.broadcasted_iota(I32, (QB, QB), 1)
        for hd in range(N_HEADS):
            t0 = _bias_lookup(bk0_ref[...], rb_ref, hd, shift=True)
            b0_s[hd] = jnp.where(a_io >= b_io, t0, NEG)
            b1_s[hd] = _bias_lookup(bk1_ref[...], rb_ref, hd, shift=True)
            bc_s[hd] = _bias_lookup(bkc_ref[...], rb_ref, hd, shift=True)
        bw4_s[...] = jnp.where(b_io >= a_io, 0.0, NEG)

    a_col = lax.broadcasted_iota(I32, (QB, 1), 0)
    attn = []
    for h in range(N_KV):
        q = jnp.concatenate([q_ref[:, (Q_PER_KV * h + g) * HEAD_DIM:(Q_PER_KV * h + g + 1) * HEAD_DIM]
                             for g in range(Q_PER_KV)], axis=0)
        hs = slice(h * HEAD_DIM, (h + 1) * HEAD_DIM)

        def tab(ref):
            return jnp.concatenate([ref[Q_PER_KV * h + g] for g in range(Q_PER_KV)], axis=0)

        n_c = kc_ref.shape[0] - 2 * CMP_STRIDE
        near0 = pl.multiple_of(8 * i, 8)
        s_f = _dot_nt(q, kc_ref[0:n_c, hs].astype(BF16))
        cf = lax.broadcasted_iota(I32, (1, n_c), 1)
        ok_f = (cf < 8 * i) & (cf >= CMP_STRIDE)
        s_f = jnp.where(ok_f, s_f, NEG)
        s_n = _dot_nt(q, kc_ref[pl.ds(near0, 32), hs].astype(BF16)) + tab(bc_s)[:, 0:32]
        cn = lax.broadcasted_iota(I32, (1, 32), 1)
        a_row = lax.broadcasted_iota(I32, (rows, 32), 0) % QB
        ok_n = (CMP_STRIDE * (cn - CMP_STRIDE) <= a_row - (CMP_LEN - 1)) & (cn + 8 * i >= CMP_STRIDE)
        s_n = jnp.where(ok_n, s_n, NEG)
        m_c = jnp.maximum(jnp.max(s_f, axis=-1, keepdims=True), jnp.max(s_n, axis=-1, keepdims=True))
        p_f = jnp.where(ok_f, jnp.exp(s_f - m_c), 0.0)
        p_n = jnp.where(ok_n, jnp.exp(s_n - m_c), 0.0)
        l_c = jnp.sum(p_f, axis=-1, keepdims=True) + jnp.sum(p_n, axis=-1, keepdims=True)
        inv_c = 1.0 / jnp.maximum(l_c, 1e-30)
        o_c = (_dot(p_f.astype(BF16), vc_ref[0:n_c, hs].astype(BF16))
               + _dot(p_n.astype(BF16), vc_ref[pl.ds(near0, 32), hs].astype(BF16))) * inv_c
        pn_f = p_f * inv_c
        pn_n = p_n * inv_c
        imp_f = pn_f[0:QB] + pn_f[QB:2 * QB] + pn_f[2 * QB:3 * QB] + pn_f[3 * QB:4 * QB]
        imp_n = pn_n[0:QB] + pn_n[QB:2 * QB] + pn_n[2 * QB:3 * QB] + pn_n[3 * QB:4 * QB]
        jn = lax.broadcasted_iota(I32, (QB, 32), 0)
        cc = lax.broadcasted_iota(I32, (QB, 32), 1) + 8 * i - CMP_STRIDE
        mnt = (((cc // CMP_PER_SEL) == jn) | (cc == CMP_PER_SEL * jn - 1)) & (cc >= 0)
        mnt = jnp.where(mnt, 1.0, 0.0).astype(BF16)
        imp_t = jnp.zeros((QB, QB), F32)
        for part in _split3(imp_f):
            imp_t = imp_t + _dot_nt(mft_ref[...], part)
        for part in _split3(imp_n):
            imp_t = imp_t + _dot_nt(mnt, part)
        j_io = lax.broadcasted_iota(I32, (QB, QB), 0)
        cur = 2 * i + lax.broadcasted_iota(I32, (QB, QB), 1) // SEL_BLOCK
        forced = (j_io == 0) | (j_io == cur) | (j_io == cur - 1)
        score_t = jnp.where(forced, 1e9, jnp.where(j_io <= cur, imp_t, -1.0))
        sel_t = _select_rounds(score_t)
        selneg = jnp.where(sel_t.T > 0.5, 0.0, NEG).astype(BF16)
        qa = jnp.concatenate([q, jnp.concatenate([selneg] * Q_PER_KV, axis=0)], axis=1)

        def sel_keys(start, n):
            ks = kvb_ref[pl.ds(start, n), hs]
            kj = (lax.broadcasted_iota(I32, (n, QB), 0) + start) // SEL_BLOCK
            e = jnp.where(kj == lax.broadcasted_iota(I32, (n, QB), 1), 1.0, 0.0).astype(BF16)
            return jnp.concatenate([ks, e], axis=1)

        def sel_vals(start, n):
            return kvb_ref[pl.ds(start, n), KV_W + h * HEAD_DIM:KV_W + (h + 1) * HEAD_DIM]

        t_diag = pl.multiple_of(i * QB, QB)
        t_prev = pl.multiple_of(jnp.maximum(i - 1, 0) * QB, QB)
        prev_off = jnp.where(i >= 1, 0.0, NEG)
        st = (jnp.full((rows, 1), NEG, F32), jnp.zeros((rows, 1), F32), jnp.zeros((rows, HEAD_DIM), F32))
        st = _softmax_update(st, _dot_nt(qa, sel_keys(t_diag, QB)) + tab(b0_s), sel_vals(t_diag, QB))
        st = _softmax_update(st, _dot_nt(qa, sel_keys(t_prev, QB)) + tab(b1_s) + prev_off, sel_vals(t_prev, QB))
        n_far = jnp.maximum(i - 1, 0)
        big = 4

        def far_big(k, st):
            start = pl.multiple_of(k * (big * QB), big * QB)
            return _softmax_update(st, _dot_nt(qa, sel_keys(start, big * QB)), sel_vals(start, big * QB))

        st = lax.fori_loop(0, n_far // big, far_big, st)

        def far_small(k, st):
            start = pl.multiple_of(((n_far // big) * big + k) * QB, QB)
            return _softmax_update(st, _dot_nt(qa, sel_keys(start, QB)), sel_vals(start, QB))

        st = lax.fori_loop(0, n_far % big, far_small, st)
        o_s = st[2] / st[1]

        kw0 = 2 * KV_W + h * HEAD_DIM
        vw0 = 3 * KV_W + h * HEAD_DIM
        st = (jnp.full((rows, 1), NEG, F32), jnp.zeros((rows, 1), F32), jnp.zeros((rows, HEAD_DIM), F32))
        for back in range(5):
            kt = i - back
            start = pl.multiple_of(jnp.maximum(kt, 0) * QB, QB)
            s = _dot_nt(q, kvb_ref[pl.ds(start, QB), kw0:kw0 + HEAD_DIM])
            if back == 0:
                s = s + tab(b0_s)
            elif back == 1:
                s = s + tab(b1_s)
            elif back == 4:
                s = s + jnp.concatenate([bw4_s[...]] * Q_PER_KV, axis=0)
            if back > 0:
                s = s + jnp.where(kt >= 0, 0.0, NEG)
            st = _softmax_update(st, s, kvb_ref[pl.ds(start, QB), vw0:vw0 + HEAD_DIM])
        o_w = st[2] / st[1]

        for g in range(Q_PER_KV):
            hd = Q_PER_KV * h + g
            rs = slice(g * QB, (g + 1) * QB)
            attn.append(gt_ref[:, hd:hd + 1] * o_c[rs] + gt_ref[:, N_HEADS + hd:N_HEADS + hd + 1] * o_s[rs]
                        + gt_ref[:, 2 * N_HEADS + hd:2 * N_HEADS + hd + 1] * o_w[rs])
    a = jnp.concatenate(attn, axis=1)
    o_ref[...] = _rms(a, gout_ref[...]).astype(BF16)


def _pattn_tables():
    a = np.arange(QB)[:, None]
    b = np.arange(QB)[None, :]
    bk0 = _bucket_np(a - b)
    bk1 = _bucket_np(a - b + QB)
    c = np.arange(LANE)[None, :] - CMP_STRIDE
    bkc = _bucket_np(a - CMP_STRIDE * c - (CMP_LEN - 1))
    cidx = np.arange(4 * QB)[None, :] - CMP_STRIDE
    j = np.arange(QB)[:, None]
    mft = (((cidx // CMP_PER_SEL) == j) | (cidx == CMP_PER_SEL * j - 1)) & (cidx >= 0)
    return (jnp.asarray(bk0), jnp.asarray(bk1), jnp.asarray(bkc), jnp.asarray(mft.astype(np.float32), dtype=BF16))


def _pattn(q, gates, kvb, kcp, vcp, rel_bias, g_out):
    rows = q.shape[0]
    bk0, bk1, bkc, mft = _pattn_tables()
    full = lambda a: pl.BlockSpec(a.shape, lambda i: (0,) * a.ndim)
    once = lambda a: pl.BlockSpec(a.shape, lambda i: (0,) * a.ndim, pipeline_mode=pl.Buffered(1))
    return pl.pallas_call(
        _pattn_body,
        grid=(rows // QB,),
        in_specs=[pl.BlockSpec((QB, ATTN_W), lambda i: (i, 0)), pl.BlockSpec((QB, LANE), lambda i: (i, 0)),
                  once(kvb), full(kcp), full(vcp), pl.BlockSpec(memory_space=pltpu.SMEM),
                  full(bk0), full(bk1), full(bkc), full(mft), full(g_out)],
        out_specs=pl.BlockSpec((QB, ATTN_W), lambda i: (i, 0)),
        out_shape=jax.ShapeDtypeStruct((rows, ATTN_W), BF16),
        scratch_shapes=[pltpu.VMEM((N_HEADS, QB, QB), F32), pltpu.VMEM((N_HEADS, QB, QB), F32),
                        pltpu.VMEM((N_HEADS, QB, LANE), F32), pltpu.VMEM((QB, QB), F32)],
        compiler_params=_cparams(("arbitrary",)),
        name="pattn",
    )(q, gates, kvb, kcp, vcp, rel_bias, bk0, bk1, bkc, mft, g_out)


def _ssm_body(u_ref, ire_ref, iim_ref, wb_ref, wc_ref, abre_ref, abim_ref, dsk_ref, wglu_ref, gout_ref,
              y_ref, fre_ref, fim_ref, xre_s, xim_s, sre_s, sim_s, *, n_seg, tc, emit):
    c = pl.program_id(0)
    sgl = SSM_LANES // SSM_SG
    row_sets = 1 if tc == 1 else n_seg

    @pl.when(c == 0)
    def _init():
        sre_s[...] = ire_ref[...]
        sim_s[...] = iim_ref[...]

    for sg in range(SSM_SG):
        ls = slice(sg * sgl, (sg + 1) * sgl)
        for j in range(row_sets):
            bu = _dot(u_ref[j, :, sg * LANE:(sg + 1) * LANE].astype(BF16), wb_ref[sg])
            if tc == 1:
                xre_s[0, :, ls] = bu[:, 0:sgl]
                xim_s[0, :, ls] = bu[:, sgl:2 * sgl]
            else:
                xre_s[:, j, ls] = bu[:, 0:sgl]
                xim_s[:, j, ls] = bu[:, sgl:2 * sgl]
        ar = abre_ref[:, ls]
        ai = abim_ref[:, ls]

        def step(t, carry):
            xr, xi = carry
            nr = ar * xr - ai * xi + xre_s[t, :, ls]
            ni = ar * xi + ai * xr + xim_s[t, :, ls]
            xre_s[t, :, ls] = nr
            xim_s[t, :, ls] = ni
            return nr, ni

        xr, xi = lax.fori_loop(0, tc, step, (sre_s[:, ls], sim_s[:, ls]))
        sre_s[:, ls] = xr
        sim_s[:, ls] = xi
    fre_ref[...] = sre_s[...]
    fim_ref[...] = sim_s[...]
    if not emit:
        y_ref[...] = jnp.zeros(y_ref.shape, y_ref.dtype)
        return
    for j in range(row_sets):
        ys = []
        for sg in range(SSM_SG):
            ls = slice(sg * sgl, (sg + 1) * sgl)
            if tc == 1:
                xr = xre_s[0, :, ls]
                xi = xim_s[0, :, ls]
            else:
                xr = xre_s[:, j, ls]
                xi = xim_s[:, j, ls]
            x2 = jnp.concatenate([xr, xi], axis=1).astype(BF16)
            ys.append(_dot(x2, wc_ref[sg]))
        y = jnp.concatenate(ys, axis=1) + dsk_ref[...] * u_ref[j]
        z = _dot(_gelu(y).astype(BF16), wglu_ref[...])
        o = z[:, 0:SSM_W] * jax.nn.sigmoid(z[:, SSM_W:2 * SSM_W])
        y_ref[j] = _rms(o, gout_ref[...]).astype(BF16)


def _ssm(u3, init_re, init_im, wb, wc, ab_re, ab_im, d_skip, w_glu, g_out, tc, emit):
    n_seg, t_len, _ = u3.shape
    if tc == 1:
        assert t_len == 1
        u3 = u3.reshape(1, n_seg, SSM_W)
        blk = (1, n_seg, SSM_W)
    else:
        blk = (n_seg, tc, SSM_W)
    full = lambda a: pl.BlockSpec(a.shape, lambda c: (0,) * a.ndim)
    body = functools.partial(_ssm_body, n_seg=n_seg, tc=tc, emit=emit)
    st = jax.ShapeDtypeStruct((n_seg, SSM_LANES), F32)
    y_shape = u3.shape if emit else blk
    y_map = (lambda c: (0, c, 0)) if (emit and tc > 1) else (lambda c: (0, 0, 0))
    y, fre, fim = pl.pallas_call(
        body,
        grid=(t_len // tc,),
        in_specs=[pl.BlockSpec(blk, (lambda c: (0, c, 0)) if tc > 1 else (lambda c: (0, 0, 0))),
                  full(init_re), full(init_im), full(wb), full(wc),
                  full(ab_re), full(ab_im), full(d_skip), full(w_glu), full(g_out)],
        out_specs=[pl.BlockSpec(blk, y_map), full(init_re), full(init_im)],
        out_shape=[jax.ShapeDtypeStruct(y_shape, BF16), st, st],
        scratch_shapes=[pltpu.VMEM((tc, n_seg, SSM_LANES), F32), pltpu.VMEM((tc, n_seg, SSM_LANES), F32),
                        pltpu.VMEM((n_seg, SSM_LANES), F32), pltpu.VMEM((n_seg, SSM_LANES), F32)],
        compiler_params=_cparams(("arbitrary",)),
        name="ssm_emit" if emit else "ssm_final",
    )(u3, init_re, init_im, wb, wc, ab_re, ab_im, d_skip, w_glu, g_out)
    return y.reshape(n_seg, -1, SSM_W), fre, fim


def _ssm_chain_body(fre_ref, fim_ref, abre_ref, abim_ref, ire_ref, iim_ref, *, n_seg, log2_len):
    pr = abre_ref[...]
    pi = abim_ref[...]
    for _ in range(log2_len):
        pr, pi = pr * pr - pi * pi, 2.0 * pr * pi
    cr = jnp.zeros((1, SSM_LANES), F32)
    ci = jnp.zeros((1, SSM_LANES), F32)
    for j in range(n_seg):
        ire_ref[j:j + 1, :] = cr
        iim_ref[j:j + 1, :] = ci
        fr = fre_ref[j:j + 1, :]
        fi = fim_ref[j:j + 1, :]
        cr, ci = fr + pr * cr - pi * ci, fi + pr * ci + pi * cr


def _ssm_chain(fre, fim, ab_re, ab_im, seg_len):
    n_seg = fre.shape[0]
    log2_len = int(math.log2(seg_len))
    assert 2 ** log2_len == seg_len
    st = jax.ShapeDtypeStruct((n_seg, SSM_LANES), F32)
    return pl.pallas_call(
        functools.partial(_ssm_chain_body, n_seg=n_seg, log2_len=log2_len),
        out_shape=[st, st],
        compiler_params=pltpu.CompilerParams(vmem_limit_bytes=VMEM_LIMIT),
        name="ssm_chain",
    )(fre, fim, ab_re, ab_im)


def _outproj_body(a_ref, s_ref, x_ref, w_ref, h_ref):
    h_ref[...] = x_ref[...] + _dot(a_ref[...], w_ref[0:ATTN_W, :]) + _dot(s_ref[...], w_ref[ATTN_W:D_MODEL, :])


def _outproj(a_n, s_n, x, w, tm):
    rows = x.shape[0]
    return pl.pallas_call(
        _outproj_body,
        grid=(rows // tm,),
        in_specs=[pl.BlockSpec((tm, ATTN_W), lambda i: (i, 0)), pl.BlockSpec((tm, SSM_W), lambda i: (i, 0)),
                  pl.BlockSpec((tm, D_MODEL), lambda i: (i, 0)), pl.BlockSpec(w.shape, lambda i: (0, 0))],
        out_specs=pl.BlockSpec((tm, D_MODEL), lambda i: (i, 0)),
        out_shape=jax.ShapeDtypeStruct((rows, D_MODEL), F32),
        compiler_params=_cparams(("arbitrary",)),
        name="outproj",
    )(a_n, s_n, x, w)


def _ffn_body(h_ref, gffn_ref, wa_ref, wg_ref, wd_ref, cw_ref, cb_ref, p2_ref, p1_ref,
              y_ref, cnew_ref, hn_s, acc_s, car_s, *, seq, tm):
    r = pl.program_id(0)
    j = pl.program_id(1)
    nj = pl.num_programs(1)

    @pl.when(j == 0)
    def _norm():
        hn_s[...] = _rms(h_ref[...], gffn_ref[...]).astype(BF16)
        acc_s[...] = jnp.zeros_like(acc_s)

    hn = hn_s[...]
    a = _dot(hn, wa_ref[...])
    g = _dot(hn, wg_ref[...])
    if seq:
        @pl.when(r == 0)
        def _first():
            car_s[j, 6:7, :] = p2_ref[...]
            car_s[j, 7:8, :] = p1_ref[...]

        prev = car_s[j]
        row = lax.broadcasted_iota(I32, a.shape, 0)
        a1 = jnp.where(row == 0, prev[7:8, :], pltpu.roll(a, 1, 0))
        a2 = jnp.where(row == 0, prev[6:7, :], jnp.where(row == 1, prev[7:8, :], pltpu.roll(a, 2, 0)))
        car_s[j] = a[tm - 8:tm, :]
        tf = a.shape[1]
        cnew_ref[:, pl.ds(pl.multiple_of(j * tf, tf), tf)] = a[tm - 8:tm, :]
    else:
        a1 = p1_ref[...]
        a2 = p2_ref[...]
        cnew_ref[...] = a
    c = cb_ref[...] + cw_ref[0:1, :] * a2 + cw_ref[1:2, :] * a1 + cw_ref[2:3, :] * a
    acc_s[...] += _dot((_gelu(c) * g).astype(BF16), wd_ref[...])

    @pl.when(j == nj - 1)
    def _out():
        y_ref[...] = h_ref[...] + acc_s[...]


def _ffn(h, g_ffn, wa, wg, wd, cw, cb, p2, p1, tm, tf, seq):
    rows = h.shape[0]
    nj = D_FF_PAD // tf
    body = functools.partial(_ffn_body, seq=seq, tm=tm)
    if seq:
        tap_spec = pl.BlockSpec((1, tf), lambda r, j: (0, j))
        cnew_spec = pl.BlockSpec((8, D_FF_PAD), lambda r, j: (0, 0))
        cnew_shape = jax.ShapeDtypeStruct((8, D_FF_PAD), F32)
    else:
        tap_spec = pl.BlockSpec((tm, tf), lambda r, j: (r, j))
        cnew_spec = pl.BlockSpec((tm, tf), lambda r, j: (r, j))
        cnew_shape = jax.ShapeDtypeStruct((rows, D_FF_PAD), F32)
    return pl.pallas_call(
        body,
        grid=(rows // tm, nj),
        in_specs=[pl.BlockSpec((tm, D_MODEL), lambda r, j: (r, 0)), pl.BlockSpec((1, D_MODEL), lambda r, j: (0, 0)),
                  pl.BlockSpec((D_MODEL, tf), lambda r, j: (0, j)), pl.BlockSpec((D_MODEL, tf), lambda r, j: (0, j)),
                  pl.BlockSpec((tf, D_MODEL), lambda r, j: (j, 0)), pl.BlockSpec((CONV_W, tf), lambda r, j: (0, j)),
                  pl.BlockSpec((1, tf), lambda r, j: (0, j)), tap_spec, tap_spec],
        out_specs=[pl.BlockSpec((tm, D_MODEL), lambda r, j: (r, 0)), cnew_spec],
        out_shape=[jax.ShapeDtypeStruct((rows, D_MODEL), F32), cnew_shape],
        scratch_shapes=[pltpu.VMEM((tm, D_MODEL), BF16), pltpu.VMEM((tm, D_MODEL), F32), pltpu.VMEM((nj, 8, tf), F32)],
        compiler_params=_cparams(("arbitrary", "arbitrary")),
        name="ffn_seq" if seq else "ffn_rows",
    )(h, g_ffn, wa, wg, wd, cw, cb, p2, p1)


def _s1_copies(pt_ref, cache_ref, x_s, sem, b, slot, n_pages):
    cps = []
    for pg in range(n_pages):
        page = pt_ref[b * n_pages + pg]
        for s in range(2 * N_KV):
            cps.append(pltpu.make_async_copy(cache_ref.at[page, :, s, :],
                                             x_s.at[slot, s, pl.ds(pg * PAGE, PAGE), :], sem.at[slot]))
    return cps


def _s1_body(pt_ref, cache_ref, q_ref, wcat_ref, cpos_ref, w2_ref, gkc_ref, rb_ref, bkc_ref, mt_ref,
             oc_ref, idx_ref, val_ref, x_s, bias_s, sem, *, n_pages, past):
    b = pl.program_id(0)
    nb = pl.num_programs(0)
    slot = b % 2

    @pl.when(b == 0)
    def _first():
        for cp in _s1_copies(pt_ref, cache_ref, x_s, sem, 0, 0, n_pages):
            cp.start()
        for hd in range(N_HEADS):
            bias_s[hd:hd + 1, :] = _bias_lookup(bkc_ref[...], rb_ref, hd)

    @pl.when(b + 1 < nb)
    def _next():
        for cp in _s1_copies(pt_ref, cache_ref, x_s, sem, b + 1, 1 - slot, n_pages):
            cp.start()

    for cp in _s1_copies(pt_ref, cache_ref, x_s, sem, b, slot, n_pages):
        cp.wait()

    n_c = past // CMP_STRIDE
    cio = lax.broadcasted_iota(I32, (N_HEADS, n_c), 1)
    hrow = lax.broadcasted_iota(I32, (N_HEADS, n_c), 0) // Q_PER_KV
    q = q_ref[0]
    s_all = jnp.zeros((N_HEADS, n_c), F32)
    vcs = []
    for h in range(N_KV):
        kc = _rms(_compress_rows(x_s.at[slot, h], wcat_ref.at[0], cpos_ref[0, 0:1, :], w2_ref[0]), gkc_ref[...])
        vcs.append(_compress_rows(x_s.at[slot, N_KV + h], wcat_ref.at[1], cpos_ref[1, 0:1, :], w2_ref[1]).astype(BF16))
        s_all = jnp.where(hrow == h, _dot_nt(q, kc.astype(BF16)), s_all)
    ok = cio < n_c - 1
    s_all = jnp.where(ok, s_all + bias_s[...], NEG)
    m = jnp.max(s_all, axis=-1, keepdims=True)
    p = jnp.where(ok, jnp.exp(s_all - m), 0.0)
    p = p / jnp.maximum(jnp.sum(p, axis=-1, keepdims=True), 1e-30)
    pb = p.astype(BF16)
    hrow_o = lax.broadcasted_iota(I32, (N_HEADS, HEAD_DIM), 0) // Q_PER_KV
    o_c = jnp.zeros((N_HEADS, HEAD_DIM), F32)
    for h in range(N_KV):
        o_c = jnp.where(hrow_o == h, _dot(pb, vcs[h]), o_c)
    oc_ref[0] = o_c
    rio = lax.broadcasted_iota(I32, (8, n_c), 0)
    imp = jnp.zeros((8, n_c), F32)
    for h in range(N_KV):
        ih = p[4 * h:4 * h + 1] + p[4 * h + 1:4 * h + 2] + p[4 * h + 2:4 * h + 3] + p[4 * h + 3:4 * h + 4]
        imp = jnp.where(rio == h, ih, imp)
    imp = jnp.concatenate([imp, jnp.zeros((LANE - 8, n_c), F32)], axis=0)
    n_j = mt_ref.shape[0]
    imp_t = jnp.zeros((n_j, LANE), F32)
    for part in _split3(imp):
        imp_t = imp_t + _dot_nt(mt_ref[...], part)
    j_io = lax.broadcasted_iota(I32, (n_j, LANE), 0)
    cur = past // SEL_BLOCK
    forced = (j_io == 0) | (j_io == cur) | (j_io == cur - 1)
    score_t = jnp.where(forced, 1e9, jnp.where(j_io <= cur, imp_t, -1.0))
    score_t = jnp.where(j_io <= cur, score_t, -jnp.inf)

    def on_pick(r, idx, okv):
        idx_ref[0, r:r + 1, :] = idx
        val_ref[0, r:r + 1, :] = jnp.where(okv, 1, 0)

    _select_rounds(score_t, on_pick)


def _s1(page_table, cache4, q3, wcat, cpos, w2, g_kc, rel_bias):
    n_b, n_pages = page_table.shape
    past = n_pages * PAGE
    n_c = past // CMP_STRIDE
    ns = past // SEL_BLOCK + 1
    n_j = -(-ns // 8) * 8
    c = np.arange(n_c)[None, :]
    bkc = _bucket_np(past - (CMP_STRIDE * c + CMP_LEN - 1))
    j = np.arange(n_j)[:, None]
    mt = (((c // CMP_PER_SEL) == j) | (c == CMP_PER_SEL * j - 1)) & (c < n_c - 1)
    mt = jnp.asarray(mt.astype(np.float32), dtype=BF16)
    full = lambda a: pl.BlockSpec(a.shape, lambda b, pt: (0,) * a.ndim)
    body = functools.partial(_s1_body, n_pages=n_pages, past=past)
    return pl.pallas_call(
        body,
        grid_spec=pltpu.PrefetchScalarGridSpec(
            num_scalar_prefetch=1,
            grid=(n_b,),
            in_specs=[pl.BlockSpec(memory_space=pl.ANY), pl.BlockSpec((1, N_HEADS, HEAD_DIM), lambda b, pt: (b, 0, 0)),
                      full(wcat), full(cpos), full(w2), full(g_kc), pl.BlockSpec(memory_space=pltpu.SMEM),
                      pl.BlockSpec((1, n_c), lambda b, pt: (0, 0)), full(mt)],
            out_specs=[pl.BlockSpec((1, N_HEADS, HEAD_DIM), lambda b, pt: (b, 0, 0)),
                       pl.BlockSpec((1, N_SEL, LANE), lambda b, pt: (b, 0, 0)),
                       pl.BlockSpec((1, N_SEL, LANE), lambda b, pt: (b, 0, 0))],
            scratch_shapes=[pltpu.VMEM((2, 2 * N_KV, past, HEAD_DIM), F32), pltpu.VMEM((N_HEADS, n_c), F32),
                            pltpu.SemaphoreType.DMA((2,))],
        ),
        out_shape=[jax.ShapeDtypeStruct((n_b, N_HEADS, HEAD_DIM), F32), jax.ShapeDtypeStruct((n_b, N_SEL, LANE), I32),
                   jax.ShapeDtypeStruct((n_b, N_SEL, LANE), I32)],
        compiler_params=_cparams(("arbitrary",)),
        name="sample_cmp",
    )(page_table.reshape(-1), cache4, q3, wcat, cpos, w2, g_kc, rel_bias, jnp.asarray(bkc), mt)


def _s2_copies(idx_ref, pt_ref, cache_ref, cwin_ref, ks_s, vs_s, kw_s, vw_s, sem, b, slot, n_pages):
    cps = []
    n_blk = n_pages * (PAGE // SEL_BLOCK)
    for h in range(N_KV):
        for r in range(N_SEL):
            jb = jnp.minimum(idx_ref[(b * N_SEL + r) * N_KV + h], n_blk - 1)
            page = pt_ref[b * n_pages + jb // 2]
            row0 = pl.multiple_of((jb % 2) * SEL_BLOCK, SEL_BLOCK)
            cps.append(pltpu.make_async_copy(cache_ref.at[page, pl.ds(row0, SEL_BLOCK), 2 * N_KV + h, :],
                                             ks_s.at[slot, h, pl.ds(r * SEL_BLOCK, SEL_BLOCK), :], sem.at[slot]))
            cps.append(pltpu.make_async_copy(cache_ref.at[page, pl.ds(row0, SEL_BLOCK), 3 * N_KV + h, :],
                                             vs_s.at[slot, h, pl.ds(r * SEL_BLOCK, SEL_BLOCK), :], sem.at[slot]))
        cps.append(pltpu.make_async_copy(cwin_ref.at[b, :, h, :], kw_s.at[slot, h], sem.at[slot]))
        cps.append(pltpu.make_async_copy(cwin_ref.at[b, :, N_KV + h, :], vw_s.at[slot, h], sem.at[slot]))
    return cps


def _s2_body(idx_ref, val_ref, pt_ref, cache_ref, cwin_ref, q_ref, oc_ref, gt_ref, kns_ref, vns_ref, knw_ref, vnw_ref,
             rb_ref, bks_ref, bkw_ref, gout_ref, o_ref, ks_s, vs_s, kw_s, vw_s, bs_s, bw_s, sem, *, n_pages, past):
    b = pl.program_id(0)
    nb = pl.num_programs(0)
    slot = b % 2
    args = (idx_ref, pt_ref, cache_ref, cwin_ref, ks_s, vs_s, kw_s, vw_s, sem)

    @pl.when(b == 0)
    def _first():
        for cp in _s2_copies(*args, 0, 0, n_pages):
            cp.start()
        for hd in range(N_HEADS):
            bs_s[hd:hd + 1, :] = _bias_lookup(bks_ref[...], rb_ref, hd)
            bw_s[hd:hd + 1, :] = _bias_lookup(bkw_ref[...], rb_ref, hd)

    @pl.when(b + 1 < nb)
    def _next():
        for cp in _s2_copies(*args, b + 1, 1 - slot, n_pages):
            cp.start()

    for cp in _s2_copies(*args, b, slot, n_pages):
        cp.wait()

    n_blk = n_pages * (PAGE // SEL_BLOCK)
    q = q_ref[0]
    qf = q.astype(F32)
    hrow = lax.broadcasted_iota(I32, (N_HEADS, 1), 0) // Q_PER_KV
    lane = lax.broadcasted_iota(I32, (N_HEADS, LANE), 1)
    bias0 = jnp.concatenate([jnp.full((1, 1), rb_ref[0, hd], F32) for hd in range(N_HEADS)], axis=0)
    b31 = jnp.concatenate([jnp.full((1, 1), rb_ref[REL_BUCKETS - 1, hd], F32) for hd in range(N_HEADS)], axis=0)

    tiles = []
    new_sel = jnp.zeros((N_HEADS, 1), F32)
    for t in range(N_SEL // 2):
        s_t = jnp.zeros((N_HEADS, LANE), F32)
        for h in range(N_KV):
            s_h = _dot_nt(q, ks_s[slot, h, pl.ds(t * LANE, LANE), :].astype(BF16))
            halves = []
            for half in range(2):
                r = 2 * t + half
                jb = idx_ref[(b * N_SEL + r) * N_KV + h]
                okr = (val_ref[(b * N_SEL + r) * N_KV + h] > 0) & (jb < n_blk)
                near = bs_s[:, (half * 2) * LANE:(half * 2 + 1) * LANE]
                nearer = bs_s[:, (half * 2 + 1) * LANE:(half * 2 + 2) * LANE]
                bias = jnp.where(jb == n_blk - 1, nearer, jnp.where(jb == n_blk - 2, near, b31))
                halves.append(jnp.where(okr, s_h + bias, NEG))
                new_sel = jnp.where((hrow == h) & (val_ref[(b * N_SEL + r) * N_KV + h] > 0) & (jb == n_blk), 1.0, new_sel)
            s_h = jnp.where(lane < SEL_BLOCK, halves[0], halves[1])
            s_t = jnp.where(hrow == h, s_h, s_t)
        tiles.append(s_t)
    s_new = jnp.sum(qf * kns_ref[0], axis=-1, keepdims=True) + bias0
    s_new = jnp.where(new_sel > 0.5, s_new, NEG)
    m = s_new
    for s_t in tiles:
        m = jnp.maximum(m, jnp.max(s_t, axis=-1, keepdims=True))
    p_new = jnp.where(new_sel > 0.5, jnp.exp(s_new - m), 0.0)
    l = p_new
    acc = p_new * vns_ref[0]
    for t, s_t in enumerate(tiles):
        p = jnp.where(s_t > 0.5 * NEG, jnp.exp(s_t - m), 0.0)
        l = l + jnp.sum(p, axis=-1, keepdims=True)
        pb = p.astype(BF16)
        for h in range(N_KV):
            pv = _dot(pb, vs_s[slot, h, pl.ds(t * LANE, LANE), :].astype(BF16))
            acc = acc + jnp.where(hrow == h, pv, 0.0)
    o_s = acc / jnp.maximum(l, 1e-30)

    n_w = kw_s.shape[2]
    wt = []
    for t in range(n_w // LANE):
        s_t = jnp.zeros((N_HEADS, LANE), F32)
        for h in range(N_KV):
            s_h = _dot_nt(q, kw_s[slot, h, pl.ds(t * LANE, LANE), :].astype(BF16))
            s_t = jnp.where(hrow == h, s_h, s_t)
        wt.append(s_t + bw_s[:, t * LANE:(t + 1) * LANE])
    s_new = jnp.sum(qf * knw_ref[0], axis=-1, keepdims=True) + bias0
    m = s_new
    for s_t in wt:
        m = jnp.maximum(m, jnp.max(s_t, axis=-1, keepdims=True))
    p_new = jnp.exp(s_new - m)
    l = p_new
    acc = p_new * vnw_ref[0]
    for t, s_t in enumerate(wt):
        p = jnp.exp(s_t - m)
        l = l + jnp.sum(p, axis=-1, keepdims=True)
        pb = p.astype(BF16)
        for h in range(N_KV):
            pv = _dot(pb, vw_s[slot, h, pl.ds(t * LANE, LANE), :].astype(BF16))
            acc = acc + jnp.where(hrow == h, pv, 0.0)
    o_w = acc / l

    gt = gt_ref[0]
    a = gt[:, 0:1] * oc_ref[0] + gt[:, 1:2] * o_s + gt[:, 2:3] * o_w
    ms = jnp.sum(jnp.sum(a * a, axis=-1, keepdims=True), axis=0, keepdims=True) / (N_HEADS * HEAD_DIM)
    o_ref[0] = (a * lax.rsqrt(ms + EPS) * gout_ref[...]).astype(BF16)


def _s2(idx, val, page_table, cache4, cwin4, q3, o_c, gates3, kns, vns, knw, vnw, rel_bias, g_out3):
    n_b, n_pages = page_table.shape
    past = n_pages * PAGE
    n_w = cwin4.shape[1]
    s = np.arange(SEL_BLOCK)
    d_near = past - ((past // SEL_BLOCK - 2) * SEL_BLOCK + s)
    d_nearer = past - ((past // SEL_BLOCK - 1) * SEL_BLOCK + s)
    z = np.zeros(SEL_BLOCK, np.int64)
    bks = np.concatenate([d_near, z, d_nearer, z, z, d_near, z, d_nearer])[None, :]
    bkw = (past - (past - n_w + np.arange(n_w)))[None, :]
    full = lambda a: pl.BlockSpec(a.shape, lambda b, *_: (0,) * a.ndim)
    per_b = lambda a: pl.BlockSpec((1,) + a.shape[1:], lambda b, *_: (b,) + (0,) * (a.ndim - 1))
    body = functools.partial(_s2_body, n_pages=n_pages, past=past)
    return pl.pallas_call(
        body,
        grid_spec=pltpu.PrefetchScalarGridSpec(
            num_scalar_prefetch=3,
            grid=(n_b,),
            in_specs=[pl.BlockSpec(memory_space=pl.ANY), pl.BlockSpec(memory_space=pl.ANY),
                      per_b(q3), per_b(o_c), per_b(gates3), per_b(kns), per_b(vns), per_b(knw), per_b(vnw),
                      pl.BlockSpec(memory_space=pltpu.SMEM), pl.BlockSpec((1, 4 * LANE), lambda b, *_: (0, 0)),
                      pl.BlockSpec((1, n_w), lambda b, *_: (0, 0)), full(g_out3)],
            out_specs=pl.BlockSpec((1, N_HEADS, HEAD_DIM), lambda b, *_: (b, 0, 0)),
            scratch_shapes=[pltpu.VMEM((2, N_KV, N_SEL * SEL_BLOCK, HEAD_DIM), F32),
                            pltpu.VMEM((2, N_KV, N_SEL * SEL_BLOCK, HEAD_DIM), F32),
                            pltpu.VMEM((2, N_KV, n_w, HEAD_DIM), F32), pltpu.VMEM((2, N_KV, n_w, HEAD_DIM), F32),
                            pltpu.VMEM((N_HEADS, 4 * LANE), F32), pltpu.VMEM((N_HEADS, n_w), F32),
                            pltpu.SemaphoreType.DMA((2,))],
        ),
        out_shape=jax.ShapeDtypeStruct((n_b, N_HEADS, HEAD_DIM), BF16),
        compiler_params=_cparams(("arbitrary",)),
        name="sample_attn",
    )(idx, val, page_table.reshape(-1), cache4, cwin4, q3, o_c, gates3, kns, vns, knw, vnw, rel_bias,
      jnp.asarray(_bucket_np(bks)), jnp.asarray(_bucket_np(bkw)), g_out3)


def _winshift_body(cwin_ref, new_ref, o_ref, sem):
    n_b, n_w = cwin_ref.shape[0], cwin_ref.shape[1]
    cps = []
    for b in range(n_b):
        cps.append(pltpu.make_async_copy(cwin_ref.at[b, pl.ds(1, n_w - 1)], o_ref.at[b, pl.ds(0, n_w - 1)], sem.at[0]))
        cps.append(pltpu.make_async_copy(new_ref.at[b], o_ref.at[b, n_w - 1], sem.at[1]))
    for cp in cps:
        cp.start()
    for cp in cps:
        cp.wait()


def _winshift(cwin4, new3):
    return pl.pallas_call(
        _winshift_body,
        in_specs=[pl.BlockSpec(memory_space=pl.ANY), pl.BlockSpec(memory_space=pl.ANY)],
        out_specs=pl.BlockSpec(memory_space=pl.ANY),
        out_shape=jax.ShapeDtypeStruct(cwin4.shape, F32),
        scratch_shapes=[pltpu.SemaphoreType.DMA((2,))],
        name="winshift",
    )(cwin4, new3)


def _block_diag_b(bb_re, bb_im):
    gl = SSM_G // SSM_SG
    eye = jnp.eye(gl, dtype=F32)

    def one(bb):
        t = jnp.transpose(bb, (1, 0, 2)).reshape(SSM_SG, gl, SSM_P, SSM_N)
        return jnp.einsum('sgpn,gh->sgphn', t, eye).reshape(SSM_SG, gl * SSM_P, gl * SSM_N)

    return jnp.concatenate([one(bb_re), one(bb_im)], axis=2).astype(BF16)


def _block_diag_c(c_re, c_im):
    gl = SSM_G // SSM_SG
    eye = jnp.eye(gl, dtype=F32)

    def one(c):
        t = jnp.transpose(c, (0, 2, 1)).reshape(SSM_SG, gl, SSM_N, SSM_P)
        return jnp.einsum('sgnp,gh->sgnhp', t, eye).reshape(SSM_SG, gl * SSM_N, gl * SSM_P)

    return jnp.concatenate([one(c_re), -one(c_im)], axis=1).astype(BF16)


def kernel(x_prompt, x_sample, cache_kv, page_table, cache_win, state_ssm, state_conv, rel_bias, g_mix, w_in, g_q, g_k, w_cmp1, pos_cmp, w_cmp2, lam_re, lam_im, log_dt, b_re, b_im, c_re, c_im, d_skip, w_glu, g_out_attn, g_out_ssm, w_out, g_ffn, w_up, conv_w, conv_b, w_down):
    depth = g_mix.shape[0]
    assert depth == 1 and x_prompt.shape[0] == 1 and x_sample.shape[1] == 1
    seq = x_prompt.shape[1]
    n_b = x_sample.shape[0]
    n_pages = page_table.shape[1]
    n_w = cache_win.shape[2]
    li = 0
    row = lambda v: v.reshape(1, -1)

    wi = w_in[li]
    c_g = ATTN_W + 6 * KV_W
    w_in_b = jnp.concatenate([wi[:, :c_g], wi[:, c_g + N_BRANCH * N_HEADS:], wi[:, c_g:c_g + N_BRANCH * N_HEADS],
                              jnp.zeros((D_MODEL, LANE - N_BRANCH * N_HEADS), F32)], axis=1).astype(BF16)
    w1 = w_cmp1[li]
    wcat = jnp.concatenate([w1[:, :CMP_STRIDE], w1[:, CMP_STRIDE:]], axis=-1).astype(BF16)
    w2 = w_cmp2[li].astype(BF16)
    w_glu_b = w_glu[li].astype(BF16)
    w_out_b = w_out[li].astype(BF16)
    padc = lambda a: jnp.pad(a, ((0, 0), (0, D_FF_PAD - D_FF)))
    wa = padc(w_up[li][:, :D_FF]).astype(BF16)
    wg = padc(w_up[li][:, D_FF:]).astype(BF16)
    wd = jnp.pad(w_down[li], ((0, D_FF_PAD - D_FF), (0, 0))).astype(BF16)
    cw = padc(conv_w[li])
    cb = padc(row(conv_b[li]))

    ab_re, ab_im, bb_re, bb_im, cpos = _prep(lam_re[li], lam_im[li], log_dt[li], b_re[li], b_im[li], pos_cmp[li], w1)
    wb = _block_diag_b(bb_re, bb_im)
    wc = _block_diag_c(c_re[li], c_im[li])
    ab_re = row(ab_re)
    ab_im = row(ab_im)

    in_args = (row(g_mix[li]), w_in_b, row(g_q[li]), row(g_k[li, 1]), row(g_k[li, 2]))
    ssm_args = (wb, wc, ab_re, ab_im, row(d_skip[li]), w_glu_b, row(g_out_ssm[li]))

    xp = x_prompt[0]
    q, kv, win, u, gates, kvb = _inproj(xp, *in_args, tm=512)
    cmp_out = _pcompress(kv, wcat, cpos, w2, row(g_k[li, 0]))
    zpad = jnp.zeros((CMP_STRIDE, 2 * KV_W), F32)
    cmp_pad = jnp.concatenate([zpad, cmp_out, zpad], axis=0)
    attn_n = _pattn(q, gates, kvb, cmp_pad[:, :KV_W], cmp_pad[:, KV_W:], rel_bias, row(g_out_attn[li]))
    n_seg = 8
    u3 = u.reshape(n_seg, seq // n_seg, SSM_W)
    zst = jnp.zeros((n_seg, SSM_LANES), F32)
    _, fre, fim = _ssm(u3, zst, zst, *ssm_args, tc=32, emit=False)
    ire, iim = _ssm_chain(fre, fim, ab_re, ab_im, seq // n_seg)
    ssm_n, hre, him = _ssm(u3, ire, iim, *ssm_args, tc=32, emit=True)
    h_p = _outproj(attn_n, ssm_n.reshape(seq, SSM_W), xp, w_out_b, tm=512)
    zrow = jnp.zeros((1, D_FF_PAD), F32)
    y_p, cnew_p = _ffn(h_p, row(g_ffn[li]), wa, wg, wd, cw, cb, zrow, zrow, tm=512, tf=512, seq=True)

    y_prompt = y_p[None]
    kv_prompt = kv.reshape(1, 1, seq, 4, N_KV, HEAD_DIM)
    win_prompt = win[seq - min(WINDOW, seq):].reshape(1, 1, min(WINDOW, seq), 2, N_KV, HEAD_DIM)
    ssm_prompt = jnp.stack([hre[n_seg - 1], him[n_seg - 1]], axis=-1).reshape(1, 1, SSM_G, SSM_N, 2)
    conv_prompt = cnew_p[6:8, :D_FF].reshape(1, 1, CONV_W - 1, D_FF)

    xs = x_sample[:, 0]
    q_s, kv_s, win_s, u_s, gates_s, _ = _inproj(xs, *in_args, tm=n_b)
    cache4 = cache_kv[li].reshape(cache_kv.shape[1], PAGE, 4 * N_KV, HEAD_DIM)
    cwin4 = cache_win[li].reshape(n_b, n_w, 2 * N_KV, HEAD_DIM)
    q3 = q_s.reshape(n_b, N_HEADS, HEAD_DIM)
    o_c, idx, val = _s1(page_table, cache4, q3, wcat, cpos, w2, row(g_k[li, 0]), rel_bias)
    idx = idx[:, :, :N_KV].reshape(-1)
    val = val[:, :, :N_KV].reshape(-1)
    rep = lambda a: jnp.repeat(a.reshape(n_b, N_KV, HEAD_DIM), Q_PER_KV, axis=1)
    kns = rep(kv_s[:, 2 * KV_W:3 * KV_W])
    vns = rep(kv_s[:, 3 * KV_W:4 * KV_W])
    knw = rep(win_s[:, :KV_W])
    vnw = rep(win_s[:, KV_W:])
    g3 = jnp.transpose(gates_s[:, :N_BRANCH * N_HEADS].reshape(n_b, N_BRANCH, N_HEADS), (0, 2, 1))
    g3 = jnp.pad(g3, ((0, 0), (0, 0), (0, LANE - N_BRANCH)))
    attn_s = _s2(idx, val, page_table, cache4, cwin4, q3, o_c, g3, kns, vns, knw, vnw, rel_bias,
                 g_out_attn[li].reshape(N_HEADS, HEAD_DIM))
    st = state_ssm[li].reshape(n_b, SSM_LANES, 2)
    ssm_s, sre, sim = _ssm(u_s.reshape(n_b, 1, SSM_W), st[:, :, 0], st[:, :, 1], *ssm_args, tc=1, emit=True)
    h_s = _outproj(attn_s.reshape(n_b, ATTN_W), ssm_s.reshape(n_b, SSM_W), xs, w_out_b, tm=n_b)
    sc = state_conv[li]
    y_s, a_s = _ffn(h_s, row(g_ffn[li]), wa, wg, wd, cw, cb, padc(sc[:, 0]), padc(sc[:, 1]), tm=n_b, tf=512, seq=False)
    win_sample = _winshift(cwin4, win_s.reshape(n_b, 2 * N_KV, HEAD_DIM))

    y_sample = y_s[:, None]
    kv_sample = kv_s.reshape(1, n_b, 1, 4, N_KV, HEAD_DIM)
    win_sample = win_sample.reshape(1, n_b, n_w, 2, N_KV, HEAD_DIM)
    ssm_sample = jnp.stack([sre, sim], axis=-1).reshape(1, n_b, SSM_G, SSM_N, 2)
    conv_sample = jnp.stack([sc[:, 1], a_s[:, :D_FF]], axis=1)[None]
    return (y_prompt, y_sample, kv_prompt, kv_sample, win_prompt, win_sample,
            ssm_prompt, ssm_sample, conv_prompt, conv_sample)
```

```python
import functools
import math

import numpy as np
import jax
import jax.numpy as jnp
from jax import lax
from jax.experimental import pallas as pl
from jax.experimental.pallas import tpu as pltpu

F32 = jnp.float32
BF16 = jnp.bfloat16
I32 = jnp.int32

D_MODEL = 2048
HEAD_DIM = 128
N_HEADS = 8
N_KV = 2
Q_PER_KV = 4
ATTN_W = 1024
KV_W = 256
N_BRANCH = 3
CMP_LEN = 32
CMP_STRIDE = 16
SEL_BLOCK = 64
CMP_PER_SEL = 4
N_SEL = 16
WINDOW = 512
REL_BUCKETS = 32
REL_MAX_DIST = 128
PAGE = 128
SSM_W = 1024
SSM_G = 64
SSM_N = 64
SSM_P = 16
SSM_SG = 8
SSM_LANES = SSM_G * SSM_N
D_FF = 5504
D_FF_PAD = 5632
CONV_W = 3
EPS = 1e-6
NEG = -1e30
QB = 128
LANE = 128
VMEM_LIMIT = 56 * 1024 * 1024


def _cparams(sem):
    return pltpu.CompilerParams(dimension_semantics=sem, vmem_limit_bytes=VMEM_LIMIT)


def _rms(x, g):
    return x * lax.rsqrt(jnp.mean(x * x, axis=-1, keepdims=True) + EPS) * g


def _gelu(x):
    return jax.nn.gelu(x)


def _dot(a, b):
    return jnp.dot(a, b, preferred_element_type=F32)


def _dot_nt(a, b):
    return lax.dot_general(a, b, (((1,), (1,)), ((), ())), preferred_element_type=F32)


def _split3(x):
    hi = x.astype(BF16)
    r1 = x - hi.astype(F32)
    mid = r1.astype(BF16)
    lo = (r1 - mid.astype(F32)).astype(BF16)
    return hi, mid, lo


def _bucket_np(d):
    n = np.maximum(d, 0)
    exact = REL_BUCKETS // 2
    nf = np.maximum(n, 1).astype(np.float32)
    large = exact + (np.log(nf / np.float32(exact)) / np.float32(math.log(REL_MAX_DIST / exact))
                     * np.float32(REL_BUCKETS - exact)).astype(np.int32)
    return np.where(n < exact, n, np.minimum(large, REL_BUCKETS - 1)).astype(np.int32)


def _bias_lookup(bkt, rb_ref, head, shift=None):
    last = rb_ref[REL_BUCKETS - 1, head]
    acc = jnp.full(bkt.shape, last, F32)
    for b in range(REL_BUCKETS - 1):
        acc = jnp.where(bkt == b, rb_ref[b, head], acc)
    if shift:
        acc = acc - last
    return acc


def _prep_body(lre_ref, lim_ref, ldt_ref, bre_ref, bim_ref, pos_ref, w1_ref,
               abre_ref, abim_ref, bbre_ref, bbim_ref, cpos_ref):
    lr = lre_ref[...]
    li = lim_ref[...]
    dt = jnp.exp(ldt_ref[...])
    mag = jnp.exp(lr * dt)
    ab_re = mag * jnp.cos(li * dt)
    ab_im = mag * jnp.sin(li * dt)
    den = lr * lr + li * li
    nr = ab_re - 1.0
    ni = ab_im
    f_re = (nr * lr + ni * li) / den
    f_im = (ni * lr - nr * li) / den
    abre_ref[...] = ab_re
    abim_ref[...] = ab_im
    for p in range(SSM_P):
        br = bre_ref[p]
        bi = bim_ref[p]
        bbre_ref[p] = f_re * br - f_im * bi
        bbim_ref[p] = f_re * bi + f_im * br
    for kind in range(2):
        cpos_ref[kind] = jnp.dot(pos_ref[kind], w1_ref[kind], preferred_element_type=F32,
                                 precision=lax.Precision.HIGHEST)


def _prep(lam_re, lam_im, log_dt, b_re, b_im, pos_cmp, w_cmp1):
    bre_t = jnp.transpose(b_re, (2, 0, 1))
    bim_t = jnp.transpose(b_im, (2, 0, 1))
    pos = jnp.zeros((2, 8, CMP_LEN * HEAD_DIM), F32).at[:, 0, :].set(pos_cmp.reshape(2, CMP_LEN * HEAD_DIM))
    w1 = w_cmp1.reshape(2, CMP_LEN * HEAD_DIM, HEAD_DIM)
    return pl.pallas_call(
        _prep_body,
        out_shape=[jax.ShapeDtypeStruct((SSM_G, SSM_N), F32), jax.ShapeDtypeStruct((SSM_G, SSM_N), F32),
                   jax.ShapeDtypeStruct((SSM_P, SSM_G, SSM_N), F32), jax.ShapeDtypeStruct((SSM_P, SSM_G, SSM_N), F32),
                   jax.ShapeDtypeStruct((2, 8, HEAD_DIM), F32)],
        compiler_params=pltpu.CompilerParams(vmem_limit_bytes=VMEM_LIMIT),
        name="prep",
    )(lam_re, lam_im, log_dt.reshape(SSM_G, 1), bre_t, bim_t, pos, w1)


IN_COLS_PAD = ATTN_W + 4 * KV_W + 2 * KV_W + SSM_W + LANE
KB_KS = (0, 2 * HEAD_DIM)
KB_E = HEAD_DIM
KB_VS = 3 * HEAD_DIM
KB_KW = 5 * HEAD_DIM
KB_VW = 7 * HEAD_DIM
KB_COLS = 9 * HEAD_DIM


def _inproj_body(x_ref, gmix_ref, w_ref, gq_ref, gks_ref, gkw_ref,
                 q_ref, kv_ref, win_ref, u_ref, gt_ref, kvb_ref, *, q_scale):
    tm = x_ref.shape[0]
    xn = _rms(x_ref[...], gmix_ref[...]).astype(BF16)
    zq = _dot(xn, w_ref[:, 0:ATTN_W])
    for h in range(N_HEADS):
        sl = slice(h * HEAD_DIM, (h + 1) * HEAD_DIM)
        q_ref[:, sl] = (_rms(zq[:, sl], gq_ref[...]) * q_scale).astype(BF16)
    zkv = _dot(xn, w_ref[:, ATTN_W:ATTN_W + 4 * KV_W])
    kv_ref[:, 0:2 * KV_W] = zkv[:, 0:2 * KV_W]
    for h in range(N_KV):
        sl = slice(2 * KV_W + h * HEAD_DIM, 2 * KV_W + (h + 1) * HEAD_DIM)
        ks = _rms(zkv[:, sl], gks_ref[...])
        kv_ref[:, sl] = ks
        kvb_ref[:, KB_KS[h]:KB_KS[h] + HEAD_DIM] = ks.astype(BF16)
    blk = (lax.broadcasted_iota(I32, (tm, HEAD_DIM), 0) + pl.program_id(0) * tm) // SEL_BLOCK
    kvb_ref[:, KB_E:KB_E + HEAD_DIM] = jnp.where(blk == lax.broadcasted_iota(I32, (tm, HEAD_DIM), 1), 1.0, 0.0).astype(BF16)
    kv_ref[:, 3 * KV_W:4 * KV_W] = zkv[:, 3 * KV_W:4 * KV_W]
    kvb_ref[:, KB_VS:KB_VS + KV_W] = zkv[:, 3 * KV_W:4 * KV_W].astype(BF16)
    c0 = ATTN_W + 4 * KV_W
    zw = _dot(xn, w_ref[:, c0:c0 + 2 * KV_W])
    for h in range(N_KV):
        sl = slice(h * HEAD_DIM, (h + 1) * HEAD_DIM)
        kw = _rms(zw[:, sl], gkw_ref[...])
        win_ref[:, sl] = kw
        kvb_ref[:, KB_KW + h * HEAD_DIM:KB_KW + (h + 1) * HEAD_DIM] = kw.astype(BF16)
    win_ref[:, KV_W:2 * KV_W] = zw[:, KV_W:2 * KV_W]
    kvb_ref[:, KB_VW:KB_VW + KV_W] = zw[:, KV_W:2 * KV_W].astype(BF16)
    c1 = c0 + 2 * KV_W
    u_ref[...] = _dot(xn, w_ref[:, c1:c1 + SSM_W])
    c2 = c1 + SSM_W
    gt_ref[...] = jax.nn.sigmoid(_dot(xn, w_ref[:, c2:c2 + LANE]))


def _seg_spec(tm, rows, n_seg):
    tiles_per_seg = rows // n_seg // tm
    return pl.BlockSpec((tm, SSM_W), lambda i: (i % tiles_per_seg, i // tiles_per_seg))


def _inproj(x, g_mix, w, g_q, g_ks, g_kw, tm, q_scale, n_seg=1):
    rows = x.shape[0]
    row_spec = lambda n: pl.BlockSpec((tm, n), lambda i: (i, 0))
    full = lambda a: pl.BlockSpec(a.shape, lambda i: (0,) * a.ndim)
    return pl.pallas_call(
        functools.partial(_inproj_body, q_scale=q_scale),
        grid=(rows // tm,),
        in_specs=[row_spec(D_MODEL), full(g_mix), full(w), full(g_q), full(g_ks), full(g_kw)],
        out_specs=[row_spec(ATTN_W), row_spec(4 * KV_W), row_spec(2 * KV_W), _seg_spec(tm, rows, n_seg), row_spec(LANE),
                   row_spec(KB_COLS)],
        out_shape=[jax.ShapeDtypeStruct((rows, ATTN_W), BF16), jax.ShapeDtypeStruct((rows, 4 * KV_W), F32),
                   jax.ShapeDtypeStruct((rows, 2 * KV_W), F32), jax.ShapeDtypeStruct((rows // n_seg, n_seg * SSM_W), F32),
                   jax.ShapeDtypeStruct((rows, LANE), F32), jax.ShapeDtypeStruct((rows, KB_COLS), BF16)],
        compiler_params=_cparams(("arbitrary",)),
        name="inproj",
    )(x, g_mix, w, g_q, g_ks, g_kw)


def _compress_rows(x_ref, wcat_ref, cpos, w2):
    n_ch = x_ref.shape[0] // CMP_STRIDE
    acc = jnp.zeros((n_ch, 2 * HEAD_DIM), F32)
    for p in range(CMP_STRIDE):
        xp = x_ref[pl.ds(p, n_ch, stride=CMP_STRIDE), :].astype(BF16)
        acc = acc + _dot(xp, wcat_ref[p])
    e_lo = acc[:, 0:HEAD_DIM]
    e_hi_next = pltpu.roll(acc[:, HEAD_DIM:2 * HEAD_DIM], n_ch - 1, 0)
    hid = _gelu(e_lo + e_hi_next + cpos)
    return _dot(hid.astype(BF16), w2)


def _pcompress_body(x_ref, wcat_ref, cpos_ref, w2_ref, gkc_ref, o_ref):
    s = pl.program_id(0)
    out = _compress_rows(x_ref, wcat_ref.at[0], cpos_ref[0, 0:1, :], w2_ref[0])
    o_ref[...] = jnp.where(s < N_KV, _rms(out, gkc_ref[...]), out)


def _pcompress(kv, wcat, cpos, w2, g_kc):
    rows = kv.shape[0]
    n_ch = rows // CMP_STRIDE
    return pl.pallas_call(
        _pcompress_body,
        grid=(2 * N_KV,),
        in_specs=[pl.BlockSpec((rows, HEAD_DIM), lambda s: (0, s)),
                  pl.BlockSpec((1, CMP_STRIDE, HEAD_DIM, 2 * HEAD_DIM), lambda s: (s // N_KV, 0, 0, 0)),
                  pl.BlockSpec((1, 8, HEAD_DIM), lambda s: (s // N_KV, 0, 0)),
                  pl.BlockSpec((1, HEAD_DIM, HEAD_DIM), lambda s: (s // N_KV, 0, 0)),
                  pl.BlockSpec((1, HEAD_DIM), lambda s: (0, 0))],
        out_specs=pl.BlockSpec((n_ch, HEAD_DIM), lambda s: (0, s)),
        out_shape=jax.ShapeDtypeStruct((n_ch, 2 * N_KV * HEAD_DIM), F32),
        compiler_params=_cparams(("arbitrary",)),
        name="pcompress",
    )(kv, wcat, cpos, w2, g_kc)


def _select_rounds(score_t, on_pick=None):
    n_j = score_t.shape[0]
    jio = lax.broadcasted_iota(I32, score_t.shape, 0)
    sel = jnp.zeros(score_t.shape, F32)
    sc = score_t
    for r in range(N_SEL):
        m = jnp.max(sc, axis=0, keepdims=True)
        idx = jnp.min(jnp.where(sc == m, jio, n_j), axis=0, keepdims=True)
        pick = jio == idx
        ok = m >= 0.0
        sel = jnp.where(pick & ok, 1.0, sel)
        sc = jnp.where(pick, -jnp.inf, sc)
        if on_pick is not None:
            on_pick(r, idx, ok)
    return sel


LOG2E = 1.4426950408889634


def _row_max(tiles):
    m = jnp.max(tiles[0], axis=-1, keepdims=True)
    for s in tiles[1:]:
        m = jnp.maximum(m, jnp.max(s, axis=-1, keepdims=True))
    return m


def _with_ones(v):
    return jnp.concatenate([v, jnp.ones(v.shape, v.dtype)], axis=1)


def _pattn_body(q_ref, gt_ref, kvb_ref, kc_ref, vc_ref, rb_ref, bk0_ref, bk1_ref, bkc_ref, mft_ref, gout_ref,
                o_ref, b0_s, b1_s, bc_s, bw4_s):
    i = pl.program_id(0)
    rows = Q_PER_KV * QB

    @pl.when(i == 0)
    def _tables():
        a_io = lax.broadcasted_iota(I32, (QB, QB), 0)
        b_io = lax.broadcasted_iota(I32, (QB, QB), 1)
        for hd in range(N_HEADS):
            t0 = _bias_lookup(bk0_ref[...], rb_ref, hd, shift=True) * LOG2E
            b0_s[hd] = jnp.where(a_io >= b_io, t0, NEG)
            b1_s[hd] = _bias_lookup(bk1_ref[...], rb_ref, hd, shift=True) * LOG2E
            bc_s[hd] = _bias_lookup(bkc_ref[...], rb_ref, hd, shift=True) * LOG2E
        bw4_s[...] = jnp.where(b_io >= a_io, 0.0, NEG)

    attn = []
    for h in range(N_KV):
        qg = [q_ref[:, (Q_PER_KV * h + g) * HEAD_DIM:(Q_PER_KV * h + g + 1) * HEAD_DIM] for g in range(Q_PER_KV)]
        q = jnp.concatenate(qg, axis=0)
        hs = slice(h * HEAD_DIM, (h + 1) * HEAD_DIM)

        def tab(ref):
            return jnp.concatenate([ref[Q_PER_KV * h + g] for g in range(Q_PER_KV)], axis=0)

        n_c = kc_ref.shape[0] - 2 * CMP_STRIDE
        near0 = pl.multiple_of(8 * i, 8)
        s_f = _dot_nt(q, kc_ref[0:n_c, hs].astype(BF16))
        cf = lax.broadcasted_iota(I32, (1, n_c), 1)
        ok_f = (cf < 8 * i) & (cf >= CMP_STRIDE)
        s_f = jnp.where(ok_f, s_f, NEG)
        s_n = _dot_nt(q, kc_ref[pl.ds(near0, 32), hs].astype(BF16)) + tab(bc_s)[:, 0:32]
        cn = lax.broadcasted_iota(I32, (1, 32), 1)
        a_row = lax.broadcasted_iota(I32, (rows, 32), 0) % QB
        ok_n = (CMP_STRIDE * (cn - CMP_STRIDE) <= a_row - (CMP_LEN - 1)) & (cn + 8 * i >= CMP_STRIDE)
        s_n = jnp.where(ok_n, s_n, NEG)
        m_c = _row_max([s_f, s_n])
        p_f = jnp.where(ok_f, jnp.exp2(s_f - m_c), 0.0)
        p_n = jnp.where(ok_n, jnp.exp2(s_n - m_c), 0.0)
        l_c = jnp.sum(p_f, axis=-1, keepdims=True) + jnp.sum(p_n, axis=-1, keepdims=True)
        inv_c = 1.0 / jnp.maximum(l_c, 1e-30)
        o_c = (_dot(p_f.astype(BF16), vc_ref[0:n_c, hs].astype(BF16))
               + _dot(p_n.astype(BF16), vc_ref[pl.ds(near0, 32), hs].astype(BF16))) * inv_c
        pn_f = p_f * inv_c
        pn_n = p_n * inv_c
        imp_f = pn_f[0:QB] + pn_f[QB:2 * QB] + pn_f[2 * QB:3 * QB] + pn_f[3 * QB:4 * QB]
        imp_n = pn_n[0:QB] + pn_n[QB:2 * QB] + pn_n[2 * QB:3 * QB] + pn_n[3 * QB:4 * QB]
        jn = lax.broadcasted_iota(I32, (QB, 32), 0)
        cc = lax.broadcasted_iota(I32, (QB, 32), 1) + 8 * i - CMP_STRIDE
        mnt = (((cc // CMP_PER_SEL) == jn) | (cc == CMP_PER_SEL * jn - 1)) & (cc >= 0)
        mnt = jnp.where(mnt, 1.0, 0.0).astype(BF16)
        imp_t = jnp.zeros((QB, QB), F32)
        for part in _split3(imp_f):
            imp_t = imp_t + _dot_nt(mft_ref[...], part)
        for part in _split3(imp_n):
            imp_t = imp_t + _dot_nt(mnt, part)
        j_io = lax.broadcasted_iota(I32, (QB, QB), 0)
        cur = 2 * i + lax.broadcasted_iota(I32, (QB, QB), 1) // SEL_BLOCK
        forced = (j_io == 0) | (j_io == cur) | (j_io == cur - 1)
        score_t = jnp.where(forced, 1e9, jnp.where(j_io <= cur, imp_t, -1.0))
        sel_t = _select_rounds(score_t)
        selneg = jnp.where(sel_t.T > 0.5, 0.0, NEG).astype(BF16)
        qa = [jnp.concatenate([qg[g], selneg] if h == 0 else [selneg, qg[g]], axis=1) for g in range(Q_PER_KV)]
        ka0 = h * HEAD_DIM
        vs0 = KB_VS + h * HEAD_DIM

        n_far = jnp.maximum(i - 1, 0) // 2
        near_s, near_v = [], []
        for back in range(3):
            kt = i - back
            start = pl.multiple_of(jnp.maximum(kt, 0) * QB, QB)
            s = _dot_nt(jnp.concatenate(qa, axis=0), kvb_ref[pl.ds(start, QB), ka0:ka0 + 2 * HEAD_DIM])
            if back == 0:
                s = s + tab(b0_s)
            elif back == 1:
                s = s + tab(b1_s) + jnp.where(kt >= 0, 0.0, NEG)
            else:
                s = s + jnp.where((kt >= 0) & (kt >= 2 * n_far), 0.0, NEG)
            near_s.append(s)
            near_v.append(_with_ones(kvb_ref[pl.ds(start, QB), vs0:vs0 + HEAD_DIM]))
        m_s = _row_max(near_s)
        acc_s = jnp.zeros((rows, 2 * HEAD_DIM), F32)
        for s, v in zip(near_s, near_v):
            acc_s = acc_s + _dot(jnp.exp2(s - m_s).astype(BF16), v)

        def far_tile(k, st):
            start = pl.multiple_of(k * (2 * QB), 2 * QB)
            kt_aug = kvb_ref[pl.ds(start, 2 * QB), ka0:ka0 + 2 * HEAD_DIM]
            v_aug = _with_ones(kvb_ref[pl.ds(start, 2 * QB), vs0:vs0 + HEAD_DIM])
            out = []
            for g in range(Q_PER_KV):
                m, acc = st[2 * g], st[2 * g + 1]
                s = _dot_nt(qa[g], kt_aug)
                m_new = jnp.maximum(m, jnp.max(s, axis=-1, keepdims=True))
                acc = jnp.exp2(m - m_new) * acc + _dot(jnp.exp2(s - m_new).astype(BF16), v_aug)
                out += [m_new, acc]
            return tuple(out)

        st0 = []
        for g in range(Q_PER_KV):
            st0 += [m_s[g * QB:(g + 1) * QB], acc_s[g * QB:(g + 1) * QB]]
        st = lax.fori_loop(0, n_far, far_tile, tuple(st0))
        acc_s = jnp.concatenate([st[2 * g + 1] for g in range(Q_PER_KV)], axis=0)
        o_s = acc_s[:, 0:HEAD_DIM] / acc_s[:, HEAD_DIM:HEAD_DIM + 1]

        kw0 = KB_KW + h * HEAD_DIM
        vw0 = KB_VW + h * HEAD_DIM
        win_s, win_v = [], []
        for back in range(5):
            kt = i - back
            start = pl.multiple_of(jnp.maximum(kt, 0) * QB, QB)
            s = _dot_nt(q, kvb_ref[pl.ds(start, QB), kw0:kw0 + HEAD_DIM])
            if back == 0:
                s = s + tab(b0_s)
            elif back == 1:
                s = s + tab(b1_s)
            elif back == 4:
                s = s + jnp.concatenate([bw4_s[...]] * Q_PER_KV, axis=0)
            if back > 0:
                s = s + jnp.where(kt >= 0, 0.0, NEG)
            win_s.append(s)
            win_v.append(_with_ones(kvb_ref[pl.ds(start, QB), vw0:vw0 + HEAD_DIM]))
        m_w = _row_max(win_s)
        acc_w = jnp.zeros((rows, 2 * HEAD_DIM), F32)
        for s, v in zip(win_s, win_v):
            acc_w = acc_w + _dot(jnp.exp2(s - m_w).astype(BF16), v)
        o_w = acc_w[:, 0:HEAD_DIM] / acc_w[:, HEAD_DIM:HEAD_DIM + 1]

        for g in range(Q_PER_KV):
            hd = Q_PER_KV * h + g
            rs = slice(g * QB, (g + 1) * QB)
            attn.append(gt_ref[:, hd:hd + 1] * o_c[rs] + gt_ref[:, N_HEADS + hd:N_HEADS + hd + 1] * o_s[rs]
                        + gt_ref[:, 2 * N_HEADS + hd:2 * N_HEADS + hd + 1] * o_w[rs])
    a = jnp.concatenate(attn, axis=1)
    o_ref[...] = _rms(a, gout_ref[...]).astype(BF16)


def _pattn_tables():
    a = np.arange(QB)[:, None]
    b = np.arange(QB)[None, :]
    bk0 = _bucket_np(a - b)
    bk1 = _bucket_np(a - b + QB)
    c = np.arange(LANE)[None, :] - CMP_STRIDE
    bkc = _bucket_np(a - CMP_STRIDE * c - (CMP_LEN - 1))
    cidx = np.arange(4 * QB)[None, :] - CMP_STRIDE
    j = np.arange(QB)[:, None]
    mft = (((cidx // CMP_PER_SEL) == j) | (cidx == CMP_PER_SEL * j - 1)) & (cidx >= 0)
    return (jnp.asarray(bk0), jnp.asarray(bk1), jnp.asarray(bkc), jnp.asarray(mft.astype(np.float32), dtype=BF16))


def _pattn(q, gates, kvb, kcp, vcp, rel_bias, g_out):
    rows = q.shape[0]
    bk0, bk1, bkc, mft = _pattn_tables()
    full = lambda a: pl.BlockSpec(a.shape, lambda i: (0,) * a.ndim)
    once = lambda a: pl.BlockSpec(a.shape, lambda i: (0,) * a.ndim, pipeline_mode=pl.Buffered(1))
    return pl.pallas_call(
        _pattn_body,
        grid=(rows // QB,),
        in_specs=[pl.BlockSpec((QB, ATTN_W), lambda i: (i, 0)), pl.BlockSpec((QB, LANE), lambda i: (i, 0)),
                  once(kvb), full(kcp), full(vcp), pl.BlockSpec(memory_space=pltpu.SMEM),
                  full(bk0), full(bk1), full(bkc), full(mft), full(g_out)],
        out_specs=pl.BlockSpec((QB, ATTN_W), lambda i: (i, 0)),
        out_shape=jax.ShapeDtypeStruct((rows, ATTN_W), BF16),
        scratch_shapes=[pltpu.VMEM((N_HEADS, QB, QB), F32), pltpu.VMEM((N_HEADS, QB, QB), F32),
                        pltpu.VMEM((N_HEADS, QB, LANE), F32), pltpu.VMEM((QB, QB), F32)],
        compiler_params=_cparams(("arbitrary",)),
        name="pattn",
    )(q, gates, kvb, kcp, vcp, rel_bias, bk0, bk1, bkc, mft, g_out)


def _ssm_body(u_ref, ire_ref, iim_ref, wb_ref, wc_ref, abre_ref, abim_ref, dsk_ref, wglu_ref, gout_ref,
              y_ref, fre_ref, fim_ref, u_s, xre_s, xim_s, y_s, sre_s, sim_s, *, n_seg, tc, emit):
    c = pl.program_id(0)
    sgl = SSM_LANES // SSM_SG

    @pl.when(c == 0)
    def _init():
        sre_s[...] = ire_ref[...]
        sim_s[...] = iim_ref[...]

    n_lt = SSM_W // LANE
    if tc == 1:
        u_cols = [u_ref[:, l * LANE:(l + 1) * LANE] for l in range(n_lt)]
    else:
        for s in range(n_seg):
            for l in range(n_lt):
                u_s[l, pl.ds(s, tc, stride=n_seg), :] = u_ref[:, s * SSM_W + l * LANE:s * SSM_W + (l + 1) * LANE]
        u_cols = [u_s[l] for l in range(n_lt)]
    for sg in range(SSM_SG):
        ls = slice(sg * sgl, (sg + 1) * sgl)
        bu = _dot(u_cols[sg].astype(BF16), wb_ref[sg])
        xre_s[:, ls] = bu[:, 0:sgl]
        xim_s[:, ls] = bu[:, sgl:2 * sgl]
        ar = abre_ref[:, ls]
        ai = abim_ref[:, ls]

        def step(t, carry):
            xr, xi = carry
            rows = pl.ds(pl.multiple_of(t * n_seg, n_seg), n_seg)
            nr = ar * xr - ai * xi + xre_s[rows, ls]
            ni = ar * xi + ai * xr + xim_s[rows, ls]
            xre_s[rows, ls] = nr
            xim_s[rows, ls] = ni
            return nr, ni

        xr, xi = lax.fori_loop(0, tc, step, (sre_s[:, ls], sim_s[:, ls]), unroll=min(tc, 4))
        sre_s[:, ls] = xr
        sim_s[:, ls] = xi
    fre_ref[...] = sre_s[...]
    fim_ref[...] = sim_s[...]
    if not emit:
        y_ref[...] = jnp.zeros(y_ref.shape, y_ref.dtype)
        return
    ys = []
    for sg in range(SSM_SG):
        ls = slice(sg * sgl, (sg + 1) * sgl)
        x2 = jnp.concatenate([xre_s[:, ls], xim_s[:, ls]], axis=1).astype(BF16)
        ys.append(_dot(x2, wc_ref[sg]))
    y = jnp.concatenate(ys, axis=1) + dsk_ref[...] * jnp.concatenate(u_cols, axis=1)
    z = _dot(_gelu(y).astype(BF16), wglu_ref[...])
    o = z[:, 0:SSM_W] * jax.nn.sigmoid(z[:, SSM_W:2 * SSM_W])
    yn = _rms(o, gout_ref[...])
    if tc == 1:
        y_ref[...] = yn.astype(BF16)
    else:
        for l in range(n_lt):
            y_s[l] = yn[:, l * LANE:(l + 1) * LANE]
        for s in range(n_seg):
            for l in range(n_lt):
                y_ref[:, s * SSM_W + l * LANE:s * SSM_W + (l + 1) * LANE] = (
                    y_s[l, pl.ds(s, tc, stride=n_seg), :].astype(BF16))


def _ssm(u2, n_seg, init_re, init_im, wb, wc, ab_re, ab_im, d_skip, w_glu, g_out, tc, emit):
    if tc == 1:
        assert u2.shape == (n_seg, SSM_W)
        t_len = 1
        blk = (n_seg, SSM_W)
    else:
        t_len = u2.shape[0]
        assert u2.shape[1] == n_seg * SSM_W and n_seg % 8 == 0
        blk = (tc, n_seg * SSM_W)
    rows = tc * n_seg
    full = lambda a: pl.BlockSpec(a.shape, lambda c: (0,) * a.ndim)
    body = functools.partial(_ssm_body, n_seg=n_seg, tc=tc, emit=emit)
    st = jax.ShapeDtypeStruct((n_seg, SSM_LANES), F32)
    y_shape = u2.shape if emit else blk
    y_map = (lambda c: (c, 0)) if emit else (lambda c: (0, 0))
    return pl.pallas_call(
        body,
        grid=(t_len // tc,),
        in_specs=[pl.BlockSpec(blk, lambda c: (c, 0)),
                  full(init_re), full(init_im), full(wb), full(wc),
                  full(ab_re), full(ab_im), full(d_skip), full(w_glu), full(g_out)],
        out_specs=[pl.BlockSpec(blk, y_map), full(init_re), full(init_im)],
        out_shape=[jax.ShapeDtypeStruct(y_shape, BF16), st, st],
        scratch_shapes=[pltpu.VMEM((SSM_W // LANE, rows, LANE), F32), pltpu.VMEM((rows, SSM_LANES), F32),
                        pltpu.VMEM((rows, SSM_LANES), F32), pltpu.VMEM((SSM_W // LANE, rows, LANE), F32),
                        pltpu.VMEM((n_seg, SSM_LANES), F32), pltpu.VMEM((n_seg, SSM_LANES), F32)],
        compiler_params=_cparams(("arbitrary",)),
        name="ssm_emit" if emit else "ssm_final",
    )(u2, init_re, init_im, wb, wc, ab_re, ab_im, d_skip, w_glu, g_out)


def _ssm_chain_body(fre_ref, fim_ref, abre_ref, abim_ref, ire_ref, iim_ref, *, n_seg, log2_len):
    pr = abre_ref[...]
    pi = abim_ref[...]
    for _ in range(log2_len):
        pr, pi = pr * pr - pi * pi, 2.0 * pr * pi
    cr = jnp.zeros((1, SSM_LANES), F32)
    ci = jnp.zeros((1, SSM_LANES), F32)
    for j in range(n_seg):
        ire_ref[j:j + 1, :] = cr
        iim_ref[j:j + 1, :] = ci
        fr = fre_ref[j:j + 1, :]
        fi = fim_ref[j:j + 1, :]
        cr, ci = fr + pr * cr - pi * ci, fi + pr * ci + pi * cr


def _ssm_chain(fre, fim, ab_re, ab_im, seg_len):
    n_seg = fre.shape[0]
    log2_len = int(math.log2(seg_len))
    assert 2 ** log2_len == seg_len
    st = jax.ShapeDtypeStruct((n_seg, SSM_LANES), F32)
    return pl.pallas_call(
        functools.partial(_ssm_chain_body, n_seg=n_seg, log2_len=log2_len),
        out_shape=[st, st],
        compiler_params=pltpu.CompilerParams(vmem_limit_bytes=VMEM_LIMIT),
        name="ssm_chain",
    )(fre, fim, ab_re, ab_im)


def _outproj_body(a_ref, s_ref, x_ref, w_ref, h_ref):
    h_ref[...] = x_ref[...] + _dot(a_ref[...], w_ref[0:ATTN_W, :]) + _dot(s_ref[...], w_ref[ATTN_W:D_MODEL, :])


def _outproj(a_n, s_n, x, w, tm, n_seg=1):
    rows = x.shape[0]
    return pl.pallas_call(
        _outproj_body,
        grid=(rows // tm,),
        in_specs=[pl.BlockSpec((tm, ATTN_W), lambda i: (i, 0)), _seg_spec(tm, rows, n_seg),
                  pl.BlockSpec((tm, D_MODEL), lambda i: (i, 0)), pl.BlockSpec(w.shape, lambda i: (0, 0))],
        out_specs=pl.BlockSpec((tm, D_MODEL), lambda i: (i, 0)),
        out_shape=jax.ShapeDtypeStruct((rows, D_MODEL), F32),
        compiler_params=_cparams(("arbitrary",)),
        name="outproj",
    )(a_n, s_n, x, w)


def _ffn_body(h_ref, gffn_ref, wa_ref, wg_ref, wd_ref, cw_ref, cb_ref, p2_ref, p1_ref,
              y_ref, cnew_ref, hn_s, acc_s, car_s, *, seq, tm):
    r = pl.program_id(0)
    j = pl.program_id(1)
    nj = pl.num_programs(1)

    @pl.when(j == 0)
    def _norm():
        hn_s[...] = _rms(h_ref[...], gffn_ref[...]).astype(BF16)
        acc_s[...] = jnp.zeros_like(acc_s)

    hn = hn_s[...]
    a = _dot(hn, wa_ref[...])
    g = _dot(hn, wg_ref[...])
    if seq:
        @pl.when(r == 0)
        def _first():
            car_s[j, 6:7, :] = p2_ref[...]
            car_s[j, 7:8, :] = p1_ref[...]

        prev = car_s[j]
        row = lax.broadcasted_iota(I32, a.shape, 0)
        a1 = jnp.where(row == 0, prev[7:8, :], pltpu.roll(a, 1, 0))
        a2 = jnp.where(row == 0, prev[6:7, :], jnp.where(row == 1, prev[7:8, :], pltpu.roll(a, 2, 0)))
        car_s[j] = a[tm - 8:tm, :]
        tf = a.shape[1]
        cnew_ref[:, pl.ds(pl.multiple_of(j * tf, tf), tf)] = a[tm - 8:tm, :]
    else:
        a1 = p1_ref[...]
        a2 = p2_ref[...]
        cnew_ref[...] = a
    c = cb_ref[...] + cw_ref[0:1, :] * a2 + cw_ref[1:2, :] * a1 + cw_ref[2:3, :] * a
    acc_s[...] += _dot((_gelu(c) * g).astype(BF16), wd_ref[...])

    @pl.when(j == nj - 1)
    def _out():
        y_ref[...] = h_ref[...] + acc_s[...]


def _ffn(h, g_ffn, wa, wg, wd, cw, cb, p2, p1, tm, tf, seq):
    rows = h.shape[0]
    nj = D_FF_PAD // tf
    body = functools.partial(_ffn_body, seq=seq, tm=tm)
    if seq:
        tap_spec = pl.BlockSpec((1, tf), lambda r, j: (0, j))
        cnew_spec = pl.BlockSpec((8, D_FF_PAD), lambda r, j: (0, 0))
        cnew_shape = jax.ShapeDtypeStruct((8, D_FF_PAD), F32)
    else:
        tap_spec = pl.BlockSpec((tm, tf), lambda r, j: (r, j))
        cnew_spec = pl.BlockSpec((tm, tf), lambda r, j: (r, j))
        cnew_shape = jax.ShapeDtypeStruct((rows, D_FF_PAD), F32)
    return pl.pallas_call(
        body,
        grid=(rows // tm, nj),
        in_specs=[pl.BlockSpec((tm, D_MODEL), lambda r, j: (r, 0)), pl.BlockSpec((1, D_MODEL), lambda r, j: (0, 0)),
                  pl.BlockSpec((D_MODEL, tf), lambda r, j: (0, j)), pl.BlockSpec((D_MODEL, tf), lambda r, j: (0, j)),
                  pl.BlockSpec((tf, D_MODEL), lambda r, j: (j, 0)), pl.BlockSpec((CONV_W, tf), lambda r, j: (0, j)),
                  pl.BlockSpec((1, tf), lambda r, j: (0, j)), tap_spec, tap_spec],
        out_specs=[pl.BlockSpec((tm, D_MODEL), lambda r, j: (r, 0)), cnew_spec],
        out_shape=[jax.ShapeDtypeStruct((rows, D_MODEL), F32), cnew_shape],
        scratch_shapes=[pltpu.VMEM((tm, D_MODEL), BF16), pltpu.VMEM((tm, D_MODEL), F32), pltpu.VMEM((nj, 8, tf), F32)],
        compiler_params=_cparams(("arbitrary", "arbitrary")),
        name="ffn_seq" if seq else "ffn_rows",
    )(h, g_ffn, wa, wg, wd, cw, cb, p2, p1)


def _s1_copies(pt_ref, cache_ref, x_s, sem, b, slot, n_pages):
    cps = []
    for pg in range(n_pages):
        page = pt_ref[b * n_pages + pg]
        for s in range(2 * N_KV):
            cps.append(pltpu.make_async_copy(cache_ref.at[page, :, s, :],
                                             x_s.at[slot, s, pl.ds(pg * PAGE, PAGE), :], sem.at[slot]))
    return cps


def _s1_body(pt_ref, cache_ref, q_ref, wcat_ref, cpos_ref, w2_ref, gkc_ref, rb_ref, bkc_ref, mt_ref,
             oc_ref, idx_ref, val_ref, x_s, bias_s, sem, *, n_pages, past):
    b = pl.program_id(0)
    nb = pl.num_programs(0)
    slot = b % 2

    @pl.when(b == 0)
    def _first():
        for cp in _s1_copies(pt_ref, cache_ref, x_s, sem, 0, 0, n_pages):
            cp.start()
        for hd in range(N_HEADS):
            bias_s[hd:hd + 1, :] = _bias_lookup(bkc_ref[...], rb_ref, hd)

    @pl.when(b + 1 < nb)
    def _next():
        for cp in _s1_copies(pt_ref, cache_ref, x_s, sem, b + 1, 1 - slot, n_pages):
            cp.start()

    for cp in _s1_copies(pt_ref, cache_ref, x_s, sem, b, slot, n_pages):
        cp.wait()

    n_c = past // CMP_STRIDE
    cio = lax.broadcasted_iota(I32, (N_HEADS, n_c), 1)
    hrow = lax.broadcasted_iota(I32, (N_HEADS, n_c), 0) // Q_PER_KV
    q = q_ref[0]
    s_all = jnp.zeros((N_HEADS, n_c), F32)
    vcs = []
    for h in range(N_KV):
        kc = _rms(_compress_rows(x_s.at[slot, h], wcat_ref.at[0], cpos_ref[0, 0:1, :], w2_ref[0]), gkc_ref[...])
        vcs.append(_compress_rows(x_s.at[slot, N_KV + h], wcat_ref.at[1], cpos_ref[1, 0:1, :], w2_ref[1]).astype(BF16))
        s_all = jnp.where(hrow == h, _dot_nt(q, kc.astype(BF16)), s_all)
    ok = cio < n_c - 1
    s_all = jnp.where(ok, s_all + bias_s[...], NEG)
    m = jnp.max(s_all, axis=-1, keepdims=True)
    p = jnp.where(ok, jnp.exp(s_all - m), 0.0)
    p = p / jnp.maximum(jnp.sum(p, axis=-1, keepdims=True), 1e-30)
    pb = p.astype(BF16)
    hrow_o = lax.broadcasted_iota(I32, (N_HEADS, HEAD_DIM), 0) // Q_PER_KV
    o_c = jnp.zeros((N_HEADS, HEAD_DIM), F32)
    for h in range(N_KV):
        o_c = jnp.where(hrow_o == h, _dot(pb, vcs[h]), o_c)
    oc_ref[0] = o_c
    rio = lax.broadcasted_iota(I32, (8, n_c), 0)
    imp = jnp.zeros((8, n_c), F32)
    for h in range(N_KV):
        ih = p[4 * h:4 * h + 1] + p[4 * h + 1:4 * h + 2] + p[4 * h + 2:4 * h + 3] + p[4 * h + 3:4 * h + 4]
        imp = jnp.where(rio == h, ih, imp)
    imp = jnp.concatenate([imp, jnp.zeros((LANE - 8, n_c), F32)], axis=0)
    n_j = mt_ref.shape[0]
    imp_t = jnp.zeros((n_j, LANE), F32)
    for part in _split3(imp):
        imp_t = imp_t + _dot_nt(mt_ref[...], part)
    j_io = lax.broadcasted_iota(I32, (n_j, LANE), 0)
    cur = past // SEL_BLOCK
    forced = (j_io == 0) | (j_io == cur) | (j_io == cur - 1)
    score_t = jnp.where(forced, 1e9, jnp.where(j_io <= cur, imp_t, -1.0))
    score_t = jnp.where(j_io <= cur, score_t, -jnp.inf)

    def on_pick(r, idx, okv):
        idx_ref[0, r:r + 1, :] = idx
        val_ref[0, r:r + 1, :] = jnp.where(okv, 1, 0)

    _select_rounds(score_t, on_pick)


def _s1(page_table, cache4, q3, wcat, cpos, w2, g_kc, rel_bias):
    n_b, n_pages = page_table.shape
    past = n_pages * PAGE
    n_c = past // CMP_STRIDE
    ns = past // SEL_BLOCK + 1
    n_j = -(-ns // 8) * 8
    c = np.arange(n_c)[None, :]
    bkc = _bucket_np(past - (CMP_STRIDE * c + CMP_LEN - 1))
    j = np.arange(n_j)[:, None]
    mt = (((c // CMP_PER_SEL) == j) | (c == CMP_PER_SEL * j - 1)) & (c < n_c - 1)
    mt = jnp.asarray(mt.astype(np.float32), dtype=BF16)
    full = lambda a: pl.BlockSpec(a.shape, lambda b, pt: (0,) * a.ndim)
    body = functools.partial(_s1_body, n_pages=n_pages, past=past)
    return pl.pallas_call(
        body,
        grid_spec=pltpu.PrefetchScalarGridSpec(
            num_scalar_prefetch=1,
            grid=(n_b,),
            in_specs=[pl.BlockSpec(memory_space=pl.ANY), pl.BlockSpec((1, N_HEADS, HEAD_DIM), lambda b, pt: (b, 0, 0)),
                      full(wcat), full(cpos), full(w2), full(g_kc), pl.BlockSpec(memory_space=pltpu.SMEM),
                      pl.BlockSpec((1, n_c), lambda b, pt: (0, 0)), full(mt)],
            out_specs=[pl.BlockSpec((1, N_HEADS, HEAD_DIM), lambda b, pt: (b, 0, 0)),
                       pl.BlockSpec((1, N_SEL, LANE), lambda b, pt: (b, 0, 0)),
                       pl.BlockSpec((1, N_SEL, LANE), lambda b, pt: (b, 0, 0))],
            scratch_shapes=[pltpu.VMEM((2, 2 * N_KV, past, HEAD_DIM), F32), pltpu.VMEM((N_HEADS, n_c), F32),
                            pltpu.SemaphoreType.DMA((2,))],
        ),
        out_shape=[jax.ShapeDtypeStruct((n_b, N_HEADS, HEAD_DIM), F32), jax.ShapeDtypeStruct((n_b, N_SEL, LANE), I32),
                   jax.ShapeDtypeStruct((n_b, N_SEL, LANE), I32)],
        compiler_params=_cparams(("arbitrary",)),
        name="sample_cmp",
    )(page_table.reshape(-1), cache4, q3, wcat, cpos, w2, g_kc, rel_bias, jnp.asarray(bkc), mt)


def _s2_copies(idx_ref, pt_ref, cache_ref, cwin_ref, ks_s, vs_s, kw_s, vw_s, sem, b, slot, n_pages):
    cps = []
    n_blk = n_pages * (PAGE // SEL_BLOCK)
    for h in range(N_KV):
        for r in range(N_SEL):
            jb = jnp.minimum(idx_ref[(b * N_SEL + r) * N_KV + h], n_blk - 1)
            page = pt_ref[b * n_pages + jb // 2]
            row0 = pl.multiple_of((jb % 2) * SEL_BLOCK, SEL_BLOCK)
            cps.append(pltpu.make_async_copy(cache_ref.at[page, pl.ds(row0, SEL_BLOCK), 2 * N_KV + h, :],
                                             ks_s.at[slot, h, pl.ds(r * SEL_BLOCK, SEL_BLOCK), :], sem.at[slot]))
            cps.append(pltpu.make_async_copy(cache_ref.at[page, pl.ds(row0, SEL_BLOCK), 3 * N_KV + h, :],
                                             vs_s.at[slot, h, pl.ds(r * SEL_BLOCK, SEL_BLOCK), :], sem.at[slot]))
        cps.append(pltpu.make_async_copy(cwin_ref.at[b, :, h, :], kw_s.at[slot, h], sem.at[slot]))
        cps.append(pltpu.make_async_copy(cwin_ref.at[b, :, N_KV + h, :], vw_s.at[slot, h], sem.at[slot]))
    return cps


def _s2_body(idx_ref, val_ref, pt_ref, cache_ref, cwin_ref, q_ref, oc_ref, gt_ref, kns_ref, vns_ref, knw_ref, vnw_ref,
             rb_ref, bks_ref, bkw_ref, gout_ref, o_ref, ks_s, vs_s, kw_s, vw_s, bs_s, bw_s, sem, *, n_pages, past):
    b = pl.program_id(0)
    nb = pl.num_programs(0)
    slot = b % 2
    args = (idx_ref, pt_ref, cache_ref, cwin_ref, ks_s, vs_s, kw_s, vw_s, sem)

    @pl.when(b == 0)
    def _first():
        for cp in _s2_copies(*args, 0, 0, n_pages):
            cp.start()
        for hd in range(N_HEADS):
            bs_s[hd:hd + 1, :] = _bias_lookup(bks_ref[...], rb_ref, hd)
            bw_s[hd:hd + 1, :] = _bias_lookup(bkw_ref[...], rb_ref, hd)

    @pl.when(b + 1 < nb)
    def _next():
        for cp in _s2_copies(*args, b + 1, 1 - slot, n_pages):
            cp.start()

    for cp in _s2_copies(*args, b, slot, n_pages):
        cp.wait()

    n_blk = n_pages * (PAGE // SEL_BLOCK)
    q = q_ref[0]
    qf = q.astype(F32)
    hrow = lax.broadcasted_iota(I32, (N_HEADS, 1), 0) // Q_PER_KV
    lane = lax.broadcasted_iota(I32, (N_HEADS, LANE), 1)
    bias0 = jnp.concatenate([jnp.full((1, 1), rb_ref[0, hd], F32) for hd in range(N_HEADS)], axis=0)
    b31 = jnp.concatenate([jnp.full((1, 1), rb_ref[REL_BUCKETS - 1, hd], F32) for hd in range(N_HEADS)], axis=0)

    tiles = []
    new_sel = jnp.zeros((N_HEADS, 1), F32)
    for t in range(N_SEL // 2):
        s_t = jnp.zeros((N_HEADS, LANE), F32)
        for h in range(N_KV):
            s_h = _dot_nt(q, ks_s[slot, h, pl.ds(t * LANE, LANE), :].astype(BF16))
            halves = []
            for half in range(2):
                r = 2 * t + half
                jb = idx_ref[(b * N_SEL + r) * N_KV + h]
                okr = (val_ref[(b * N_SEL + r) * N_KV + h] > 0) & (jb < n_blk)
                near = bs_s[:, (half * 2) * LANE:(half * 2 + 1) * LANE]
                nearer = bs_s[:, (half * 2 + 1) * LANE:(half * 2 + 2) * LANE]
                bias = jnp.where(jb == n_blk - 1, nearer, jnp.where(jb == n_blk - 2, near, b31))
                halves.append(jnp.where(okr, s_h + bias, NEG))
                new_sel = jnp.where((hrow == h) & (val_ref[(b * N_SEL + r) * N_KV + h] > 0) & (jb == n_blk), 1.0, new_sel)
            s_h = jnp.where(lane < SEL_BLOCK, halves[0], halves[1])
            s_t = jnp.where(hrow == h, s_h, s_t)
        tiles.append(s_t)
    s_new = jnp.sum(qf * kns_ref[0], axis=-1, keepdims=True) + bias0
    s_new = jnp.where(new_sel > 0.5, s_new, NEG)
    m = s_new
    for s_t in tiles:
        m = jnp.maximum(m, jnp.max(s_t, axis=-1, keepdims=True))
    p_new = jnp.where(new_sel > 0.5, jnp.exp(s_new - m), 0.0)
    l = p_new
    acc = p_new * vns_ref[0]
    for t, s_t in enumerate(tiles):
        p = jnp.where(s_t > 0.5 * NEG, jnp.exp(s_t - m), 0.0)
        l = l + jnp.sum(p, axis=-1, keepdims=True)
        pb = p.astype(BF16)
        for h in range(N_KV):
            pv = _dot(pb, vs_s[slot, h, pl.ds(t * LANE, LANE), :].astype(BF16))
            acc = acc + jnp.where(hrow == h, pv, 0.0)
    o_s = acc / jnp.maximum(l, 1e-30)

    n_w = kw_s.shape[2]
    wt = []
    for t in range(n_w // LANE):
        s_t = jnp.zeros((N_HEADS, LANE), F32)
        for h in range(N_KV):
            s_h = _dot_nt(q, kw_s[slot, h, pl.ds(t * LANE, LANE), :].astype(BF16))
            s_t = jnp.where(hrow == h, s_h, s_t)
        wt.append(s_t + bw_s[:, t * LANE:(t + 1) * LANE])
    s_new = jnp.sum(qf * knw_ref[0], axis=-1, keepdims=True) + bias0
    m = s_new
    for s_t in wt:
        m = jnp.maximum(m, jnp.max(s_t, axis=-1, keepdims=True))
    p_new = jnp.exp(s_new - m)
    l = p_new
    acc = p_new * vnw_ref[0]
    for t, s_t in enumerate(wt):
        p = jnp.exp(s_t - m)
        l = l + jnp.sum(p, axis=-1, keepdims=True)
        pb = p.astype(BF16)
        for h in range(N_KV):
            pv = _dot(pb, vw_s[slot, h, pl.ds(t * LANE, LANE), :].astype(BF16))
            acc = acc + jnp.where(hrow == h, pv, 0.0)
    o_w = acc / l

    gt = gt_ref[0]
    a = gt[:, 0:1] * oc_ref[0] + gt[:, 1:2] * o_s + gt[:, 2:3] * o_w
    ms = jnp.sum(jnp.sum(a * a, axis=-1, keepdims=True), axis=0, keepdims=True) / (N_HEADS * HEAD_DIM)
    o_ref[0] = (a * lax.rsqrt(ms + EPS) * gout_ref[...]).astype(BF16)


def _s2(idx, val, page_table, cache4, cwin4, q3, o_c, gates3, kns, vns, knw, vnw, rel_bias, g_out3):
    n_b, n_pages = page_table.shape
    past = n_pages * PAGE
    n_w = cwin4.shape[1]
    s = np.arange(SEL_BLOCK)
    d_near = past - ((past // SEL_BLOCK - 2) * SEL_BLOCK + s)
    d_nearer = past - ((past // SEL_BLOCK - 1) * SEL_BLOCK + s)
    z = np.zeros(SEL_BLOCK, np.int64)
    bks = np.concatenate([d_near, z, d_nearer, z, z, d_near, z, d_nearer])[None, :]
    bkw = (past - (past - n_w + np.arange(n_w)))[None, :]
    full = lambda a: pl.BlockSpec(a.shape, lambda b, *_: (0,) * a.ndim)
    per_b = lambda a: pl.BlockSpec((1,) + a.shape[1:], lambda b, *_: (b,) + (0,) * (a.ndim - 1))
    body = functools.partial(_s2_body, n_pages=n_pages, past=past)
    return pl.pallas_call(
        body,
        grid_spec=pltpu.PrefetchScalarGridSpec(
            num_scalar_prefetch=3,
            grid=(n_b,),
            in_specs=[pl.BlockSpec(memory_space=pl.ANY), pl.BlockSpec(memory_space=pl.ANY),
                      per_b(q3), per_b(o_c), per_b(gates3), per_b(kns), per_b(vns), per_b(knw), per_b(vnw),
                      pl.BlockSpec(memory_space=pltpu.SMEM), pl.BlockSpec((1, 4 * LANE), lambda b, *_: (0, 0)),
                      pl.BlockSpec((1, n_w), lambda b, *_: (0, 0)), full(g_out3)],
            out_specs=pl.BlockSpec((1, N_HEADS, HEAD_DIM), lambda b, *_: (b, 0, 0)),
            scratch_shapes=[pltpu.VMEM((2, N_KV, N_SEL * SEL_BLOCK, HEAD_DIM), F32),
                            pltpu.VMEM((2, N_KV, N_SEL * SEL_BLOCK, HEAD_DIM), F32),
                            pltpu.VMEM((2, N_KV, n_w, HEAD_DIM), F32), pltpu.VMEM((2, N_KV, n_w, HEAD_DIM), F32),
                            pltpu.VMEM((N_HEADS, 4 * LANE), F32), pltpu.VMEM((N_HEADS, n_w), F32),
                            pltpu.SemaphoreType.DMA((2,))],
        ),
        out_shape=jax.ShapeDtypeStruct((n_b, N_HEADS, HEAD_DIM), BF16),
        compiler_params=_cparams(("arbitrary",)),
        name="sample_attn",
    )(idx, val, page_table.reshape(-1), cache4, cwin4, q3, o_c, gates3, kns, vns, knw, vnw, rel_bias,
      jnp.asarray(_bucket_np(bks)), jnp.asarray(_bucket_np(bkw)), g_out3)


def _winshift_body(cwin_ref, new_ref, o_ref):
    n_w = cwin_ref.shape[1]
    o_ref[0, 0:n_w - 1] = cwin_ref[0, 1:n_w]
    o_ref[0, n_w - 1] = new_ref[0]


def _winshift(cwin4, new3):
    n_b = cwin4.shape[0]
    return pl.pallas_call(
        _winshift_body,
        grid=(n_b,),
        in_specs=[pl.BlockSpec((1,) + cwin4.shape[1:], lambda b: (b, 0, 0, 0)),
                  pl.BlockSpec((1,) + new3.shape[1:], lambda b: (b, 0, 0))],
        out_specs=pl.BlockSpec((1,) + cwin4.shape[1:], lambda b: (b, 0, 0, 0)),
        out_shape=jax.ShapeDtypeStruct(cwin4.shape, F32),
        compiler_params=_cparams(("arbitrary",)),
        name="winshift",
    )(cwin4, new3)


def _block_diag_b(bb_re, bb_im):
    gl = SSM_G // SSM_SG
    eye = jnp.eye(gl, dtype=F32)

    def one(bb):
        t = jnp.transpose(bb, (1, 0, 2)).reshape(SSM_SG, gl, SSM_P, SSM_N)
        return jnp.einsum('sgpn,gh->sgphn', t, eye).reshape(SSM_SG, gl * SSM_P, gl * SSM_N)

    return jnp.concatenate([one(bb_re), one(bb_im)], axis=2).astype(BF16)


def _block_diag_c(c_re, c_im):
    gl = SSM_G // SSM_SG
    eye = jnp.eye(gl, dtype=F32)

    def one(c):
        t = jnp.transpose(c, (0, 2, 1)).reshape(SSM_SG, gl, SSM_N, SSM_P)
        return jnp.einsum('sgnp,gh->sgnhp', t, eye).reshape(SSM_SG, gl * SSM_N, gl * SSM_P)

    return jnp.concatenate([one(c_re), -one(c_im)], axis=1).astype(BF16)


def kernel(x_prompt, x_sample, cache_kv, page_table, cache_win, state_ssm, state_conv, rel_bias, g_mix, w_in, g_q, g_k, w_cmp1, pos_cmp, w_cmp2, lam_re, lam_im, log_dt, b_re, b_im, c_re, c_im, d_skip, w_glu, g_out_attn, g_out_ssm, w_out, g_ffn, w_up, conv_w, conv_b, w_down):
    depth = g_mix.shape[0]
    assert depth == 1 and x_prompt.shape[0] == 1 and x_sample.shape[1] == 1
    seq = x_prompt.shape[1]
    n_b = x_sample.shape[0]
    n_pages = page_table.shape[1]
    n_w = cache_win.shape[2]
    li = 0
    row = lambda v: v.reshape(1, -1)

    wi = w_in[li]
    c_g = ATTN_W + 6 * KV_W
    w_in_b = jnp.concatenate([wi[:, :c_g], wi[:, c_g + N_BRANCH * N_HEADS:], wi[:, c_g:c_g + N_BRANCH * N_HEADS],
                              jnp.zeros((D_MODEL, LANE - N_BRANCH * N_HEADS), F32)], axis=1).astype(BF16)
    w1 = w_cmp1[li]
    wcat = jnp.concatenate([w1[:, :CMP_STRIDE], w1[:, CMP_STRIDE:]], axis=-1).astype(BF16)
    w2 = w_cmp2[li].astype(BF16)
    w_glu_b = w_glu[li].astype(BF16)
    w_out_b = w_out[li].astype(BF16)
    padc = lambda a: jnp.pad(a, ((0, 0), (0, D_FF_PAD - D_FF)))
    wa = padc(w_up[li][:, :D_FF]).astype(BF16)
    wg = padc(w_up[li][:, D_FF:]).astype(BF16)
    wd = jnp.pad(w_down[li], ((0, D_FF_PAD - D_FF), (0, 0))).astype(BF16)
    cw = padc(conv_w[li])
    cb = padc(row(conv_b[li]))

    ab_re, ab_im, bb_re, bb_im, cpos = _prep(lam_re[li], lam_im[li], log_dt[li], b_re[li], b_im[li], pos_cmp[li], w1)
    wb = _block_diag_b(bb_re, bb_im)
    wc = _block_diag_c(c_re[li], c_im[li])
    ab_re = row(ab_re)
    ab_im = row(ab_im)

    in_args = (row(g_mix[li]), w_in_b, row(g_q[li]), row(g_k[li, 1]), row(g_k[li, 2]))
    ssm_args = (wb, wc, ab_re, ab_im, row(d_skip[li]), w_glu_b, row(g_out_ssm[li]))

    xp = x_prompt[0]
    n_seg = 8
    q, kv, win, u2, gates, kvb = _inproj(xp, *in_args, tm=512, q_scale=HEAD_DIM ** -0.5 * LOG2E, n_seg=n_seg)
    cmp_out = _pcompress(kv, wcat, cpos, w2, row(g_k[li, 0]))
    zpad = jnp.zeros((CMP_STRIDE, 2 * KV_W), F32)
    cmp_pad = jnp.concatenate([zpad, cmp_out, zpad], axis=0)
    attn_n = _pattn(q, gates, kvb, cmp_pad[:, :KV_W], cmp_pad[:, KV_W:], rel_bias, row(g_out_attn[li]))
    zst = jnp.zeros((n_seg, SSM_LANES), F32)
    _, fre, fim = _ssm(u2, n_seg, zst, zst, *ssm_args, tc=64, emit=False)
    ire, iim = _ssm_chain(fre, fim, ab_re, ab_im, seq // n_seg)
    ssm_n, hre, him = _ssm(u2, n_seg, ire, iim, *ssm_args, tc=64, emit=True)
    h_p = _outproj(attn_n, ssm_n, xp, w_out_b, tm=512, n_seg=n_seg)
    zrow = jnp.zeros((1, D_FF_PAD), F32)
    y_p, cnew_p = _ffn(h_p, row(g_ffn[li]), wa, wg, wd, cw, cb, zrow, zrow, tm=512, tf=512, seq=True)

    y_prompt = y_p[None]
    kv_prompt = kv.reshape(1, 1, seq, 4, N_KV, HEAD_DIM)
    win_prompt = win[seq - min(WINDOW, seq):].reshape(1, 1, min(WINDOW, seq), 2, N_KV, HEAD_DIM)
    ssm_prompt = jnp.stack([hre[n_seg - 1], him[n_seg - 1]], axis=-1).reshape(1, 1, SSM_G, SSM_N, 2)
    conv_prompt = cnew_p[6:8, :D_FF].reshape(1, 1, CONV_W - 1, D_FF)

    xs = x_sample[:, 0]
    q_s, kv_s, win_s, u_s, gates_s, _ = _inproj(xs, *in_args, tm=n_b, q_scale=HEAD_DIM ** -0.5)
    cache4 = cache_kv[li].reshape(cache_kv.shape[1], PAGE, 4 * N_KV, HEAD_DIM)
    cwin4 = cache_win[li].reshape(n_b, n_w, 2 * N_KV, HEAD_DIM)
    q3 = q_s.reshape(n_b, N_HEADS, HEAD_DIM)
    o_c, idx, val = _s1(page_table, cache4, q3, wcat, cpos, w2, row(g_k[li, 0]), rel_bias)
    idx = idx[:, :, :N_KV].reshape(-1)
    val = val[:, :, :N_KV].reshape(-1)
    rep = lambda a: jnp.repeat(a.reshape(n_b, N_KV, HEAD_DIM), Q_PER_KV, axis=1)
    kns = rep(kv_s[:, 2 * KV_W:3 * KV_W])
    vns = rep(kv_s[:, 3 * KV_W:4 * KV_W])
    knw = rep(win_s[:, :KV_W])
    vnw = rep(win_s[:, KV_W:])
    g3 = jnp.transpose(gates_s[:, :N_BRANCH * N_HEADS].reshape(n_b, N_BRANCH, N_HEADS), (0, 2, 1))
    g3 = jnp.pad(g3, ((0, 0), (0, 0), (0, LANE - N_BRANCH)))
    attn_s = _s2(idx, val, page_table, cache4, cwin4, q3, o_c, g3, kns, vns, knw, vnw, rel_bias,
                 g_out_attn[li].reshape(N_HEADS, HEAD_DIM))
    st = state_ssm[li].reshape(n_b, SSM_LANES, 2)
    ssm_s, sre, sim = _ssm(u_s, n_b, st[:, :, 0], st[:, :, 1], *ssm_args, tc=1, emit=True)
    h_s = _outproj(attn_s.reshape(n_b, ATTN_W), ssm_s, xs, w_out_b, tm=n_b)
    sc = state_conv[li]
    y_s, a_s = _ffn(h_s, row(g_ffn[li]), wa, wg, wd, cw, cb, padc(sc[:, 0]), padc(sc[:, 1]), tm=n_b, tf=512, seq=False)
    win_sample = _winshift(cwin4, win_s.reshape(n_b, 2 * N_KV, HEAD_DIM))

    y_sample = y_s[:, None]
    kv_sample = kv_s.reshape(1, n_b, 1, 4, N_KV, HEAD_DIM)
    win_sample = win_sample.reshape(1, n_b, n_w, 2, N_KV, HEAD_DIM)
    ssm_sample = jnp.stack([sre, sim], axis=-1).reshape(1, n_b, SSM_G, SSM_N, 2)
    conv_sample = jnp.stack([sc[:, 1], a_s[:, :D_FF]], axis=1)[None]
    return (y_prompt, y_sample, kv_prompt, kv_sample, win_prompt, win_sample,
            ssm_prompt, ssm_sample, conv_prompt, conv_sample)
```

```python
import functools
import math

import numpy as np
import jax
import jax.numpy as jnp
from jax import lax
from jax.experimental import pallas as pl
from jax.experimental.pallas import tpu as pltpu

F32 = jnp.float32
BF16 = jnp.bfloat16
I32 = jnp.int32

D_MODEL = 2048
HEAD_DIM = 128
N_HEADS = 8
N_KV = 2
Q_PER_KV = 4
ATTN_W = 1024
KV_W = 256
N_BRANCH = 3
CMP_LEN = 32
CMP_STRIDE = 16
SEL_BLOCK = 64
CMP_PER_SEL = 4
N_SEL = 16
WINDOW = 512
REL_BUCKETS = 32
REL_MAX_DIST = 128
PAGE = 128
SSM_W = 1024
SSM_G = 64
SSM_N = 64
SSM_P = 16
SSM_SG = 8
SSM_LANES = SSM_G * SSM_N
D_FF = 5504
D_FF_PAD = 5632
CONV_W = 3
EPS = 1e-6
NEG = -1e30
QB = 128
LANE = 128
VMEM_LIMIT = 56 * 1024 * 1024


def _cparams(sem):
    return pltpu.CompilerParams(dimension_semantics=sem, vmem_limit_bytes=VMEM_LIMIT)


def _rms(x, g):
    return x * lax.rsqrt(jnp.mean(x * x, axis=-1, keepdims=True) + EPS) * g


def _gelu(x):
    return jax.nn.gelu(x)


def _dot(a, b):
    return jnp.dot(a, b, preferred_element_type=F32)


def _dot_nt(a, b):
    return lax.dot_general(a, b, (((1,), (1,)), ((), ())), preferred_element_type=F32)


def _split3(x):
    hi = x.astype(BF16)
    r1 = x - hi.astype(F32)
    mid = r1.astype(BF16)
    lo = (r1 - mid.astype(F32)).astype(BF16)
    return hi, mid, lo


def _bucket_np(d):
    n = np.maximum(d, 0)
    exact = REL_BUCKETS // 2
    nf = np.maximum(n, 1).astype(np.float32)
    large = exact + (np.log(nf / np.float32(exact)) / np.float32(math.log(REL_MAX_DIST / exact))
                     * np.float32(REL_BUCKETS - exact)).astype(np.int32)
    return np.where(n < exact, n, np.minimum(large, REL_BUCKETS - 1)).astype(np.int32)


def _bias_lookup(bkt, rb_ref, head, shift=None):
    last = rb_ref[REL_BUCKETS - 1, head]
    acc = jnp.full(bkt.shape, last, F32)
    for b in range(REL_BUCKETS - 1):
        acc = jnp.where(bkt == b, rb_ref[b, head], acc)
    if shift:
        acc = acc - last
    return acc


def _prep_body(lre_ref, lim_ref, ldt_ref, bre_ref, bim_ref, pos_ref, w1_ref,
               abre_ref, abim_ref, bbre_ref, bbim_ref, cpos_ref):
    lr = lre_ref[...]
    li = lim_ref[...]
    dt = jnp.exp(ldt_ref[...])
    mag = jnp.exp(lr * dt)
    ab_re = mag * jnp.cos(li * dt)
    ab_im = mag * jnp.sin(li * dt)
    den = lr * lr + li * li
    nr = ab_re - 1.0
    ni = ab_im
    f_re = (nr * lr + ni * li) / den
    f_im = (ni * lr - nr * li) / den
    abre_ref[...] = ab_re
    abim_ref[...] = ab_im
    for p in range(SSM_P):
        br = bre_ref[p]
        bi = bim_ref[p]
        bbre_ref[p] = f_re * br - f_im * bi
        bbim_ref[p] = f_re * bi + f_im * br
    for kind in range(2):
        cpos_ref[kind] = jnp.dot(pos_ref[kind], w1_ref[kind], preferred_element_type=F32,
                                 precision=lax.Precision.HIGHEST)


def _prep(lam_re, lam_im, log_dt, b_re, b_im, pos_cmp, w_cmp1):
    bre_t = jnp.transpose(b_re, (2, 0, 1))
    bim_t = jnp.transpose(b_im, (2, 0, 1))
    pos = jnp.zeros((2, 8, CMP_LEN * HEAD_DIM), F32).at[:, 0, :].set(pos_cmp.reshape(2, CMP_LEN * HEAD_DIM))
    w1 = w_cmp1.reshape(2, CMP_LEN * HEAD_DIM, HEAD_DIM)
    return pl.pallas_call(
        _prep_body,
        out_shape=[jax.ShapeDtypeStruct((SSM_G, SSM_N), F32), jax.ShapeDtypeStruct((SSM_G, SSM_N), F32),
                   jax.ShapeDtypeStruct((SSM_P, SSM_G, SSM_N), F32), jax.ShapeDtypeStruct((SSM_P, SSM_G, SSM_N), F32),
                   jax.ShapeDtypeStruct((2, 8, HEAD_DIM), F32)],
        compiler_params=pltpu.CompilerParams(vmem_limit_bytes=VMEM_LIMIT),
        name="prep",
    )(lam_re, lam_im, log_dt.reshape(SSM_G, 1), bre_t, bim_t, pos, w1)


IN_COLS_PAD = ATTN_W + 4 * KV_W + 2 * KV_W + SSM_W + LANE
KT_KS = (0, 2 * HEAD_DIM)
KT_E = HEAD_DIM
KT_KW = 3 * HEAD_DIM
KT_COLS = 5 * HEAD_DIM
VT_ROWS = 4 * HEAD_DIM


def _inproj_body(x_ref, gmix_ref, w_ref, gq_ref, gks_ref, gkw_ref,
                 q_ref, kv_ref, win_ref, u_ref, gt_ref, *kt_vt, q_scale):
    tm = x_ref.shape[0]
    xn = _rms(x_ref[...], gmix_ref[...]).astype(BF16)
    zq = _dot(xn, w_ref[:, 0:ATTN_W])
    for h in range(N_HEADS):
        sl = slice(h * HEAD_DIM, (h + 1) * HEAD_DIM)
        q_ref[:, sl] = (_rms(zq[:, sl], gq_ref[...]) * q_scale).astype(BF16)
    zkv = _dot(xn, w_ref[:, ATTN_W:ATTN_W + 4 * KV_W])
    kv_ref[:, 0:2 * KV_W] = zkv[:, 0:2 * KV_W]
    ks, kw = [], []
    for h in range(N_KV):
        sl = slice(2 * KV_W + h * HEAD_DIM, 2 * KV_W + (h + 1) * HEAD_DIM)
        ks.append(_rms(zkv[:, sl], gks_ref[...]))
        kv_ref[:, sl] = ks[h]
    kv_ref[:, 3 * KV_W:4 * KV_W] = zkv[:, 3 * KV_W:4 * KV_W]
    c0 = ATTN_W + 4 * KV_W
    zw = _dot(xn, w_ref[:, c0:c0 + 2 * KV_W])
    for h in range(N_KV):
        sl = slice(h * HEAD_DIM, (h + 1) * HEAD_DIM)
        kw.append(_rms(zw[:, sl], gkw_ref[...]))
        win_ref[:, sl] = kw[h]
    win_ref[:, KV_W:2 * KV_W] = zw[:, KV_W:2 * KV_W]
    if kt_vt:
        kt_ref, vt_ref = kt_vt
        for h in range(N_KV):
            kt_ref[:, KT_KS[h]:KT_KS[h] + HEAD_DIM] = ks[h].astype(BF16)
            kt_ref[:, KT_KW + h * HEAD_DIM:KT_KW + (h + 1) * HEAD_DIM] = kw[h].astype(BF16)
        blk = (lax.broadcasted_iota(I32, (tm, HEAD_DIM), 0) + pl.program_id(0) * tm) // SEL_BLOCK
        kt_ref[:, KT_E:KT_E + HEAD_DIM] = jnp.where(blk == lax.broadcasted_iota(I32, (tm, HEAD_DIM), 1), 1.0, 0.0).astype(BF16)
        vt_ref[0:KV_W, :] = zkv[:, 3 * KV_W:4 * KV_W].T.astype(BF16)
        vt_ref[KV_W:2 * KV_W, :] = zw[:, KV_W:2 * KV_W].T.astype(BF16)
    c1 = c0 + 2 * KV_W
    u_ref[...] = _dot(xn, w_ref[:, c1:c1 + SSM_W])
    c2 = c1 + SSM_W
    gt_ref[...] = jax.nn.sigmoid(_dot(xn, w_ref[:, c2:c2 + LANE]))


def _seg_spec(tm, rows, n_seg):
    tiles_per_seg = rows // n_seg // tm
    return pl.BlockSpec((tm, SSM_W), lambda i: (i % tiles_per_seg, i // tiles_per_seg))


def _inproj(x, g_mix, w, g_q, g_ks, g_kw, tm, q_scale, n_seg=1, attn_operands=False):
    rows = x.shape[0]
    row_spec = lambda n: pl.BlockSpec((tm, n), lambda i: (i, 0))
    full = lambda a: pl.BlockSpec(a.shape, lambda i: (0,) * a.ndim)
    out_specs = [row_spec(ATTN_W), row_spec(4 * KV_W), row_spec(2 * KV_W), _seg_spec(tm, rows, n_seg), row_spec(LANE)]
    out_shape = [jax.ShapeDtypeStruct((rows, ATTN_W), BF16), jax.ShapeDtypeStruct((rows, 4 * KV_W), F32),
                 jax.ShapeDtypeStruct((rows, 2 * KV_W), F32), jax.ShapeDtypeStruct((rows // n_seg, n_seg * SSM_W), F32),
                 jax.ShapeDtypeStruct((rows, LANE), F32)]
    if attn_operands:
        out_specs += [row_spec(KT_COLS), pl.BlockSpec((VT_ROWS, tm), lambda i: (0, i))]
        out_shape += [jax.ShapeDtypeStruct((rows, KT_COLS), BF16), jax.ShapeDtypeStruct((VT_ROWS, rows), BF16)]
    return pl.pallas_call(
        functools.partial(_inproj_body, q_scale=q_scale),
        grid=(rows // tm,),
        in_specs=[row_spec(D_MODEL), full(g_mix), full(w), full(g_q), full(g_ks), full(g_kw)],
        out_specs=out_specs,
        out_shape=out_shape,
        compiler_params=_cparams(("arbitrary",)),
        name="inproj",
    )(x, g_mix, w, g_q, g_ks, g_kw)


def _compress_rows(x_ref, wcat_ref, cpos, w2):
    n_ch = x_ref.shape[0] // CMP_STRIDE
    acc = jnp.zeros((n_ch, 2 * HEAD_DIM), F32)
    for p in range(CMP_STRIDE):
        xp = x_ref[pl.ds(p, n_ch, stride=CMP_STRIDE), :].astype(BF16)
        acc = acc + _dot(xp, wcat_ref[p])
    e_lo = acc[:, 0:HEAD_DIM]
    e_hi_next = pltpu.roll(acc[:, HEAD_DIM:2 * HEAD_DIM], n_ch - 1, 0)
    hid = _gelu(e_lo + e_hi_next + cpos)
    return _dot(hid.astype(BF16), w2)


def _pcompress_body(x_ref, wcat_ref, cpos_ref, w2_ref, gkc_ref, o_ref):
    s = pl.program_id(0)
    out = _compress_rows(x_ref, wcat_ref.at[0], cpos_ref[0, 0:1, :], w2_ref[0])
    o_ref[...] = jnp.where(s < N_KV, _rms(out, gkc_ref[...]), out)


def _pcompress(kv, wcat, cpos, w2, g_kc):
    rows = kv.shape[0]
    n_ch = rows // CMP_STRIDE
    return pl.pallas_call(
        _pcompress_body,
        grid=(2 * N_KV,),
        in_specs=[pl.BlockSpec((rows, HEAD_DIM), lambda s: (0, s)),
                  pl.BlockSpec((1, CMP_STRIDE, HEAD_DIM, 2 * HEAD_DIM), lambda s: (s // N_KV, 0, 0, 0)),
                  pl.BlockSpec((1, 8, HEAD_DIM), lambda s: (s // N_KV, 0, 0)),
                  pl.BlockSpec((1, HEAD_DIM, HEAD_DIM), lambda s: (s // N_KV, 0, 0)),
                  pl.BlockSpec((1, HEAD_DIM), lambda s: (0, 0))],
        out_specs=pl.BlockSpec((n_ch, HEAD_DIM), lambda s: (0, s)),
        out_shape=jax.ShapeDtypeStruct((n_ch, 2 * N_KV * HEAD_DIM), F32),
        compiler_params=_cparams(("arbitrary",)),
        name="pcompress",
    )(kv, wcat, cpos, w2, g_kc)


def _select_rounds(score_t, on_pick=None):
    n_j = score_t.shape[0]
    jio = lax.broadcasted_iota(I32, score_t.shape, 0)
    sel = jnp.zeros(score_t.shape, F32)
    sc = score_t
    for r in range(N_SEL):
        m = jnp.max(sc, axis=0, keepdims=True)
        idx = jnp.min(jnp.where(sc == m, jio, n_j), axis=0, keepdims=True)
        pick = jio == idx
        ok = m >= 0.0
        sel = jnp.where(pick & ok, 1.0, sel)
        sc = jnp.where(pick, -jnp.inf, sc)
        if on_pick is not None:
            on_pick(r, idx, ok)
    return sel


LOG2E = 1.4426950408889634


ONES_ROWS = 16
FAR_TILES = 8


def _col_max(tiles):
    m = jnp.max(tiles[0], axis=0, keepdims=True)
    for s in tiles[1:]:
        m = jnp.maximum(m, jnp.max(s, axis=0, keepdims=True))
    return m


def _lanes4(x):
    return jnp.concatenate([x] * Q_PER_KV, axis=1)


def _pattn_t_body(q_ref, gt_ref, kt_ref, vt_ref, kc_ref, vc_ref, vct_ref, rb_ref, bk0_ref, bk1_ref, bkc_ref, mft_ref,
                  gout_ref, o_ref, b0_s, b1_s, bc_s, bw4_s):
    i = pl.program_id(0)
    cols = Q_PER_KV * QB

    @pl.when(i == 0)
    def _tables():
        b_io = lax.broadcasted_iota(I32, (QB, QB), 0)
        a_io = lax.broadcasted_iota(I32, (QB, QB), 1)
        for hd in range(N_HEADS):
            t0 = _bias_lookup(bk0_ref[...], rb_ref, hd, shift=True) * LOG2E
            b0_s[hd] = jnp.where(a_io >= b_io, t0, NEG)
            b1_s[hd] = _bias_lookup(bk1_ref[...], rb_ref, hd, shift=True) * LOG2E
            bc_s[hd] = _bias_lookup(bkc_ref[...], rb_ref, hd, shift=True) * LOG2E
        bw4_s[...] = jnp.where(b_io >= a_io, 0.0, NEG)

    gt_t = gt_ref[...].T
    n_far = jnp.maximum(i - 1, 0) // FAR_TILES
    far_keys = FAR_TILES * QB

    def tab(ref, h):
        return jnp.concatenate([ref[Q_PER_KV * h + g] for g in range(Q_PER_KV)], axis=1)

    def v_aug(row0, start, n):
        return jnp.concatenate([vt_ref[row0:row0 + HEAD_DIM, pl.ds(start, n)], jnp.ones((ONES_ROWS, n), BF16)], axis=0)

    q_ts, qa_ts, o_cs, sel_st = [], [], [], []
    for h in range(N_KV):
        q_t = jnp.concatenate(
            [q_ref[:, (Q_PER_KV * h + g) * HEAD_DIM:(Q_PER_KV * h + g + 1) * HEAD_DIM].astype(F32).T.astype(BF16)
             for g in range(Q_PER_KV)], axis=1)
        hs = slice(h * HEAD_DIM, (h + 1) * HEAD_DIM)

        n_c = kc_ref.shape[0] - 2 * CMP_STRIDE
        near0 = pl.multiple_of(8 * i, 8)
        cf = lax.broadcasted_iota(I32, (n_c, QB), 0)
        ok_f = _lanes4((cf < 8 * i) & (cf >= CMP_STRIDE))
        s_f = jnp.where(ok_f, _dot(kc_ref[0:n_c, hs].astype(BF16), q_t), NEG)
        cn = lax.broadcasted_iota(I32, (32, cols), 0)
        a_n = lax.broadcasted_iota(I32, (32, cols), 1) % QB
        ok_n = (CMP_STRIDE * (cn - CMP_STRIDE) <= a_n - (CMP_LEN - 1)) & (cn + 8 * i >= CMP_STRIDE)
        s_n = jnp.where(ok_n, _dot(kc_ref[pl.ds(near0, 32), hs].astype(BF16), q_t) + tab(bc_s, h), NEG)
        m_c = _col_max([s_f, s_n])
        p_f = jnp.where(ok_f, jnp.exp2(s_f - m_c), 0.0)
        p_n = jnp.where(ok_n, jnp.exp2(s_n - m_c), 0.0)
        l_c = jnp.sum(p_f, axis=0, keepdims=True) + jnp.sum(p_n, axis=0, keepdims=True)
        inv_c = 1.0 / jnp.maximum(l_c, 1e-30)
        vc_near_t = vc_ref[pl.ds(near0, 32), hs].T.astype(BF16)
        o_c = (_dot(vct_ref[hs, 0:n_c].astype(BF16), p_f.astype(BF16)) + _dot(vc_near_t, p_n.astype(BF16))) * inv_c
        pn_f = p_f * inv_c
        pn_n = p_n * inv_c
        imp_f = pn_f[:, 0:QB] + pn_f[:, QB:2 * QB] + pn_f[:, 2 * QB:3 * QB] + pn_f[:, 3 * QB:4 * QB]
        imp_n = pn_n[:, 0:QB] + pn_n[:, QB:2 * QB] + pn_n[:, 2 * QB:3 * QB] + pn_n[:, 3 * QB:4 * QB]
        jn = lax.broadcasted_iota(I32, (QB, 32), 0)
        cc = lax.broadcasted_iota(I32, (QB, 32), 1) + 8 * i - CMP_STRIDE
        mnt = (((cc // CMP_PER_SEL) == jn) | (cc == CMP_PER_SEL * jn - 1)) & (cc >= 0)
        mnt = jnp.where(mnt, 1.0, 0.0).astype(BF16)
        imp_t = jnp.zeros((QB, QB), F32)
        for part in _split3(imp_f):
            imp_t = imp_t + _dot(mft_ref[...], part)
        for part in _split3(imp_n):
            imp_t = imp_t + _dot(mnt, part)
        j_io = lax.broadcasted_iota(I32, (QB, QB), 0)
        cur = 2 * i + lax.broadcasted_iota(I32, (QB, QB), 1) // SEL_BLOCK
        forced = (j_io == 0) | (j_io == cur) | (j_io == cur - 1)
        score_t = jnp.where(forced, 1e9, jnp.where(j_io <= cur, imp_t, -1.0))
        sel_t = _select_rounds(score_t)
        selneg = _lanes4(jnp.where(sel_t > 0.5, 0.0, NEG).astype(BF16))
        qa_t = jnp.concatenate([q_t, selneg] if h == 0 else [selneg, q_t], axis=0)
        ka0 = h * HEAD_DIM

        near_s, near_v = [], []
        for back in range(FAR_TILES + 1):
            kt = i - back
            start = pl.multiple_of(jnp.maximum(kt, 0) * QB, QB)
            s = _dot(kt_ref[pl.ds(start, QB), ka0:ka0 + 2 * HEAD_DIM], qa_t)
            if back == 0:
                s = s + tab(b0_s, h)
            elif back == 1:
                s = s + tab(b1_s, h) + jnp.where(kt >= 0, 0.0, NEG)
            else:
                s = s + jnp.where((kt >= 0) & (kt >= FAR_TILES * n_far), 0.0, NEG)
            near_s.append(s)
            near_v.append(v_aug(h * HEAD_DIM, start, QB))
        m_s = _col_max(near_s)
        acc_s = jnp.zeros((HEAD_DIM + ONES_ROWS, cols), F32)
        for s, v in zip(near_s, near_v):
            acc_s = acc_s + _dot(v, jnp.exp2(s - m_s).astype(BF16))
        q_ts.append(q_t)
        qa_ts.append(qa_t)
        o_cs.append(o_c)
        sel_st += [m_s, acc_s]

    def far_step(k, st):
        start = pl.multiple_of(k * far_keys, far_keys)
        out = []
        for h in range(N_KV):
            m, acc = st[2 * h], st[2 * h + 1]
            s = _dot(kt_ref[pl.ds(start, far_keys), h * HEAD_DIM:(h + 2) * HEAD_DIM], qa_ts[h])
            m_new = jnp.maximum(m, jnp.max(s, axis=0, keepdims=True))
            p = jnp.exp2(s - m_new).astype(BF16)
            out += [m_new, jnp.exp2(m - m_new) * acc + _dot(v_aug(h * HEAD_DIM, start, far_keys), p)]
        return tuple(out)

    sel_st = lax.fori_loop(0, n_far, far_step, tuple(sel_st))

    attn = [None] * N_HEADS
    for h in range(N_KV):
        q_t = q_ts[h]
        acc_s = sel_st[2 * h + 1]
        o_s = acc_s[0:HEAD_DIM] / acc_s[HEAD_DIM:HEAD_DIM + 1]
        o_c = o_cs[h]

        kw0 = KT_KW + h * HEAD_DIM
        win_s, win_v = [], []
        for back in range(5):
            kt = i - back
            start = pl.multiple_of(jnp.maximum(kt, 0) * QB, QB)
            s = _dot(kt_ref[pl.ds(start, QB), kw0:kw0 + HEAD_DIM], q_t)
            if back == 0:
                s = s + tab(b0_s, h)
            elif back == 1:
                s = s + tab(b1_s, h)
            elif back == 4:
                s = s + _lanes4(bw4_s[...])
            if back > 0:
                s = s + jnp.where(kt >= 0, 0.0, NEG)
            win_s.append(s)
            win_v.append(v_aug((N_KV + h) * HEAD_DIM, start, QB))
        m_w = _col_max(win_s)
        acc_w = jnp.zeros((HEAD_DIM + ONES_ROWS, cols), F32)
        for s, v in zip(win_s, win_v):
            acc_w = acc_w + _dot(v, jnp.exp2(s - m_w).astype(BF16))
        o_w = acc_w[0:HEAD_DIM] / acc_w[HEAD_DIM:HEAD_DIM + 1]

        for g in range(Q_PER_KV):
            hd = Q_PER_KV * h + g
            cs = slice(g * QB, (g + 1) * QB)
            o_t = (gt_t[hd:hd + 1] * o_c[:, cs] + gt_t[N_HEADS + hd:N_HEADS + hd + 1] * o_s[:, cs]
                   + gt_t[2 * N_HEADS + hd:2 * N_HEADS + hd + 1] * o_w[:, cs])
            attn[hd] = o_t.T
    a = jnp.concatenate(attn, axis=1)
    o_ref[...] = _rms(a, gout_ref[...]).astype(BF16)


def _pattn_t_tables():
    b = np.arange(QB)[:, None]
    a = np.arange(QB)[None, :]
    bk0 = _bucket_np(a - b)
    bk1 = _bucket_np(a - b + QB)
    c = np.arange(32)[:, None] - CMP_STRIDE
    bkc = _bucket_np(a - CMP_STRIDE * c - (CMP_LEN - 1))
    cidx = np.arange(4 * QB)[None, :] - CMP_STRIDE
    j = np.arange(QB)[:, None]
    mft = (((cidx // CMP_PER_SEL) == j) | (cidx == CMP_PER_SEL * j - 1)) & (cidx >= 0)
    return (jnp.asarray(bk0), jnp.asarray(bk1), jnp.asarray(bkc), jnp.asarray(mft.astype(np.float32), dtype=BF16))


def _pattn_t(q, gates, kt, vt, kcp, vcp, vcpt, rel_bias, g_out):
    rows = q.shape[0]
    bk0, bk1, bkc, mft = _pattn_t_tables()
    full = lambda a: pl.BlockSpec(a.shape, lambda i: (0,) * a.ndim)
    once = lambda a: pl.BlockSpec(a.shape, lambda i: (0,) * a.ndim, pipeline_mode=pl.Buffered(1))
    return pl.pallas_call(
        _pattn_t_body,
        grid=(rows // QB,),
        in_specs=[pl.BlockSpec((QB, ATTN_W), lambda i: (i, 0)), pl.BlockSpec((QB, LANE), lambda i: (i, 0)),
                  once(kt), once(vt), full(kcp), full(vcp), full(vcpt), pl.BlockSpec(memory_space=pltpu.SMEM),
                  full(bk0), full(bk1), full(bkc), full(mft), full(g_out)],
        out_specs=pl.BlockSpec((QB, ATTN_W), lambda i: (i, 0)),
        out_shape=jax.ShapeDtypeStruct((rows, ATTN_W), BF16),
        scratch_shapes=[pltpu.VMEM((N_HEADS, QB, QB), F32), pltpu.VMEM((N_HEADS, QB, QB), F32),
                        pltpu.VMEM((N_HEADS, 32, QB), F32), pltpu.VMEM((QB, QB), F32)],
        compiler_params=_cparams(("arbitrary",)),
        name="pattn",
    )(q, gates, kt, vt, kcp, vcp, vcpt, rel_bias, bk0, bk1, bkc, mft, g_out)


def _ssm_body(u_ref, ire_ref, iim_ref, wb_ref, wc_ref, abre_ref, abim_ref, dsk_ref, wglu_ref, gout_ref,
              y_ref, fre_ref, fim_ref, u_s, xre_s, xim_s, y_s, sre_s, sim_s, *, n_seg, tc, emit):
    c = pl.program_id(0)
    sgl = SSM_LANES // SSM_SG

    @pl.when(c == 0)
    def _init():
        sre_s[...] = ire_ref[...]
        sim_s[...] = iim_ref[...]

    n_lt = SSM_W // LANE
    if tc == 1:
        u_cols = [u_ref[:, l * LANE:(l + 1) * LANE] for l in range(n_lt)]
    else:
        for s in range(n_seg):
            for l in range(n_lt):
                u_s[l, pl.ds(s, tc, stride=n_seg), :] = u_ref[:, s * SSM_W + l * LANE:s * SSM_W + (l + 1) * LANE]
        u_cols = [u_s[l] for l in range(n_lt)]
    for sg in range(SSM_SG):
        ls = slice(sg * sgl, (sg + 1) * sgl)
        bu = _dot(u_cols[sg].astype(BF16), wb_ref[sg])
        xre_s[:, ls] = bu[:, 0:sgl]
        xim_s[:, ls] = bu[:, sgl:2 * sgl]
        ar = abre_ref[:, ls]
        ai = abim_ref[:, ls]

        def step(t, carry):
            xr, xi = carry
            rows = pl.ds(pl.multiple_of(t * n_seg, n_seg), n_seg)
            nr = ar * xr - ai * xi + xre_s[rows, ls]
            ni = ar * xi + ai * xr + xim_s[rows, ls]
            xre_s[rows, ls] = nr
            xim_s[rows, ls] = ni
            return nr, ni

        xr, xi = lax.fori_loop(0, tc, step, (sre_s[:, ls], sim_s[:, ls]), unroll=min(tc, 4))
        sre_s[:, ls] = xr
        sim_s[:, ls] = xi
    fre_ref[...] = sre_s[...]
    fim_ref[...] = sim_s[...]
    if not emit:
        y_ref[...] = jnp.zeros(y_ref.shape, y_ref.dtype)
        return
    ys = []
    for sg in range(SSM_SG):
        ls = slice(sg * sgl, (sg + 1) * sgl)
        x2 = jnp.concatenate([xre_s[:, ls], xim_s[:, ls]], axis=1).astype(BF16)
        ys.append(_dot(x2, wc_ref[sg]))
    y = jnp.concatenate(ys, axis=1) + dsk_ref[...] * jnp.concatenate(u_cols, axis=1)
    z = _dot(_gelu(y).astype(BF16), wglu_ref[...])
    o = z[:, 0:SSM_W] * jax.nn.sigmoid(z[:, SSM_W:2 * SSM_W])
    yn = _rms(o, gout_ref[...])
    if tc == 1:
        y_ref[...] = yn.astype(BF16)
    else:
        for l in range(n_lt):
            y_s[l] = yn[:, l * LANE:(l + 1) * LANE]
        for s in range(n_seg):
            for l in range(n_lt):
                y_ref[:, s * SSM_W + l * LANE:s * SSM_W + (l + 1) * LANE] = (
                    y_s[l, pl.ds(s, tc, stride=n_seg), :].astype(BF16))


def _ssm(u2, n_seg, init_re, init_im, wb, wc, ab_re, ab_im, d_skip, w_glu, g_out, tc, emit):
    if tc == 1:
        assert u2.shape == (n_seg, SSM_W)
        t_len = 1
        blk = (n_seg, SSM_W)
    else:
        t_len = u2.shape[0]
        assert u2.shape[1] == n_seg * SSM_W and n_seg % 8 == 0
        blk = (tc, n_seg * SSM_W)
    rows = tc * n_seg
    full = lambda a: pl.BlockSpec(a.shape, lambda c: (0,) * a.ndim)
    body = functools.partial(_ssm_body, n_seg=n_seg, tc=tc, emit=emit)
    st = jax.ShapeDtypeStruct((n_seg, SSM_LANES), F32)
    y_shape = u2.shape if emit else blk
    y_map = (lambda c: (c, 0)) if emit else (lambda c: (0, 0))
    return pl.pallas_call(
        body,
        grid=(t_len // tc,),
        in_specs=[pl.BlockSpec(blk, lambda c: (c, 0)),
                  full(init_re), full(init_im), full(wb), full(wc),
                  full(ab_re), full(ab_im), full(d_skip), full(w_glu), full(g_out)],
        out_specs=[pl.BlockSpec(blk, y_map), full(init_re), full(init_im)],
        out_shape=[jax.ShapeDtypeStruct(y_shape, BF16), st, st],
        scratch_shapes=[pltpu.VMEM((SSM_W // LANE, rows, LANE), F32), pltpu.VMEM((rows, SSM_LANES), F32),
                        pltpu.VMEM((rows, SSM_LANES), F32), pltpu.VMEM((SSM_W // LANE, rows, LANE), F32),
                        pltpu.VMEM((n_seg, SSM_LANES), F32), pltpu.VMEM((n_seg, SSM_LANES), F32)],
        compiler_params=_cparams(("arbitrary",)),
        name="ssm_emit" if emit else "ssm_final",
    )(u2, init_re, init_im, wb, wc, ab_re, ab_im, d_skip, w_glu, g_out)


def _ssm_chain_body(fre_ref, fim_ref, abre_ref, abim_ref, ire_ref, iim_ref, *, n_seg, log2_len):
    pr = abre_ref[...]
    pi = abim_ref[...]
    for _ in range(log2_len):
        pr, pi = pr * pr - pi * pi, 2.0 * pr * pi
    cr = jnp.zeros((1, SSM_LANES), F32)
    ci = jnp.zeros((1, SSM_LANES), F32)
    for j in range(n_seg):
        ire_ref[j:j + 1, :] = cr
        iim_ref[j:j + 1, :] = ci
        fr = fre_ref[j:j + 1, :]
        fi = fim_ref[j:j + 1, :]
        cr, ci = fr + pr * cr - pi * ci, fi + pr * ci + pi * cr


def _ssm_chain(fre, fim, ab_re, ab_im, seg_len):
    n_seg = fre.shape[0]
    log2_len = int(math.log2(seg_len))
    assert 2 ** log2_len == seg_len
    st = jax.ShapeDtypeStruct((n_seg, SSM_LANES), F32)
    return pl.pallas_call(
        functools.partial(_ssm_chain_body, n_seg=n_seg, log2_len=log2_len),
        out_shape=[st, st],
        compiler_params=pltpu.CompilerParams(vmem_limit_bytes=VMEM_LIMIT),
        name="ssm_chain",
    )(fre, fim, ab_re, ab_im)


def _outproj_body(a_ref, s_ref, x_ref, w_ref, h_ref):
    h_ref[...] = x_ref[...] + _dot(a_ref[...], w_ref[0:ATTN_W, :]) + _dot(s_ref[...], w_ref[ATTN_W:D_MODEL, :])


def _outproj(a_n, s_n, x, w, tm, n_seg=1):
    rows = x.shape[0]
    return pl.pallas_call(
        _outproj_body,
        grid=(rows // tm,),
        in_specs=[pl.BlockSpec((tm, ATTN_W), lambda i: (i, 0)), _seg_spec(tm, rows, n_seg),
                  pl.BlockSpec((tm, D_MODEL), lambda i: (i, 0)), pl.BlockSpec(w.shape, lambda i: (0, 0))],
        out_specs=pl.BlockSpec((tm, D_MODEL), lambda i: (i, 0)),
        out_shape=jax.ShapeDtypeStruct((rows, D_MODEL), F32),
        compiler_params=_cparams(("arbitrary",)),
        name="outproj",
    )(a_n, s_n, x, w)


def _ffn_body(h_ref, gffn_ref, wa_ref, wg_ref, wd_ref, cw_ref, cb_ref, p2_ref, p1_ref,
              y_ref, cnew_ref, hn_s, acc_s, car_s, *, seq, tm):
    r = pl.program_id(0)
    j = pl.program_id(1)
    nj = pl.num_programs(1)

    @pl.when(j == 0)
    def _norm():
        hn_s[...] = _rms(h_ref[...], gffn_ref[...]).astype(BF16)
        acc_s[...] = jnp.zeros_like(acc_s)

    hn = hn_s[...]
    a = _dot(hn, wa_ref[...])
    g = _dot(hn, wg_ref[...])
    if seq:
        @pl.when(r == 0)
        def _first():
            car_s[j, 6:7, :] = p2_ref[...]
            car_s[j, 7:8, :] = p1_ref[...]

        prev = car_s[j]
        row = lax.broadcasted_iota(I32, a.shape, 0)
        a1 = jnp.where(row == 0, prev[7:8, :], pltpu.roll(a, 1, 0))
        a2 = jnp.where(row == 0, prev[6:7, :], jnp.where(row == 1, prev[7:8, :], pltpu.roll(a, 2, 0)))
        car_s[j] = a[tm - 8:tm, :]
        tf = a.shape[1]
        cnew_ref[:, pl.ds(pl.multiple_of(j * tf, tf), tf)] = a[tm - 8:tm, :]
    else:
        a1 = p1_ref[...]
        a2 = p2_ref[...]
        cnew_ref[...] = a
    c = cb_ref[...] + cw_ref[0:1, :] * a2 + cw_ref[1:2, :] * a1 + cw_ref[2:3, :] * a
    acc_s[...] += _dot((_gelu(c) * g).astype(BF16), wd_ref[...])

    @pl.when(j == nj - 1)
    def _out():
        y_ref[...] = h_ref[...] + acc_s[...]


def _ffn(h, g_ffn, wa, wg, wd, cw, cb, p2, p1, tm, tf, seq):
    rows = h.shape[0]
    nj = D_FF_PAD // tf
    body = functools.partial(_ffn_body, seq=seq, tm=tm)
    if seq:
        tap_spec = pl.BlockSpec((1, tf), lambda r, j: (0, j))
        cnew_spec = pl.BlockSpec((8, D_FF_PAD), lambda r, j: (0, 0))
        cnew_shape = jax.ShapeDtypeStruct((8, D_FF_PAD), F32)
    else:
        tap_spec = pl.BlockSpec((tm, tf), lambda r, j: (r, j))
        cnew_spec = pl.BlockSpec((tm, tf), lambda r, j: (r, j))
        cnew_shape = jax.ShapeDtypeStruct((rows, D_FF_PAD), F32)
    return pl.pallas_call(
        body,
        grid=(rows // tm, nj),
        in_specs=[pl.BlockSpec((tm, D_MODEL), lambda r, j: (r, 0)), pl.BlockSpec((1, D_MODEL), lambda r, j: (0, 0)),
                  pl.BlockSpec((D_MODEL, tf), lambda r, j: (0, j)), pl.BlockSpec((D_MODEL, tf), lambda r, j: (0, j)),
                  pl.BlockSpec((tf, D_MODEL), lambda r, j: (j, 0)), pl.BlockSpec((CONV_W, tf), lambda r, j: (0, j)),
                  pl.BlockSpec((1, tf), lambda r, j: (0, j)), tap_spec, tap_spec],
        out_specs=[pl.BlockSpec((tm, D_MODEL), lambda r, j: (r, 0)), cnew_spec],
        out_shape=[jax.ShapeDtypeStruct((rows, D_MODEL), F32), cnew_shape],
        scratch_shapes=[pltpu.VMEM((tm, D_MODEL), BF16), pltpu.VMEM((tm, D_MODEL), F32), pltpu.VMEM((nj, 8, tf), F32)],
        compiler_params=_cparams(("arbitrary", "arbitrary")),
        name="ffn_seq" if seq else "ffn_rows",
    )(h, g_ffn, wa, wg, wd, cw, cb, p2, p1)


def _s1_copies(pt_ref, cache_ref, x_s, sem, b, slot, n_pages):
    cps = []
    for pg in range(n_pages):
        page = pt_ref[b * n_pages + pg]
        for s in range(2 * N_KV):
            cps.append(pltpu.make_async_copy(cache_ref.at[page, :, s, :],
                                             x_s.at[slot, s, pl.ds(pg * PAGE, PAGE), :], sem.at[slot]))
    return cps


def _s1_body(pt_ref, cache_ref, q_ref, wcat_ref, cpos_ref, w2_ref, gkc_ref, rb_ref, bkc_ref, mt_ref,
             oc_ref, idx_ref, val_ref, x_s, bias_s, sem, *, n_pages, past):
    b = pl.program_id(0)
    nb = pl.num_programs(0)
    slot = b % 2

    @pl.when(b == 0)
    def _first():
        for cp in _s1_copies(pt_ref, cache_ref, x_s, sem, 0, 0, n_pages):
            cp.start()
        for hd in range(N_HEADS):
            bias_s[hd:hd + 1, :] = _bias_lookup(bkc_ref[...], rb_ref, hd)

    @pl.when(b + 1 < nb)
    def _next():
        for cp in _s1_copies(pt_ref, cache_ref, x_s, sem, b + 1, 1 - slot, n_pages):
            cp.start()

    for cp in _s1_copies(pt_ref, cache_ref, x_s, sem, b, slot, n_pages):
        cp.wait()

    n_c = past // CMP_STRIDE
    cio = lax.broadcasted_iota(I32, (N_HEADS, n_c), 1)
    hrow = lax.broadcasted_iota(I32, (N_HEADS, n_c), 0) // Q_PER_KV
    q = q_ref[0]
    s_all = jnp.zeros((N_HEADS, n_c), F32)
    vcs = []
    for h in range(N_KV):
        kc = _rms(_compress_rows(x_s.at[slot, h], wcat_ref.at[0], cpos_ref[0, 0:1, :], w2_ref[0]), gkc_ref[...])
        vcs.append(_compress_rows(x_s.at[slot, N_KV + h], wcat_ref.at[1], cpos_ref[1, 0:1, :], w2_ref[1]).astype(BF16))
        s_all = jnp.where(hrow == h, _dot_nt(q, kc.astype(BF16)), s_all)
    ok = cio < n_c - 1
    s_all = jnp.where(ok, s_all + bias_s[...], NEG)
    m = jnp.max(s_all, axis=-1, keepdims=True)
    p = jnp.where(ok, jnp.exp(s_all - m), 0.0)
    p = p / jnp.maximum(jnp.sum(p, axis=-1, keepdims=True), 1e-30)
    pb = p.astype(BF16)
    hrow_o = lax.broadcasted_iota(I32, (N_HEADS, HEAD_DIM), 0) // Q_PER_KV
    o_c = jnp.zeros((N_HEADS, HEAD_DIM), F32)
    for h in range(N_KV):
        o_c = jnp.where(hrow_o == h, _dot(pb, vcs[h]), o_c)
    oc_ref[0] = o_c
    rio = lax.broadcasted_iota(I32, (8, n_c), 0)
    imp = jnp.zeros((8, n_c), F32)
    for h in range(N_KV):
        ih = p[4 * h:4 * h + 1] + p[4 * h + 1:4 * h + 2] + p[4 * h + 2:4 * h + 3] + p[4 * h + 3:4 * h + 4]
        imp = jnp.where(rio == h, ih, imp)
    imp = jnp.concatenate([imp, jnp.zeros((LANE - 8, n_c), F32)], axis=0)
    n_j = mt_ref.shape[0]
    imp_t = jnp.zeros((n_j, LANE), F32)
    for part in _split3(imp):
        imp_t = imp_t + _dot_nt(mt_ref[...], part)
    j_io = lax.broadcasted_iota(I32, (n_j, LANE), 0)
    cur = past // SEL_BLOCK
    forced = (j_io == 0) | (j_io == cur) | (j_io == cur - 1)
    score_t = jnp.where(forced, 1e9, jnp.where(j_io <= cur, imp_t, -1.0))
    score_t = jnp.where(j_io <= cur, score_t, -jnp.inf)

    def on_pick(r, idx, okv):
        idx_ref[0, r:r + 1, :] = idx
        val_ref[0, r:r + 1, :] = jnp.where(okv, 1, 0)

    _select_rounds(score_t, on_pick)


def _s1(page_table, cache4, q3, wcat, cpos, w2, g_kc, rel_bias):
    n_b, n_pages = page_table.shape
    past = n_pages * PAGE
    n_c = past // CMP_STRIDE
    ns = past // SEL_BLOCK + 1
    n_j = -(-ns // 8) * 8
    c = np.arange(n_c)[None, :]
    bkc = _bucket_np(past - (CMP_STRIDE * c + CMP_LEN - 1))
    j = np.arange(n_j)[:, None]
    mt = (((c // CMP_PER_SEL) == j) | (c == CMP_PER_SEL * j - 1)) & (c < n_c - 1)
    mt = jnp.asarray(mt.astype(np.float32), dtype=BF16)
    full = lambda a: pl.BlockSpec(a.shape, lambda b, pt: (0,) * a.ndim)
    body = functools.partial(_s1_body, n_pages=n_pages, past=past)
    return pl.pallas_call(
        body,
        grid_spec=pltpu.PrefetchScalarGridSpec(
            num_scalar_prefetch=1,
            grid=(n_b,),
            in_specs=[pl.BlockSpec(memory_space=pl.ANY), pl.BlockSpec((1, N_HEADS, HEAD_DIM), lambda b, pt: (b, 0, 0)),
                      full(wcat), full(cpos), full(w2), full(g_kc), pl.BlockSpec(memory_space=pltpu.SMEM),
                      pl.BlockSpec((1, n_c), lambda b, pt: (0, 0)), full(mt)],
            out_specs=[pl.BlockSpec((1, N_HEADS, HEAD_DIM), lambda b, pt: (b, 0, 0)),
                       pl.BlockSpec((1, N_SEL, LANE), lambda b, pt: (b, 0, 0)),
                       pl.BlockSpec((1, N_SEL, LANE), lambda b, pt: (b, 0, 0))],
            scratch_shapes=[pltpu.VMEM((2, 2 * N_KV, past, HEAD_DIM), F32), pltpu.VMEM((N_HEADS, n_c), F32),
                            pltpu.SemaphoreType.DMA((2,))],
        ),
        out_shape=[jax.ShapeDtypeStruct((n_b, N_HEADS, HEAD_DIM), F32), jax.ShapeDtypeStruct((n_b, N_SEL, LANE), I32),
                   jax.ShapeDtypeStruct((n_b, N_SEL, LANE), I32)],
        compiler_params=_cparams(("arbitrary",)),
        name="sample_cmp",
    )(page_table.reshape(-1), cache4, q3, wcat, cpos, w2, g_kc, rel_bias, jnp.asarray(bkc), mt)


def _s2_copies(idx_ref, pt_ref, cache_ref, cwin_ref, ks_s, vs_s, kw_s, vw_s, sem, b, slot, n_pages):
    cps = []
    n_blk = n_pages * (PAGE // SEL_BLOCK)
    for h in range(N_KV):
        for r in range(N_SEL):
            jb = jnp.minimum(idx_ref[(b * N_SEL + r) * N_KV + h], n_blk - 1)
            page = pt_ref[b * n_pages + jb // 2]
            row0 = pl.multiple_of((jb % 2) * SEL_BLOCK, SEL_BLOCK)
            cps.append(pltpu.make_async_copy(cache_ref.at[page, pl.ds(row0, SEL_BLOCK), 2 * N_KV + h, :],
                                             ks_s.at[slot, h, pl.ds(r * SEL_BLOCK, SEL_BLOCK), :], sem.at[slot]))
            cps.append(pltpu.make_async_copy(cache_ref.at[page, pl.ds(row0, SEL_BLOCK), 3 * N_KV + h, :],
                                             vs_s.at[slot, h, pl.ds(r * SEL_BLOCK, SEL_BLOCK), :], sem.at[slot]))
        cps.append(pltpu.make_async_copy(cwin_ref.at[b, :, h, :], kw_s.at[slot, h], sem.at[slot]))
        cps.append(pltpu.make_async_copy(cwin_ref.at[b, :, N_KV + h, :], vw_s.at[slot, h], sem.at[slot]))
    return cps


def _s2_body(idx_ref, val_ref, pt_ref, cache_ref, cwin_ref, q_ref, oc_ref, gt_ref, kns_ref, vns_ref, knw_ref, vnw_ref,
             rb_ref, bks_ref, bkw_ref, gout_ref, o_ref, ks_s, vs_s, kw_s, vw_s, bs_s, bw_s, sem, *, n_pages, past):
    b = pl.program_id(0)
    nb = pl.num_programs(0)
    slot = b % 2
    args = (idx_ref, pt_ref, cache_ref, cwin_ref, ks_s, vs_s, kw_s, vw_s, sem)

    @pl.when(b == 0)
    def _first():
        for cp in _s2_copies(*args, 0, 0, n_pages):
            cp.start()
        for hd in range(N_HEADS):
            bs_s[hd:hd + 1, :] = _bias_lookup(bks_ref[...], rb_ref, hd)
            bw_s[hd:hd + 1, :] = _bias_lookup(bkw_ref[...], rb_ref, hd)

    @pl.when(b + 1 < nb)
    def _next():
        for cp in _s2_copies(*args, b + 1, 1 - slot, n_pages):
            cp.start()

    for cp in _s2_copies(*args, b, slot, n_pages):
        cp.wait()

    n_blk = n_pages * (PAGE // SEL_BLOCK)
    q = q_ref[0]
    qf = q.astype(F32)
    hrow = lax.broadcasted_iota(I32, (N_HEADS, 1), 0) // Q_PER_KV
    lane = lax.broadcasted_iota(I32, (N_HEADS, LANE), 1)
    bias0 = jnp.concatenate([jnp.full((1, 1), rb_ref[0, hd], F32) for hd in range(N_HEADS)], axis=0)
    b31 = jnp.concatenate([jnp.full((1, 1), rb_ref[REL_BUCKETS - 1, hd], F32) for hd in range(N_HEADS)], axis=0)

    tiles = []
    new_sel = jnp.zeros((N_HEADS, 1), F32)
    for t in range(N_SEL // 2):
        s_t = jnp.zeros((N_HEADS, LANE), F32)
        for h in range(N_KV):
            s_h = _dot_nt(q, ks_s[slot, h, pl.ds(t * LANE, LANE), :].astype(BF16))
            halves = []
            for half in range(2):
                r = 2 * t + half
                jb = idx_ref[(b * N_SEL + r) * N_KV + h]
                okr = (val_ref[(b * N_SEL + r) * N_KV + h] > 0) & (jb < n_blk)
                near = bs_s[:, (half * 2) * LANE:(half * 2 + 1) * LANE]
                nearer = bs_s[:, (half * 2 + 1) * LANE:(half * 2 + 2) * LANE]
                bias = jnp.where(jb == n_blk - 1, nearer, jnp.where(jb == n_blk - 2, near, b31))
                halves.append(jnp.where(okr, s_h + bias, NEG))
                new_sel = jnp.where((hrow == h) & (val_ref[(b * N_SEL + r) * N_KV + h] > 0) & (jb == n_blk), 1.0, new_sel)
            s_h = jnp.where(lane < SEL_BLOCK, halves[0], halves[1])
            s_t = jnp.where(hrow == h, s_h, s_t)
        tiles.append(s_t)
    s_new = jnp.sum(qf * kns_ref[0], axis=-1, keepdims=True) + bias0
    s_new = jnp.where(new_sel > 0.5, s_new, NEG)
    m = s_new
    for s_t in tiles:
        m = jnp.maximum(m, jnp.max(s_t, axis=-1, keepdims=True))
    p_new = jnp.where(new_sel > 0.5, jnp.exp(s_new - m), 0.0)
    l = p_new
    acc = p_new * vns_ref[0]
    for t, s_t in enumerate(tiles):
        p = jnp.where(s_t > 0.5 * NEG, jnp.exp(s_t - m), 0.0)
        l = l + jnp.sum(p, axis=-1, keepdims=True)
        pb = p.astype(BF16)
        for h in range(N_KV):
            pv = _dot(pb, vs_s[slot, h, pl.ds(t * LANE, LANE), :].astype(BF16))
            acc = acc + jnp.where(hrow == h, pv, 0.0)
    o_s = acc / jnp.maximum(l, 1e-30)

    n_w = kw_s.shape[2]
    wt = []
    for t in range(n_w // LANE):
        s_t = jnp.zeros((N_HEADS, LANE), F32)
        for h in range(N_KV):
            s_h = _dot_nt(q, kw_s[slot, h, pl.ds(t * LANE, LANE), :].astype(BF16))
            s_t = jnp.where(hrow == h, s_h, s_t)
        wt.append(s_t + bw_s[:, t * LANE:(t + 1) * LANE])
    s_new = jnp.sum(qf * knw_ref[0], axis=-1, keepdims=True) + bias0
    m = s_new
    for s_t in wt:
        m = jnp.maximum(m, jnp.max(s_t, axis=-1, keepdims=True))
    p_new = jnp.exp(s_new - m)
    l = p_new
    acc = p_new * vnw_ref[0]
    for t, s_t in enumerate(wt):
        p = jnp.exp(s_t - m)
        l = l + jnp.sum(p, axis=-1, keepdims=True)
        pb = p.astype(BF16)
        for h in range(N_KV):
            pv = _dot(pb, vw_s[slot, h, pl.ds(t * LANE, LANE), :].astype(BF16))
            acc = acc + jnp.where(hrow == h, pv, 0.0)
    o_w = acc / l

    gt = gt_ref[0]
    a = gt[:, 0:1] * oc_ref[0] + gt[:, 1:2] * o_s + gt[:, 2:3] * o_w
    ms = jnp.sum(jnp.sum(a * a, axis=-1, keepdims=True), axis=0, keepdims=True) / (N_HEADS * HEAD_DIM)
    o_ref[0] = (a * lax.rsqrt(ms + EPS) * gout_ref[...]).astype(BF16)


def _s2(idx, val, page_table, cache4, cwin4, q3, o_c, gates3, kns, vns, knw, vnw, rel_bias, g_out3):
    n_b, n_pages = page_table.shape
    past = n_pages * PAGE
    n_w = cwin4.shape[1]
    s = np.arange(SEL_BLOCK)
    d_near = past - ((past // SEL_BLOCK - 2) * SEL_BLOCK + s)
    d_nearer = past - ((past // SEL_BLOCK - 1) * SEL_BLOCK + s)
    z = np.zeros(SEL_BLOCK, np.int64)
    bks = np.concatenate([d_near, z, d_nearer, z, z, d_near, z, d_nearer])[None, :]
    bkw = (past - (past - n_w + np.arange(n_w)))[None, :]
    full = lambda a: pl.BlockSpec(a.shape, lambda b, *_: (0,) * a.ndim)
    per_b = lambda a: pl.BlockSpec((1,) + a.shape[1:], lambda b, *_: (b,) + (0,) * (a.ndim - 1))
    body = functools.partial(_s2_body, n_pages=n_pages, past=past)
    return pl.pallas_call(
        body,
        grid_spec=pltpu.PrefetchScalarGridSpec(
            num_scalar_prefetch=3,
            grid=(n_b,),
            in_specs=[pl.BlockSpec(memory_space=pl.ANY), pl.BlockSpec(memory_space=pl.ANY),
                      per_b(q3), per_b(o_c), per_b(gates3), per_b(kns), per_b(vns), per_b(knw), per_b(vnw),
                      pl.BlockSpec(memory_space=pltpu.SMEM), pl.BlockSpec((1, 4 * LANE), lambda b, *_: (0, 0)),
                      pl.BlockSpec((1, n_w), lambda b, *_: (0, 0)), full(g_out3)],
            out_specs=pl.BlockSpec((1, N_HEADS, HEAD_DIM), lambda b, *_: (b, 0, 0)),
            scratch_shapes=[pltpu.VMEM((2, N_KV, N_SEL * SEL_BLOCK, HEAD_DIM), F32),
                            pltpu.VMEM((2, N_KV, N_SEL * SEL_BLOCK, HEAD_DIM), F32),
                            pltpu.VMEM((2, N_KV, n_w, HEAD_DIM), F32), pltpu.VMEM((2, N_KV, n_w, HEAD_DIM), F32),
                            pltpu.VMEM((N_HEADS, 4 * LANE), F32), pltpu.VMEM((N_HEADS, n_w), F32),
                            pltpu.SemaphoreType.DMA((2,))],
        ),
        out_shape=jax.ShapeDtypeStruct((n_b, N_HEADS, HEAD_DIM), BF16),
        compiler_params=_cparams(("arbitrary",)),
        name="sample_attn",
    )(idx, val, page_table.reshape(-1), cache4, cwin4, q3, o_c, gates3, kns, vns, knw, vnw, rel_bias,
      jnp.asarray(_bucket_np(bks)), jnp.asarray(_bucket_np(bkw)), g_out3)


def _winshift_body(cwin_ref, new_ref, o_ref):
    n_w = cwin_ref.shape[1]
    o_ref[0, 0:n_w - 1] = cwin_ref[0, 1:n_w]
    o_ref[0, n_w - 1] = new_ref[0]


def _winshift(cwin4, new3):
    n_b = cwin4.shape[0]
    return pl.pallas_call(
        _winshift_body,
        grid=(n_b,),
        in_specs=[pl.BlockSpec((1,) + cwin4.shape[1:], lambda b: (b, 0, 0, 0)),
                  pl.BlockSpec((1,) + new3.shape[1:], lambda b: (b, 0, 0))],
        out_specs=pl.BlockSpec((1,) + cwin4.shape[1:], lambda b: (b, 0, 0, 0)),
        out_shape=jax.ShapeDtypeStruct(cwin4.shape, F32),
        compiler_params=_cparams(("arbitrary",)),
        name="winshift",
    )(cwin4, new3)


def _block_diag_b(bb_re, bb_im):
    gl = SSM_G // SSM_SG
    eye = jnp.eye(gl, dtype=F32)

    def one(bb):
        t = jnp.transpose(bb, (1, 0, 2)).reshape(SSM_SG, gl, SSM_P, SSM_N)
        return jnp.einsum('sgpn,gh->sgphn', t, eye).reshape(SSM_SG, gl * SSM_P, gl * SSM_N)

    return jnp.concatenate([one(bb_re), one(bb_im)], axis=2).astype(BF16)


def _block_diag_c(c_re, c_im):
    gl = SSM_G // SSM_SG
    eye = jnp.eye(gl, dtype=F32)

    def one(c):
        t = jnp.transpose(c, (0, 2, 1)).reshape(SSM_SG, gl, SSM_N, SSM_P)
        return jnp.einsum('sgnp,gh->sgnhp', t, eye).reshape(SSM_SG, gl * SSM_N, gl * SSM_P)

    return jnp.concatenate([one(c_re), -one(c_im)], axis=1).astype(BF16)


def kernel(x_prompt, x_sample, cache_kv, page_table, cache_win, state_ssm, state_conv, rel_bias, g_mix, w_in, g_q, g_k, w_cmp1, pos_cmp, w_cmp2, lam_re, lam_im, log_dt, b_re, b_im, c_re, c_im, d_skip, w_glu, g_out_attn, g_out_ssm, w_out, g_ffn, w_up, conv_w, conv_b, w_down):
    depth = g_mix.shape[0]
    assert depth == 1 and x_prompt.shape[0] == 1 and x_sample.shape[1] == 1
    seq = x_prompt.shape[1]
    n_b = x_sample.shape[0]
    n_pages = page_table.shape[1]
    n_w = cache_win.shape[2]
    li = 0
    row = lambda v: v.reshape(1, -1)

    wi = w_in[li]
    c_g = ATTN_W + 6 * KV_W
    w_in_b = jnp.concatenate([wi[:, :c_g], wi[:, c_g + N_BRANCH * N_HEADS:], wi[:, c_g:c_g + N_BRANCH * N_HEADS],
                              jnp.zeros((D_MODEL, LANE - N_BRANCH * N_HEADS), F32)], axis=1).astype(BF16)
    w1 = w_cmp1[li]
    wcat = jnp.concatenate([w1[:, :CMP_STRIDE], w1[:, CMP_STRIDE:]], axis=-1).astype(BF16)
    w2 = w_cmp2[li].astype(BF16)
    w_glu_b = w_glu[li].astype(BF16)
    w_out_b = w_out[li].astype(BF16)
    padc = lambda a: jnp.pad(a, ((0, 0), (0, D_FF_PAD - D_FF)))
    wa = padc(w_up[li][:, :D_FF]).astype(BF16)
    wg = padc(w_up[li][:, D_FF:]).astype(BF16)
    wd = jnp.pad(w_down[li], ((0, D_FF_PAD - D_FF), (0, 0))).astype(BF16)
    cw = padc(conv_w[li])
    cb = padc(row(conv_b[li]))

    ab_re, ab_im, bb_re, bb_im, cpos = _prep(lam_re[li], lam_im[li], log_dt[li], b_re[li], b_im[li], pos_cmp[li], w1)
    wb = _block_diag_b(bb_re, bb_im)
    wc = _block_diag_c(c_re[li], c_im[li])
    ab_re = row(ab_re)
    ab_im = row(ab_im)

    in_args = (row(g_mix[li]), w_in_b, row(g_q[li]), row(g_k[li, 1]), row(g_k[li, 2]))
    ssm_args = (wb, wc, ab_re, ab_im, row(d_skip[li]), w_glu_b, row(g_out_ssm[li]))

    xp = x_prompt[0]
    n_seg = 8
    q, kv, win, u2, gates, kt, vt = _inproj(xp, *in_args, tm=512, q_scale=HEAD_DIM ** -0.5 * LOG2E, n_seg=n_seg,
                                            attn_operands=True)
    cmp_out = _pcompress(kv, wcat, cpos, w2, row(g_k[li, 0]))
    zpad = jnp.zeros((CMP_STRIDE, 2 * KV_W), F32)
    cmp_pad = jnp.concatenate([zpad, cmp_out, zpad], axis=0)
    attn_n = _pattn_t(q, gates, kt, vt, cmp_pad[:, :KV_W], cmp_pad[:, KV_W:], cmp_pad[:, KV_W:].T, rel_bias,
                      row(g_out_attn[li]))
    zst = jnp.zeros((n_seg, SSM_LANES), F32)
    _, fre, fim = _ssm(u2, n_seg, zst, zst, *ssm_args, tc=64, emit=False)
    ire, iim = _ssm_chain(fre, fim, ab_re, ab_im, seq // n_seg)
    ssm_n, hre, him = _ssm(u2, n_seg, ire, iim, *ssm_args, tc=64, emit=True)
    h_p = _outproj(attn_n, ssm_n, xp, w_out_b, tm=512, n_seg=n_seg)
    zrow = jnp.zeros((1, D_FF_PAD), F32)
    y_p, cnew_p = _ffn(h_p, row(g_ffn[li]), wa, wg, wd, cw, cb, zrow, zrow, tm=512, tf=512, seq=True)

    y_prompt = y_p[None]
    kv_prompt = kv.reshape(1, 1, seq, 4, N_KV, HEAD_DIM)
    win_prompt = win[seq - min(WINDOW, seq):].reshape(1, 1, min(WINDOW, seq), 2, N_KV, HEAD_DIM)
    ssm_prompt = jnp.stack([hre[n_seg - 1], him[n_seg - 1]], axis=-1).reshape(1, 1, SSM_G, SSM_N, 2)
    conv_prompt = cnew_p[6:8, :D_FF].reshape(1, 1, CONV_W - 1, D_FF)

    xs = x_sample[:, 0]
    q_s, kv_s, win_s, u_s, gates_s = _inproj(xs, *in_args, tm=n_b, q_scale=HEAD_DIM ** -0.5)
    cache4 = cache_kv[li].reshape(cache_kv.shape[1], PAGE, 4 * N_KV, HEAD_DIM)
    cwin4 = cache_win[li].reshape(n_b, n_w, 2 * N_KV, HEAD_DIM)
    q3 = q_s.reshape(n_b, N_HEADS, HEAD_DIM)
    o_c, idx, val = _s1(page_table, cache4, q3, wcat, cpos, w2, row(g_k[li, 0]), rel_bias)
    idx = idx[:, :, :N_KV].reshape(-1)
    val = val[:, :, :N_KV].reshape(-1)
    rep = lambda a: jnp.repeat(a.reshape(n_b, N_KV, HEAD_DIM), Q_PER_KV, axis=1)
    kns = rep(kv_s[:, 2 * KV_W:3 * KV_W])
    vns = rep(kv_s[:, 3 * KV_W:4 * KV_W])
    knw = rep(win_s[:, :KV_W])
    vnw = rep(win_s[:, KV_W:])
    g3 = jnp.transpose(gates_s[:, :N_BRANCH * N_HEADS].reshape(n_b, N_BRANCH, N_HEADS), (0, 2, 1))
    g3 = jnp.pad(g3, ((0, 0), (0, 0), (0, LANE - N_BRANCH)))
    attn_s = _s2(idx, val, page_table, cache4, cwin4, q3, o_c, g3, kns, vns, knw, vnw, rel_bias,
                 g_out_attn[li].reshape(N_HEADS, HEAD_DIM))
    st = state_ssm[li].reshape(n_b, SSM_LANES, 2)
    ssm_s, sre, sim = _ssm(u_s, n_b, st[:, :, 0], st[:, :, 1], *ssm_args, tc=1, emit=True)
    h_s = _outproj(attn_s.reshape(n_b, ATTN_W), ssm_s, xs, w_out_b, tm=n_b)
    sc = state_conv[li]
    y_s, a_s = _ffn(h_s, row(g_ffn[li]), wa, wg, wd, cw, cb, padc(sc[:, 0]), padc(sc[:, 1]), tm=n_b, tf=512, seq=False)
    win_sample = _winshift(cwin4, win_s.reshape(n_b, 2 * N_KV, HEAD_DIM))

    y_sample = y_s[:, None]
    kv_sample = kv_s.reshape(1, n_b, 1, 4, N_KV, HEAD_DIM)
    win_sample = win_sample.reshape(1, n_b, n_w, 2, N_KV, HEAD_DIM)
    ssm_sample = jnp.stack([sre, sim], axis=-1).reshape(1, n_b, SSM_G, SSM_N, 2)
    conv_sample = jnp.stack([sc[:, 1], a_s[:, :D_FF]], axis=1)[None]
    return (y_prompt, y_sample, kv_prompt, kv_sample, win_prompt, win_sample,
            ssm_prompt, ssm_sample, conv_prompt, conv_sample)
```

```python
import functools
import math

import numpy as np
import jax
import jax.numpy as jnp
from jax import lax
from jax.experimental import pallas as pl
from jax.experimental.pallas import tpu as pltpu

F32 = jnp.float32
BF16 = jnp.bfloat16
I32 = jnp.int32

D_MODEL = 2048
HEAD_DIM = 128
N_HEADS = 8
N_KV = 2
Q_PER_KV = 4
ATTN_W = 1024
KV_W = 256
N_BRANCH = 3
CMP_LEN = 32
CMP_STRIDE = 16
SEL_BLOCK = 64
CMP_PER_SEL = 4
N_SEL = 16
WINDOW = 512
REL_BUCKETS = 32
REL_MAX_DIST = 128
PAGE = 128
SSM_W = 1024
SSM_G = 64
SSM_N = 64
SSM_P = 16
SSM_SG = 8
SSM_LANES = SSM_G * SSM_N
D_FF = 5504
D_FF_PAD = 5632
CONV_W = 3
EPS = 1e-6
NEG = -1e30
QB = 128
LANE = 128
VMEM_LIMIT = 56 * 1024 * 1024


def _cparams(sem):
    return pltpu.CompilerParams(dimension_semantics=sem, vmem_limit_bytes=VMEM_LIMIT)


def _rms(x, g):
    return x * lax.rsqrt(jnp.mean(x * x, axis=-1, keepdims=True) + EPS) * g


def _gelu(x):
    return jax.nn.gelu(x)


def _dot(a, b):
    return jnp.dot(a, b, preferred_element_type=F32)


def _dot_nt(a, b):
    return lax.dot_general(a, b, (((1,), (1,)), ((), ())), preferred_element_type=F32)


def _split3(x):
    hi = x.astype(BF16)
    r1 = x - hi.astype(F32)
    mid = r1.astype(BF16)
    lo = (r1 - mid.astype(F32)).astype(BF16)
    return hi, mid, lo


def _bucket_np(d):
    n = np.maximum(d, 0)
    exact = REL_BUCKETS // 2
    nf = np.maximum(n, 1).astype(np.float32)
    large = exact + (np.log(nf / np.float32(exact)) / np.float32(math.log(REL_MAX_DIST / exact))
                     * np.float32(REL_BUCKETS - exact)).astype(np.int32)
    return np.where(n < exact, n, np.minimum(large, REL_BUCKETS - 1)).astype(np.int32)


def _bias_lookup(bkt, rb_ref, head, shift=None):
    last = rb_ref[REL_BUCKETS - 1, head]
    acc = jnp.full(bkt.shape, last, F32)
    for b in range(REL_BUCKETS - 1):
        acc = jnp.where(bkt == b, rb_ref[b, head], acc)
    if shift:
        acc = acc - last
    return acc


def _prep_body(lre_ref, lim_ref, ldt_ref, bre_ref, bim_ref, pos_ref, w1_ref,
               abre_ref, abim_ref, bbre_ref, bbim_ref, cpos_ref):
    lr = lre_ref[...]
    li = lim_ref[...]
    dt = jnp.exp(ldt_ref[...])
    mag = jnp.exp(lr * dt)
    ab_re = mag * jnp.cos(li * dt)
    ab_im = mag * jnp.sin(li * dt)
    den = lr * lr + li * li
    nr = ab_re - 1.0
    ni = ab_im
    f_re = (nr * lr + ni * li) / den
    f_im = (ni * lr - nr * li) / den
    abre_ref[...] = ab_re
    abim_ref[...] = ab_im
    for p in range(SSM_P):
        br = bre_ref[p]
        bi = bim_ref[p]
        bbre_ref[p] = f_re * br - f_im * bi
        bbim_ref[p] = f_re * bi + f_im * br
    for kind in range(2):
        cpos_ref[kind] = jnp.dot(pos_ref[kind], w1_ref[kind], preferred_element_type=F32,
                                 precision=lax.Precision.HIGHEST)


def _prep(lam_re, lam_im, log_dt, b_re, b_im, pos_cmp, w_cmp1):
    bre_t = jnp.transpose(b_re, (2, 0, 1))
    bim_t = jnp.transpose(b_im, (2, 0, 1))
    pos = jnp.zeros((2, 8, CMP_LEN * HEAD_DIM), F32).at[:, 0, :].set(pos_cmp.reshape(2, CMP_LEN * HEAD_DIM))
    w1 = w_cmp1.reshape(2, CMP_LEN * HEAD_DIM, HEAD_DIM)
    return pl.pallas_call(
        _prep_body,
        out_shape=[jax.ShapeDtypeStruct((SSM_G, SSM_N), F32), jax.ShapeDtypeStruct((SSM_G, SSM_N), F32),
                   jax.ShapeDtypeStruct((SSM_P, SSM_G, SSM_N), F32), jax.ShapeDtypeStruct((SSM_P, SSM_G, SSM_N), F32),
                   jax.ShapeDtypeStruct((2, 8, HEAD_DIM), F32)],
        compiler_params=pltpu.CompilerParams(vmem_limit_bytes=VMEM_LIMIT),
        name="prep",
    )(lam_re, lam_im, log_dt.reshape(SSM_G, 1), bre_t, bim_t, pos, w1)


IN_COLS_PAD = ATTN_W + 4 * KV_W + 2 * KV_W + SSM_W + LANE
KT_KS = (0, 2 * HEAD_DIM)
KT_E = HEAD_DIM
KT_KW = 3 * HEAD_DIM
KT_COLS = 5 * HEAD_DIM
VT_ROWS = 4 * HEAD_DIM


def _inproj_body(x_ref, gmix_ref, w_ref, gq_ref, gks_ref, gkw_ref,
                 q_ref, kv_ref, win_ref, u_ref, gt_ref, *extra, q_scale):
    tm = x_ref.shape[0]
    xn = _rms(x_ref[...], gmix_ref[...]).astype(BF16)
    zq = _dot(xn, w_ref[:, 0:ATTN_W])
    for h in range(N_HEADS):
        sl = slice(h * HEAD_DIM, (h + 1) * HEAD_DIM)
        q_ref[:, sl] = (_rms(zq[:, sl], gq_ref[...]) * q_scale).astype(BF16)
    zkv = _dot(xn, w_ref[:, ATTN_W:ATTN_W + 4 * KV_W])
    ks, kw = [], []
    for s in range(4 * N_KV):
        col = zkv[:, s * HEAD_DIM:(s + 1) * HEAD_DIM]
        if s // N_KV == 2:
            col = _rms(col, gks_ref[...])
            ks.append(col)
        kv_ref[:, s, :] = col
    c0 = ATTN_W + 4 * KV_W
    zw = _dot(xn, w_ref[:, c0:c0 + 2 * KV_W])
    for s in range(2 * N_KV):
        col = zw[:, s * HEAD_DIM:(s + 1) * HEAD_DIM]
        if s // N_KV == 0:
            col = _rms(col, gkw_ref[...])
            kw.append(col)
        win_ref[:, s, :] = col
    if extra:
        cmp_ref, kt_ref, vt_ref = extra
        cmp_ref[...] = zkv[:, 0:2 * KV_W]
        for h in range(N_KV):
            kt_ref[:, KT_KS[h]:KT_KS[h] + HEAD_DIM] = ks[h].astype(BF16)
            kt_ref[:, KT_KW + h * HEAD_DIM:KT_KW + (h + 1) * HEAD_DIM] = kw[h].astype(BF16)
        blk = (lax.broadcasted_iota(I32, (tm, HEAD_DIM), 0) + pl.program_id(0) * tm) // SEL_BLOCK
        kt_ref[:, KT_E:KT_E + HEAD_DIM] = jnp.where(blk == lax.broadcasted_iota(I32, (tm, HEAD_DIM), 1), 1.0, 0.0).astype(BF16)
        vt_ref[0:KV_W, :] = zkv[:, 3 * KV_W:4 * KV_W].T.astype(BF16)
        vt_ref[KV_W:2 * KV_W, :] = zw[:, KV_W:2 * KV_W].T.astype(BF16)
    c1 = c0 + 2 * KV_W
    u_ref[...] = _dot(xn, w_ref[:, c1:c1 + SSM_W])
    c2 = c1 + SSM_W
    gt_ref[...] = jax.nn.sigmoid(_dot(xn, w_ref[:, c2:c2 + LANE]))


def _seg_spec(tm, rows, n_seg):
    tiles_per_seg = rows // n_seg // tm
    return pl.BlockSpec((tm, SSM_W), lambda i: (i % tiles_per_seg, i // tiles_per_seg))


def _inproj(x, g_mix, w, g_q, g_ks, g_kw, tm, q_scale, n_seg=1, attn_operands=False):
    rows = x.shape[0]
    row_spec = lambda n: pl.BlockSpec((tm, n), lambda i: (i, 0))
    full = lambda a: pl.BlockSpec(a.shape, lambda i: (0,) * a.ndim)
    out_specs = [row_spec(ATTN_W), pl.BlockSpec((tm, 4 * N_KV, HEAD_DIM), lambda i: (i, 0, 0)),
                 pl.BlockSpec((tm, 2 * N_KV, HEAD_DIM), lambda i: (i, 0, 0)), _seg_spec(tm, rows, n_seg), row_spec(LANE)]
    out_shape = [jax.ShapeDtypeStruct((rows, ATTN_W), BF16), jax.ShapeDtypeStruct((rows, 4 * N_KV, HEAD_DIM), F32),
                 jax.ShapeDtypeStruct((rows, 2 * N_KV, HEAD_DIM), F32),
                 jax.ShapeDtypeStruct((rows // n_seg, n_seg * SSM_W), F32), jax.ShapeDtypeStruct((rows, LANE), F32)]
    if attn_operands:
        out_specs += [row_spec(2 * KV_W), row_spec(KT_COLS), pl.BlockSpec((VT_ROWS, tm), lambda i: (0, i))]
        out_shape += [jax.ShapeDtypeStruct((rows, 2 * KV_W), F32), jax.ShapeDtypeStruct((rows, KT_COLS), BF16),
                      jax.ShapeDtypeStruct((VT_ROWS, rows), BF16)]
    return pl.pallas_call(
        functools.partial(_inproj_body, q_scale=q_scale),
        grid=(rows // tm,),
        in_specs=[row_spec(D_MODEL), full(g_mix), full(w), full(g_q), full(g_ks), full(g_kw)],
        out_specs=out_specs,
        out_shape=out_shape,
        compiler_params=_cparams(("arbitrary",)),
        name="inproj",
    )(x, g_mix, w, g_q, g_ks, g_kw)


def _compress_rows(x_ref, wcat_ref, cpos, w2):
    n_ch = x_ref.shape[0] // CMP_STRIDE
    acc = jnp.zeros((n_ch, 2 * HEAD_DIM), F32)
    for p in range(CMP_STRIDE):
        xp = x_ref[pl.ds(p, n_ch, stride=CMP_STRIDE), :].astype(BF16)
        acc = acc + _dot(xp, wcat_ref[p])
    e_lo = acc[:, 0:HEAD_DIM]
    e_hi_next = pltpu.roll(acc[:, HEAD_DIM:2 * HEAD_DIM], n_ch - 1, 0)
    hid = _gelu(e_lo + e_hi_next + cpos)
    return _dot(hid.astype(BF16), w2)


def _pcompress_body(x_ref, wcat_ref, cpos_ref, w2_ref, gkc_ref, o_ref):
    s = pl.program_id(0)
    out = _compress_rows(x_ref, wcat_ref.at[0], cpos_ref[0, 0:1, :], w2_ref[0])
    o_ref[...] = jnp.where(s < N_KV, _rms(out, gkc_ref[...]), out)


def _pcompress(kv, wcat, cpos, w2, g_kc):
    rows = kv.shape[0]
    n_ch = rows // CMP_STRIDE
    return pl.pallas_call(
        _pcompress_body,
        grid=(2 * N_KV,),
        in_specs=[pl.BlockSpec((rows, HEAD_DIM), lambda s: (0, s)),
                  pl.BlockSpec((1, CMP_STRIDE, HEAD_DIM, 2 * HEAD_DIM), lambda s: (s // N_KV, 0, 0, 0)),
                  pl.BlockSpec((1, 8, HEAD_DIM), lambda s: (s // N_KV, 0, 0)),
                  pl.BlockSpec((1, HEAD_DIM, HEAD_DIM), lambda s: (s // N_KV, 0, 0)),
                  pl.BlockSpec((1, HEAD_DIM), lambda s: (0, 0))],
        out_specs=pl.BlockSpec((n_ch, HEAD_DIM), lambda s: (0, s)),
        out_shape=jax.ShapeDtypeStruct((n_ch, 2 * N_KV * HEAD_DIM), F32),
        compiler_params=_cparams(("arbitrary",)),
        name="pcompress",
    )(kv, wcat, cpos, w2, g_kc)


def _select_rounds(score_t, on_pick=None):
    n_j = score_t.shape[0]
    jio = lax.broadcasted_iota(I32, score_t.shape, 0)
    sel = jnp.zeros(score_t.shape, F32)
    sc = score_t
    for r in range(N_SEL):
        m = jnp.max(sc, axis=0, keepdims=True)
        idx = jnp.min(jnp.where(sc == m, jio, n_j), axis=0, keepdims=True)
        pick = jio == idx
        ok = m >= 0.0
        sel = jnp.where(pick & ok, 1.0, sel)
        sc = jnp.where(pick, -jnp.inf, sc)
        if on_pick is not None:
            on_pick(r, idx, ok)
    return sel


LOG2E = 1.4426950408889634


ONES_ROWS = 16
FAR_TILES = 8


def _col_max(tiles):
    m = jnp.max(tiles[0], axis=0, keepdims=True)
    for s in tiles[1:]:
        m = jnp.maximum(m, jnp.max(s, axis=0, keepdims=True))
    return m


def _lanes4(x):
    return jnp.concatenate([x] * Q_PER_KV, axis=1)


def _pattn_t_body(q_ref, gt_ref, kt_ref, vt_ref, kc_ref, vc_ref, vct_ref, rb_ref, bk0_ref, bk1_ref, bkc_ref, mft_ref,
                  gout_ref, o_ref, b0_s, b1_s, bc_s, bw4_s):
    i = pl.program_id(0)
    cols = Q_PER_KV * QB

    @pl.when(i == 0)
    def _tables():
        b_io = lax.broadcasted_iota(I32, (QB, QB), 0)
        a_io = lax.broadcasted_iota(I32, (QB, QB), 1)
        for hd in range(N_HEADS):
            t0 = _bias_lookup(bk0_ref[...], rb_ref, hd, shift=True) * LOG2E
            b0_s[hd] = jnp.where(a_io >= b_io, t0, NEG)
            b1_s[hd] = _bias_lookup(bk1_ref[...], rb_ref, hd, shift=True) * LOG2E
            bc_s[hd] = _bias_lookup(bkc_ref[...], rb_ref, hd, shift=True) * LOG2E
        bw4_s[...] = jnp.where(b_io >= a_io, 0.0, NEG)

    gt_t = gt_ref[...].T
    n_far = jnp.maximum(i - 1, 0) // FAR_TILES
    far_keys = FAR_TILES * QB

    def tab(ref, h):
        return jnp.concatenate([ref[Q_PER_KV * h + g] for g in range(Q_PER_KV)], axis=1)

    def v_aug(row0, start, n):
        return jnp.concatenate([vt_ref[row0:row0 + HEAD_DIM, pl.ds(start, n)], jnp.ones((ONES_ROWS, n), BF16)], axis=0)

    q_ts, qa_ts, o_cs, sel_st = [], [], [], []
    for h in range(N_KV):
        q_t = jnp.concatenate(
            [q_ref[:, (Q_PER_KV * h + g) * HEAD_DIM:(Q_PER_KV * h + g + 1) * HEAD_DIM].astype(F32).T.astype(BF16)
             for g in range(Q_PER_KV)], axis=1)
        hs = slice(h * HEAD_DIM, (h + 1) * HEAD_DIM)

        n_c = kc_ref.shape[0] - 2 * CMP_STRIDE
        near0 = pl.multiple_of(8 * i, 8)
        cf = lax.broadcasted_iota(I32, (n_c, QB), 0)
        ok_f = _lanes4((cf < 8 * i) & (cf >= CMP_STRIDE))
        s_f = jnp.where(ok_f, _dot(kc_ref[0:n_c, hs].astype(BF16), q_t), NEG)
        cn = lax.broadcasted_iota(I32, (32, cols), 0)
        a_n = lax.broadcasted_iota(I32, (32, cols), 1) % QB
        ok_n = (CMP_STRIDE * (cn - CMP_STRIDE) <= a_n - (CMP_LEN - 1)) & (cn + 8 * i >= CMP_STRIDE)
        s_n = jnp.where(ok_n, _dot(kc_ref[pl.ds(near0, 32), hs].astype(BF16), q_t) + tab(bc_s, h), NEG)
        m_c = _col_max([s_f, s_n])
        p_f = jnp.where(ok_f, jnp.exp2(s_f - m_c), 0.0)
        p_n = jnp.where(ok_n, jnp.exp2(s_n - m_c), 0.0)
        l_c = jnp.sum(p_f, axis=0, keepdims=True) + jnp.sum(p_n, axis=0, keepdims=True)
        inv_c = 1.0 / jnp.maximum(l_c, 1e-30)
        vc_near_t = vc_ref[pl.ds(near0, 32), hs].T.astype(BF16)
        o_c = (_dot(vct_ref[hs, 0:n_c].astype(BF16), p_f.astype(BF16)) + _dot(vc_near_t, p_n.astype(BF16))) * inv_c
        pn_f = p_f * inv_c
        pn_n = p_n * inv_c
        imp_f = pn_f[:, 0:QB] + pn_f[:, QB:2 * QB] + pn_f[:, 2 * QB:3 * QB] + pn_f[:, 3 * QB:4 * QB]
        imp_n = pn_n[:, 0:QB] + pn_n[:, QB:2 * QB] + pn_n[:, 2 * QB:3 * QB] + pn_n[:, 3 * QB:4 * QB]
        jn = lax.broadcasted_iota(I32, (QB, 32), 0)
        cc = lax.broadcasted_iota(I32, (QB, 32), 1) + 8 * i - CMP_STRIDE
        mnt = (((cc // CMP_PER_SEL) == jn) | (cc == CMP_PER_SEL * jn - 1)) & (cc >= 0)
        mnt = jnp.where(mnt, 1.0, 0.0).astype(BF16)
        imp_t = jnp.zeros((QB, QB), F32)
        for part in _split3(imp_f):
            imp_t = imp_t + _dot(mft_ref[...], part)
        for part in _split3(imp_n):
            imp_t = imp_t + _dot(mnt, part)
        j_io = lax.broadcasted_iota(I32, (QB, QB), 0)
        cur = 2 * i + lax.broadcasted_iota(I32, (QB, QB), 1) // SEL_BLOCK
        forced = (j_io == 0) | (j_io == cur) | (j_io == cur - 1)
        score_t = jnp.where(forced, 1e9, jnp.where(j_io <= cur, imp_t, -1.0))
        sel_t = _select_rounds(score_t)
        selneg = _lanes4(jnp.where(sel_t > 0.5, 0.0, NEG).astype(BF16))
        qa_t = jnp.concatenate([q_t, selneg] if h == 0 else [selneg, q_t], axis=0)
        ka0 = h * HEAD_DIM

        near_s, near_v = [], []
        for back in range(FAR_TILES + 1):
            kt = i - back
            start = pl.multiple_of(jnp.maximum(kt, 0) * QB, QB)
            s = _dot(kt_ref[pl.ds(start, QB), ka0:ka0 + 2 * HEAD_DIM], qa_t)
            if back == 0:
                s = s + tab(b0_s, h)
            elif back == 1:
                s = s + tab(b1_s, h) + jnp.where(kt >= 0, 0.0, NEG)
            else:
                s = s + jnp.where((kt >= 0) & (kt >= FAR_TILES * n_far), 0.0, NEG)
            near_s.append(s)
            near_v.append(v_aug(h * HEAD_DIM, start, QB))
        m_s = _col_max(near_s)
        acc_s = jnp.zeros((HEAD_DIM + ONES_ROWS, cols), F32)
        for s, v in zip(near_s, near_v):
            acc_s = acc_s + _dot(v, jnp.exp2(s - m_s).astype(BF16))
        q_ts.append(q_t)
        qa_ts.append(qa_t)
        o_cs.append(o_c)
        sel_st += [m_s, acc_s]

    def far_step(k, st):
        start = pl.multiple_of(k * far_keys, far_keys)
        out = []
        for h in range(N_KV):
            m, acc = st[2 * h], st[2 * h + 1]
            s = _dot(kt_ref[pl.ds(start, far_keys), h * HEAD_DIM:(h + 2) * HEAD_DIM], qa_ts[h])
            m_new = jnp.maximum(m, jnp.max(s, axis=0, keepdims=True))
            p = jnp.exp2(s - m_new).astype(BF16)
            out += [m_new, jnp.exp2(m - m_new) * acc + _dot(v_aug(h * HEAD_DIM, start, far_keys), p)]
        return tuple(out)

    sel_st = lax.fori_loop(0, n_far, far_step, tuple(sel_st))

    attn = [None] * N_HEADS
    for h in range(N_KV):
        q_t = q_ts[h]
        acc_s = sel_st[2 * h + 1]
        o_s = acc_s[0:HEAD_DIM] / acc_s[HEAD_DIM:HEAD_DIM + 1]
        o_c = o_cs[h]

        kw0 = KT_KW + h * HEAD_DIM
        win_s, win_v = [], []
        for back in range(5):
            kt = i - back
            start = pl.multiple_of(jnp.maximum(kt, 0) * QB, QB)
            s = _dot(kt_ref[pl.ds(start, QB), kw0:kw0 + HEAD_DIM], q_t)
            if back == 0:
                s = s + tab(b0_s, h)
            elif back == 1:
                s = s + tab(b1_s, h)
            elif back == 4:
                s = s + _lanes4(bw4_s[...])
            if back > 0:
                s = s + jnp.where(kt >= 0, 0.0, NEG)
            win_s.append(s)
            win_v.append(v_aug((N_KV + h) * HEAD_DIM, start, QB))
        m_w = _col_max(win_s)
        acc_w = jnp.zeros((HEAD_DIM + ONES_ROWS, cols), F32)
        for s, v in zip(win_s, win_v):
            acc_w = acc_w + _dot(v, jnp.exp2(s - m_w).astype(BF16))
        o_w = acc_w[0:HEAD_DIM] / acc_w[HEAD_DIM:HEAD_DIM + 1]

        for g in range(Q_PER_KV):
            hd = Q_PER_KV * h + g
            cs = slice(g * QB, (g + 1) * QB)
            o_t = (gt_t[hd:hd + 1] * o_c[:, cs] + gt_t[N_HEADS + hd:N_HEADS + hd + 1] * o_s[:, cs]
                   + gt_t[2 * N_HEADS + hd:2 * N_HEADS + hd + 1] * o_w[:, cs])
            attn[hd] = o_t.T
    a = jnp.concatenate(attn, axis=1)
    o_ref[...] = _rms(a, gout_ref[...]).astype(BF16)


def _pattn_t_tables():
    b = np.arange(QB)[:, None]
    a = np.arange(QB)[None, :]
    bk0 = _bucket_np(a - b)
    bk1 = _bucket_np(a - b + QB)
    c = np.arange(32)[:, None] - CMP_STRIDE
    bkc = _bucket_np(a - CMP_STRIDE * c - (CMP_LEN - 1))
    cidx = np.arange(4 * QB)[None, :] - CMP_STRIDE
    j = np.arange(QB)[:, None]
    mft = (((cidx // CMP_PER_SEL) == j) | (cidx == CMP_PER_SEL * j - 1)) & (cidx >= 0)
    return (jnp.asarray(bk0), jnp.asarray(bk1), jnp.asarray(bkc), jnp.asarray(mft.astype(np.float32), dtype=BF16))


def _pattn_t(q, gates, kt, vt, kcp, vcp, vcpt, rel_bias, g_out):
    rows = q.shape[0]
    bk0, bk1, bkc, mft = _pattn_t_tables()
    full = lambda a: pl.BlockSpec(a.shape, lambda i: (0,) * a.ndim)
    once = lambda a: pl.BlockSpec(a.shape, lambda i: (0,) * a.ndim, pipeline_mode=pl.Buffered(1))
    return pl.pallas_call(
        _pattn_t_body,
        grid=(rows // QB,),
        in_specs=[pl.BlockSpec((QB, ATTN_W), lambda i: (i, 0)), pl.BlockSpec((QB, LANE), lambda i: (i, 0)),
                  once(kt), once(vt), full(kcp), full(vcp), full(vcpt), pl.BlockSpec(memory_space=pltpu.SMEM),
                  full(bk0), full(bk1), full(bkc), full(mft), full(g_out)],
        out_specs=pl.BlockSpec((QB, ATTN_W), lambda i: (i, 0)),
        out_shape=jax.ShapeDtypeStruct((rows, ATTN_W), BF16),
        scratch_shapes=[pltpu.VMEM((N_HEADS, QB, QB), F32), pltpu.VMEM((N_HEADS, QB, QB), F32),
                        pltpu.VMEM((N_HEADS, 32, QB), F32), pltpu.VMEM((QB, QB), F32)],
        compiler_params=_cparams(("arbitrary",)),
        name="pattn",
    )(q, gates, kt, vt, kcp, vcp, vcpt, rel_bias, bk0, bk1, bkc, mft, g_out)


def _ssm_body(u_ref, ire_ref, iim_ref, wb_ref, wc_ref, abre_ref, abim_ref, dsk_ref, wglu_ref, gout_ref,
              y_ref, fre_ref, fim_ref, u_s, xre_s, xim_s, y_s, sre_s, sim_s, *, n_seg, tc, emit):
    c = pl.program_id(0)
    sgl = SSM_LANES // SSM_SG

    @pl.when(c == 0)
    def _init():
        sre_s[...] = ire_ref[...]
        sim_s[...] = iim_ref[...]

    n_lt = SSM_W // LANE
    if tc == 1:
        u_cols = [u_ref[:, l * LANE:(l + 1) * LANE] for l in range(n_lt)]
    else:
        for s in range(n_seg):
            for l in range(n_lt):
                u_s[l, pl.ds(s, tc, stride=n_seg), :] = u_ref[:, s * SSM_W + l * LANE:s * SSM_W + (l + 1) * LANE]
        u_cols = [u_s[l] for l in range(n_lt)]
    for sg in range(SSM_SG):
        ls = slice(sg * sgl, (sg + 1) * sgl)
        bu = _dot(u_cols[sg].astype(BF16), wb_ref[sg])
        xre_s[:, ls] = bu[:, 0:sgl]
        xim_s[:, ls] = bu[:, sgl:2 * sgl]
        ar = abre_ref[:, ls]
        ai = abim_ref[:, ls]

        def step(t, carry):
            xr, xi = carry
            rows = pl.ds(pl.multiple_of(t * n_seg, n_seg), n_seg)
            nr = ar * xr - ai * xi + xre_s[rows, ls]
            ni = ar * xi + ai * xr + xim_s[rows, ls]
            xre_s[rows, ls] = nr
            xim_s[rows, ls] = ni
            return nr, ni

        xr, xi = lax.fori_loop(0, tc, step, (sre_s[:, ls], sim_s[:, ls]), unroll=min(tc, 4))
        sre_s[:, ls] = xr
        sim_s[:, ls] = xi
    fre_ref[...] = sre_s[...]
    fim_ref[...] = sim_s[...]
    if not emit:
        y_ref[...] = jnp.zeros(y_ref.shape, y_ref.dtype)
        return
    ys = []
    for sg in range(SSM_SG):
        ls = slice(sg * sgl, (sg + 1) * sgl)
        x2 = jnp.concatenate([xre_s[:, ls], xim_s[:, ls]], axis=1).astype(BF16)
        ys.append(_dot(x2, wc_ref[sg]))
    y = jnp.concatenate(ys, axis=1) + dsk_ref[...] * jnp.concatenate(u_cols, axis=1)
    z = _dot(_gelu(y).astype(BF16), wglu_ref[...])
    o = z[:, 0:SSM_W] * jax.nn.sigmoid(z[:, SSM_W:2 * SSM_W])
    yn = _rms(o, gout_ref[...])
    if tc == 1:
        y_ref[...] = yn.astype(BF16)
    else:
        for l in range(n_lt):
            y_s[l] = yn[:, l * LANE:(l + 1) * LANE]
        for s in range(n_seg):
            for l in range(n_lt):
                y_ref[:, s * SSM_W + l * LANE:s * SSM_W + (l + 1) * LANE] = (
                    y_s[l, pl.ds(s, tc, stride=n_seg), :].astype(BF16))


def _ssm(u2, n_seg, init_re, init_im, wb, wc, ab_re, ab_im, d_skip, w_glu, g_out, tc, emit):
    if tc == 1:
        assert u2.shape == (n_seg, SSM_W)
        t_len = 1
        blk = (n_seg, SSM_W)
    else:
        t_len = u2.shape[0]
        assert u2.shape[1] == n_seg * SSM_W and n_seg % 8 == 0
        blk = (tc, n_seg * SSM_W)
    rows = tc * n_seg
    full = lambda a: pl.BlockSpec(a.shape, lambda c: (0,) * a.ndim)
    body = functools.partial(_ssm_body, n_seg=n_seg, tc=tc, emit=emit)
    st = jax.ShapeDtypeStruct((n_seg, SSM_LANES), F32)
    y_shape = u2.shape if emit else blk
    y_map = (lambda c: (c, 0)) if emit else (lambda c: (0, 0))
    return pl.pallas_call(
        body,
        grid=(t_len // tc,),
        in_specs=[pl.BlockSpec(blk, lambda c: (c, 0)),
                  full(init_re), full(init_im), full(wb), full(wc),
                  full(ab_re), full(ab_im), full(d_skip), full(w_glu), full(g_out)],
        out_specs=[pl.BlockSpec(blk, y_map), full(init_re), full(init_im)],
        out_shape=[jax.ShapeDtypeStruct(y_shape, BF16), st, st],
        scratch_shapes=[pltpu.VMEM((SSM_W // LANE, rows, LANE), F32), pltpu.VMEM((rows, SSM_LANES), F32),
                        pltpu.VMEM((rows, SSM_LANES), F32), pltpu.VMEM((SSM_W // LANE, rows, LANE), F32),
                        pltpu.VMEM((n_seg, SSM_LANES), F32), pltpu.VMEM((n_seg, SSM_LANES), F32)],
        compiler_params=_cparams(("arbitrary",)),
        name="ssm_emit" if emit else "ssm_final",
    )(u2, init_re, init_im, wb, wc, ab_re, ab_im, d_skip, w_glu, g_out)


def _ssm_chain_body(fre_ref, fim_ref, abre_ref, abim_ref, ire_ref, iim_ref, *, n_seg, log2_len):
    pr = abre_ref[...]
    pi = abim_ref[...]
    for _ in range(log2_len):
        pr, pi = pr * pr - pi * pi, 2.0 * pr * pi
    cr = jnp.zeros((1, SSM_LANES), F32)
    ci = jnp.zeros((1, SSM_LANES), F32)
    for j in range(n_seg):
        ire_ref[j:j + 1, :] = cr
        iim_ref[j:j + 1, :] = ci
        fr = fre_ref[j:j + 1, :]
        fi = fim_ref[j:j + 1, :]
        cr, ci = fr + pr * cr - pi * ci, fi + pr * ci + pi * cr


def _ssm_chain(fre, fim, ab_re, ab_im, seg_len):
    n_seg = fre.shape[0]
    log2_len = int(math.log2(seg_len))
    assert 2 ** log2_len == seg_len
    st = jax.ShapeDtypeStruct((n_seg, SSM_LANES), F32)
    return pl.pallas_call(
        functools.partial(_ssm_chain_body, n_seg=n_seg, log2_len=log2_len),
        out_shape=[st, st],
        compiler_params=pltpu.CompilerParams(vmem_limit_bytes=VMEM_LIMIT),
        name="ssm_chain",
    )(fre, fim, ab_re, ab_im)


FFN_CHUNKS = 2


def _ffn_body(x_ref, a_ref, s_ref, wo_ref, gffn_ref, wa_ref, wg_ref, wd_ref, cw_ref, cb_ref, p2_ref, p1_ref,
              y_ref, cnew_ref, hn_s, car_s, *, seq, tm):
    r = pl.program_id(0)
    j = pl.program_id(1)

    @pl.when(j == 0)
    def _mix():
        h = x_ref[...] + _dot(a_ref[...], wo_ref[0:ATTN_W, :]) + _dot(s_ref[...], wo_ref[ATTN_W:D_MODEL, :])
        y_ref[...] = h
        hn_s[...] = _rms(h, gffn_ref[...]).astype(BF16)

    if seq:
        @pl.when(r == 0)
        def _first():
            car_s[j, 6:7, :] = p2_ref[...]
            car_s[j, 7:8, :] = p1_ref[...]

    hn = hn_s[...]
    tf = wa_ref.shape[1]
    tc = tf // FFN_CHUNKS
    part = None
    for k in range(FFN_CHUNKS):
        cs = slice(k * tc, (k + 1) * tc)
        a = _dot(hn, wa_ref[:, cs])
        g = _dot(hn, wg_ref[:, cs])
        if seq:
            p2 = car_s[j, 6:7, cs]
            p1 = car_s[j, 7:8, cs]
            row = lax.broadcasted_iota(I32, a.shape, 0)
            a1 = jnp.where(row == 0, p1, pltpu.roll(a, 1, 0))
            a2 = jnp.where(row == 0, p2, jnp.where(row == 1, p1, pltpu.roll(a, 2, 0)))
            car_s[j, :, cs] = a[tm - 8:tm, :]
            cnew_ref[:, pl.ds(pl.multiple_of(j * tf + k * tc, tc), tc)] = a[tm - 8:tm, :]
        else:
            a1 = p1_ref[:, cs]
            a2 = p2_ref[:, cs]
            cnew_ref[:, cs] = a
        c = cb_ref[:, cs] + cw_ref[0:1, cs] * a2 + cw_ref[1:2, cs] * a1 + cw_ref[2:3, cs] * a
        d = _dot((_gelu(c) * g).astype(BF16), wd_ref[cs, :])
        part = d if part is None else part + d
    y_ref[...] += part


def _ffn(x, a_n, s_n, w_out, g_ffn, wa, wg, wd, cw, cb, p2, p1, tm, tf, seq, n_seg=1):
    rows = x.shape[0]
    nj = D_FF_PAD // tf
    body = functools.partial(_ffn_body, seq=seq, tm=tm)
    tiles_per_seg = rows // n_seg // tm
    if seq:
        tap_spec = pl.BlockSpec((1, tf), lambda r, j: (0, j))
        cnew_spec = pl.BlockSpec((8, D_FF_PAD), lambda r, j: (0, 0))
        cnew_shape = jax.ShapeDtypeStruct((8, D_FF_PAD), F32)
    else:
        tap_spec = pl.BlockSpec((tm, tf), lambda r, j: (r, j))
        cnew_spec = pl.BlockSpec((tm, tf), lambda r, j: (r, j))
        cnew_shape = jax.ShapeDtypeStruct((rows, D_FF_PAD), F32)
    return pl.pallas_call(
        body,
        grid=(rows // tm, nj),
        in_specs=[pl.BlockSpec((tm, D_MODEL), lambda r, j: (r, 0)), pl.BlockSpec((tm, ATTN_W), lambda r, j: (r, 0)),
                  pl.BlockSpec((tm, SSM_W), lambda r, j: (r % tiles_per_seg, r // tiles_per_seg)),
                  pl.BlockSpec(w_out.shape, lambda r, j: (0, 0), pipeline_mode=pl.Buffered(1)),
                  pl.BlockSpec((1, D_MODEL), lambda r, j: (0, 0)),
                  pl.BlockSpec((D_MODEL, tf), lambda r, j: (0, j)), pl.BlockSpec((D_MODEL, tf), lambda r, j: (0, j)),
                  pl.BlockSpec((tf, D_MODEL), lambda r, j: (j, 0)), pl.BlockSpec((CONV_W, tf), lambda r, j: (0, j)),
                  pl.BlockSpec((1, tf), lambda r, j: (0, j)), tap_spec, tap_spec],
        out_specs=[pl.BlockSpec((tm, D_MODEL), lambda r, j: (r, 0)), cnew_spec],
        out_shape=[jax.ShapeDtypeStruct((rows, D_MODEL), F32), cnew_shape],
        scratch_shapes=[pltpu.VMEM((tm, D_MODEL), BF16), pltpu.VMEM((nj, 8, tf), F32)],
        compiler_params=_cparams(("arbitrary", "arbitrary")),
        name="ffn_seq" if seq else "ffn_rows",
    )(x, a_n, s_n, w_out, g_ffn, wa, wg, wd, cw, cb, p2, p1)


def _s1_copies(pt_ref, cache_ref, x_s, sem, b, slot, n_pages):
    cps = []
    for pg in range(n_pages):
        page = pt_ref[b * n_pages + pg]
        for s in range(2 * N_KV):
            cps.append(pltpu.make_async_copy(cache_ref.at[page, :, s, :],
                                             x_s.at[slot, s, pl.ds(pg * PAGE, PAGE), :], sem.at[slot]))
    return cps


def _s1_body(pt_ref, cache_ref, q_ref, wcat_ref, cpos_ref, w2_ref, gkc_ref, rb_ref, bkc_ref, mt_ref,
             oc_ref, idx_ref, val_ref, x_s, bias_s, sem, *, n_pages, past):
    b = pl.program_id(0)
    nb = pl.num_programs(0)
    slot = b % 2

    @pl.when(b == 0)
    def _first():
        for cp in _s1_copies(pt_ref, cache_ref, x_s, sem, 0, 0, n_pages):
            cp.start()
        for hd in range(N_HEADS):
            bias_s[hd:hd + 1, :] = _bias_lookup(bkc_ref[...], rb_ref, hd)

    @pl.when(b + 1 < nb)
    def _next():
        for cp in _s1_copies(pt_ref, cache_ref, x_s, sem, b + 1, 1 - slot, n_pages):
            cp.start()

    for cp in _s1_copies(pt_ref, cache_ref, x_s, sem, b, slot, n_pages):
        cp.wait()

    n_c = past // CMP_STRIDE
    cio = lax.broadcasted_iota(I32, (N_HEADS, n_c), 1)
    hrow = lax.broadcasted_iota(I32, (N_HEADS, n_c), 0) // Q_PER_KV
    q = q_ref[0]
    s_all = jnp.zeros((N_HEADS, n_c), F32)
    vcs = []
    for h in range(N_KV):
        kc = _rms(_compress_rows(x_s.at[slot, h], wcat_ref.at[0], cpos_ref[0, 0:1, :], w2_ref[0]), gkc_ref[...])
        vcs.append(_compress_rows(x_s.at[slot, N_KV + h], wcat_ref.at[1], cpos_ref[1, 0:1, :], w2_ref[1]).astype(BF16))
        s_all = jnp.where(hrow == h, _dot_nt(q, kc.astype(BF16)), s_all)
    ok = cio < n_c - 1
    s_all = jnp.where(ok, s_all + bias_s[...], NEG)
    m = jnp.max(s_all, axis=-1, keepdims=True)
    p = jnp.where(ok, jnp.exp(s_all - m), 0.0)
    p = p / jnp.maximum(jnp.sum(p, axis=-1, keepdims=True), 1e-30)
    pb = p.astype(BF16)
    hrow_o = lax.broadcasted_iota(I32, (N_HEADS, HEAD_DIM), 0) // Q_PER_KV
    o_c = jnp.zeros((N_HEADS, HEAD_DIM), F32)
    for h in range(N_KV):
        o_c = jnp.where(hrow_o == h, _dot(pb, vcs[h]), o_c)
    oc_ref[0] = o_c
    rio = lax.broadcasted_iota(I32, (8, n_c), 0)
    imp = jnp.zeros((8, n_c), F32)
    for h in range(N_KV):
        ih = p[4 * h:4 * h + 1] + p[4 * h + 1:4 * h + 2] + p[4 * h + 2:4 * h + 3] + p[4 * h + 3:4 * h + 4]
        imp = jnp.where(rio == h, ih, imp)
    imp = jnp.concatenate([imp, jnp.zeros((LANE - 8, n_c), F32)], axis=0)
    n_j = mt_ref.shape[0]
    imp_t = jnp.zeros((n_j, LANE), F32)
    for part in _split3(imp):
        imp_t = imp_t + _dot_nt(mt_ref[...], part)
    j_io = lax.broadcasted_iota(I32, (n_j, LANE), 0)
    cur = past // SEL_BLOCK
    forced = (j_io == 0) | (j_io == cur) | (j_io == cur - 1)
    score_t = jnp.where(forced, 1e9, jnp.where(j_io <= cur, imp_t, -1.0))
    score_t = jnp.where(j_io <= cur, score_t, -jnp.inf)

    def on_pick(r, idx, okv):
        idx_ref[0, r:r + 1, :] = idx
        val_ref[0, r:r + 1, :] = jnp.where(okv, 1, 0)

    _select_rounds(score_t, on_pick)


def _s1(page_table, cache4, q3, wcat, cpos, w2, g_kc, rel_bias):
    n_b, n_pages = page_table.shape
    past = n_pages * PAGE
    n_c = past // CMP_STRIDE
    ns = past // SEL_BLOCK + 1
    n_j = -(-ns // 8) * 8
    c = np.arange(n_c)[None, :]
    bkc = _bucket_np(past - (CMP_STRIDE * c + CMP_LEN - 1))
    j = np.arange(n_j)[:, None]
    mt = (((c // CMP_PER_SEL) == j) | (c == CMP_PER_SEL * j - 1)) & (c < n_c - 1)
    mt = jnp.asarray(mt.astype(np.float32), dtype=BF16)
    full = lambda a: pl.BlockSpec(a.shape, lambda b, pt: (0,) * a.ndim)
    body = functools.partial(_s1_body, n_pages=n_pages, past=past)
    return pl.pallas_call(
        body,
        grid_spec=pltpu.PrefetchScalarGridSpec(
            num_scalar_prefetch=1,
            grid=(n_b,),
            in_specs=[pl.BlockSpec(memory_space=pl.ANY), pl.BlockSpec((1, N_HEADS, HEAD_DIM), lambda b, pt: (b, 0, 0)),
                      full(wcat), full(cpos), full(w2), full(g_kc), pl.BlockSpec(memory_space=pltpu.SMEM),
                      pl.BlockSpec((1, n_c), lambda b, pt: (0, 0)), full(mt)],
            out_specs=[pl.BlockSpec((1, N_HEADS, HEAD_DIM), lambda b, pt: (b, 0, 0)),
                       pl.BlockSpec((1, N_SEL, LANE), lambda b, pt: (b, 0, 0)),
                       pl.BlockSpec((1, N_SEL, LANE), lambda b, pt: (b, 0, 0))],
            scratch_shapes=[pltpu.VMEM((2, 2 * N_KV, past, HEAD_DIM), F32), pltpu.VMEM((N_HEADS, n_c), F32),
                            pltpu.SemaphoreType.DMA((2,))],
        ),
        out_shape=[jax.ShapeDtypeStruct((n_b, N_HEADS, HEAD_DIM), F32), jax.ShapeDtypeStruct((n_b, N_SEL, LANE), I32),
                   jax.ShapeDtypeStruct((n_b, N_SEL, LANE), I32)],
        compiler_params=_cparams(("arbitrary",)),
        name="sample_cmp",
    )(page_table.reshape(-1), cache4, q3, wcat, cpos, w2, g_kc, rel_bias, jnp.asarray(bkc), mt)


def _s2_copies(idx_ref, pt_ref, cache_ref, cwin_ref, ks_s, vs_s, kw_s, vw_s, sem, b, slot, n_pages):
    cps = []
    n_blk = n_pages * (PAGE // SEL_BLOCK)
    for h in range(N_KV):
        for r in range(N_SEL):
            jb = jnp.minimum(idx_ref[(b * N_SEL + r) * N_KV + h], n_blk - 1)
            page = pt_ref[b * n_pages + jb // 2]
            row0 = pl.multiple_of((jb % 2) * SEL_BLOCK, SEL_BLOCK)
            cps.append(pltpu.make_async_copy(cache_ref.at[page, pl.ds(row0, SEL_BLOCK), 2 * N_KV + h, :],
                                             ks_s.at[slot, h, pl.ds(r * SEL_BLOCK, SEL_BLOCK), :], sem.at[slot]))
            cps.append(pltpu.make_async_copy(cache_ref.at[page, pl.ds(row0, SEL_BLOCK), 3 * N_KV + h, :],
                                             vs_s.at[slot, h, pl.ds(r * SEL_BLOCK, SEL_BLOCK), :], sem.at[slot]))
        cps.append(pltpu.make_async_copy(cwin_ref.at[b, :, h, :], kw_s.at[slot, h], sem.at[slot]))
        cps.append(pltpu.make_async_copy(cwin_ref.at[b, :, N_KV + h, :], vw_s.at[slot, h], sem.at[slot]))
    return cps


def _s2_body(idx_ref, val_ref, pt_ref, cache_ref, cwin_ref, q_ref, oc_ref, gt_ref, kns_ref, vns_ref, knw_ref, vnw_ref,
             rb_ref, bks_ref, bkw_ref, gout_ref, o_ref, ks_s, vs_s, kw_s, vw_s, bs_s, bw_s, sem, *, n_pages, past):
    b = pl.program_id(0)
    nb = pl.num_programs(0)
    slot = b % 2
    args = (idx_ref, pt_ref, cache_ref, cwin_ref, ks_s, vs_s, kw_s, vw_s, sem)

    @pl.when(b == 0)
    def _first():
        for cp in _s2_copies(*args, 0, 0, n_pages):
            cp.start()
        for hd in range(N_HEADS):
            bs_s[hd:hd + 1, :] = _bias_lookup(bks_ref[...], rb_ref, hd)
            bw_s[hd:hd + 1, :] = _bias_lookup(bkw_ref[...], rb_ref, hd)

    @pl.when(b + 1 < nb)
    def _next():
        for cp in _s2_copies(*args, b + 1, 1 - slot, n_pages):
            cp.start()

    for cp in _s2_copies(*args, b, slot, n_pages):
        cp.wait()

    n_blk = n_pages * (PAGE // SEL_BLOCK)
    q = q_ref[0]
    qf = q.astype(F32)
    hrow = lax.broadcasted_iota(I32, (N_HEADS, 1), 0) // Q_PER_KV
    lane = lax.broadcasted_iota(I32, (N_HEADS, LANE), 1)
    bias0 = jnp.concatenate([jnp.full((1, 1), rb_ref[0, hd], F32) for hd in range(N_HEADS)], axis=0)
    b31 = jnp.concatenate([jnp.full((1, 1), rb_ref[REL_BUCKETS - 1, hd], F32) for hd in range(N_HEADS)], axis=0)

    tiles = []
    new_sel = jnp.zeros((N_HEADS, 1), F32)
    for t in range(N_SEL // 2):
        s_t = jnp.zeros((N_HEADS, LANE), F32)
        for h in range(N_KV):
            s_h = _dot_nt(q, ks_s[slot, h, pl.ds(t * LANE, LANE), :].astype(BF16))
            halves = []
            for half in range(2):
                r = 2 * t + half
                jb = idx_ref[(b * N_SEL + r) * N_KV + h]
                okr = (val_ref[(b * N_SEL + r) * N_KV + h] > 0) & (jb < n_blk)
                near = bs_s[:, (half * 2) * LANE:(half * 2 + 1) * LANE]
                nearer = bs_s[:, (half * 2 + 1) * LANE:(half * 2 + 2) * LANE]
                bias = jnp.where(jb == n_blk - 1, nearer, jnp.where(jb == n_blk - 2, near, b31))
                halves.append(jnp.where(okr, s_h + bias, NEG))
                new_sel = jnp.where((hrow == h) & (val_ref[(b * N_SEL + r) * N_KV + h] > 0) & (jb == n_blk), 1.0, new_sel)
            s_h = jnp.where(lane < SEL_BLOCK, halves[0], halves[1])
            s_t = jnp.where(hrow == h, s_h, s_t)
        tiles.append(s_t)
    s_new = jnp.sum(qf * kns_ref[0], axis=-1, keepdims=True) + bias0
    s_new = jnp.where(new_sel > 0.5, s_new, NEG)
    m = s_new
    for s_t in tiles:
        m = jnp.maximum(m, jnp.max(s_t, axis=-1, keepdims=True))
    p_new = jnp.where(new_sel > 0.5, jnp.exp(s_new - m), 0.0)
    l = p_new
    acc = p_new * vns_ref[0]
    for t, s_t in enumerate(tiles):
        p = jnp.where(s_t > 0.5 * NEG, jnp.exp(s_t - m), 0.0)
        l = l + jnp.sum(p, axis=-1, keepdims=True)
        pb = p.astype(BF16)
        for h in range(N_KV):
            pv = _dot(pb, vs_s[slot, h, pl.ds(t * LANE, LANE), :].astype(BF16))
            acc = acc + jnp.where(hrow == h, pv, 0.0)
    o_s = acc / jnp.maximum(l, 1e-30)

    n_w = kw_s.shape[2]
    wt = []
    for t in range(n_w // LANE):
        s_t = jnp.zeros((N_HEADS, LANE), F32)
        for h in range(N_KV):
            s_h = _dot_nt(q, kw_s[slot, h, pl.ds(t * LANE, LANE), :].astype(BF16))
            s_t = jnp.where(hrow == h, s_h, s_t)
        wt.append(s_t + bw_s[:, t * LANE:(t + 1) * LANE])
    s_new = jnp.sum(qf * knw_ref[0], axis=-1, keepdims=True) + bias0
    m = s_new
    for s_t in wt:
        m = jnp.maximum(m, jnp.max(s_t, axis=-1, keepdims=True))
    p_new = jnp.exp(s_new - m)
    l = p_new
    acc = p_new * vnw_ref[0]
    for t, s_t in enumerate(wt):
        p = jnp.exp(s_t - m)
        l = l + jnp.sum(p, axis=-1, keepdims=True)
        pb = p.astype(BF16)
        for h in range(N_KV):
            pv = _dot(pb, vw_s[slot, h, pl.ds(t * LANE, LANE), :].astype(BF16))
            acc = acc + jnp.where(hrow == h, pv, 0.0)
    o_w = acc / l

    gt = gt_ref[0]
    a = gt[:, 0:1] * oc_ref[0] + gt[:, 1:2] * o_s + gt[:, 2:3] * o_w
    ms = jnp.sum(jnp.sum(a * a, axis=-1, keepdims=True), axis=0, keepdims=True) / (N_HEADS * HEAD_DIM)
    o_ref[0] = (a * lax.rsqrt(ms + EPS) * gout_ref[...]).astype(BF16)


def _s2(idx, val, page_table, cache4, cwin4, q3, o_c, gates3, kns, vns, knw, vnw, rel_bias, g_out3):
    n_b, n_pages = page_table.shape
    past = n_pages * PAGE
    n_w = cwin4.shape[1]
    s = np.arange(SEL_BLOCK)
    d_near = past - ((past // SEL_BLOCK - 2) * SEL_BLOCK + s)
    d_nearer = past - ((past // SEL_BLOCK - 1) * SEL_BLOCK + s)
    z = np.zeros(SEL_BLOCK, np.int64)
    bks = np.concatenate([d_near, z, d_nearer, z, z, d_near, z, d_nearer])[None, :]
    bkw = (past - (past - n_w + np.arange(n_w)))[None, :]
    full = lambda a: pl.BlockSpec(a.shape, lambda b, *_: (0,) * a.ndim)
    per_b = lambda a: pl.BlockSpec((1,) + a.shape[1:], lambda b, *_: (b,) + (0,) * (a.ndim - 1))
    body = functools.partial(_s2_body, n_pages=n_pages, past=past)
    return pl.pallas_call(
        body,
        grid_spec=pltpu.PrefetchScalarGridSpec(
            num_scalar_prefetch=3,
            grid=(n_b,),
            in_specs=[pl.BlockSpec(memory_space=pl.ANY), pl.BlockSpec(memory_space=pl.ANY),
                      per_b(q3), per_b(o_c), per_b(gates3), per_b(kns), per_b(vns), per_b(knw), per_b(vnw),
                      pl.BlockSpec(memory_space=pltpu.SMEM), pl.BlockSpec((1, 4 * LANE), lambda b, *_: (0, 0)),
                      pl.BlockSpec((1, n_w), lambda b, *_: (0, 0)), full(g_out3)],
            out_specs=pl.BlockSpec((1, N_HEADS, HEAD_DIM), lambda b, *_: (b, 0, 0)),
            scratch_shapes=[pltpu.VMEM((2, N_KV, N_SEL * SEL_BLOCK, HEAD_DIM), F32),
                            pltpu.VMEM((2, N_KV, N_SEL * SEL_BLOCK, HEAD_DIM), F32),
                            pltpu.VMEM((2, N_KV, n_w, HEAD_DIM), F32), pltpu.VMEM((2, N_KV, n_w, HEAD_DIM), F32),
                            pltpu.VMEM((N_HEADS, 4 * LANE), F32), pltpu.VMEM((N_HEADS, n_w), F32),
                            pltpu.SemaphoreType.DMA((2,))],
        ),
        out_shape=jax.ShapeDtypeStruct((n_b, N_HEADS, HEAD_DIM), BF16),
        compiler_params=_cparams(("arbitrary",)),
        name="sample_attn",
    )(idx, val, page_table.reshape(-1), cache4, cwin4, q3, o_c, gates3, kns, vns, knw, vnw, rel_bias,
      jnp.asarray(_bucket_np(bks)), jnp.asarray(_bucket_np(bkw)), g_out3)


def _winshift_body(cwin_ref, new_ref, o_ref):
    n_w = cwin_ref.shape[1]
    o_ref[0, 0:n_w - 1] = cwin_ref[0, 1:n_w]
    o_ref[0, n_w - 1] = new_ref[0]


def _winshift(cwin4, new3):
    n_b = cwin4.shape[0]
    return pl.pallas_call(
        _winshift_body,
        grid=(n_b,),
        in_specs=[pl.BlockSpec((1,) + cwin4.shape[1:], lambda b: (b, 0, 0, 0)),
                  pl.BlockSpec((1,) + new3.shape[1:], lambda b: (b, 0, 0))],
        out_specs=pl.BlockSpec((1,) + cwin4.shape[1:], lambda b: (b, 0, 0, 0)),
        out_shape=jax.ShapeDtypeStruct(cwin4.shape, F32),
        compiler_params=_cparams(("arbitrary",)),
        name="winshift",
    )(cwin4, new3)


def _block_diag_b(bb_re, bb_im):
    gl = SSM_G // SSM_SG
    eye = jnp.eye(gl, dtype=F32)

    def one(bb):
        t = jnp.transpose(bb, (1, 0, 2)).reshape(SSM_SG, gl, SSM_P, SSM_N)
        return jnp.einsum('sgpn,gh->sgphn', t, eye).reshape(SSM_SG, gl * SSM_P, gl * SSM_N)

    return jnp.concatenate([one(bb_re), one(bb_im)], axis=2).astype(BF16)


def _block_diag_c(c_re, c_im):
    gl = SSM_G // SSM_SG
    eye = jnp.eye(gl, dtype=F32)

    def one(c):
        t = jnp.transpose(c, (0, 2, 1)).reshape(SSM_SG, gl, SSM_N, SSM_P)
        return jnp.einsum('sgnp,gh->sgnhp', t, eye).reshape(SSM_SG, gl * SSM_N, gl * SSM_P)

    return jnp.concatenate([one(c_re), -one(c_im)], axis=1).astype(BF16)


def kernel(x_prompt, x_sample, cache_kv, page_table, cache_win, state_ssm, state_conv, rel_bias, g_mix, w_in, g_q, g_k, w_cmp1, pos_cmp, w_cmp2, lam_re, lam_im, log_dt, b_re, b_im, c_re, c_im, d_skip, w_glu, g_out_attn, g_out_ssm, w_out, g_ffn, w_up, conv_w, conv_b, w_down):
    depth = g_mix.shape[0]
    assert depth == 1 and x_prompt.shape[0] == 1 and x_sample.shape[1] == 1
    seq = x_prompt.shape[1]
    n_b = x_sample.shape[0]
    n_pages = page_table.shape[1]
    n_w = cache_win.shape[2]
    li = 0
    row = lambda v: v.reshape(1, -1)

    wi = w_in[li]
    c_g = ATTN_W + 6 * KV_W
    w_in_b = jnp.concatenate([wi[:, :c_g], wi[:, c_g + N_BRANCH * N_HEADS:], wi[:, c_g:c_g + N_BRANCH * N_HEADS],
                              jnp.zeros((D_MODEL, LANE - N_BRANCH * N_HEADS), F32)], axis=1).astype(BF16)
    w1 = w_cmp1[li]
    wcat = jnp.concatenate([w1[:, :CMP_STRIDE], w1[:, CMP_STRIDE:]], axis=-1).astype(BF16)
    w2 = w_cmp2[li].astype(BF16)
    w_glu_b = w_glu[li].astype(BF16)
    w_out_b = w_out[li].astype(BF16)
    padc = lambda a: jnp.pad(a, ((0, 0), (0, D_FF_PAD - D_FF)))
    wa = padc(w_up[li][:, :D_FF]).astype(BF16)
    wg = padc(w_up[li][:, D_FF:]).astype(BF16)
    wd = jnp.pad(w_down[li], ((0, D_FF_PAD - D_FF), (0, 0))).astype(BF16)
    cw = padc(conv_w[li])
    cb = padc(row(conv_b[li]))

    ab_re, ab_im, bb_re, bb_im, cpos = _prep(lam_re[li], lam_im[li], log_dt[li], b_re[li], b_im[li], pos_cmp[li], w1)
    wb = _block_diag_b(bb_re, bb_im)
    wc = _block_diag_c(c_re[li], c_im[li])
    ab_re = row(ab_re)
    ab_im = row(ab_im)

    in_args = (row(g_mix[li]), w_in_b, row(g_q[li]), row(g_k[li, 1]), row(g_k[li, 2]))
    ssm_args = (wb, wc, ab_re, ab_im, row(d_skip[li]), w_glu_b, row(g_out_ssm[li]))

    xp = x_prompt[0]
    n_seg = 8
    q, kv, win, u2, gates, cmp_rows, kt, vt = _inproj(xp, *in_args, tm=512, q_scale=HEAD_DIM ** -0.5 * LOG2E, n_seg=n_seg,
                                            attn_operands=True)
    cmp_out = _pcompress(cmp_rows, wcat, cpos, w2, row(g_k[li, 0]))
    zpad = jnp.zeros((CMP_STRIDE, 2 * KV_W), F32)
    cmp_pad = jnp.concatenate([zpad, cmp_out, zpad], axis=0)
    attn_n = _pattn_t(q, gates, kt, vt, cmp_pad[:, :KV_W], cmp_pad[:, KV_W:], cmp_pad[:, KV_W:].T, rel_bias,
                      row(g_out_attn[li]))
    zst = jnp.zeros((n_seg, SSM_LANES), F32)
    _, fre, fim = _ssm(u2, n_seg, zst, zst, *ssm_args, tc=64, emit=False)
    ire, iim = _ssm_chain(fre, fim, ab_re, ab_im, seq // n_seg)
    ssm_n, hre, him = _ssm(u2, n_seg, ire, iim, *ssm_args, tc=64, emit=True)
    zrow = jnp.zeros((1, D_FF_PAD), F32)
    ffn_w = (w_out_b, row(g_ffn[li]), wa, wg, wd, cw, cb)
    y_p, cnew_p = _ffn(xp, attn_n, ssm_n, *ffn_w, zrow, zrow, tm=512, tf=512, seq=True, n_seg=n_seg)

    y_prompt = y_p[None]
    kv_prompt = kv.reshape(1, 1, seq, 4, N_KV, HEAD_DIM)
    win_prompt = win[seq - min(WINDOW, seq):].reshape(1, 1, min(WINDOW, seq), 2, N_KV, HEAD_DIM)
    ssm_prompt = jnp.stack([hre[n_seg - 1], him[n_seg - 1]], axis=-1).reshape(1, 1, SSM_G, SSM_N, 2)
    conv_prompt = cnew_p[6:8, :D_FF].reshape(1, 1, CONV_W - 1, D_FF)

    xs = x_sample[:, 0]
    q_s, kv_s, win_s, u_s, gates_s = _inproj(xs, *in_args, tm=n_b, q_scale=HEAD_DIM ** -0.5)
    cache4 = cache_kv[li].reshape(cache_kv.shape[1], PAGE, 4 * N_KV, HEAD_DIM)
    cwin4 = cache_win[li].reshape(n_b, n_w, 2 * N_KV, HEAD_DIM)
    q3 = q_s.reshape(n_b, N_HEADS, HEAD_DIM)
    o_c, idx, val = _s1(page_table, cache4, q3, wcat, cpos, w2, row(g_k[li, 0]), rel_bias)
    idx = idx[:, :, :N_KV].reshape(-1)
    val = val[:, :, :N_KV].reshape(-1)
    rep = lambda a: jnp.repeat(a.reshape(n_b, N_KV, HEAD_DIM), Q_PER_KV, axis=1)
    kns = rep(kv_s[:, 2 * N_KV:3 * N_KV])
    vns = rep(kv_s[:, 3 * N_KV:4 * N_KV])
    knw = rep(win_s[:, 0:N_KV])
    vnw = rep(win_s[:, N_KV:2 * N_KV])
    g3 = jnp.transpose(gates_s[:, :N_BRANCH * N_HEADS].reshape(n_b, N_BRANCH, N_HEADS), (0, 2, 1))
    g3 = jnp.pad(g3, ((0, 0), (0, 0), (0, LANE - N_BRANCH)))
    attn_s = _s2(idx, val, page_table, cache4, cwin4, q3, o_c, g3, kns, vns, knw, vnw, rel_bias,
                 g_out_attn[li].reshape(N_HEADS, HEAD_DIM))
    st = state_ssm[li].reshape(n_b, SSM_LANES, 2)
    ssm_s, sre, sim = _ssm(u_s, n_b, st[:, :, 0], st[:, :, 1], *ssm_args, tc=1, emit=True)
    sc = state_conv[li]
    y_s, a_s = _ffn(xs, attn_s.reshape(n_b, ATTN_W), ssm_s, *ffn_w, padc(sc[:, 0]), padc(sc[:, 1]),
                    tm=n_b, tf=512, seq=False)
    win_sample = _winshift(cwin4, win_s)

    y_sample = y_s[:, None]
    kv_sample = kv_s.reshape(1, n_b, 1, 4, N_KV, HEAD_DIM)
    win_sample = win_sample.reshape(1, n_b, n_w, 2, N_KV, HEAD_DIM)
    ssm_sample = jnp.stack([sre, sim], axis=-1).reshape(1, n_b, SSM_G, SSM_N, 2)
    conv_sample = jnp.stack([sc[:, 1], a_s[:, :D_FF]], axis=1)[None]
    return (y_prompt, y_sample, kv_prompt, kv_sample, win_prompt, win_sample,
            ssm_prompt, ssm_sample, conv_prompt, conv_sample)
```

```python
import functools
import math

import numpy as np
import jax
import jax.numpy as jnp
from jax import lax
from jax.experimental import pallas as pl
from jax.experimental.pallas import tpu as pltpu

F32 = jnp.float32
BF16 = jnp.bfloat16
I32 = jnp.int32

D_MODEL = 2048
HEAD_DIM = 128
N_HEADS = 8
N_KV = 2
Q_PER_KV = 4
ATTN_W = 1024
KV_W = 256
N_BRANCH = 3
CMP_LEN = 32
CMP_STRIDE = 16
SEL_BLOCK = 64
CMP_PER_SEL = 4
N_SEL = 16
WINDOW = 512
REL_BUCKETS = 32
REL_MAX_DIST = 128
PAGE = 128
SSM_W = 1024
SSM_G = 64
SSM_N = 64
SSM_P = 16
SSM_SG = 8
SSM_LANES = SSM_G * SSM_N
D_FF = 5504
D_FF_PAD = 5632
CONV_W = 3
EPS = 1e-6
NEG = -1e30
QB = 128
LANE = 128
VMEM_LIMIT = 56 * 1024 * 1024


def _cparams(sem):
    return pltpu.CompilerParams(dimension_semantics=sem, vmem_limit_bytes=VMEM_LIMIT)


def _rms(x, g):
    return x * lax.rsqrt(jnp.mean(x * x, axis=-1, keepdims=True) + EPS) * g


def _gelu(x):
    return jax.nn.gelu(x)


def _dot(a, b):
    return jnp.dot(a, b, preferred_element_type=F32)


def _dot_nt(a, b):
    return lax.dot_general(a, b, (((1,), (1,)), ((), ())), preferred_element_type=F32)


def _split3(x):
    hi = x.astype(BF16)
    r1 = x - hi.astype(F32)
    mid = r1.astype(BF16)
    lo = (r1 - mid.astype(F32)).astype(BF16)
    return hi, mid, lo


def _bucket_np(d):
    n = np.maximum(d, 0)
    exact = REL_BUCKETS // 2
    nf = np.maximum(n, 1).astype(np.float32)
    large = exact + (np.log(nf / np.float32(exact)) / np.float32(math.log(REL_MAX_DIST / exact))
                     * np.float32(REL_BUCKETS - exact)).astype(np.int32)
    return np.where(n < exact, n, np.minimum(large, REL_BUCKETS - 1)).astype(np.int32)


def _bias_lookup(bkt, rb_ref, head, shift=None):
    last = rb_ref[REL_BUCKETS - 1, head]
    acc = jnp.full(bkt.shape, last, F32)
    for b in range(REL_BUCKETS - 1):
        acc = jnp.where(bkt == b, rb_ref[b, head], acc)
    if shift:
        acc = acc - last
    return acc


def _prep_body(lre_ref, lim_ref, ldt_ref, bre_ref, bim_ref, pos_ref, w1_ref,
               abre_ref, abim_ref, bbre_ref, bbim_ref, cpos_ref):
    lr = lre_ref[...]
    li = lim_ref[...]
    dt = jnp.exp(ldt_ref[...])
    mag = jnp.exp(lr * dt)
    ab_re = mag * jnp.cos(li * dt)
    ab_im = mag * jnp.sin(li * dt)
    den = lr * lr + li * li
    nr = ab_re - 1.0
    ni = ab_im
    f_re = (nr * lr + ni * li) / den
    f_im = (ni * lr - nr * li) / den
    abre_ref[...] = ab_re
    abim_ref[...] = ab_im
    for p in range(SSM_P):
        br = bre_ref[p]
        bi = bim_ref[p]
        bbre_ref[p] = f_re * br - f_im * bi
        bbim_ref[p] = f_re * bi + f_im * br
    for kind in range(2):
        cpos_ref[kind] = jnp.dot(pos_ref[kind], w1_ref[kind], preferred_element_type=F32,
                                 precision=lax.Precision.HIGHEST)


def _prep(lam_re, lam_im, log_dt, b_re, b_im, pos_cmp, w_cmp1):
    bre_t = jnp.transpose(b_re, (2, 0, 1))
    bim_t = jnp.transpose(b_im, (2, 0, 1))
    pos = jnp.zeros((2, 8, CMP_LEN * HEAD_DIM), F32).at[:, 0, :].set(pos_cmp.reshape(2, CMP_LEN * HEAD_DIM))
    w1 = w_cmp1.reshape(2, CMP_LEN * HEAD_DIM, HEAD_DIM)
    return pl.pallas_call(
        _prep_body,
        out_shape=[jax.ShapeDtypeStruct((SSM_G, SSM_N), F32), jax.ShapeDtypeStruct((SSM_G, SSM_N), F32),
                   jax.ShapeDtypeStruct((SSM_P, SSM_G, SSM_N), F32), jax.ShapeDtypeStruct((SSM_P, SSM_G, SSM_N), F32),
                   jax.ShapeDtypeStruct((2, 8, HEAD_DIM), F32)],
        compiler_params=pltpu.CompilerParams(vmem_limit_bytes=VMEM_LIMIT),
        name="prep",
    )(lam_re, lam_im, log_dt.reshape(SSM_G, 1), bre_t, bim_t, pos, w1)


IN_COLS_PAD = ATTN_W + 4 * KV_W + 2 * KV_W + SSM_W + LANE
KT_KS = (0, 2 * HEAD_DIM)
KT_E = HEAD_DIM
KT_KW = 3 * HEAD_DIM
KT_COLS = 5 * HEAD_DIM
VT_ROWS = 4 * HEAD_DIM


def _inproj_body(x_ref, gmix_ref, w_ref, gq_ref, gks_ref, gkw_ref,
                 q_ref, kv_ref, win_ref, u_ref, gt_ref, *extra, q_scale):
    tm = x_ref.shape[0]
    xn = _rms(x_ref[...], gmix_ref[...]).astype(BF16)
    zq = _dot(xn, w_ref[:, 0:ATTN_W])
    for h in range(N_HEADS):
        sl = slice(h * HEAD_DIM, (h + 1) * HEAD_DIM)
        q_ref[:, sl] = (_rms(zq[:, sl], gq_ref[...]) * q_scale).astype(BF16)
    zkv = _dot(xn, w_ref[:, ATTN_W:ATTN_W + 4 * KV_W])
    ks, kw = [], []
    for s in range(4 * N_KV):
        col = zkv[:, s * HEAD_DIM:(s + 1) * HEAD_DIM]
        if s // N_KV == 2:
            col = _rms(col, gks_ref[...])
            ks.append(col)
        kv_ref[:, s, :] = col
    c0 = ATTN_W + 4 * KV_W
    zw = _dot(xn, w_ref[:, c0:c0 + 2 * KV_W])
    for s in range(2 * N_KV):
        col = zw[:, s * HEAD_DIM:(s + 1) * HEAD_DIM]
        if s // N_KV == 0:
            col = _rms(col, gkw_ref[...])
            kw.append(col)
        win_ref[:, s, :] = col
    if extra:
        cmp_ref, kt_ref, vt_ref = extra
        cmp_ref[...] = zkv[:, 0:2 * KV_W]
        for h in range(N_KV):
            kt_ref[:, KT_KS[h]:KT_KS[h] + HEAD_DIM] = ks[h].astype(BF16)
            kt_ref[:, KT_KW + h * HEAD_DIM:KT_KW + (h + 1) * HEAD_DIM] = kw[h].astype(BF16)
        blk = (lax.broadcasted_iota(I32, (tm, HEAD_DIM), 0) + pl.program_id(0) * tm) // SEL_BLOCK
        kt_ref[:, KT_E:KT_E + HEAD_DIM] = jnp.where(blk == lax.broadcasted_iota(I32, (tm, HEAD_DIM), 1), 1.0, 0.0).astype(BF16)
        vt_ref[0:KV_W, :] = zkv[:, 3 * KV_W:4 * KV_W].T.astype(BF16)
        vt_ref[KV_W:2 * KV_W, :] = zw[:, KV_W:2 * KV_W].T.astype(BF16)
    c1 = c0 + 2 * KV_W
    zt = _dot(xn, w_ref[:, c1:IN_COLS_PAD])
    gt_ref[...] = jax.nn.sigmoid(zt[:, 0:LANE])
    u_ref[...] = zt[:, N_BRANCH * N_HEADS:N_BRANCH * N_HEADS + SSM_W]


def _seg_spec(tm, rows, n_seg):
    tiles_per_seg = rows // n_seg // tm
    return pl.BlockSpec((tm, SSM_W), lambda i: (i % tiles_per_seg, i // tiles_per_seg))


def _inproj(x, g_mix, w, g_q, g_ks, g_kw, tm, q_scale, n_seg=1, attn_operands=False):
    rows = x.shape[0]
    row_spec = lambda n: pl.BlockSpec((tm, n), lambda i: (i, 0))
    full = lambda a: pl.BlockSpec(a.shape, lambda i: (0,) * a.ndim)
    out_specs = [row_spec(ATTN_W), pl.BlockSpec((tm, 4 * N_KV, HEAD_DIM), lambda i: (i, 0, 0)),
                 pl.BlockSpec((tm, 2 * N_KV, HEAD_DIM), lambda i: (i, 0, 0)), _seg_spec(tm, rows, n_seg), row_spec(LANE)]
    out_shape = [jax.ShapeDtypeStruct((rows, ATTN_W), BF16), jax.ShapeDtypeStruct((rows, 4 * N_KV, HEAD_DIM), F32),
                 jax.ShapeDtypeStruct((rows, 2 * N_KV, HEAD_DIM), F32),
                 jax.ShapeDtypeStruct((rows // n_seg, n_seg * SSM_W), F32), jax.ShapeDtypeStruct((rows, LANE), F32)]
    if attn_operands:
        out_specs += [row_spec(2 * KV_W), row_spec(KT_COLS), pl.BlockSpec((VT_ROWS, tm), lambda i: (0, i))]
        out_shape += [jax.ShapeDtypeStruct((rows, 2 * KV_W), F32), jax.ShapeDtypeStruct((rows, KT_COLS), BF16),
                      jax.ShapeDtypeStruct((VT_ROWS, rows), BF16)]
    return pl.pallas_call(
        functools.partial(_inproj_body, q_scale=q_scale),
        grid=(rows // tm,),
        in_specs=[row_spec(D_MODEL), full(g_mix), full(w), full(g_q), full(g_ks), full(g_kw)],
        out_specs=out_specs,
        out_shape=out_shape,
        compiler_params=_cparams(("arbitrary",)),
        name="inproj",
    )(x, g_mix, w, g_q, g_ks, g_kw)


PERM_ROWS = 2 * PAGE


def _perm_matrix():
    k = np.arange(PERM_ROWS // CMP_STRIDE)
    p = np.arange(CMP_STRIDE)
    m = np.zeros((PERM_ROWS, PERM_ROWS), np.float32)
    m[(p[:, None] * len(k) + k[None, :]).ravel(), (CMP_STRIDE * k[None, :] + p[:, None]).ravel()] = 1.0
    return jnp.asarray(m, dtype=BF16)


def _compress_pair(x_refs, perm_ref, xp_s, wcat2_ref, cpos, w2):
    n_rows = x_refs[0].shape[0]
    n_grp = n_rows // PERM_ROWS
    n_ch = n_rows // CMP_STRIDE
    ck = PERM_ROWS // CMP_STRIDE

    def perm(g, carry):
        rows = pl.ds(pl.multiple_of(g * PERM_ROWS, PERM_ROWS), PERM_ROWS)
        xcat = jnp.concatenate([x_refs[0][rows, :], x_refs[1][rows, :]], axis=1).astype(BF16)
        y = _dot(perm_ref[...], xcat).astype(BF16)
        for hd in range(N_KV):
            chunks = pl.ds(pl.multiple_of(hd * n_ch + g * ck, ck), ck)
            for p in range(CMP_STRIDE):
                xp_s[p, chunks, :] = y[p * ck:(p + 1) * ck, hd * HEAD_DIM:(hd + 1) * HEAD_DIM]
        return carry

    lax.fori_loop(0, n_grp, perm, 0, unroll=8)
    acc = jnp.zeros((N_KV * n_ch, 2 * HEAD_DIM), F32)
    for q in range(CMP_STRIDE // 2):
        acc = acc + _dot(jnp.concatenate([xp_s[2 * q], xp_s[2 * q + 1]], axis=-1), wcat2_ref[q])
    outs = []
    for hd in range(N_KV):
        a = acc[hd * n_ch:(hd + 1) * n_ch]
        e_hi_next = pltpu.roll(a[:, HEAD_DIM:2 * HEAD_DIM], n_ch - 1, 0)
        hid = _gelu(a[:, 0:HEAD_DIM] + e_hi_next + cpos)
        outs.append(_dot(hid.astype(BF16), w2))
    return outs


def _pcompress_body(x0_ref, x1_ref, perm_ref, wcat2_ref, cpos_ref, w2_ref, gkc_ref, o_ref, xp_s):
    kind = pl.program_id(0)
    outs = _compress_pair((x0_ref, x1_ref), perm_ref, xp_s, wcat2_ref.at[0], cpos_ref[0, 0:1, :], w2_ref[0])
    for hd in range(N_KV):
        o_ref[:, hd * HEAD_DIM:(hd + 1) * HEAD_DIM] = jnp.where(kind == 0, _rms(outs[hd], gkc_ref[...]), outs[hd])


def _pcompress(cmp_rows, wcat2, cpos, w2, g_kc):
    rows = cmp_rows.shape[0]
    n_ch = rows // CMP_STRIDE
    perm = _perm_matrix()
    return pl.pallas_call(
        _pcompress_body,
        grid=(2,),
        in_specs=[pl.BlockSpec((rows, HEAD_DIM), lambda kd: (0, 2 * kd)), pl.BlockSpec((rows, HEAD_DIM), lambda kd: (0, 2 * kd + 1)),
                  pl.BlockSpec(perm.shape, lambda kd: (0, 0)),
                  pl.BlockSpec((1,) + wcat2.shape[1:], lambda kd: (kd, 0, 0, 0)),
                  pl.BlockSpec((1, 8, HEAD_DIM), lambda kd: (kd, 0, 0)),
                  pl.BlockSpec((1, HEAD_DIM, HEAD_DIM), lambda kd: (kd, 0, 0)),
                  pl.BlockSpec((1, HEAD_DIM), lambda kd: (0, 0))],
        out_specs=pl.BlockSpec((n_ch, KV_W), lambda kd: (0, kd)),
        out_shape=jax.ShapeDtypeStruct((n_ch, 2 * KV_W), F32),
        scratch_shapes=[pltpu.VMEM((CMP_STRIDE, N_KV * n_ch, HEAD_DIM), BF16)],
        compiler_params=_cparams(("arbitrary",)),
        name="pcompress",
    )(cmp_rows, cmp_rows, perm, wcat2, cpos, w2, g_kc)


def _select_rounds(score_t, on_pick=None):
    n_j = score_t.shape[0]
    jio = lax.broadcasted_iota(I32, score_t.shape, 0)
    sel = jnp.zeros(score_t.shape, F32)
    sc = score_t
    for r in range(N_SEL):
        m = jnp.max(sc, axis=0, keepdims=True)
        idx = jnp.min(jnp.where(sc == m, jio, n_j), axis=0, keepdims=True)
        pick = jio == idx
        ok = m >= 0.0
        sel = jnp.where(pick & ok, 1.0, sel)
        sc = jnp.where(pick, -jnp.inf, sc)
        if on_pick is not None:
            on_pick(r, idx, ok)
    return sel


LOG2E = 1.4426950408889634


ONES_ROWS = 16
FAR_TILES = 8


def _col_max(tiles):
    m = jnp.max(tiles[0], axis=0, keepdims=True)
    for s in tiles[1:]:
        m = jnp.maximum(m, jnp.max(s, axis=0, keepdims=True))
    return m


def _lanes4(x):
    return jnp.concatenate([x] * Q_PER_KV, axis=1)


def _pattn_t_body(q_ref, gt_ref, kt_ref, vt_ref, kc_ref, vc_ref, vct_ref, rb_ref, bk0_ref, bk1_ref, bkc_ref, mft_ref,
                  gout_ref, o_ref, b0_s, b1_s, bc_s, bw4_s):
    i = pl.program_id(0)
    cols = Q_PER_KV * QB

    @pl.when(i == 0)
    def _tables():
        b_io = lax.broadcasted_iota(I32, (QB, QB), 0)
        a_io = lax.broadcasted_iota(I32, (QB, QB), 1)
        for hd in range(N_HEADS):
            t0 = _bias_lookup(bk0_ref[...], rb_ref, hd, shift=True) * LOG2E
            b0_s[hd] = jnp.where(a_io >= b_io, t0, NEG)
            b1_s[hd] = _bias_lookup(bk1_ref[...], rb_ref, hd, shift=True) * LOG2E
            bc_s[hd] = _bias_lookup(bkc_ref[...], rb_ref, hd, shift=True) * LOG2E
        bw4_s[...] = jnp.where(b_io >= a_io, 0.0, NEG)

    gt_t = gt_ref[...].T
    n_far = jnp.maximum(i - 1, 0) // FAR_TILES
    far_keys = FAR_TILES * QB

    def tab(ref, h):
        return jnp.concatenate([ref[Q_PER_KV * h + g] for g in range(Q_PER_KV)], axis=1)

    def v_aug(row0, start, n):
        return jnp.concatenate([vt_ref[row0:row0 + HEAD_DIM, pl.ds(start, n)], jnp.ones((ONES_ROWS, n), BF16)], axis=0)

    q_ts, qa_ts, o_cs, sel_st = [], [], [], []
    for h in range(N_KV):
        q_t = jnp.concatenate(
            [q_ref[:, (Q_PER_KV * h + g) * HEAD_DIM:(Q_PER_KV * h + g + 1) * HEAD_DIM].astype(F32).T.astype(BF16)
             for g in range(Q_PER_KV)], axis=1)
        hs = slice(h * HEAD_DIM, (h + 1) * HEAD_DIM)

        n_c = kc_ref.shape[0] - 2 * CMP_STRIDE
        near0 = pl.multiple_of(8 * i, 8)
        cf = lax.broadcasted_iota(I32, (n_c, QB), 0)
        ok_f = _lanes4((cf < 8 * i) & (cf >= CMP_STRIDE))
        s_f = jnp.where(ok_f, _dot(kc_ref[0:n_c, hs].astype(BF16), q_t), NEG)
        cn = lax.broadcasted_iota(I32, (32, cols), 0)
        a_n = lax.broadcasted_iota(I32, (32, cols), 1) % QB
        ok_n = (CMP_STRIDE * (cn - CMP_STRIDE) <= a_n - (CMP_LEN - 1)) & (cn + 8 * i >= CMP_STRIDE)
        s_n = jnp.where(ok_n, _dot(kc_ref[pl.ds(near0, 32), hs].astype(BF16), q_t) + tab(bc_s, h), NEG)
        m_c = _col_max([s_f, s_n])
        p_f = jnp.where(ok_f, jnp.exp2(s_f - m_c), 0.0)
        p_n = jnp.where(ok_n, jnp.exp2(s_n - m_c), 0.0)
        l_c = jnp.sum(p_f, axis=0, keepdims=True) + jnp.sum(p_n, axis=0, keepdims=True)
        inv_c = 1.0 / jnp.maximum(l_c, 1e-30)
        vc_near_t = vc_ref[pl.ds(near0, 32), hs].T.astype(BF16)
        o_c = (_dot(vct_ref[hs, 0:n_c].astype(BF16), p_f.astype(BF16)) + _dot(vc_near_t, p_n.astype(BF16))) * inv_c
        pn_f = p_f * inv_c
        pn_n = p_n * inv_c
        imp_f = pn_f[:, 0:QB] + pn_f[:, QB:2 * QB] + pn_f[:, 2 * QB:3 * QB] + pn_f[:, 3 * QB:4 * QB]
        imp_n = pn_n[:, 0:QB] + pn_n[:, QB:2 * QB] + pn_n[:, 2 * QB:3 * QB] + pn_n[:, 3 * QB:4 * QB]
        jn = lax.broadcasted_iota(I32, (QB, 32), 0)
        cc = lax.broadcasted_iota(I32, (QB, 32), 1) + 8 * i - CMP_STRIDE
        mnt = (((cc // CMP_PER_SEL) == jn) | (cc == CMP_PER_SEL * jn - 1)) & (cc >= 0)
        mnt = jnp.where(mnt, 1.0, 0.0).astype(BF16)
        imp_t = jnp.zeros((QB, QB), F32)
        for part in _split3(imp_f):
            imp_t = imp_t + _dot(mft_ref[...], part)
        for part in _split3(imp_n):
            imp_t = imp_t + _dot(mnt, part)
        j_io = lax.broadcasted_iota(I32, (QB, QB), 0)
        cur = 2 * i + lax.broadcasted_iota(I32, (QB, QB), 1) // SEL_BLOCK
        forced = (j_io == 0) | (j_io == cur) | (j_io == cur - 1)
        score_t = jnp.where(forced, 1e9, jnp.where(j_io <= cur, imp_t, -1.0))
        sel_t = _select_rounds(score_t)
        selneg = _lanes4(jnp.where(sel_t > 0.5, 0.0, NEG).astype(BF16))
        qa_t = jnp.concatenate([q_t, selneg] if h == 0 else [selneg, q_t], axis=0)
        ka0 = h * HEAD_DIM

        near_s, near_v = [], []
        for back in range(FAR_TILES + 1):
            kt = i - back
            start = pl.multiple_of(jnp.maximum(kt, 0) * QB, QB)
            s = _dot(kt_ref[pl.ds(start, QB), ka0:ka0 + 2 * HEAD_DIM], qa_t)
            if back == 0:
                s = s + tab(b0_s, h)
            elif back == 1:
                s = s + tab(b1_s, h) + jnp.where(kt >= 0, 0.0, NEG)
            else:
                s = s + jnp.where((kt >= 0) & (kt >= FAR_TILES * n_far), 0.0, NEG)
            near_s.append(s)
            near_v.append(v_aug(h * HEAD_DIM, start, QB))
        m_s = _col_max(near_s)
        acc_s = jnp.zeros((HEAD_DIM + ONES_ROWS, cols), F32)
        for s, v in zip(near_s, near_v):
            acc_s = acc_s + _dot(v, jnp.exp2(s - m_s).astype(BF16))
        q_ts.append(q_t)
        qa_ts.append(qa_t)
        o_cs.append(o_c)
        sel_st += [m_s, acc_s]

    def far_step(k, st):
        start = pl.multiple_of(k * far_keys, far_keys)
        out = []
        for h in range(N_KV):
            m, acc = st[2 * h], st[2 * h + 1]
            s = _dot(kt_ref[pl.ds(start, far_keys), h * HEAD_DIM:(h + 2) * HEAD_DIM], qa_ts[h])
            m_new = jnp.maximum(m, jnp.max(s, axis=0, keepdims=True))
            p = jnp.exp2(s - m_new).astype(BF16)
            out += [m_new, jnp.exp2(m - m_new) * acc + _dot(v_aug(h * HEAD_DIM, start, far_keys), p)]
        return tuple(out)

    sel_st = lax.fori_loop(0, n_far, far_step, tuple(sel_st))

    attn = [None] * N_HEADS
    for h in range(N_KV):
        q_t = q_ts[h]
        acc_s = sel_st[2 * h + 1]
        o_s = acc_s[0:HEAD_DIM] / acc_s[HEAD_DIM:HEAD_DIM + 1]
        o_c = o_cs[h]

        kw0 = KT_KW + h * HEAD_DIM
        win_s, win_v = [], []
        for back in range(5):
            kt = i - back
            start = pl.multiple_of(jnp.maximum(kt, 0) * QB, QB)
            s = _dot(kt_ref[pl.ds(start, QB), kw0:kw0 + HEAD_DIM], q_t)
            if back == 0:
                s = s + tab(b0_s, h)
            elif back == 1:
                s = s + tab(b1_s, h)
            elif back == 4:
                s = s + _lanes4(bw4_s[...])
            if back > 0:
                s = s + jnp.where(kt >= 0, 0.0, NEG)
            win_s.append(s)
            win_v.append(v_aug((N_KV + h) * HEAD_DIM, start, QB))
        m_w = _col_max(win_s)
        acc_w = jnp.zeros((HEAD_DIM + ONES_ROWS, cols), F32)
        for s, v in zip(win_s, win_v):
            acc_w = acc_w + _dot(v, jnp.exp2(s - m_w).astype(BF16))
        o_w = acc_w[0:HEAD_DIM] / acc_w[HEAD_DIM:HEAD_DIM + 1]

        for g in range(Q_PER_KV):
            hd = Q_PER_KV * h + g
            cs = slice(g * QB, (g + 1) * QB)
            o_t = (gt_t[hd:hd + 1] * o_c[:, cs] + gt_t[N_HEADS + hd:N_HEADS + hd + 1] * o_s[:, cs]
                   + gt_t[2 * N_HEADS + hd:2 * N_HEADS + hd + 1] * o_w[:, cs])
            attn[hd] = o_t.T
    a = jnp.concatenate(attn, axis=1)
    o_ref[...] = _rms(a, gout_ref[...]).astype(BF16)


def _pattn_t_tables():
    b = np.arange(QB)[:, None]
    a = np.arange(QB)[None, :]
    bk0 = _bucket_np(a - b)
    bk1 = _bucket_np(a - b + QB)
    c = np.arange(32)[:, None] - CMP_STRIDE
    bkc = _bucket_np(a - CMP_STRIDE * c - (CMP_LEN - 1))
    cidx = np.arange(4 * QB)[None, :] - CMP_STRIDE
    j = np.arange(QB)[:, None]
    mft = (((cidx // CMP_PER_SEL) == j) | (cidx == CMP_PER_SEL * j - 1)) & (cidx >= 0)
    return (jnp.asarray(bk0), jnp.asarray(bk1), jnp.asarray(bkc), jnp.asarray(mft.astype(np.float32), dtype=BF16))


def _pattn_t(q, gates, kt, vt, kcp, vcp, vcpt, rel_bias, g_out):
    rows = q.shape[0]
    bk0, bk1, bkc, mft = _pattn_t_tables()
    full = lambda a: pl.BlockSpec(a.shape, lambda i: (0,) * a.ndim)
    once = lambda a: pl.BlockSpec(a.shape, lambda i: (0,) * a.ndim, pipeline_mode=pl.Buffered(1))
    return pl.pallas_call(
        _pattn_t_body,
        grid=(rows // QB,),
        in_specs=[pl.BlockSpec((QB, ATTN_W), lambda i: (i, 0)), pl.BlockSpec((QB, LANE), lambda i: (i, 0)),
                  once(kt), once(vt), full(kcp), full(vcp), full(vcpt), pl.BlockSpec(memory_space=pltpu.SMEM),
                  full(bk0), full(bk1), full(bkc), full(mft), full(g_out)],
        out_specs=pl.BlockSpec((QB, ATTN_W), lambda i: (i, 0)),
        out_shape=jax.ShapeDtypeStruct((rows, ATTN_W), BF16),
        scratch_shapes=[pltpu.VMEM((N_HEADS, QB, QB), F32), pltpu.VMEM((N_HEADS, QB, QB), F32),
                        pltpu.VMEM((N_HEADS, 32, QB), F32), pltpu.VMEM((QB, QB), F32)],
        compiler_params=_cparams(("arbitrary",)),
        name="pattn",
    )(q, gates, kt, vt, kcp, vcp, vcpt, rel_bias, bk0, bk1, bkc, mft, g_out)


def _ssm_body(u_ref, ire_ref, iim_ref, wb_ref, wc_ref, abre_ref, abim_ref, dsk_ref, wglu_ref, gout_ref,
              y_ref, fre_ref, fim_ref, u_s, xre_s, xim_s, y_s, sre_s, sim_s, *, n_seg, tc, emit):
    c = pl.program_id(0)
    sgl = SSM_LANES // SSM_SG

    @pl.when(c == 0)
    def _init():
        sre_s[...] = ire_ref[...]
        sim_s[...] = iim_ref[...]

    n_lt = SSM_W // LANE
    if tc == 1:
        u_cols = [u_ref[:, l * LANE:(l + 1) * LANE] for l in range(n_lt)]
    else:
        for s in range(n_seg):
            for l in range(n_lt):
                u_s[l, pl.ds(s, tc, stride=n_seg), :] = u_ref[:, s * SSM_W + l * LANE:s * SSM_W + (l + 1) * LANE]
        u_cols = [u_s[l] for l in range(n_lt)]
    for sg in range(SSM_SG):
        ls = slice(sg * sgl, (sg + 1) * sgl)
        bu = _dot(u_cols[sg].astype(BF16), wb_ref[sg])
        xre_s[:, ls] = bu[:, 0:sgl]
        xim_s[:, ls] = bu[:, sgl:2 * sgl]
        ar = abre_ref[:, ls]
        ai = abim_ref[:, ls]

        def step(t, carry):
            xr, xi = carry
            rows = pl.ds(pl.multiple_of(t * n_seg, n_seg), n_seg)
            nr = ar * xr - ai * xi + xre_s[rows, ls]
            ni = ar * xi + ai * xr + xim_s[rows, ls]
            xre_s[rows, ls] = nr
            xim_s[rows, ls] = ni
            return nr, ni

        xr, xi = lax.fori_loop(0, tc, step, (sre_s[:, ls], sim_s[:, ls]), unroll=min(tc, 4))
        sre_s[:, ls] = xr
        sim_s[:, ls] = xi
    fre_ref[...] = sre_s[...]
    fim_ref[...] = sim_s[...]
    if not emit:
        y_ref[...] = jnp.zeros(y_ref.shape, y_ref.dtype)
        return
    ys = []
    for sg in range(SSM_SG):
        ls = slice(sg * sgl, (sg + 1) * sgl)
        x2 = jnp.concatenate([xre_s[:, ls], xim_s[:, ls]], axis=1).astype(BF16)
        ys.append(_dot(x2, wc_ref[sg]))
    y = jnp.concatenate(ys, axis=1) + dsk_ref[...] * jnp.concatenate(u_cols, axis=1)
    z = _dot(_gelu(y).astype(BF16), wglu_ref[...])
    o = z[:, 0:SSM_W] * jax.nn.sigmoid(z[:, SSM_W:2 * SSM_W])
    yn = _rms(o, gout_ref[...])
    if tc == 1:
        y_ref[...] = yn.astype(BF16)
    else:
        for l in range(n_lt):
            y_s[l] = yn[:, l * LANE:(l + 1) * LANE]
        for s in range(n_seg):
            for l in range(n_lt):
                y_ref[:, s * SSM_W + l * LANE:s * SSM_W + (l + 1) * LANE] = (
                    y_s[l, pl.ds(s, tc, stride=n_seg), :].astype(BF16))


def _ssm(u2, n_seg, init_re, init_im, wb, wc, ab_re, ab_im, d_skip, w_glu, g_out, tc, emit):
    if tc == 1:
        assert u2.shape == (n_seg, SSM_W)
        t_len = 1
        blk = (n_seg, SSM_W)
    else:
        t_len = u2.shape[0]
        assert u2.shape[1] == n_seg * SSM_W and n_seg % 8 == 0
        blk = (tc, n_seg * SSM_W)
    rows = tc * n_seg
    full = lambda a: pl.BlockSpec(a.shape, lambda c: (0,) * a.ndim)
    body = functools.partial(_ssm_body, n_seg=n_seg, tc=tc, emit=emit)
    st = jax.ShapeDtypeStruct((n_seg, SSM_LANES), F32)
    y_shape = u2.shape if emit else blk
    y_map = (lambda c: (c, 0)) if emit else (lambda c: (0, 0))
    return pl.pallas_call(
        body,
        grid=(t_len // tc,),
        in_specs=[pl.BlockSpec(blk, lambda c: (c, 0)),
                  full(init_re), full(init_im), full(wb), full(wc),
                  full(ab_re), full(ab_im), full(d_skip), full(w_glu), full(g_out)],
        out_specs=[pl.BlockSpec(blk, y_map), full(init_re), full(init_im)],
        out_shape=[jax.ShapeDtypeStruct(y_shape, BF16), st, st],
        scratch_shapes=[pltpu.VMEM((SSM_W // LANE, rows, LANE), F32), pltpu.VMEM((rows, SSM_LANES), F32),
                        pltpu.VMEM((rows, SSM_LANES), F32), pltpu.VMEM((SSM_W // LANE, rows, LANE), F32),
                        pltpu.VMEM((n_seg, SSM_LANES), F32), pltpu.VMEM((n_seg, SSM_LANES), F32)],
        compiler_params=_cparams(("arbitrary",)),
        name="ssm_emit" if emit else "ssm_final",
    )(u2, init_re, init_im, wb, wc, ab_re, ab_im, d_skip, w_glu, g_out)


def _ssm_chain_body(fre_ref, fim_ref, abre_ref, abim_ref, ire_ref, iim_ref, *, n_seg, log2_len):
    pr = abre_ref[...]
    pi = abim_ref[...]
    for _ in range(log2_len):
        pr, pi = pr * pr - pi * pi, 2.0 * pr * pi
    cr = jnp.zeros((1, SSM_LANES), F32)
    ci = jnp.zeros((1, SSM_LANES), F32)
    for j in range(n_seg):
        ire_ref[j:j + 1, :] = cr
        iim_ref[j:j + 1, :] = ci
        fr = fre_ref[j:j + 1, :]
        fi = fim_ref[j:j + 1, :]
        cr, ci = fr + pr * cr - pi * ci, fi + pr * ci + pi * cr


def _ssm_chain(fre, fim, ab_re, ab_im, seg_len):
    n_seg = fre.shape[0]
    log2_len = int(math.log2(seg_len))
    assert 2 ** log2_len == seg_len
    st = jax.ShapeDtypeStruct((n_seg, SSM_LANES), F32)
    return pl.pallas_call(
        functools.partial(_ssm_chain_body, n_seg=n_seg, log2_len=log2_len),
        out_shape=[st, st],
        compiler_params=pltpu.CompilerParams(vmem_limit_bytes=VMEM_LIMIT),
        name="ssm_chain",
    )(fre, fim, ab_re, ab_im)


FFN_CHUNKS = 2


def _ffn_body(x_ref, a_ref, s_ref, wo_ref, gffn_ref, wa_ref, wg_ref, wd_ref, cw_ref, cb_ref, p2_ref, p1_ref,
              y_ref, cnew_ref, hn_s, car_s, *, seq, tm):
    r = pl.program_id(0)
    j = pl.program_id(1)

    @pl.when(j == 0)
    def _mix():
        h = x_ref[...] + _dot(a_ref[...], wo_ref[0:ATTN_W, :]) + _dot(s_ref[...], wo_ref[ATTN_W:D_MODEL, :])
        y_ref[...] = h
        hn_s[...] = _rms(h, gffn_ref[...]).astype(BF16)

    if seq:
        @pl.when(r == 0)
        def _first():
            car_s[j, 6:7, :] = p2_ref[...]
            car_s[j, 7:8, :] = p1_ref[...]

    hn = hn_s[...]
    tf = wa_ref.shape[1]
    tc = tf // FFN_CHUNKS
    part = None
    for k in range(FFN_CHUNKS):
        cs = slice(k * tc, (k + 1) * tc)
        a = _dot(hn, wa_ref[:, cs])
        g = _dot(hn, wg_ref[:, cs])
        if seq:
            p2 = car_s[j, 6:7, cs]
            p1 = car_s[j, 7:8, cs]
            row = lax.broadcasted_iota(I32, a.shape, 0)
            a1 = jnp.where(row == 0, p1, pltpu.roll(a, 1, 0))
            a2 = jnp.where(row == 0, p2, jnp.where(row == 1, p1, pltpu.roll(a, 2, 0)))
            car_s[j, :, cs] = a[tm - 8:tm, :]
            cnew_ref[:, pl.ds(pl.multiple_of(j * tf + k * tc, tc), tc)] = a[tm - 8:tm, :]
        else:
            a1 = p1_ref[:, cs]
            a2 = p2_ref[:, cs]
            cnew_ref[:, cs] = a
        c = cb_ref[:, cs] + cw_ref[0:1, cs] * a2 + cw_ref[1:2, cs] * a1 + cw_ref[2:3, cs] * a
        d = _dot((_gelu(c) * g).astype(BF16), wd_ref[cs, :])
        part = d if part is None else part + d
    y_ref[...] += part


def _ffn(x, a_n, s_n, w_out, g_ffn, wa, wg, wd, cw, cb, p2, p1, tm, tf, seq, n_seg=1):
    rows = x.shape[0]
    nj = D_FF_PAD // tf
    body = functools.partial(_ffn_body, seq=seq, tm=tm)
    tiles_per_seg = rows // n_seg // tm
    if seq:
        tap_spec = pl.BlockSpec((1, tf), lambda r, j: (0, j))
        cnew_spec = pl.BlockSpec((8, D_FF_PAD), lambda r, j: (0, 0))
        cnew_shape = jax.ShapeDtypeStruct((8, D_FF_PAD), F32)
    else:
        tap_spec = pl.BlockSpec((tm, tf), lambda r, j: (r, j))
        cnew_spec = pl.BlockSpec((tm, tf), lambda r, j: (r, j))
        cnew_shape = jax.ShapeDtypeStruct((rows, D_FF_PAD), F32)
    return pl.pallas_call(
        body,
        grid=(rows // tm, nj),
        in_specs=[pl.BlockSpec((tm, D_MODEL), lambda r, j: (r, 0)), pl.BlockSpec((tm, ATTN_W), lambda r, j: (r, 0)),
                  pl.BlockSpec((tm, SSM_W), lambda r, j: (r % tiles_per_seg, r // tiles_per_seg)),
                  pl.BlockSpec(w_out.shape, lambda r, j: (0, 0), pipeline_mode=pl.Buffered(1)),
                  pl.BlockSpec((1, D_MODEL), lambda r, j: (0, 0)),
                  pl.BlockSpec((D_MODEL, tf), lambda r, j: (0, j)), pl.BlockSpec((D_MODEL, tf), lambda r, j: (0, j)),
                  pl.BlockSpec((tf, D_MODEL), lambda r, j: (j, 0)), pl.BlockSpec((CONV_W, tf), lambda r, j: (0, j)),
                  pl.BlockSpec((1, tf), lambda r, j: (0, j)), tap_spec, tap_spec],
        out_specs=[pl.BlockSpec((tm, D_MODEL), lambda r, j: (r, 0)), cnew_spec],
        out_shape=[jax.ShapeDtypeStruct((rows, D_MODEL), F32), cnew_shape],
        scratch_shapes=[pltpu.VMEM((tm, D_MODEL), BF16), pltpu.VMEM((nj, 8, tf), F32)],
        compiler_params=_cparams(("arbitrary", "arbitrary")),
        name="ffn_seq" if seq else "ffn_rows",
    )(x, a_n, s_n, w_out, g_ffn, wa, wg, wd, cw, cb, p2, p1)


def _s1_copies(pt_ref, cache_ref, x_s, sem, b, slot, n_pages):
    cps = []
    for pg in range(n_pages):
        page = pt_ref[b * n_pages + pg]
        for s in range(2 * N_KV):
            cps.append(pltpu.make_async_copy(cache_ref.at[page, :, s, :],
                                             x_s.at[slot, s, pl.ds(pg * PAGE, PAGE), :], sem.at[slot]))
    return cps


def _s1_body(pt_ref, cache_ref, q_ref, perm_ref, wcat2_ref, cpos_ref, w2_ref, gkc_ref, rb_ref, bkc_ref, mt_ref,
             oc_ref, idx_ref, val_ref, x_s, xp_s, bias_s, sem, *, n_pages, past):
    b = pl.program_id(0)
    nb = pl.num_programs(0)
    slot = b % 2

    @pl.when(b == 0)
    def _first():
        for cp in _s1_copies(pt_ref, cache_ref, x_s, sem, 0, 0, n_pages):
            cp.start()
        for hd in range(N_HEADS):
            bias_s[hd:hd + 1, :] = _bias_lookup(bkc_ref[...], rb_ref, hd)

    @pl.when(b + 1 < nb)
    def _next():
        for cp in _s1_copies(pt_ref, cache_ref, x_s, sem, b + 1, 1 - slot, n_pages):
            cp.start()

    for cp in _s1_copies(pt_ref, cache_ref, x_s, sem, b, slot, n_pages):
        cp.wait()

    n_c = past // CMP_STRIDE
    cio = lax.broadcasted_iota(I32, (N_HEADS, n_c), 1)
    hrow = lax.broadcasted_iota(I32, (N_HEADS, n_c), 0) // Q_PER_KV
    q = q_ref[0]
    s_all = jnp.zeros((N_HEADS, n_c), F32)
    kcs = _compress_pair((x_s.at[slot, 0], x_s.at[slot, 1]), perm_ref, xp_s, wcat2_ref.at[0], cpos_ref[0, 0:1, :], w2_ref[0])
    vcs = _compress_pair((x_s.at[slot, N_KV], x_s.at[slot, N_KV + 1]), perm_ref, xp_s, wcat2_ref.at[1], cpos_ref[1, 0:1, :],
                         w2_ref[1])
    vcs = [v.astype(BF16) for v in vcs]
    for h in range(N_KV):
        kc = _rms(kcs[h], gkc_ref[...])
        s_all = jnp.where(hrow == h, _dot_nt(q, kc.astype(BF16)), s_all)
    ok = cio < n_c - 1
    s_all = jnp.where(ok, s_all + bias_s[...], NEG)
    m = jnp.max(s_all, axis=-1, keepdims=True)
    p = jnp.where(ok, jnp.exp(s_all - m), 0.0)
    p = p / jnp.maximum(jnp.sum(p, axis=-1, keepdims=True), 1e-30)
    pb = p.astype(BF16)
    hrow_o = lax.broadcasted_iota(I32, (N_HEADS, HEAD_DIM), 0) // Q_PER_KV
    o_c = jnp.zeros((N_HEADS, HEAD_DIM), F32)
    for h in range(N_KV):
        o_c = jnp.where(hrow_o == h, _dot(pb, vcs[h]), o_c)
    oc_ref[0] = o_c
    rio = lax.broadcasted_iota(I32, (8, n_c), 0)
    imp = jnp.zeros((8, n_c), F32)
    for h in range(N_KV):
        ih = p[4 * h:4 * h + 1] + p[4 * h + 1:4 * h + 2] + p[4 * h + 2:4 * h + 3] + p[4 * h + 3:4 * h + 4]
        imp = jnp.where(rio == h, ih, imp)
    imp = jnp.concatenate([imp, jnp.zeros((LANE - 8, n_c), F32)], axis=0)
    n_j = mt_ref.shape[0]
    imp_t = jnp.zeros((n_j, LANE), F32)
    for part in _split3(imp):
        imp_t = imp_t + _dot_nt(mt_ref[...], part)
    j_io = lax.broadcasted_iota(I32, (n_j, LANE), 0)
    cur = past // SEL_BLOCK
    forced = (j_io == 0) | (j_io == cur) | (j_io == cur - 1)
    score_t = jnp.where(forced, 1e9, jnp.where(j_io <= cur, imp_t, -1.0))
    score_t = jnp.where(j_io <= cur, score_t, -jnp.inf)

    def on_pick(r, idx, okv):
        idx_ref[0, r:r + 1, :] = idx
        val_ref[0, r:r + 1, :] = jnp.where(okv, 1, 0)

    _select_rounds(score_t, on_pick)


def _s1(page_table, cache4, q3, wcat2, cpos, w2, g_kc, rel_bias):
    n_b, n_pages = page_table.shape
    past = n_pages * PAGE
    n_c = past // CMP_STRIDE
    perm = _perm_matrix()
    ns = past // SEL_BLOCK + 1
    n_j = -(-ns // 8) * 8
    c = np.arange(n_c)[None, :]
    bkc = _bucket_np(past - (CMP_STRIDE * c + CMP_LEN - 1))
    j = np.arange(n_j)[:, None]
    mt = (((c // CMP_PER_SEL) == j) | (c == CMP_PER_SEL * j - 1)) & (c < n_c - 1)
    mt = jnp.asarray(mt.astype(np.float32), dtype=BF16)
    full = lambda a: pl.BlockSpec(a.shape, lambda b, pt: (0,) * a.ndim)
    body = functools.partial(_s1_body, n_pages=n_pages, past=past)
    return pl.pallas_call(
        body,
        grid_spec=pltpu.PrefetchScalarGridSpec(
            num_scalar_prefetch=1,
            grid=(n_b,),
            in_specs=[pl.BlockSpec(memory_space=pl.ANY), pl.BlockSpec((1, N_HEADS, HEAD_DIM), lambda b, pt: (b, 0, 0)),
                      full(perm), full(wcat2), full(cpos), full(w2), full(g_kc), pl.BlockSpec(memory_space=pltpu.SMEM),
                      pl.BlockSpec((1, n_c), lambda b, pt: (0, 0)), full(mt)],
            out_specs=[pl.BlockSpec((1, N_HEADS, HEAD_DIM), lambda b, pt: (b, 0, 0)),
                       pl.BlockSpec((1, N_SEL, LANE), lambda b, pt: (b, 0, 0)),
                       pl.BlockSpec((1, N_SEL, LANE), lambda b, pt: (b, 0, 0))],
            scratch_shapes=[pltpu.VMEM((2, 2 * N_KV, past, HEAD_DIM), F32),
                            pltpu.VMEM((CMP_STRIDE, N_KV * n_c, HEAD_DIM), BF16), pltpu.VMEM((N_HEADS, n_c), F32),
                            pltpu.SemaphoreType.DMA((2,))],
        ),
        out_shape=[jax.ShapeDtypeStruct((n_b, N_HEADS, HEAD_DIM), F32), jax.ShapeDtypeStruct((n_b, N_SEL, LANE), I32),
                   jax.ShapeDtypeStruct((n_b, N_SEL, LANE), I32)],
        compiler_params=_cparams(("arbitrary",)),
        name="sample_cmp",
    )(page_table.reshape(-1), cache4, q3, perm, wcat2, cpos, w2, g_kc, rel_bias, jnp.asarray(bkc), mt)


def _s2_copies(idx_ref, pt_ref, cache_ref, cwin_ref, ks_s, vs_s, kw_s, vw_s, sem, b, slot, n_pages):
    cps = []
    n_blk = n_pages * (PAGE // SEL_BLOCK)
    for h in range(N_KV):
        for r in range(N_SEL):
            jb = jnp.minimum(idx_ref[(b * N_SEL + r) * N_KV + h], n_blk - 1)
            page = pt_ref[b * n_pages + jb // 2]
            row0 = pl.multiple_of((jb % 2) * SEL_BLOCK, SEL_BLOCK)
            cps.append(pltpu.make_async_copy(cache_ref.at[page, pl.ds(row0, SEL_BLOCK), 2 * N_KV + h, :],
                                             ks_s.at[slot, h, pl.ds(r * SEL_BLOCK, SEL_BLOCK), :], sem.at[slot]))
            cps.append(pltpu.make_async_copy(cache_ref.at[page, pl.ds(row0, SEL_BLOCK), 3 * N_KV + h, :],
                                             vs_s.at[slot, h, pl.ds(r * SEL_BLOCK, SEL_BLOCK), :], sem.at[slot]))
        cps.append(pltpu.make_async_copy(cwin_ref.at[b, :, h, :], kw_s.at[slot, h], sem.at[slot]))
        cps.append(pltpu.make_async_copy(cwin_ref.at[b, :, N_KV + h, :], vw_s.at[slot, h], sem.at[slot]))
    return cps


def _s2_body(idx_ref, val_ref, pt_ref, cache_ref, cwin_ref, q_ref, oc_ref, gt_ref, kns_ref, vns_ref, knw_ref, vnw_ref,
             rb_ref, bks_ref, bkw_ref, gout_ref, o_ref, ks_s, vs_s, kw_s, vw_s, bs_s, bw_s, sem, *, n_pages, past):
    b = pl.program_id(0)
    nb = pl.num_programs(0)
    slot = b % 2
    args = (idx_ref, pt_ref, cache_ref, cwin_ref, ks_s, vs_s, kw_s, vw_s, sem)

    @pl.when(b == 0)
    def _first():
        for cp in _s2_copies(*args, 0, 0, n_pages):
            cp.start()
        for hd in range(N_HEADS):
            bs_s[hd:hd + 1, :] = _bias_lookup(bks_ref[...], rb_ref, hd)
            bw_s[hd:hd + 1, :] = _bias_lookup(bkw_ref[...], rb_ref, hd)

    @pl.when(b + 1 < nb)
    def _next():
        for cp in _s2_copies(*args, b + 1, 1 - slot, n_pages):
            cp.start()

    for cp in _s2_copies(*args, b, slot, n_pages):
        cp.wait()

    n_blk = n_pages * (PAGE // SEL_BLOCK)
    q = q_ref[0]
    qf = q.astype(F32)
    hrow = lax.broadcasted_iota(I32, (N_HEADS, 1), 0) // Q_PER_KV
    lane = lax.broadcasted_iota(I32, (N_HEADS, LANE), 1)
    bias0 = jnp.concatenate([jnp.full((1, 1), rb_ref[0, hd], F32) for hd in range(N_HEADS)], axis=0)
    b31 = jnp.concatenate([jnp.full((1, 1), rb_ref[REL_BUCKETS - 1, hd], F32) for hd in range(N_HEADS)], axis=0)

    tiles = []
    new_sel = jnp.zeros((N_HEADS, 1), F32)
    for t in range(N_SEL // 2):
        s_t = jnp.zeros((N_HEADS, LANE), F32)
        for h in range(N_KV):
            s_h = _dot_nt(q, ks_s[slot, h, pl.ds(t * LANE, LANE), :].astype(BF16))
            halves = []
            for half in range(2):
                r = 2 * t + half
                jb = idx_ref[(b * N_SEL + r) * N_KV + h]
                okr = (val_ref[(b * N_SEL + r) * N_KV + h] > 0) & (jb < n_blk)
                near = bs_s[:, (half * 2) * LANE:(half * 2 + 1) * LANE]
                nearer = bs_s[:, (half * 2 + 1) * LANE:(half * 2 + 2) * LANE]
                bias = jnp.where(jb == n_blk - 1, nearer, jnp.where(jb == n_blk - 2, near, b31))
                halves.append(jnp.where(okr, s_h + bias, NEG))
                new_sel = jnp.where((hrow == h) & (val_ref[(b * N_SEL + r) * N_KV + h] > 0) & (jb == n_blk), 1.0, new_sel)
            s_h = jnp.where(lane < SEL_BLOCK, halves[0], halves[1])
            s_t = jnp.where(hrow == h, s_h, s_t)
        tiles.append(s_t)
    s_new = jnp.sum(qf * kns_ref[0], axis=-1, keepdims=True) + bias0
    s_new = jnp.where(new_sel > 0.5, s_new, NEG)
    m = s_new
    for s_t in tiles:
        m = jnp.maximum(m, jnp.max(s_t, axis=-1, keepdims=True))
    p_new = jnp.where(new_sel > 0.5, jnp.exp(s_new - m), 0.0)
    l = p_new
    acc = p_new * vns_ref[0]
    for t, s_t in enumerate(tiles):
        p = jnp.where(s_t > 0.5 * NEG, jnp.exp(s_t - m), 0.0)
        l = l + jnp.sum(p, axis=-1, keepdims=True)
        pb = p.astype(BF16)
        for h in range(N_KV):
            pv = _dot(pb, vs_s[slot, h, pl.ds(t * LANE, LANE), :].astype(BF16))
            acc = acc + jnp.where(hrow == h, pv, 0.0)
    o_s = acc / jnp.maximum(l, 1e-30)

    n_w = kw_s.shape[2]
    wt = []
    for t in range(n_w // LANE):
        s_t = jnp.zeros((N_HEADS, LANE), F32)
        for h in range(N_KV):
            s_h = _dot_nt(q, kw_s[slot, h, pl.ds(t * LANE, LANE), :].astype(BF16))
            s_t = jnp.where(hrow == h, s_h, s_t)
        wt.append(s_t + bw_s[:, t * LANE:(t + 1) * LANE])
    s_new = jnp.sum(qf * knw_ref[0], axis=-1, keepdims=True) + bias0
    m = s_new
    for s_t in wt:
        m = jnp.maximum(m, jnp.max(s_t, axis=-1, keepdims=True))
    p_new = jnp.exp(s_new - m)
    l = p_new
    acc = p_new * vnw_ref[0]
    for t, s_t in enumerate(wt):
        p = jnp.exp(s_t - m)
        l = l + jnp.sum(p, axis=-1, keepdims=True)
        pb = p.astype(BF16)
        for h in range(N_KV):
            pv = _dot(pb, vw_s[slot, h, pl.ds(t * LANE, LANE), :].astype(BF16))
            acc = acc + jnp.where(hrow == h, pv, 0.0)
    o_w = acc / l

    gt = gt_ref[0]
    a = gt[:, 0:1] * oc_ref[0] + gt[:, 1:2] * o_s + gt[:, 2:3] * o_w
    ms = jnp.sum(jnp.sum(a * a, axis=-1, keepdims=True), axis=0, keepdims=True) / (N_HEADS * HEAD_DIM)
    o_ref[0] = (a * lax.rsqrt(ms + EPS) * gout_ref[...]).astype(BF16)


def _s2(idx, val, page_table, cache4, cwin4, q3, o_c, gates3, kns, vns, knw, vnw, rel_bias, g_out3):
    n_b, n_pages = page_table.shape
    past = n_pages * PAGE
    n_w = cwin4.shape[1]
    s = np.arange(SEL_BLOCK)
    d_near = past - ((past // SEL_BLOCK - 2) * SEL_BLOCK + s)
    d_nearer = past - ((past // SEL_BLOCK - 1) * SEL_BLOCK + s)
    z = np.zeros(SEL_BLOCK, np.int64)
    bks = np.concatenate([d_near, z, d_nearer, z, z, d_near, z, d_nearer])[None, :]
    bkw = (past - (past - n_w + np.arange(n_w)))[None, :]
    full = lambda a: pl.BlockSpec(a.shape, lambda b, *_: (0,) * a.ndim)
    per_b = lambda a: pl.BlockSpec((1,) + a.shape[1:], lambda b, *_: (b,) + (0,) * (a.ndim - 1))
    body = functools.partial(_s2_body, n_pages=n_pages, past=past)
    return pl.pallas_call(
        body,
        grid_spec=pltpu.PrefetchScalarGridSpec(
            num_scalar_prefetch=3,
            grid=(n_b,),
            in_specs=[pl.BlockSpec(memory_space=pl.ANY), pl.BlockSpec(memory_space=pl.ANY),
                      per_b(q3), per_b(o_c), per_b(gates3), per_b(kns), per_b(vns), per_b(knw), per_b(vnw),
                      pl.BlockSpec(memory_space=pltpu.SMEM), pl.BlockSpec((1, 4 * LANE), lambda b, *_: (0, 0)),
                      pl.BlockSpec((1, n_w), lambda b, *_: (0, 0)), full(g_out3)],
            out_specs=pl.BlockSpec((1, N_HEADS, HEAD_DIM), lambda b, *_: (b, 0, 0)),
            scratch_shapes=[pltpu.VMEM((2, N_KV, N_SEL * SEL_BLOCK, HEAD_DIM), F32),
                            pltpu.VMEM((2, N_KV, N_SEL * SEL_BLOCK, HEAD_DIM), F32),
                            pltpu.VMEM((2, N_KV, n_w, HEAD_DIM), F32), pltpu.VMEM((2, N_KV, n_w, HEAD_DIM), F32),
                            pltpu.VMEM((N_HEADS, 4 * LANE), F32), pltpu.VMEM((N_HEADS, n_w), F32),
                            pltpu.SemaphoreType.DMA((2,))],
        ),
        out_shape=jax.ShapeDtypeStruct((n_b, N_HEADS, HEAD_DIM), BF16),
        compiler_params=_cparams(("arbitrary",)),
        name="sample_attn",
    )(idx, val, page_table.reshape(-1), cache4, cwin4, q3, o_c, gates3, kns, vns, knw, vnw, rel_bias,
      jnp.asarray(_bucket_np(bks)), jnp.asarray(_bucket_np(bkw)), g_out3)


def _winshift_body(cwin_ref, new_ref, o_ref):
    n_w = cwin_ref.shape[1]
    o_ref[0, 0:n_w - 1] = cwin_ref[0, 1:n_w]
    o_ref[0, n_w - 1] = new_ref[0]


def _winshift(cwin4, new3):
    n_b = cwin4.shape[0]
    return pl.pallas_call(
        _winshift_body,
        grid=(n_b,),
        in_specs=[pl.BlockSpec((1,) + cwin4.shape[1:], lambda b: (b, 0, 0, 0)),
                  pl.BlockSpec((1,) + new3.shape[1:], lambda b: (b, 0, 0))],
        out_specs=pl.BlockSpec((1,) + cwin4.shape[1:], lambda b: (b, 0, 0, 0)),
        out_shape=jax.ShapeDtypeStruct(cwin4.shape, F32),
        compiler_params=_cparams(("arbitrary",)),
        name="winshift",
    )(cwin4, new3)


def _block_diag_b(bb_re, bb_im):
    gl = SSM_G // SSM_SG
    eye = jnp.eye(gl, dtype=F32)

    def one(bb):
        t = jnp.transpose(bb, (1, 0, 2)).reshape(SSM_SG, gl, SSM_P, SSM_N)
        return jnp.einsum('sgpn,gh->sgphn', t, eye).reshape(SSM_SG, gl * SSM_P, gl * SSM_N)

    return jnp.concatenate([one(bb_re), one(bb_im)], axis=2).astype(BF16)


def _block_diag_c(c_re, c_im):
    gl = SSM_G // SSM_SG
    eye = jnp.eye(gl, dtype=F32)

    def one(c):
        t = jnp.transpose(c, (0, 2, 1)).reshape(SSM_SG, gl, SSM_N, SSM_P)
        return jnp.einsum('sgnp,gh->sgnhp', t, eye).reshape(SSM_SG, gl * SSM_N, gl * SSM_P)

    return jnp.concatenate([one(c_re), -one(c_im)], axis=1).astype(BF16)


def kernel(x_prompt, x_sample, cache_kv, page_table, cache_win, state_ssm, state_conv, rel_bias, g_mix, w_in, g_q, g_k, w_cmp1, pos_cmp, w_cmp2, lam_re, lam_im, log_dt, b_re, b_im, c_re, c_im, d_skip, w_glu, g_out_attn, g_out_ssm, w_out, g_ffn, w_up, conv_w, conv_b, w_down):
    depth = g_mix.shape[0]
    assert depth == 1 and x_prompt.shape[0] == 1 and x_sample.shape[1] == 1
    seq = x_prompt.shape[1]
    n_b = x_sample.shape[0]
    n_pages = page_table.shape[1]
    n_w = cache_win.shape[2]
    li = 0
    row = lambda v: v.reshape(1, -1)

    w_in_b = jnp.pad(w_in[li].astype(BF16), ((0, 0), (0, IN_COLS_PAD - w_in.shape[2])))
    w1 = w_cmp1[li]
    wcat = jnp.concatenate([w1[:, :CMP_STRIDE], w1[:, CMP_STRIDE:]], axis=-1).astype(BF16).reshape(
        2, CMP_STRIDE // 2, 2 * HEAD_DIM, 2 * HEAD_DIM)
    w2 = w_cmp2[li].astype(BF16)
    w_glu_b = w_glu[li].astype(BF16)
    w_out_b = w_out[li].astype(BF16)
    padc = lambda a: jnp.pad(a, ((0, 0), (0, D_FF_PAD - D_FF)))
    wa = padc(w_up[li][:, :D_FF]).astype(BF16)
    wg = padc(w_up[li][:, D_FF:]).astype(BF16)
    wd = jnp.pad(w_down[li], ((0, D_FF_PAD - D_FF), (0, 0))).astype(BF16)
    cw = padc(conv_w[li])
    cb = padc(row(conv_b[li]))

    ab_re, ab_im, bb_re, bb_im, cpos = _prep(lam_re[li], lam_im[li], log_dt[li], b_re[li], b_im[li], pos_cmp[li], w1)
    wb = _block_diag_b(bb_re, bb_im)
    wc = _block_diag_c(c_re[li], c_im[li])
    ab_re = row(ab_re)
    ab_im = row(ab_im)

    in_args = (row(g_mix[li]), w_in_b, row(g_q[li]), row(g_k[li, 1]), row(g_k[li, 2]))
    ssm_args = (wb, wc, ab_re, ab_im, row(d_skip[li]), w_glu_b, row(g_out_ssm[li]))

    xp = x_prompt[0]
    n_seg = 8
    q, kv, win, u2, gates, cmp_rows, kt, vt = _inproj(xp, *in_args, tm=512, q_scale=HEAD_DIM ** -0.5 * LOG2E, n_seg=n_seg,
                                            attn_operands=True)
    cmp_out = _pcompress(cmp_rows, wcat, cpos, w2, row(g_k[li, 0]))
    zpad = jnp.zeros((CMP_STRIDE, 2 * KV_W), F32)
    cmp_pad = jnp.concatenate([zpad, cmp_out, zpad], axis=0)
    attn_n = _pattn_t(q, gates, kt, vt, cmp_pad[:, :KV_W], cmp_pad[:, KV_W:], cmp_pad[:, KV_W:].T, rel_bias,
                      row(g_out_attn[li]))
    zst = jnp.zeros((n_seg, SSM_LANES), F32)
    _, fre, fim = _ssm(u2, n_seg, zst, zst, *ssm_args, tc=64, emit=False)
    ire, iim = _ssm_chain(fre, fim, ab_re, ab_im, seq // n_seg)
    ssm_n, hre, him = _ssm(u2, n_seg, ire, iim, *ssm_args, tc=64, emit=True)
    zrow = jnp.zeros((1, D_FF_PAD), F32)
    ffn_w = (w_out_b, row(g_ffn[li]), wa, wg, wd, cw, cb)
    y_p, cnew_p = _ffn(xp, attn_n, ssm_n, *ffn_w, zrow, zrow, tm=512, tf=512, seq=True, n_seg=n_seg)

    y_prompt = y_p[None]
    kv_prompt = kv.reshape(1, 1, seq, 4, N_KV, HEAD_DIM)
    win_prompt = win[seq - min(WINDOW, seq):].reshape(1, 1, min(WINDOW, seq), 2, N_KV, HEAD_DIM)
    ssm_prompt = jnp.stack([hre[n_seg - 1], him[n_seg - 1]], axis=-1).reshape(1, 1, SSM_G, SSM_N, 2)
    conv_prompt = cnew_p[6:8, :D_FF].reshape(1, 1, CONV_W - 1, D_FF)

    xs = x_sample[:, 0]
    q_s, kv_s, win_s, u_s, gates_s = _inproj(xs, *in_args, tm=n_b, q_scale=HEAD_DIM ** -0.5)
    cache4 = cache_kv[li].reshape(cache_kv.shape[1], PAGE, 4 * N_KV, HEAD_DIM)
    cwin4 = cache_win[li].reshape(n_b, n_w, 2 * N_KV, HEAD_DIM)
    q3 = q_s.reshape(n_b, N_HEADS, HEAD_DIM)
    o_c, idx, val = _s1(page_table, cache4, q3, wcat, cpos, w2, row(g_k[li, 0]), rel_bias)
    idx = idx[:, :, :N_KV].reshape(-1)
    val = val[:, :, :N_KV].reshape(-1)
    rep = lambda a: jnp.repeat(a.reshape(n_b, N_KV, HEAD_DIM), Q_PER_KV, axis=1)
    kns = rep(kv_s[:, 2 * N_KV:3 * N_KV])
    vns = rep(kv_s[:, 3 * N_KV:4 * N_KV])
    knw = rep(win_s[:, 0:N_KV])
    vnw = rep(win_s[:, N_KV:2 * N_KV])
    g3 = jnp.transpose(gates_s[:, :N_BRANCH * N_HEADS].reshape(n_b, N_BRANCH, N_HEADS), (0, 2, 1))
    g3 = jnp.pad(g3, ((0, 0), (0, 0), (0, LANE - N_BRANCH)))
    attn_s = _s2(idx, val, page_table, cache4, cwin4, q3, o_c, g3, kns, vns, knw, vnw, rel_bias,
                 g_out_attn[li].reshape(N_HEADS, HEAD_DIM))
    st = state_ssm[li].reshape(n_b, SSM_LANES, 2)
    ssm_s, sre, sim = _ssm(u_s, n_b, st[:, :, 0], st[:, :, 1], *ssm_args, tc=1, emit=True)
    sc = state_conv[li]
    y_s, a_s = _ffn(xs, attn_s.reshape(n_b, ATTN_W), ssm_s, *ffn_w, padc(sc[:, 0]), padc(sc[:, 1]),
                    tm=n_b, tf=512, seq=False)
    win_sample = _winshift(cwin4, win_s)

    y_sample = y_s[:, None]
    kv_sample = kv_s.reshape(1, n_b, 1, 4, N_KV, HEAD_DIM)
    win_sample = win_sample.reshape(1, n_b, n_w, 2, N_KV, HEAD_DIM)
    ssm_sample = jnp.stack([sre, sim], axis=-1).reshape(1, n_b, SSM_G, SSM_N, 2)
    conv_sample = jnp.stack([sc[:, 1], a_s[:, :D_FF]], axis=1)[None]
    return (y_prompt, y_sample, kv_prompt, kv_sample, win_prompt, win_sample,
            ssm_prompt, ssm_sample, conv_prompt, conv_sample)
```

```python
import functools
import math

import numpy as np
import jax
import jax.numpy as jnp
from jax import lax
from jax.experimental import pallas as pl
from jax.experimental.pallas import tpu as pltpu

F32 = jnp.float32
BF16 = jnp.bfloat16
I32 = jnp.int32

D_MODEL = 2048
HEAD_DIM = 128
N_HEADS = 8
N_KV = 2
Q_PER_KV = 4
ATTN_W = 1024
KV_W = 256
N_BRANCH = 3
CMP_LEN = 32
CMP_STRIDE = 16
SEL_BLOCK = 64
CMP_PER_SEL = 4
N_SEL = 16
WINDOW = 512
REL_BUCKETS = 32
REL_MAX_DIST = 128
PAGE = 128
SSM_W = 1024
SSM_G = 64
SSM_N = 64
SSM_P = 16
SSM_SG = 8
SSM_LANES = SSM_G * SSM_N
D_FF = 5504
D_FF_PAD = 5632
CONV_W = 3
EPS = 1e-6
NEG = -1e30
QB = 128
LANE = 128
VMEM_LIMIT = 56 * 1024 * 1024


def _cparams(sem):
    return pltpu.CompilerParams(dimension_semantics=sem, vmem_limit_bytes=VMEM_LIMIT)


def _rms(x, g):
    return x * lax.rsqrt(jnp.mean(x * x, axis=-1, keepdims=True) + EPS) * g


def _gelu(x):
    return jax.nn.gelu(x)


def _dot(a, b):
    return jnp.dot(a, b, preferred_element_type=F32)


def _dot_nt(a, b):
    return lax.dot_general(a, b, (((1,), (1,)), ((), ())), preferred_element_type=F32)


def _split3(x):
    hi = x.astype(BF16)
    r1 = x - hi.astype(F32)
    mid = r1.astype(BF16)
    lo = (r1 - mid.astype(F32)).astype(BF16)
    return hi, mid, lo


def _bucket_np(d):
    n = np.maximum(d, 0)
    exact = REL_BUCKETS // 2
    nf = np.maximum(n, 1).astype(np.float32)
    large = exact + (np.log(nf / np.float32(exact)) / np.float32(math.log(REL_MAX_DIST / exact))
                     * np.float32(REL_BUCKETS - exact)).astype(np.int32)
    return np.where(n < exact, n, np.minimum(large, REL_BUCKETS - 1)).astype(np.int32)


def _bias_lookup(bkt, rb_ref, head, shift=None):
    last = rb_ref[REL_BUCKETS - 1, head]
    acc = jnp.full(bkt.shape, last, F32)
    for b in range(REL_BUCKETS - 1):
        acc = jnp.where(bkt == b, rb_ref[b, head], acc)
    if shift:
        acc = acc - last
    return acc


def _cast_pad_body(w_ref, o_ref):
    cols = w_ref.shape[1]
    cols_pad = o_ref.shape[2]
    full = (cols // LANE) * LANE
    o_ref[0, :, 0:full] = w_ref[:, 0:full].astype(BF16)
    if full < cols:
        tail = w_ref[:, full:cols].astype(BF16)
        o_ref[0, :, full:full + LANE] = jnp.concatenate(
            [tail, jnp.zeros((tail.shape[0], full + LANE - cols), BF16)], axis=1)
        full += LANE
    if full < cols_pad:
        o_ref[0, :, full:cols_pad] = jnp.zeros((o_ref.shape[1], cols_pad - full), BF16)


def _cast_pad(w, n_parts, cols_pad, row_block):
    rows, width = w.shape
    cols = width // n_parts
    return pl.pallas_call(
        _cast_pad_body,
        grid=(n_parts, rows // row_block),
        in_specs=[pl.BlockSpec((row_block, cols), lambda p, r: (r, p))],
        out_specs=pl.BlockSpec((1, row_block, cols_pad), lambda p, r: (p, r, 0)),
        out_shape=jax.ShapeDtypeStruct((n_parts, rows, cols_pad), BF16),
        compiler_params=_cparams(("arbitrary", "arbitrary")),
        name="cast_pad",
    )(w)


def _prep_body(lre_ref, lim_ref, ldt_ref, bre_ref, bim_ref, pos_ref, w1_ref,
               abre_ref, abim_ref, bbre_ref, bbim_ref, cpos_ref):
    lr = lre_ref[...]
    li = lim_ref[...]
    dt = jnp.exp(ldt_ref[...])
    mag = jnp.exp(lr * dt)
    ab_re = mag * jnp.cos(li * dt)
    ab_im = mag * jnp.sin(li * dt)
    den = lr * lr + li * li
    nr = ab_re - 1.0
    ni = ab_im
    f_re = (nr * lr + ni * li) / den
    f_im = (ni * lr - nr * li) / den
    abre_ref[...] = ab_re
    abim_ref[...] = ab_im
    for p in range(SSM_P):
        br = bre_ref[p]
        bi = bim_ref[p]
        bbre_ref[p] = f_re * br - f_im * bi
        bbim_ref[p] = f_re * bi + f_im * br
    for kind in range(2):
        cpos_ref[kind] = jnp.dot(pos_ref[kind], w1_ref[kind], preferred_element_type=F32,
                                 precision=lax.Precision.HIGHEST)


def _prep(lam_re, lam_im, log_dt, b_re, b_im, pos_cmp, w_cmp1):
    bre_t = jnp.transpose(b_re, (2, 0, 1))
    bim_t = jnp.transpose(b_im, (2, 0, 1))
    pos = jnp.zeros((2, 8, CMP_LEN * HEAD_DIM), F32).at[:, 0, :].set(pos_cmp.reshape(2, CMP_LEN * HEAD_DIM))
    w1 = w_cmp1.reshape(2, CMP_LEN * HEAD_DIM, HEAD_DIM)
    return pl.pallas_call(
        _prep_body,
        out_shape=[jax.ShapeDtypeStruct((SSM_G, SSM_N), F32), jax.ShapeDtypeStruct((SSM_G, SSM_N), F32),
                   jax.ShapeDtypeStruct((SSM_P, SSM_G, SSM_N), F32), jax.ShapeDtypeStruct((SSM_P, SSM_G, SSM_N), F32),
                   jax.ShapeDtypeStruct((2, 8, HEAD_DIM), F32)],
        compiler_params=pltpu.CompilerParams(vmem_limit_bytes=VMEM_LIMIT),
        name="prep",
    )(lam_re, lam_im, log_dt.reshape(SSM_G, 1), bre_t, bim_t, pos, w1)


IN_COLS_PAD = ATTN_W + 4 * KV_W + 2 * KV_W + SSM_W + LANE
KT_KS = (0, 2 * HEAD_DIM)
KT_E = HEAD_DIM
KT_KW = 3 * HEAD_DIM
KT_COLS = 5 * HEAD_DIM
VT_ROWS = 4 * HEAD_DIM


def _inproj_body(x_ref, gmix_ref, w_ref, gq_ref, gks_ref, gkw_ref,
                 q_ref, kv_ref, win_ref, u_ref, gt_ref, *extra, q_scale):
    tm = x_ref.shape[0]
    xn = _rms(x_ref[...], gmix_ref[...]).astype(BF16)
    zq = _dot(xn, w_ref[:, 0:ATTN_W])
    for h in range(N_HEADS):
        sl = slice(h * HEAD_DIM, (h + 1) * HEAD_DIM)
        q_ref[:, sl] = (_rms(zq[:, sl], gq_ref[...]) * q_scale).astype(BF16)
    zkv = _dot(xn, w_ref[:, ATTN_W:ATTN_W + 4 * KV_W])
    ks, kw = [], []
    for s in range(4 * N_KV):
        col = zkv[:, s * HEAD_DIM:(s + 1) * HEAD_DIM]
        if s // N_KV == 2:
            col = _rms(col, gks_ref[...])
            ks.append(col)
        kv_ref[:, s, :] = col
    c0 = ATTN_W + 4 * KV_W
    zw = _dot(xn, w_ref[:, c0:c0 + 2 * KV_W])
    for s in range(2 * N_KV):
        col = zw[:, s * HEAD_DIM:(s + 1) * HEAD_DIM]
        if s // N_KV == 0:
            col = _rms(col, gkw_ref[...])
            kw.append(col)
        win_ref[:, s, :] = col
    if extra:
        cmp_ref, kt_ref, vt_ref = extra
        cmp_ref[...] = zkv[:, 0:2 * KV_W]
        for h in range(N_KV):
            kt_ref[:, KT_KS[h]:KT_KS[h] + HEAD_DIM] = ks[h].astype(BF16)
            kt_ref[:, KT_KW + h * HEAD_DIM:KT_KW + (h + 1) * HEAD_DIM] = kw[h].astype(BF16)
        blk = (lax.broadcasted_iota(I32, (tm, HEAD_DIM), 0) + pl.program_id(0) * tm) // SEL_BLOCK
        kt_ref[:, KT_E:KT_E + HEAD_DIM] = jnp.where(blk == lax.broadcasted_iota(I32, (tm, HEAD_DIM), 1), 1.0, 0.0).astype(BF16)
        vt_ref[0:KV_W, :] = zkv[:, 3 * KV_W:4 * KV_W].T.astype(BF16)
        vt_ref[KV_W:2 * KV_W, :] = zw[:, KV_W:2 * KV_W].T.astype(BF16)
    c1 = c0 + 2 * KV_W
    zt = _dot(xn, w_ref[:, c1:IN_COLS_PAD])
    gt_ref[...] = jax.nn.sigmoid(zt[:, 0:LANE])
    u_ref[...] = zt[:, N_BRANCH * N_HEADS:N_BRANCH * N_HEADS + SSM_W]


def _seg_spec(tm, rows, n_seg):
    tiles_per_seg = rows // n_seg // tm
    return pl.BlockSpec((tm, SSM_W), lambda i: (i % tiles_per_seg, i // tiles_per_seg))


def _inproj(x, g_mix, w, g_q, g_ks, g_kw, tm, q_scale, n_seg=1, attn_operands=False):
    rows = x.shape[0]
    row_spec = lambda n: pl.BlockSpec((tm, n), lambda i: (i, 0))
    full = lambda a: pl.BlockSpec(a.shape, lambda i: (0,) * a.ndim)
    out_specs = [row_spec(ATTN_W), pl.BlockSpec((tm, 4 * N_KV, HEAD_DIM), lambda i: (i, 0, 0)),
                 pl.BlockSpec((tm, 2 * N_KV, HEAD_DIM), lambda i: (i, 0, 0)), _seg_spec(tm, rows, n_seg), row_spec(LANE)]
    out_shape = [jax.ShapeDtypeStruct((rows, ATTN_W), BF16), jax.ShapeDtypeStruct((rows, 4 * N_KV, HEAD_DIM), F32),
                 jax.ShapeDtypeStruct((rows, 2 * N_KV, HEAD_DIM), F32),
                 jax.ShapeDtypeStruct((rows // n_seg, n_seg * SSM_W), F32), jax.ShapeDtypeStruct((rows, LANE), F32)]
    if attn_operands:
        out_specs += [row_spec(2 * KV_W), row_spec(KT_COLS), pl.BlockSpec((VT_ROWS, tm), lambda i: (0, i))]
        out_shape += [jax.ShapeDtypeStruct((rows, 2 * KV_W), F32), jax.ShapeDtypeStruct((rows, KT_COLS), BF16),
                      jax.ShapeDtypeStruct((VT_ROWS, rows), BF16)]
    return pl.pallas_call(
        functools.partial(_inproj_body, q_scale=q_scale),
        grid=(rows // tm,),
        in_specs=[row_spec(D_MODEL), full(g_mix), full(w), full(g_q), full(g_ks), full(g_kw)],
        out_specs=out_specs,
        out_shape=out_shape,
        compiler_params=_cparams(("arbitrary",)),
        name="inproj",
    )(x, g_mix, w, g_q, g_ks, g_kw)


PERM_ROWS = 2 * PAGE


def _perm_matrix():
    k = np.arange(PERM_ROWS // CMP_STRIDE)
    p = np.arange(CMP_STRIDE)
    m = np.zeros((PERM_ROWS, PERM_ROWS), np.float32)
    m[(p[:, None] * len(k) + k[None, :]).ravel(), (CMP_STRIDE * k[None, :] + p[:, None]).ravel()] = 1.0
    return jnp.asarray(m, dtype=BF16)


def _compress_pair(x_refs, perm_ref, xp_s, wcat2_ref, cpos, w2):
    n_rows = x_refs[0].shape[0]
    n_grp = n_rows // PERM_ROWS
    n_ch = n_rows // CMP_STRIDE
    ck = PERM_ROWS // CMP_STRIDE

    def perm(g, carry):
        rows = pl.ds(pl.multiple_of(g * PERM_ROWS, PERM_ROWS), PERM_ROWS)
        xcat = jnp.concatenate([x_refs[0][rows, :], x_refs[1][rows, :]], axis=1).astype(BF16)
        y = _dot(perm_ref[...], xcat).astype(BF16)
        for hd in range(N_KV):
            chunks = pl.ds(pl.multiple_of(hd * n_ch + g * ck, ck), ck)
            for p in range(CMP_STRIDE):
                xp_s[p, chunks, :] = y[p * ck:(p + 1) * ck, hd * HEAD_DIM:(hd + 1) * HEAD_DIM]
        return carry

    lax.fori_loop(0, n_grp, perm, 0, unroll=8)
    acc = jnp.zeros((N_KV * n_ch, 2 * HEAD_DIM), F32)
    for q in range(CMP_STRIDE // 2):
        acc = acc + _dot(jnp.concatenate([xp_s[2 * q], xp_s[2 * q + 1]], axis=-1), wcat2_ref[q])
    outs = []
    for hd in range(N_KV):
        a = acc[hd * n_ch:(hd + 1) * n_ch]
        e_hi_next = pltpu.roll(a[:, HEAD_DIM:2 * HEAD_DIM], n_ch - 1, 0)
        hid = _gelu(a[:, 0:HEAD_DIM] + e_hi_next + cpos)
        outs.append(_dot(hid.astype(BF16), w2))
    return outs


def _pcompress_body(x0_ref, x1_ref, perm_ref, wcat2_ref, cpos_ref, w2_ref, gkc_ref, o_ref, xp_s):
    kind = pl.program_id(0)
    outs = _compress_pair((x0_ref, x1_ref), perm_ref, xp_s, wcat2_ref.at[0], cpos_ref[0, 0:1, :], w2_ref[0])
    for hd in range(N_KV):
        o_ref[:, hd * HEAD_DIM:(hd + 1) * HEAD_DIM] = jnp.where(kind == 0, _rms(outs[hd], gkc_ref[...]), outs[hd])


def _pcompress(cmp_rows, wcat2, cpos, w2, g_kc):
    rows = cmp_rows.shape[0]
    n_ch = rows // CMP_STRIDE
    perm = _perm_matrix()
    return pl.pallas_call(
        _pcompress_body,
        grid=(2,),
        in_specs=[pl.BlockSpec((rows, HEAD_DIM), lambda kd: (0, 2 * kd)), pl.BlockSpec((rows, HEAD_DIM), lambda kd: (0, 2 * kd + 1)),
                  pl.BlockSpec(perm.shape, lambda kd: (0, 0)),
                  pl.BlockSpec((1,) + wcat2.shape[1:], lambda kd: (kd, 0, 0, 0)),
                  pl.BlockSpec((1, 8, HEAD_DIM), lambda kd: (kd, 0, 0)),
                  pl.BlockSpec((1, HEAD_DIM, HEAD_DIM), lambda kd: (kd, 0, 0)),
                  pl.BlockSpec((1, HEAD_DIM), lambda kd: (0, 0))],
        out_specs=pl.BlockSpec((n_ch, KV_W), lambda kd: (0, kd)),
        out_shape=jax.ShapeDtypeStruct((n_ch, 2 * KV_W), F32),
        scratch_shapes=[pltpu.VMEM((CMP_STRIDE, N_KV * n_ch, HEAD_DIM), BF16)],
        compiler_params=_cparams(("arbitrary",)),
        name="pcompress",
    )(cmp_rows, cmp_rows, perm, wcat2, cpos, w2, g_kc)


def _select_rounds(score_t, on_pick=None):
    n_j = score_t.shape[0]
    jio = lax.broadcasted_iota(I32, score_t.shape, 0)
    sc = score_t
    for r in range(N_SEL):
        m = jnp.max(sc, axis=0, keepdims=True)
        idx = jnp.min(jnp.where(sc == m, jio, n_j), axis=0, keepdims=True)
        sc = jnp.where(jio == idx, -jnp.inf, sc)
        if on_pick is not None:
            on_pick(r, idx, m >= 0.0)
    return (sc == -jnp.inf) & (score_t >= 0.0)


LOG2E = 1.4426950408889634


ONES_ROWS = 16
FAR_TILES = 8


def _col_max(tiles):
    m = jnp.max(tiles[0], axis=0, keepdims=True)
    for s in tiles[1:]:
        m = jnp.maximum(m, jnp.max(s, axis=0, keepdims=True))
    return m


def _lanes4(x):
    return jnp.concatenate([x] * Q_PER_KV, axis=1)


def _pattn_t_body(q_ref, gt_ref, kt_ref, vt_ref, kc_ref, vc_ref, vct_ref, rb_ref, bk0_ref, bk1_ref, bkc_ref, mft_ref,
                  gout_ref, o_ref, b0_s, b1_s, bc_s, bw4_s):
    i = pl.program_id(0)
    cols = Q_PER_KV * QB

    @pl.when(i == 0)
    def _tables():
        b_io = lax.broadcasted_iota(I32, (QB, QB), 0)
        a_io = lax.broadcasted_iota(I32, (QB, QB), 1)
        for hd in range(N_HEADS):
            t0 = _bias_lookup(bk0_ref[...], rb_ref, hd, shift=True) * LOG2E
            b0_s[hd] = jnp.where(a_io >= b_io, t0, NEG)
            b1_s[hd] = _bias_lookup(bk1_ref[...], rb_ref, hd, shift=True) * LOG2E
            bc_s[hd] = _bias_lookup(bkc_ref[...], rb_ref, hd, shift=True) * LOG2E
        bw4_s[...] = jnp.where(b_io >= a_io, 0.0, NEG)

    gt_t = gt_ref[...].T
    n_far = jnp.maximum(i - 1, 0) // FAR_TILES
    far_keys = FAR_TILES * QB

    def tab(ref, h):
        return jnp.concatenate([ref[Q_PER_KV * h + g] for g in range(Q_PER_KV)], axis=1)

    def v_aug(row0, start, n):
        return jnp.concatenate([vt_ref[row0:row0 + HEAD_DIM, pl.ds(start, n)], jnp.ones((ONES_ROWS, n), BF16)], axis=0)

    q_ts, qa_ts, o_cs, sel_st = [], [], [], []
    for h in range(N_KV):
        q_t = jnp.concatenate(
            [q_ref[:, (Q_PER_KV * h + g) * HEAD_DIM:(Q_PER_KV * h + g + 1) * HEAD_DIM].astype(F32).T.astype(BF16)
             for g in range(Q_PER_KV)], axis=1)
        hs = slice(h * HEAD_DIM, (h + 1) * HEAD_DIM)

        n_c = kc_ref.shape[0] - 2 * CMP_STRIDE
        near0 = pl.multiple_of(8 * i, 8)
        cf = lax.broadcasted_iota(I32, (n_c, QB), 0)
        ok_f = _lanes4((cf < 8 * i) & (cf >= CMP_STRIDE))
        s_f = jnp.where(ok_f, _dot(kc_ref[0:n_c, hs].astype(BF16), q_t), NEG)
        cn = lax.broadcasted_iota(I32, (32, cols), 0)
        a_n = lax.broadcasted_iota(I32, (32, cols), 1) % QB
        ok_n = (CMP_STRIDE * (cn - CMP_STRIDE) <= a_n - (CMP_LEN - 1)) & (cn + 8 * i >= CMP_STRIDE)
        s_n = jnp.where(ok_n, _dot(kc_ref[pl.ds(near0, 32), hs].astype(BF16), q_t) + tab(bc_s, h), NEG)
        m_c = _col_max([s_f, s_n])
        p_f = jnp.where(ok_f, jnp.exp2(s_f - m_c), 0.0)
        p_n = jnp.where(ok_n, jnp.exp2(s_n - m_c), 0.0)
        l_c = jnp.sum(p_f, axis=0, keepdims=True) + jnp.sum(p_n, axis=0, keepdims=True)
        inv_c = 1.0 / jnp.maximum(l_c, 1e-30)
        vc_near_t = vc_ref[pl.ds(near0, 32), hs].T.astype(BF16)
        o_c = (_dot(vct_ref[hs, 0:n_c].astype(BF16), p_f.astype(BF16)) + _dot(vc_near_t, p_n.astype(BF16))) * inv_c
        pn_f = p_f * inv_c
        pn_n = p_n * inv_c
        imp_f = pn_f[:, 0:QB] + pn_f[:, QB:2 * QB] + pn_f[:, 2 * QB:3 * QB] + pn_f[:, 3 * QB:4 * QB]
        imp_n = pn_n[:, 0:QB] + pn_n[:, QB:2 * QB] + pn_n[:, 2 * QB:3 * QB] + pn_n[:, 3 * QB:4 * QB]
        jn = lax.broadcasted_iota(I32, (QB, 32), 0)
        cc = lax.broadcasted_iota(I32, (QB, 32), 1) + 8 * i - CMP_STRIDE
        mnt = (((cc // CMP_PER_SEL) == jn) | (cc == CMP_PER_SEL * jn - 1)) & (cc >= 0)
        mnt = jnp.where(mnt, 1.0, 0.0).astype(BF16)
        imp_t = jnp.zeros((QB, QB), F32)
        for part in _split3(imp_f):
            imp_t = imp_t + _dot(mft_ref[...], part)
        for part in _split3(imp_n):
            imp_t = imp_t + _dot(mnt, part)
        j_io = lax.broadcasted_iota(I32, (QB, QB), 0)
        cur = 2 * i + lax.broadcasted_iota(I32, (QB, QB), 1) // SEL_BLOCK
        forced = (j_io == 0) | (j_io == cur) | (j_io == cur - 1)
        score_t = jnp.where(forced, 1e9, jnp.where(j_io <= cur, imp_t, -1.0))
        sel_t = _select_rounds(score_t)
        selneg = _lanes4(jnp.where(sel_t, 0.0, NEG).astype(BF16))
        qa_t = jnp.concatenate([q_t, selneg] if h == 0 else [selneg, q_t], axis=0)
        ka0 = h * HEAD_DIM

        near_s, near_v = [], []
        for back in range(FAR_TILES + 1):
            kt = i - back
            start = pl.multiple_of(jnp.maximum(kt, 0) * QB, QB)
            s = _dot(kt_ref[pl.ds(start, QB), ka0:ka0 + 2 * HEAD_DIM], qa_t)
            if back == 0:
                s = s + tab(b0_s, h)
            elif back == 1:
                s = s + tab(b1_s, h) + jnp.where(kt >= 0, 0.0, NEG)
            else:
                s = s + jnp.where((kt >= 0) & (kt >= FAR_TILES * n_far), 0.0, NEG)
            near_s.append(s)
            near_v.append(v_aug(h * HEAD_DIM, start, QB))
        m_s = _col_max(near_s)
        acc_s = jnp.zeros((HEAD_DIM + ONES_ROWS, cols), F32)
        for s, v in zip(near_s, near_v):
            acc_s = acc_s + _dot(v, jnp.exp2(s - m_s).astype(BF16))
        q_ts.append(q_t)
        qa_ts.append(qa_t)
        o_cs.append(o_c)
        sel_st += [m_s, acc_s]

    def far_step(k, st):
        start = pl.multiple_of(k * far_keys, far_keys)
        out = []
        for h in range(N_KV):
            m, acc = st[2 * h], st[2 * h + 1]
            s = _dot(kt_ref[pl.ds(start, far_keys), h * HEAD_DIM:(h + 2) * HEAD_DIM], qa_ts[h])
            m_new = jnp.maximum(m, jnp.max(s, axis=0, keepdims=True))
            p = jnp.exp2(s - m_new).astype(BF16)
            out += [m_new, jnp.exp2(m - m_new) * acc + _dot(v_aug(h * HEAD_DIM, start, far_keys), p)]
        return tuple(out)

    sel_st = lax.fori_loop(0, n_far, far_step, tuple(sel_st))

    attn = [None] * N_HEADS
    for h in range(N_KV):
        q_t = q_ts[h]
        acc_s = sel_st[2 * h + 1]
        o_s = acc_s[0:HEAD_DIM] / acc_s[HEAD_DIM:HEAD_DIM + 1]
        o_c = o_cs[h]

        kw0 = KT_KW + h * HEAD_DIM
        win_s, win_v = [], []
        for back in range(5):
            kt = i - back
            start = pl.multiple_of(jnp.maximum(kt, 0) * QB, QB)
            s = _dot(kt_ref[pl.ds(start, QB), kw0:kw0 + HEAD_DIM], q_t)
            if back == 0:
                s = s + tab(b0_s, h)
            elif back == 1:
                s = s + tab(b1_s, h)
            elif back == 4:
                s = s + _lanes4(bw4_s[...])
            if back > 0:
                s = s + jnp.where(kt >= 0, 0.0, NEG)
            win_s.append(s)
            win_v.append(v_aug((N_KV + h) * HEAD_DIM, start, QB))
        m_w = _col_max(win_s)
        acc_w = jnp.zeros((HEAD_DIM + ONES_ROWS, cols), F32)
        for s, v in zip(win_s, win_v):
            acc_w = acc_w + _dot(v, jnp.exp2(s - m_w).astype(BF16))
        o_w = acc_w[0:HEAD_DIM] / acc_w[HEAD_DIM:HEAD_DIM + 1]

        for g in range(Q_PER_KV):
            hd = Q_PER_KV * h + g
            cs = slice(g * QB, (g + 1) * QB)
            o_t = (gt_t[hd:hd + 1] * o_c[:, cs] + gt_t[N_HEADS + hd:N_HEADS + hd + 1] * o_s[:, cs]
                   + gt_t[2 * N_HEADS + hd:2 * N_HEADS + hd + 1] * o_w[:, cs])
            attn[hd] = o_t.T
    a = jnp.concatenate(attn, axis=1)
    o_ref[...] = _rms(a, gout_ref[...]).astype(BF16)


def _pattn_t_tables():
    b = np.arange(QB)[:, None]
    a = np.arange(QB)[None, :]
    bk0 = _bucket_np(a - b)
    bk1 = _bucket_np(a - b + QB)
    c = np.arange(32)[:, None] - CMP_STRIDE
    bkc = _bucket_np(a - CMP_STRIDE * c - (CMP_LEN - 1))
    cidx = np.arange(4 * QB)[None, :] - CMP_STRIDE
    j = np.arange(QB)[:, None]
    mft = (((cidx // CMP_PER_SEL) == j) | (cidx == CMP_PER_SEL * j - 1)) & (cidx >= 0)
    return (jnp.asarray(bk0), jnp.asarray(bk1), jnp.asarray(bkc), jnp.asarray(mft.astype(np.float32), dtype=BF16))


def _pattn_t(q, gates, kt, vt, kcp, vcp, vcpt, rel_bias, g_out):
    rows = q.shape[0]
    bk0, bk1, bkc, mft = _pattn_t_tables()
    full = lambda a: pl.BlockSpec(a.shape, lambda i: (0,) * a.ndim)
    once = lambda a: pl.BlockSpec(a.shape, lambda i: (0,) * a.ndim, pipeline_mode=pl.Buffered(1))
    return pl.pallas_call(
        _pattn_t_body,
        grid=(rows // QB,),
        in_specs=[pl.BlockSpec((QB, ATTN_W), lambda i: (i, 0)), pl.BlockSpec((QB, LANE), lambda i: (i, 0)),
                  once(kt), once(vt), full(kcp), full(vcp), full(vcpt), pl.BlockSpec(memory_space=pltpu.SMEM),
                  full(bk0), full(bk1), full(bkc), full(mft), full(g_out)],
        out_specs=pl.BlockSpec((QB, ATTN_W), lambda i: (i, 0)),
        out_shape=jax.ShapeDtypeStruct((rows, ATTN_W), BF16),
        scratch_shapes=[pltpu.VMEM((N_HEADS, QB, QB), F32), pltpu.VMEM((N_HEADS, QB, QB), F32),
                        pltpu.VMEM((N_HEADS, 32, QB), F32), pltpu.VMEM((QB, QB), F32)],
        compiler_params=_cparams(("arbitrary",)),
        name="pattn",
    )(q, gates, kt, vt, kcp, vcp, vcpt, rel_bias, bk0, bk1, bkc, mft, g_out)


def _ssm_body(u_ref, ire_ref, iim_ref, wb_ref, wc_ref, abre_ref, abim_ref, dsk_ref, wglu_ref, gout_ref,
              y_ref, fre_ref, fim_ref, u_s, xre_s, xim_s, y_s, sre_s, sim_s, *, n_seg, tc, emit):
    c = pl.program_id(0)
    sgl = SSM_LANES // SSM_SG

    @pl.when(c == 0)
    def _init():
        sre_s[...] = ire_ref[...]
        sim_s[...] = iim_ref[...]

    n_lt = SSM_W // LANE
    if tc == 1:
        u_cols = [u_ref[:, l * LANE:(l + 1) * LANE] for l in range(n_lt)]
    else:
        for s in range(n_seg):
            for l in range(n_lt):
                u_s[l, pl.ds(s, tc, stride=n_seg), :] = u_ref[:, s * SSM_W + l * LANE:s * SSM_W + (l + 1) * LANE]
        u_cols = [u_s[l] for l in range(n_lt)]
    for sg in range(SSM_SG):
        ls = slice(sg * sgl, (sg + 1) * sgl)
        bu = _dot(u_cols[sg].astype(BF16), wb_ref[sg])
        xre_s[:, ls] = bu[:, 0:sgl]
        xim_s[:, ls] = bu[:, sgl:2 * sgl]
        ar = abre_ref[:, ls]
        ai = abim_ref[:, ls]

        def step(t, carry):
            xr, xi = carry
            rows = pl.ds(pl.multiple_of(t * n_seg, n_seg), n_seg)
            nr = ar * xr - ai * xi + xre_s[rows, ls]
            ni = ar * xi + ai * xr + xim_s[rows, ls]
            xre_s[rows, ls] = nr
            xim_s[rows, ls] = ni
            return nr, ni

        xr, xi = lax.fori_loop(0, tc, step, (sre_s[:, ls], sim_s[:, ls]), unroll=min(tc, 4))
        sre_s[:, ls] = xr
        sim_s[:, ls] = xi
    fre_ref[...] = sre_s[...]
    fim_ref[...] = sim_s[...]
    if not emit:
        y_ref[...] = jnp.zeros(y_ref.shape, y_ref.dtype)
        return
    ys = []
    for sg in range(SSM_SG):
        ls = slice(sg * sgl, (sg + 1) * sgl)
        x2 = jnp.concatenate([xre_s[:, ls], xim_s[:, ls]], axis=1).astype(BF16)
        ys.append(_dot(x2, wc_ref[sg]))
    y = jnp.concatenate(ys, axis=1) + dsk_ref[...] * jnp.concatenate(u_cols, axis=1)
    z = _dot(_gelu(y).astype(BF16), wglu_ref[...])
    o = z[:, 0:SSM_W] * jax.nn.sigmoid(z[:, SSM_W:2 * SSM_W])
    yn = _rms(o, gout_ref[...])
    if tc == 1:
        y_ref[...] = yn.astype(BF16)
    else:
        for l in range(n_lt):
            y_s[l] = yn[:, l * LANE:(l + 1) * LANE]
        for s in range(n_seg):
            for l in range(n_lt):
                y_ref[:, s * SSM_W + l * LANE:s * SSM_W + (l + 1) * LANE] = (
                    y_s[l, pl.ds(s, tc, stride=n_seg), :].astype(BF16))


def _ssm(u2, n_seg, init_re, init_im, wb, wc, ab_re, ab_im, d_skip, w_glu, g_out, tc, emit):
    if tc == 1:
        assert u2.shape == (n_seg, SSM_W)
        t_len = 1
        blk = (n_seg, SSM_W)
    else:
        t_len = u2.shape[0]
        assert u2.shape[1] == n_seg * SSM_W and n_seg % 8 == 0
        blk = (tc, n_seg * SSM_W)
    rows = tc * n_seg
    full = lambda a: pl.BlockSpec(a.shape, lambda c: (0,) * a.ndim)
    body = functools.partial(_ssm_body, n_seg=n_seg, tc=tc, emit=emit)
    st = jax.ShapeDtypeStruct((n_seg, SSM_LANES), F32)
    y_shape = u2.shape if emit else blk
    y_map = (lambda c: (c, 0)) if emit else (lambda c: (0, 0))
    return pl.pallas_call(
        body,
        grid=(t_len // tc,),
        in_specs=[pl.BlockSpec(blk, lambda c: (c, 0)),
                  full(init_re), full(init_im), full(wb), full(wc),
                  full(ab_re), full(ab_im), full(d_skip), full(w_glu), full(g_out)],
        out_specs=[pl.BlockSpec(blk, y_map), full(init_re), full(init_im)],
        out_shape=[jax.ShapeDtypeStruct(y_shape, BF16), st, st],
        scratch_shapes=[pltpu.VMEM((SSM_W // LANE, rows, LANE), F32), pltpu.VMEM((rows, SSM_LANES), F32),
                        pltpu.VMEM((rows, SSM_LANES), F32), pltpu.VMEM((SSM_W // LANE, rows, LANE), F32),
                        pltpu.VMEM((n_seg, SSM_LANES), F32), pltpu.VMEM((n_seg, SSM_LANES), F32)],
        compiler_params=_cparams(("arbitrary",)),
        name="ssm_emit" if emit else "ssm_final",
    )(u2, init_re, init_im, wb, wc, ab_re, ab_im, d_skip, w_glu, g_out)


def _ssm_chain_body(fre_ref, fim_ref, abre_ref, abim_ref, ire_ref, iim_ref, *, n_seg, log2_len):
    pr = abre_ref[...]
    pi = abim_ref[...]
    for _ in range(log2_len):
        pr, pi = pr * pr - pi * pi, 2.0 * pr * pi
    cr = jnp.zeros((1, SSM_LANES), F32)
    ci = jnp.zeros((1, SSM_LANES), F32)
    for j in range(n_seg):
        ire_ref[j:j + 1, :] = cr
        iim_ref[j:j + 1, :] = ci
        fr = fre_ref[j:j + 1, :]
        fi = fim_ref[j:j + 1, :]
        cr, ci = fr + pr * cr - pi * ci, fi + pr * ci + pi * cr


def _ssm_chain(fre, fim, ab_re, ab_im, seg_len):
    n_seg = fre.shape[0]
    log2_len = int(math.log2(seg_len))
    assert 2 ** log2_len == seg_len
    st = jax.ShapeDtypeStruct((n_seg, SSM_LANES), F32)
    return pl.pallas_call(
        functools.partial(_ssm_chain_body, n_seg=n_seg, log2_len=log2_len),
        out_shape=[st, st],
        compiler_params=pltpu.CompilerParams(vmem_limit_bytes=VMEM_LIMIT),
        name="ssm_chain",
    )(fre, fim, ab_re, ab_im)


FFN_CHUNKS = 2


def _ffn_body(x_ref, a_ref, s_ref, wo_ref, gffn_ref, wa_ref, wg_ref, wd_ref, cw_ref, cb_ref, p2_ref, p1_ref,
              y_ref, cnew_ref, hn_s, car_s, *, seq, tm):
    r = pl.program_id(0)
    j = pl.program_id(1)

    @pl.when(j == 0)
    def _mix():
        h = x_ref[...] + _dot(a_ref[...], wo_ref[0:ATTN_W, :]) + _dot(s_ref[...], wo_ref[ATTN_W:D_MODEL, :])
        y_ref[...] = h
        hn_s[...] = _rms(h, gffn_ref[...]).astype(BF16)

    if seq:
        @pl.when(r == 0)
        def _first():
            car_s[j, 6:7, :] = p2_ref[...]
            car_s[j, 7:8, :] = p1_ref[...]

    hn = hn_s[...]
    tf = wa_ref.shape[1]
    tc = tf // FFN_CHUNKS
    part = None
    for k in range(FFN_CHUNKS):
        cs = slice(k * tc, (k + 1) * tc)
        a = _dot(hn, wa_ref[:, cs])
        g = _dot(hn, wg_ref[:, cs])
        if seq:
            p2 = car_s[j, 6:7, cs]
            p1 = car_s[j, 7:8, cs]
            row = lax.broadcasted_iota(I32, a.shape, 0)
            a1 = jnp.where(row == 0, p1, pltpu.roll(a, 1, 0))
            a2 = jnp.where(row == 0, p2, jnp.where(row == 1, p1, pltpu.roll(a, 2, 0)))
            car_s[j, :, cs] = a[tm - 8:tm, :]
            cnew_ref[:, pl.ds(pl.multiple_of(j * tf + k * tc, tc), tc)] = a[tm - 8:tm, :]
        else:
            a1 = p1_ref[:, cs]
            a2 = p2_ref[:, cs]
            cnew_ref[:, cs] = a
        c = cb_ref[:, cs] + cw_ref[0:1, cs] * a2 + cw_ref[1:2, cs] * a1 + cw_ref[2:3, cs] * a
        d = _dot((_gelu(c) * g).astype(BF16), wd_ref[cs, :])
        part = d if part is None else part + d
    y_ref[...] += part


def _ffn(x, a_n, s_n, w_out, g_ffn, w_up, wd, cw, cb, p2, p1, tm, tf, seq, n_seg=1):
    rows = x.shape[0]
    nj = D_FF_PAD // tf
    body = functools.partial(_ffn_body, seq=seq, tm=tm)
    tiles_per_seg = rows // n_seg // tm
    if seq:
        tap_spec = pl.BlockSpec((1, tf), lambda r, j: (0, j))
        cnew_spec = pl.BlockSpec((8, D_FF_PAD), lambda r, j: (0, 0))
        cnew_shape = jax.ShapeDtypeStruct((8, D_FF_PAD), F32)
    else:
        tap_spec = pl.BlockSpec((tm, tf), lambda r, j: (r, j))
        cnew_spec = pl.BlockSpec((tm, tf), lambda r, j: (r, j))
        cnew_shape = jax.ShapeDtypeStruct((rows, D_FF_PAD), F32)
    return pl.pallas_call(
        body,
        grid=(rows // tm, nj),
        in_specs=[pl.BlockSpec((tm, D_MODEL), lambda r, j: (r, 0)), pl.BlockSpec((tm, ATTN_W), lambda r, j: (r, 0)),
                  pl.BlockSpec((tm, SSM_W), lambda r, j: (r % tiles_per_seg, r // tiles_per_seg)),
                  pl.BlockSpec(w_out.shape, lambda r, j: (0, 0), pipeline_mode=pl.Buffered(1)),
                  pl.BlockSpec((1, D_MODEL), lambda r, j: (0, 0)),
                  pl.BlockSpec((None, D_MODEL, tf), lambda r, j: (0, 0, j)),
                  pl.BlockSpec((None, D_MODEL, tf), lambda r, j: (1, 0, j)),
                  pl.BlockSpec((tf, D_MODEL), lambda r, j: (j, 0)), pl.BlockSpec((CONV_W, tf), lambda r, j: (0, j)),
                  pl.BlockSpec((1, tf), lambda r, j: (0, j)), tap_spec, tap_spec],
        out_specs=[pl.BlockSpec((tm, D_MODEL), lambda r, j: (r, 0)), cnew_spec],
        out_shape=[jax.ShapeDtypeStruct((rows, D_MODEL), F32), cnew_shape],
        scratch_shapes=[pltpu.VMEM((tm, D_MODEL), BF16), pltpu.VMEM((nj, 8, tf), F32)],
        compiler_params=_cparams(("arbitrary", "arbitrary")),
        name="ffn_seq" if seq else "ffn_rows",
    )(x, a_n, s_n, w_out, g_ffn, w_up, w_up, wd, cw, cb, p2, p1)


def _s1_copies(pt_ref, cache_ref, x_s, sem, b, slot, n_pages):
    cps = []
    for pg in range(n_pages):
        page = pt_ref[b * n_pages + pg]
        for s in range(2 * N_KV):
            cps.append(pltpu.make_async_copy(cache_ref.at[page, :, s, :],
                                             x_s.at[slot, s, pl.ds(pg * PAGE, PAGE), :], sem.at[slot]))
    return cps


def _s1_body(pt_ref, cache_ref, q_ref, perm_ref, wcat2_ref, cpos_ref, w2_ref, gkc_ref, rb_ref, bkc_ref, mt_ref,
             oc_ref, idx_ref, val_ref, x_s, xp_s, bias_s, sem, *, n_pages, past):
    b = pl.program_id(0)
    nb = pl.num_programs(0)
    slot = b % 2

    @pl.when(b == 0)
    def _first():
        for cp in _s1_copies(pt_ref, cache_ref, x_s, sem, 0, 0, n_pages):
            cp.start()
        for hd in range(N_HEADS):
            bias_s[hd:hd + 1, :] = _bias_lookup(bkc_ref[...], rb_ref, hd)

    @pl.when(b + 1 < nb)
    def _next():
        for cp in _s1_copies(pt_ref, cache_ref, x_s, sem, b + 1, 1 - slot, n_pages):
            cp.start()

    for cp in _s1_copies(pt_ref, cache_ref, x_s, sem, b, slot, n_pages):
        cp.wait()

    n_c = past // CMP_STRIDE
    cio = lax.broadcasted_iota(I32, (N_HEADS, n_c), 1)
    hrow = lax.broadcasted_iota(I32, (N_HEADS, n_c), 0) // Q_PER_KV
    q = q_ref[0]
    s_all = jnp.zeros((N_HEADS, n_c), F32)
    kcs = _compress_pair((x_s.at[slot, 0], x_s.at[slot, 1]), perm_ref, xp_s, wcat2_ref.at[0], cpos_ref[0, 0:1, :], w2_ref[0])
    vcs = _compress_pair((x_s.at[slot, N_KV], x_s.at[slot, N_KV + 1]), perm_ref, xp_s, wcat2_ref.at[1], cpos_ref[1, 0:1, :],
                         w2_ref[1])
    vcs = [v.astype(BF16) for v in vcs]
    for h in range(N_KV):
        kc = _rms(kcs[h], gkc_ref[...])
        s_all = jnp.where(hrow == h, _dot_nt(q, kc.astype(BF16)), s_all)
    ok = cio < n_c - 1
    s_all = jnp.where(ok, s_all + bias_s[...], NEG)
    m = jnp.max(s_all, axis=-1, keepdims=True)
    p = jnp.where(ok, jnp.exp(s_all - m), 0.0)
    p = p / jnp.maximum(jnp.sum(p, axis=-1, keepdims=True), 1e-30)
    pb = p.astype(BF16)
    hrow_o = lax.broadcasted_iota(I32, (N_HEADS, HEAD_DIM), 0) // Q_PER_KV
    o_c = jnp.zeros((N_HEADS, HEAD_DIM), F32)
    for h in range(N_KV):
        o_c = jnp.where(hrow_o == h, _dot(pb, vcs[h]), o_c)
    oc_ref[0] = o_c
    rio = lax.broadcasted_iota(I32, (8, n_c), 0)
    imp = jnp.zeros((8, n_c), F32)
    for h in range(N_KV):
        ih = p[4 * h:4 * h + 1] + p[4 * h + 1:4 * h + 2] + p[4 * h + 2:4 * h + 3] + p[4 * h + 3:4 * h + 4]
        imp = jnp.where(rio == h, ih, imp)
    imp = jnp.concatenate([imp, jnp.zeros((LANE - 8, n_c), F32)], axis=0)
    n_j = mt_ref.shape[0]
    imp_t = jnp.zeros((n_j, LANE), F32)
    for part in _split3(imp):
        imp_t = imp_t + _dot_nt(mt_ref[...], part)
    j_io = lax.broadcasted_iota(I32, (n_j, LANE), 0)
    cur = past // SEL_BLOCK
    forced = (j_io == 0) | (j_io == cur) | (j_io == cur - 1)
    score_t = jnp.where(forced, 1e9, jnp.where(j_io <= cur, imp_t, -1.0))
    score_t = jnp.where(j_io <= cur, score_t, -jnp.inf)

    def on_pick(r, idx, okv):
        idx_ref[0, r:r + 1, :] = idx
        val_ref[0, r:r + 1, :] = jnp.where(okv, 1, 0)

    _select_rounds(score_t, on_pick)


def _s1(page_table, cache4, q3, wcat2, cpos, w2, g_kc, rel_bias):
    n_b, n_pages = page_table.shape
    past = n_pages * PAGE
    n_c = past // CMP_STRIDE
    perm = _perm_matrix()
    ns = past // SEL_BLOCK + 1
    n_j = -(-ns // 8) * 8
    c = np.arange(n_c)[None, :]
    bkc = _bucket_np(past - (CMP_STRIDE * c + CMP_LEN - 1))
    j = np.arange(n_j)[:, None]
    mt = (((c // CMP_PER_SEL) == j) | (c == CMP_PER_SEL * j - 1)) & (c < n_c - 1)
    mt = jnp.asarray(mt.astype(np.float32), dtype=BF16)
    full = lambda a: pl.BlockSpec(a.shape, lambda b, pt: (0,) * a.ndim)
    body = functools.partial(_s1_body, n_pages=n_pages, past=past)
    return pl.pallas_call(
        body,
        grid_spec=pltpu.PrefetchScalarGridSpec(
            num_scalar_prefetch=1,
            grid=(n_b,),
            in_specs=[pl.BlockSpec(memory_space=pl.ANY), pl.BlockSpec((1, N_HEADS, HEAD_DIM), lambda b, pt: (b, 0, 0)),
                      full(perm), full(wcat2), full(cpos), full(w2), full(g_kc), pl.BlockSpec(memory_space=pltpu.SMEM),
                      pl.BlockSpec((1, n_c), lambda b, pt: (0, 0)), full(mt)],
            out_specs=[pl.BlockSpec((1, N_HEADS, HEAD_DIM), lambda b, pt: (b, 0, 0)),
                       pl.BlockSpec((1, N_SEL, LANE), lambda b, pt: (b, 0, 0)),
                       pl.BlockSpec((1, N_SEL, LANE), lambda b, pt: (b, 0, 0))],
            scratch_shapes=[pltpu.VMEM((2, 2 * N_KV, past, HEAD_DIM), F32),
                            pltpu.VMEM((CMP_STRIDE, N_KV * n_c, HEAD_DIM), BF16), pltpu.VMEM((N_HEADS, n_c), F32),
                            pltpu.SemaphoreType.DMA((2,))],
        ),
        out_shape=[jax.ShapeDtypeStruct((n_b, N_HEADS, HEAD_DIM), F32), jax.ShapeDtypeStruct((n_b, N_SEL, LANE), I32),
                   jax.ShapeDtypeStruct((n_b, N_SEL, LANE), I32)],
        compiler_params=_cparams(("arbitrary",)),
        name="sample_cmp",
    )(page_table.reshape(-1), cache4, q3, perm, wcat2, cpos, w2, g_kc, rel_bias, jnp.asarray(bkc), mt)


def _s2_copies(idx_ref, pt_ref, cache_ref, cwin_ref, ks_s, vs_s, kw_s, vw_s, sem, b, slot, n_pages):
    cps = []
    n_blk = n_pages * (PAGE // SEL_BLOCK)
    for h in range(N_KV):
        for r in range(N_SEL):
            jb = jnp.minimum(idx_ref[(b * N_SEL + r) * N_KV + h], n_blk - 1)
            page = pt_ref[b * n_pages + jb // 2]
            row0 = pl.multiple_of((jb % 2) * SEL_BLOCK, SEL_BLOCK)
            cps.append(pltpu.make_async_copy(cache_ref.at[page, pl.ds(row0, SEL_BLOCK), 2 * N_KV + h, :],
                                             ks_s.at[slot, h, pl.ds(r * SEL_BLOCK, SEL_BLOCK), :], sem.at[slot]))
            cps.append(pltpu.make_async_copy(cache_ref.at[page, pl.ds(row0, SEL_BLOCK), 3 * N_KV + h, :],
                                             vs_s.at[slot, h, pl.ds(r * SEL_BLOCK, SEL_BLOCK), :], sem.at[slot]))
        cps.append(pltpu.make_async_copy(cwin_ref.at[b, :, h, :], kw_s.at[slot, h], sem.at[slot]))
        cps.append(pltpu.make_async_copy(cwin_ref.at[b, :, N_KV + h, :], vw_s.at[slot, h], sem.at[slot]))
    return cps


def _s2_body(idx_ref, val_ref, pt_ref, cache_ref, cwin_ref, q_ref, oc_ref, gt_ref, kns_ref, vns_ref, knw_ref, vnw_ref,
             rb_ref, bks_ref, bkw_ref, gout_ref, o_ref, ks_s, vs_s, kw_s, vw_s, bs_s, bw_s, sem, *, n_pages, past):
    b = pl.program_id(0)
    nb = pl.num_programs(0)
    slot = b % 2
    args = (idx_ref, pt_ref, cache_ref, cwin_ref, ks_s, vs_s, kw_s, vw_s, sem)

    @pl.when(b == 0)
    def _first():
        for cp in _s2_copies(*args, 0, 0, n_pages):
            cp.start()
        for hd in range(N_HEADS):
            bs_s[hd:hd + 1, :] = _bias_lookup(bks_ref[...], rb_ref, hd)
            bw_s[hd:hd + 1, :] = _bias_lookup(bkw_ref[...], rb_ref, hd)

    @pl.when(b + 1 < nb)
    def _next():
        for cp in _s2_copies(*args, b + 1, 1 - slot, n_pages):
            cp.start()

    for cp in _s2_copies(*args, b, slot, n_pages):
        cp.wait()

    n_blk = n_pages * (PAGE // SEL_BLOCK)
    q = q_ref[0]
    qf = q.astype(F32)
    hrow = lax.broadcasted_iota(I32, (N_HEADS, 1), 0) // Q_PER_KV
    lane = lax.broadcasted_iota(I32, (N_HEADS, LANE), 1)
    bias0 = jnp.concatenate([jnp.full((1, 1), rb_ref[0, hd], F32) for hd in range(N_HEADS)], axis=0)
    b31 = jnp.concatenate([jnp.full((1, 1), rb_ref[REL_BUCKETS - 1, hd], F32) for hd in range(N_HEADS)], axis=0)

    tiles = []
    new_sel = jnp.zeros((N_HEADS, 1), F32)
    for t in range(N_SEL // 2):
        s_t = jnp.zeros((N_HEADS, LANE), F32)
        for h in range(N_KV):
            s_h = _dot_nt(q, ks_s[slot, h, pl.ds(t * LANE, LANE), :].astype(BF16))
            halves = []
            for half in range(2):
                r = 2 * t + half
                jb = idx_ref[(b * N_SEL + r) * N_KV + h]
                okr = (val_ref[(b * N_SEL + r) * N_KV + h] > 0) & (jb < n_blk)
                near = bs_s[:, (half * 2) * LANE:(half * 2 + 1) * LANE]
                nearer = bs_s[:, (half * 2 + 1) * LANE:(half * 2 + 2) * LANE]
                bias = jnp.where(jb == n_blk - 1, nearer, jnp.where(jb == n_blk - 2, near, b31))
                halves.append(jnp.where(okr, s_h + bias, NEG))
                new_sel = jnp.where((hrow == h) & (val_ref[(b * N_SEL + r) * N_KV + h] > 0) & (jb == n_blk), 1.0, new_sel)
            s_h = jnp.where(lane < SEL_BLOCK, halves[0], halves[1])
            s_t = jnp.where(hrow == h, s_h, s_t)
        tiles.append(s_t)
    s_new = jnp.sum(qf * kns_ref[0], axis=-1, keepdims=True) + bias0
    s_new = jnp.where(new_sel > 0.5, s_new, NEG)
    m = s_new
    for s_t in tiles:
        m = jnp.maximum(m, jnp.max(s_t, axis=-1, keepdims=True))
    p_new = jnp.where(new_sel > 0.5, jnp.exp(s_new - m), 0.0)
    l = p_new
    acc = p_new * vns_ref[0]
    for t, s_t in enumerate(tiles):
        p = jnp.where(s_t > 0.5 * NEG, jnp.exp(s_t - m), 0.0)
        l = l + jnp.sum(p, axis=-1, keepdims=True)
        pb = p.astype(BF16)
        for h in range(N_KV):
            pv = _dot(pb, vs_s[slot, h, pl.ds(t * LANE, LANE), :].astype(BF16))
            acc = acc + jnp.where(hrow == h, pv, 0.0)
    o_s = acc / jnp.maximum(l, 1e-30)

    n_w = kw_s.shape[2]
    wt = []
    for t in range(n_w // LANE):
        s_t = jnp.zeros((N_HEADS, LANE), F32)
        for h in range(N_KV):
            s_h = _dot_nt(q, kw_s[slot, h, pl.ds(t * LANE, LANE), :].astype(BF16))
            s_t = jnp.where(hrow == h, s_h, s_t)
        wt.append(s_t + bw_s[:, t * LANE:(t + 1) * LANE])
    s_new = jnp.sum(qf * knw_ref[0], axis=-1, keepdims=True) + bias0
    m = s_new
    for s_t in wt:
        m = jnp.maximum(m, jnp.max(s_t, axis=-1, keepdims=True))
    p_new = jnp.exp(s_new - m)
    l = p_new
    acc = p_new * vnw_ref[0]
    for t, s_t in enumerate(wt):
        p = jnp.exp(s_t - m)
        l = l + jnp.sum(p, axis=-1, keepdims=True)
        pb = p.astype(BF16)
        for h in range(N_KV):
            pv = _dot(pb, vw_s[slot, h, pl.ds(t * LANE, LANE), :].astype(BF16))
            acc = acc + jnp.where(hrow == h, pv, 0.0)
    o_w = acc / l

    gt = gt_ref[0]
    a = gt[:, 0:1] * oc_ref[0] + gt[:, 1:2] * o_s + gt[:, 2:3] * o_w
    ms = jnp.sum(jnp.sum(a * a, axis=-1, keepdims=True), axis=0, keepdims=True) / (N_HEADS * HEAD_DIM)
    o_ref[0] = (a * lax.rsqrt(ms + EPS) * gout_ref[...]).astype(BF16)


def _s2(idx, val, page_table, cache4, cwin4, q3, o_c, gates3, kns, vns, knw, vnw, rel_bias, g_out3):
    n_b, n_pages = page_table.shape
    past = n_pages * PAGE
    n_w = cwin4.shape[1]
    s = np.arange(SEL_BLOCK)
    d_near = past - ((past // SEL_BLOCK - 2) * SEL_BLOCK + s)
    d_nearer = past - ((past // SEL_BLOCK - 1) * SEL_BLOCK + s)
    z = np.zeros(SEL_BLOCK, np.int64)
    bks = np.concatenate([d_near, z, d_nearer, z, z, d_near, z, d_nearer])[None, :]
    bkw = (past - (past - n_w + np.arange(n_w)))[None, :]
    full = lambda a: pl.BlockSpec(a.shape, lambda b, *_: (0,) * a.ndim)
    per_b = lambda a: pl.BlockSpec((1,) + a.shape[1:], lambda b, *_: (b,) + (0,) * (a.ndim - 1))
    body = functools.partial(_s2_body, n_pages=n_pages, past=past)
    return pl.pallas_call(
        body,
        grid_spec=pltpu.PrefetchScalarGridSpec(
            num_scalar_prefetch=3,
            grid=(n_b,),
            in_specs=[pl.BlockSpec(memory_space=pl.ANY), pl.BlockSpec(memory_space=pl.ANY),
                      per_b(q3), per_b(o_c), per_b(gates3), per_b(kns), per_b(vns), per_b(knw), per_b(vnw),
                      pl.BlockSpec(memory_space=pltpu.SMEM), pl.BlockSpec((1, 4 * LANE), lambda b, *_: (0, 0)),
                      pl.BlockSpec((1, n_w), lambda b, *_: (0, 0)), full(g_out3)],
            out_specs=pl.BlockSpec((1, N_HEADS, HEAD_DIM), lambda b, *_: (b, 0, 0)),
            scratch_shapes=[pltpu.VMEM((2, N_KV, N_SEL * SEL_BLOCK, HEAD_DIM), F32),
                            pltpu.VMEM((2, N_KV, N_SEL * SEL_BLOCK, HEAD_DIM), F32),
                            pltpu.VMEM((2, N_KV, n_w, HEAD_DIM), F32), pltpu.VMEM((2, N_KV, n_w, HEAD_DIM), F32),
                            pltpu.VMEM((N_HEADS, 4 * LANE), F32), pltpu.VMEM((N_HEADS, n_w), F32),
                            pltpu.SemaphoreType.DMA((2,))],
        ),
        out_shape=jax.ShapeDtypeStruct((n_b, N_HEADS, HEAD_DIM), BF16),
        compiler_params=_cparams(("arbitrary",)),
        name="sample_attn",
    )(idx, val, page_table.reshape(-1), cache4, cwin4, q3, o_c, gates3, kns, vns, knw, vnw, rel_bias,
      jnp.asarray(_bucket_np(bks)), jnp.asarray(_bucket_np(bkw)), g_out3)


def _winshift_body(cwin_ref, new_ref, o_ref):
    n_w = cwin_ref.shape[1]
    o_ref[0, 0:n_w - 1] = cwin_ref[0, 1:n_w]
    o_ref[0, n_w - 1] = new_ref[0]


def _winshift(cwin4, new3):
    n_b = cwin4.shape[0]
    return pl.pallas_call(
        _winshift_body,
        grid=(n_b,),
        in_specs=[pl.BlockSpec((1,) + cwin4.shape[1:], lambda b: (b, 0, 0, 0)),
                  pl.BlockSpec((1,) + new3.shape[1:], lambda b: (b, 0, 0))],
        out_specs=pl.BlockSpec((1,) + cwin4.shape[1:], lambda b: (b, 0, 0, 0)),
        out_shape=jax.ShapeDtypeStruct(cwin4.shape, F32),
        compiler_params=_cparams(("arbitrary",)),
        name="winshift",
    )(cwin4, new3)


def _block_diag_b(bb_re, bb_im):
    gl = SSM_G // SSM_SG
    eye = jnp.eye(gl, dtype=F32)

    def one(bb):
        t = jnp.transpose(bb, (1, 0, 2)).reshape(SSM_SG, gl, SSM_P, SSM_N)
        return jnp.einsum('sgpn,gh->sgphn', t, eye).reshape(SSM_SG, gl * SSM_P, gl * SSM_N)

    return jnp.concatenate([one(bb_re), one(bb_im)], axis=2).astype(BF16)


def _block_diag_c(c_re, c_im):
    gl = SSM_G // SSM_SG
    eye = jnp.eye(gl, dtype=F32)

    def one(c):
        t = jnp.transpose(c, (0, 2, 1)).reshape(SSM_SG, gl, SSM_N, SSM_P)
        return jnp.einsum('sgnp,gh->sgnhp', t, eye).reshape(SSM_SG, gl * SSM_N, gl * SSM_P)

    return jnp.concatenate([one(c_re), -one(c_im)], axis=1).astype(BF16)


def kernel(x_prompt, x_sample, cache_kv, page_table, cache_win, state_ssm, state_conv, rel_bias, g_mix, w_in, g_q, g_k, w_cmp1, pos_cmp, w_cmp2, lam_re, lam_im, log_dt, b_re, b_im, c_re, c_im, d_skip, w_glu, g_out_attn, g_out_ssm, w_out, g_ffn, w_up, conv_w, conv_b, w_down):
    depth = g_mix.shape[0]
    assert depth == 1 and x_prompt.shape[0] == 1 and x_sample.shape[1] == 1
    seq = x_prompt.shape[1]
    n_b = x_sample.shape[0]
    n_pages = page_table.shape[1]
    n_w = cache_win.shape[2]
    li = 0
    row = lambda v: v.reshape(1, -1)

    w_in_b = _cast_pad(w_in[li], 1, IN_COLS_PAD, 256)[0]
    w1 = w_cmp1[li]
    wcat = jnp.concatenate([w1[:, :CMP_STRIDE], w1[:, CMP_STRIDE:]], axis=-1).astype(BF16).reshape(
        2, CMP_STRIDE // 2, 2 * HEAD_DIM, 2 * HEAD_DIM)
    w2 = w_cmp2[li].astype(BF16)
    w_glu_b = w_glu[li].astype(BF16)
    w_out_b = w_out[li].astype(BF16)
    padc = lambda a: jnp.pad(a, ((0, 0), (0, D_FF_PAD - D_FF)))
    w_up_b = _cast_pad(w_up[li], 2, D_FF_PAD, 256)
    wd = jnp.pad(w_down[li], ((0, D_FF_PAD - D_FF), (0, 0))).astype(BF16)
    cw = padc(conv_w[li])
    cb = padc(row(conv_b[li]))

    ab_re, ab_im, bb_re, bb_im, cpos = _prep(lam_re[li], lam_im[li], log_dt[li], b_re[li], b_im[li], pos_cmp[li], w1)
    wb = _block_diag_b(bb_re, bb_im)
    wc = _block_diag_c(c_re[li], c_im[li])
    ab_re = row(ab_re)
    ab_im = row(ab_im)

    in_args = (row(g_mix[li]), w_in_b, row(g_q[li]), row(g_k[li, 1]), row(g_k[li, 2]))
    ssm_args = (wb, wc, ab_re, ab_im, row(d_skip[li]), w_glu_b, row(g_out_ssm[li]))

    xp = x_prompt[0]
    n_seg = 8
    q, kv, win, u2, gates, cmp_rows, kt, vt = _inproj(xp, *in_args, tm=512, q_scale=HEAD_DIM ** -0.5 * LOG2E, n_seg=n_seg,
                                            attn_operands=True)
    cmp_out = _pcompress(cmp_rows, wcat, cpos, w2, row(g_k[li, 0]))
    zpad = jnp.zeros((CMP_STRIDE, 2 * KV_W), F32)
    cmp_pad = jnp.concatenate([zpad, cmp_out, zpad], axis=0)
    attn_n = _pattn_t(q, gates, kt, vt, cmp_pad[:, :KV_W], cmp_pad[:, KV_W:], cmp_pad[:, KV_W:].T, rel_bias,
                      row(g_out_attn[li]))
    zst = jnp.zeros((n_seg, SSM_LANES), F32)
    _, fre, fim = _ssm(u2, n_seg, zst, zst, *ssm_args, tc=64, emit=False)
    ire, iim = _ssm_chain(fre, fim, ab_re, ab_im, seq // n_seg)
    ssm_n, hre, him = _ssm(u2, n_seg, ire, iim, *ssm_args, tc=64, emit=True)
    zrow = jnp.zeros((1, D_FF_PAD), F32)
    ffn_w = (w_out_b, row(g_ffn[li]), w_up_b, wd, cw, cb)
    y_p, cnew_p = _ffn(xp, attn_n, ssm_n, *ffn_w, zrow, zrow, tm=512, tf=512, seq=True, n_seg=n_seg)

    y_prompt = y_p[None]
    kv_prompt = kv.reshape(1, 1, seq, 4, N_KV, HEAD_DIM)
    win_prompt = win[seq - min(WINDOW, seq):].reshape(1, 1, min(WINDOW, seq), 2, N_KV, HEAD_DIM)
    ssm_prompt = jnp.stack([hre[n_seg - 1], him[n_seg - 1]], axis=-1).reshape(1, 1, SSM_G, SSM_N, 2)
    conv_prompt = cnew_p[6:8, :D_FF].reshape(1, 1, CONV_W - 1, D_FF)

    xs = x_sample[:, 0]
    q_s, kv_s, win_s, u_s, gates_s = _inproj(xs, *in_args, tm=n_b, q_scale=HEAD_DIM ** -0.5)
    cache4 = cache_kv[li].reshape(cache_kv.shape[1], PAGE, 4 * N_KV, HEAD_DIM)
    cwin4 = cache_win[li].reshape(n_b, n_w, 2 * N_KV, HEAD_DIM)
    q3 = q_s.reshape(n_b, N_HEADS, HEAD_DIM)
    o_c, idx, val = _s1(page_table, cache4, q3, wcat, cpos, w2, row(g_k[li, 0]), rel_bias)
    idx = idx[:, :, :N_KV].reshape(-1)
    val = val[:, :, :N_KV].reshape(-1)
    rep = lambda a: jnp.repeat(a.reshape(n_b, N_KV, HEAD_DIM), Q_PER_KV, axis=1)
    kns = rep(kv_s[:, 2 * N_KV:3 * N_KV])
    vns = rep(kv_s[:, 3 * N_KV:4 * N_KV])
    knw = rep(win_s[:, 0:N_KV])
    vnw = rep(win_s[:, N_KV:2 * N_KV])
    g3 = jnp.transpose(gates_s[:, :N_BRANCH * N_HEADS].reshape(n_b, N_BRANCH, N_HEADS), (0, 2, 1))
    g3 = jnp.pad(g3, ((0, 0), (0, 0), (0, LANE - N_BRANCH)))
    attn_s = _s2(idx, val, page_table, cache4, cwin4, q3, o_c, g3, kns, vns, knw, vnw, rel_bias,
                 g_out_attn[li].reshape(N_HEADS, HEAD_DIM))
    st = state_ssm[li].reshape(n_b, SSM_LANES, 2)
    ssm_s, sre, sim = _ssm(u_s, n_b, st[:, :, 0], st[:, :, 1], *ssm_args, tc=1, emit=True)
    sc = state_conv[li]
    y_s, a_s = _ffn(xs, attn_s.reshape(n_b, ATTN_W), ssm_s, *ffn_w, padc(sc[:, 0]), padc(sc[:, 1]),
                    tm=n_b, tf=512, seq=False)
    win_sample = _winshift(cwin4, win_s)

    y_sample = y_s[:, None]
    kv_sample = kv_s.reshape(1, n_b, 1, 4, N_KV, HEAD_DIM)
    win_sample = win_sample.reshape(1, n_b, n_w, 2, N_KV, HEAD_DIM)
    ssm_sample = jnp.stack([sre, sim], axis=-1).reshape(1, n_b, SSM_G, SSM_N, 2)
    conv_sample = jnp.stack([sc[:, 1], a_s[:, :D_FF]], axis=1)[None]
    return (y_prompt, y_sample, kv_prompt, kv_sample, win_prompt, win_sample,
            ssm_prompt, ssm_sample, conv_prompt, conv_sample)
```

```python
import functools
import math

import numpy as np
import jax
import jax.numpy as jnp
from jax import lax
from jax.experimental import pallas as pl
from jax.experimental.pallas import tpu as pltpu

F32 = jnp.float32
BF16 = jnp.bfloat16
I32 = jnp.int32

D_MODEL = 2048
HEAD_DIM = 128
N_HEADS = 8
N_KV = 2
Q_PER_KV = 4
ATTN_W = 1024
KV_W = 256
N_BRANCH = 3
CMP_LEN = 32
CMP_STRIDE = 16
SEL_BLOCK = 64
CMP_PER_SEL = 4
N_SEL = 16
WINDOW = 512
REL_BUCKETS = 32
REL_MAX_DIST = 128
PAGE = 128
SSM_W = 1024
SSM_G = 64
SSM_N = 64
SSM_P = 16
SSM_SG = 8
SSM_LANES = SSM_G * SSM_N
D_FF = 5504
D_FF_PAD = 5632
CONV_W = 3
EPS = 1e-6
NEG = -1e30
QB = 128
LANE = 128
VMEM_LIMIT = 56 * 1024 * 1024


def _cparams(sem):
    return pltpu.CompilerParams(dimension_semantics=sem, vmem_limit_bytes=VMEM_LIMIT)


def _rms(x, g):
    return x * lax.rsqrt(jnp.mean(x * x, axis=-1, keepdims=True) + EPS) * g


def _gelu(x):
    return jax.nn.gelu(x)


def _dot(a, b):
    return jnp.dot(a, b, preferred_element_type=F32)


def _dot_nt(a, b):
    return lax.dot_general(a, b, (((1,), (1,)), ((), ())), preferred_element_type=F32)


def _split3(x):
    hi = x.astype(BF16)
    r1 = x - hi.astype(F32)
    mid = r1.astype(BF16)
    lo = (r1 - mid.astype(F32)).astype(BF16)
    return hi, mid, lo


def _bucket_np(d):
    n = np.maximum(d, 0)
    exact = REL_BUCKETS // 2
    nf = np.maximum(n, 1).astype(np.float32)
    large = exact + (np.log(nf / np.float32(exact)) / np.float32(math.log(REL_MAX_DIST / exact))
                     * np.float32(REL_BUCKETS - exact)).astype(np.int32)
    return np.where(n < exact, n, np.minimum(large, REL_BUCKETS - 1)).astype(np.int32)


def _bias_lookup(bkt, rb_ref, head, shift=None):
    last = rb_ref[REL_BUCKETS - 1, head]
    acc = jnp.full(bkt.shape, last, F32)
    for b in range(REL_BUCKETS - 1):
        acc = jnp.where(bkt == b, rb_ref[b, head], acc)
    if shift:
        acc = acc - last
    return acc


def _cast_pad_body(w_ref, o_ref):
    cols = w_ref.shape[1]
    cols_pad = o_ref.shape[2]
    full = (cols // LANE) * LANE
    o_ref[0, :, 0:full] = w_ref[:, 0:full].astype(BF16)
    if full < cols:
        tail = w_ref[:, full:cols].astype(BF16)
        o_ref[0, :, full:full + LANE] = jnp.concatenate(
            [tail, jnp.zeros((tail.shape[0], full + LANE - cols), BF16)], axis=1)
        full += LANE
    if full < cols_pad:
        o_ref[0, :, full:cols_pad] = jnp.zeros((o_ref.shape[1], cols_pad - full), BF16)


def _cast_pad(w, n_parts, cols_pad, row_block):
    rows, width = w.shape
    cols = width // n_parts
    return pl.pallas_call(
        _cast_pad_body,
        grid=(n_parts, rows // row_block),
        in_specs=[pl.BlockSpec((row_block, cols), lambda p, r: (r, p))],
        out_specs=pl.BlockSpec((1, row_block, cols_pad), lambda p, r: (p, r, 0)),
        out_shape=jax.ShapeDtypeStruct((n_parts, rows, cols_pad), BF16),
        compiler_params=_cparams(("arbitrary", "arbitrary")),
        name="cast_pad",
    )(w)


def _cast_rows_body(w_ref, o_ref, *, n_src):
    i = pl.program_id(0)
    o_ref[...] = jnp.where(i < n_src, w_ref[...].astype(BF16), jnp.zeros(o_ref.shape, BF16))


def _cast_pad_rows(w, rows_pad, row_block):
    rows, cols = w.shape
    n_src = rows // row_block
    assert rows == n_src * row_block and row_block % 16 == 0 and rows_pad - rows <= row_block
    return pl.pallas_call(
        functools.partial(_cast_rows_body, n_src=n_src),
        grid=(n_src + 1,),
        in_specs=[pl.BlockSpec((row_block, cols), lambda i: (jnp.minimum(i, n_src - 1), 0))],
        out_specs=pl.BlockSpec((row_block, cols), lambda i: (i, 0)),
        out_shape=jax.ShapeDtypeStruct((rows_pad, cols), BF16),
        compiler_params=_cparams(("arbitrary",)),
        name="cast_pad_rows",
    )(w)


def _cast_t_body(w_ref, tail_ref, o_ref, *, n_src):
    i = pl.program_id(0)
    x = jnp.where(i < n_src, w_ref[...], tail_ref[...])
    o_ref[...] = x.T.astype(BF16)


def _cast_transposed(wt, cols_pad):
    cols, rows = wt.shape
    n_src = cols // LANE
    assert cols_pad == (n_src + 1) * LANE and cols > n_src * LANE
    tail = jnp.pad(wt[n_src * LANE:], ((0, cols_pad - cols), (0, 0)))
    return pl.pallas_call(
        functools.partial(_cast_t_body, n_src=n_src),
        grid=(n_src + 1,),
        in_specs=[pl.BlockSpec((LANE, rows), lambda i: (jnp.minimum(i, n_src - 1), 0)),
                  pl.BlockSpec((LANE, rows), lambda i: (0, 0))],
        out_specs=pl.BlockSpec((rows, LANE), lambda i: (0, i)),
        out_shape=jax.ShapeDtypeStruct((rows, cols_pad), BF16),
        compiler_params=_cparams(("arbitrary",)),
        name="cast_transposed",
    )(wt, tail)


def _prep_body(lre_ref, lim_ref, ldt_ref, bre_ref, bim_ref, pos_ref, w1_ref,
               abre_ref, abim_ref, bbre_ref, bbim_ref, cpos_ref):
    lr = lre_ref[...]
    li = lim_ref[...]
    dt = jnp.exp(ldt_ref[...])
    mag = jnp.exp(lr * dt)
    ab_re = mag * jnp.cos(li * dt)
    ab_im = mag * jnp.sin(li * dt)
    den = lr * lr + li * li
    nr = ab_re - 1.0
    ni = ab_im
    f_re = (nr * lr + ni * li) / den
    f_im = (ni * lr - nr * li) / den
    abre_ref[...] = ab_re
    abim_ref[...] = ab_im
    for p in range(SSM_P):
        br = bre_ref[p]
        bi = bim_ref[p]
        bbre_ref[p] = f_re * br - f_im * bi
        bbim_ref[p] = f_re * bi + f_im * br
    for kind in range(2):
        cpos_ref[kind] = jnp.dot(pos_ref[kind], w1_ref[kind], preferred_element_type=F32,
                                 precision=lax.Precision.HIGHEST)


def _prep(lam_re, lam_im, log_dt, b_re, b_im, pos_cmp, w_cmp1):
    bre_t = jnp.transpose(b_re, (2, 0, 1))
    bim_t = jnp.transpose(b_im, (2, 0, 1))
    pos = jnp.zeros((2, 8, CMP_LEN * HEAD_DIM), F32).at[:, 0, :].set(pos_cmp.reshape(2, CMP_LEN * HEAD_DIM))
    w1 = w_cmp1.reshape(2, CMP_LEN * HEAD_DIM, HEAD_DIM)
    return pl.pallas_call(
        _prep_body,
        out_shape=[jax.ShapeDtypeStruct((SSM_G, SSM_N), F32), jax.ShapeDtypeStruct((SSM_G, SSM_N), F32),
                   jax.ShapeDtypeStruct((SSM_P, SSM_G, SSM_N), F32), jax.ShapeDtypeStruct((SSM_P, SSM_G, SSM_N), F32),
                   jax.ShapeDtypeStruct((2, 8, HEAD_DIM), F32)],
        compiler_params=pltpu.CompilerParams(vmem_limit_bytes=VMEM_LIMIT),
        name="prep",
    )(lam_re, lam_im, log_dt.reshape(SSM_G, 1), bre_t, bim_t, pos, w1)


IN_COLS_PAD = ATTN_W + 4 * KV_W + 2 * KV_W + SSM_W + LANE
KT_KS = (0, 2 * HEAD_DIM)
KT_E = HEAD_DIM
KT_KW = 3 * HEAD_DIM
KT_COLS = 5 * HEAD_DIM
VT_ROWS = 4 * HEAD_DIM


def _inproj_body(x_ref, gmix_ref, w_ref, gq_ref, gks_ref, gkw_ref,
                 q_ref, kv_ref, win_ref, u_ref, gt_ref, *extra, q_scale):
    tm = x_ref.shape[0]
    xn = _rms(x_ref[...], gmix_ref[...]).astype(BF16)
    zq = _dot(xn, w_ref[:, 0:ATTN_W])
    for h in range(N_HEADS):
        sl = slice(h * HEAD_DIM, (h + 1) * HEAD_DIM)
        q_ref[:, sl] = (_rms(zq[:, sl], gq_ref[...]) * q_scale).astype(BF16)
    zkv = _dot(xn, w_ref[:, ATTN_W:ATTN_W + 4 * KV_W])
    ks, kw = [], []
    for s in range(4 * N_KV):
        col = zkv[:, s * HEAD_DIM:(s + 1) * HEAD_DIM]
        if s // N_KV == 2:
            col = _rms(col, gks_ref[...])
            ks.append(col)
        kv_ref[:, s, :] = col
    c0 = ATTN_W + 4 * KV_W
    zw = _dot(xn, w_ref[:, c0:c0 + 2 * KV_W])
    for s in range(2 * N_KV):
        col = zw[:, s * HEAD_DIM:(s + 1) * HEAD_DIM]
        if s // N_KV == 0:
            col = _rms(col, gkw_ref[...])
            kw.append(col)
        win_ref[:, s, :] = col
    if extra:
        cmp_ref, kt_ref, vt_ref = extra
        cmp_ref[...] = zkv[:, 0:2 * KV_W]
        for h in range(N_KV):
            kt_ref[:, KT_KS[h]:KT_KS[h] + HEAD_DIM] = ks[h].astype(BF16)
            kt_ref[:, KT_KW + h * HEAD_DIM:KT_KW + (h + 1) * HEAD_DIM] = kw[h].astype(BF16)
        blk = (lax.broadcasted_iota(I32, (tm, HEAD_DIM), 0) + pl.program_id(0) * tm) // SEL_BLOCK
        kt_ref[:, KT_E:KT_E + HEAD_DIM] = jnp.where(blk == lax.broadcasted_iota(I32, (tm, HEAD_DIM), 1), 1.0, 0.0).astype(BF16)
        vt_ref[0:KV_W, :] = zkv[:, 3 * KV_W:4 * KV_W].T.astype(BF16)
        vt_ref[KV_W:2 * KV_W, :] = zw[:, KV_W:2 * KV_W].T.astype(BF16)
    c1 = c0 + 2 * KV_W
    zt = _dot(xn, w_ref[:, c1:IN_COLS_PAD])
    gt_ref[...] = jax.nn.sigmoid(zt[:, 0:LANE])
    u_ref[...] = zt[:, N_BRANCH * N_HEADS:N_BRANCH * N_HEADS + SSM_W]


def _seg_spec(tm, rows, n_seg):
    tiles_per_seg = rows // n_seg // tm
    return pl.BlockSpec((tm, SSM_W), lambda i: (i % tiles_per_seg, i // tiles_per_seg))


def _inproj(x, g_mix, w, g_q, g_ks, g_kw, tm, q_scale, n_seg=1, attn_operands=False):
    rows = x.shape[0]
    row_spec = lambda n: pl.BlockSpec((tm, n), lambda i: (i, 0))
    full = lambda a: pl.BlockSpec(a.shape, lambda i: (0,) * a.ndim)
    out_specs = [row_spec(ATTN_W), pl.BlockSpec((tm, 4 * N_KV, HEAD_DIM), lambda i: (i, 0, 0)),
                 pl.BlockSpec((tm, 2 * N_KV, HEAD_DIM), lambda i: (i, 0, 0)), _seg_spec(tm, rows, n_seg), row_spec(LANE)]
    out_shape = [jax.ShapeDtypeStruct((rows, ATTN_W), BF16), jax.ShapeDtypeStruct((rows, 4 * N_KV, HEAD_DIM), F32),
                 jax.ShapeDtypeStruct((rows, 2 * N_KV, HEAD_DIM), F32),
                 jax.ShapeDtypeStruct((rows // n_seg, n_seg * SSM_W), F32), jax.ShapeDtypeStruct((rows, LANE), F32)]
    if attn_operands:
        out_specs += [row_spec(2 * KV_W), row_spec(KT_COLS), pl.BlockSpec((VT_ROWS, tm), lambda i: (0, i))]
        out_shape += [jax.ShapeDtypeStruct((rows, 2 * KV_W), F32), jax.ShapeDtypeStruct((rows, KT_COLS), BF16),
                      jax.ShapeDtypeStruct((VT_ROWS, rows), BF16)]
    return pl.pallas_call(
        functools.partial(_inproj_body, q_scale=q_scale),
        grid=(rows // tm,),
        in_specs=[row_spec(D_MODEL), full(g_mix), full(w), full(g_q), full(g_ks), full(g_kw)],
        out_specs=out_specs,
        out_shape=out_shape,
        compiler_params=_cparams(("arbitrary",)),
        name="inproj",
    )(x, g_mix, w, g_q, g_ks, g_kw)


PERM_ROWS = 2 * PAGE


def _perm_matrix():
    k = np.arange(PERM_ROWS // CMP_STRIDE)
    p = np.arange(CMP_STRIDE)
    m = np.zeros((PERM_ROWS, PERM_ROWS), np.float32)
    m[(p[:, None] * len(k) + k[None, :]).ravel(), (CMP_STRIDE * k[None, :] + p[:, None]).ravel()] = 1.0
    return jnp.asarray(m, dtype=BF16)


def _compress_pair(x_refs, perm_ref, xp_s, wcat2_ref, cpos, w2):
    n_rows = x_refs[0].shape[0]
    n_grp = n_rows // PERM_ROWS
    n_ch = n_rows // CMP_STRIDE
    ck = PERM_ROWS // CMP_STRIDE

    def perm(g, carry):
        rows = pl.ds(pl.multiple_of(g * PERM_ROWS, PERM_ROWS), PERM_ROWS)
        xcat = jnp.concatenate([x_refs[0][rows, :], x_refs[1][rows, :]], axis=1).astype(BF16)
        y = _dot(perm_ref[...], xcat).astype(BF16)
        for hd in range(N_KV):
            chunks = pl.ds(pl.multiple_of(hd * n_ch + g * ck, ck), ck)
            for p in range(CMP_STRIDE):
                xp_s[p, chunks, :] = y[p * ck:(p + 1) * ck, hd * HEAD_DIM:(hd + 1) * HEAD_DIM]
        return carry

    lax.fori_loop(0, n_grp, perm, 0, unroll=8)
    acc = jnp.zeros((N_KV * n_ch, 2 * HEAD_DIM), F32)
    for q in range(CMP_STRIDE // 2):
        acc = acc + _dot(jnp.concatenate([xp_s[2 * q], xp_s[2 * q + 1]], axis=-1), wcat2_ref[q])
    outs = []
    for hd in range(N_KV):
        a = acc[hd * n_ch:(hd + 1) * n_ch]
        e_hi_next = pltpu.roll(a[:, HEAD_DIM:2 * HEAD_DIM], n_ch - 1, 0)
        hid = _gelu(a[:, 0:HEAD_DIM] + e_hi_next + cpos)
        outs.append(_dot(hid.astype(BF16), w2))
    return outs


def _pcompress_body(x0_ref, x1_ref, perm_ref, wcat2_ref, cpos_ref, w2_ref, gkc_ref, o_ref, xp_s):
    kind = pl.program_id(0)
    outs = _compress_pair((x0_ref, x1_ref), perm_ref, xp_s, wcat2_ref.at[0], cpos_ref[0, 0:1, :], w2_ref[0])
    for hd in range(N_KV):
        o_ref[:, hd * HEAD_DIM:(hd + 1) * HEAD_DIM] = jnp.where(kind == 0, _rms(outs[hd], gkc_ref[...]), outs[hd])


def _pcompress(cmp_rows, wcat2, cpos, w2, g_kc):
    rows = cmp_rows.shape[0]
    n_ch = rows // CMP_STRIDE
    perm = _perm_matrix()
    return pl.pallas_call(
        _pcompress_body,
        grid=(2,),
        in_specs=[pl.BlockSpec((rows, HEAD_DIM), lambda kd: (0, 2 * kd)), pl.BlockSpec((rows, HEAD_DIM), lambda kd: (0, 2 * kd + 1)),
                  pl.BlockSpec(perm.shape, lambda kd: (0, 0)),
                  pl.BlockSpec((1,) + wcat2.shape[1:], lambda kd: (kd, 0, 0, 0)),
                  pl.BlockSpec((1, 8, HEAD_DIM), lambda kd: (kd, 0, 0)),
                  pl.BlockSpec((1, HEAD_DIM, HEAD_DIM), lambda kd: (kd, 0, 0)),
                  pl.BlockSpec((1, HEAD_DIM), lambda kd: (0, 0))],
        out_specs=pl.BlockSpec((n_ch, KV_W), lambda kd: (0, kd)),
        out_shape=jax.ShapeDtypeStruct((n_ch, 2 * KV_W), F32),
        scratch_shapes=[pltpu.VMEM((CMP_STRIDE, N_KV * n_ch, HEAD_DIM), BF16)],
        compiler_params=_cparams(("arbitrary",)),
        name="pcompress",
    )(cmp_rows, cmp_rows, perm, wcat2, cpos, w2, g_kc)


def _select_rounds(score_t, on_pick=None):
    n_j = score_t.shape[0]
    jio = lax.broadcasted_iota(I32, score_t.shape, 0)
    sc = score_t
    for r in range(N_SEL):
        m = jnp.max(sc, axis=0, keepdims=True)
        idx = jnp.min(jnp.where(sc == m, jio, n_j), axis=0, keepdims=True)
        sc = jnp.where(jio == idx, -jnp.inf, sc)
        if on_pick is not None:
            on_pick(r, idx, m >= 0.0)
    return (sc == -jnp.inf) & (score_t >= 0.0)


LOG2E = 1.4426950408889634


ONES_ROWS = 16
FAR_TILES = 8


def _col_max(tiles):
    m = jnp.max(tiles[0], axis=0, keepdims=True)
    for s in tiles[1:]:
        m = jnp.maximum(m, jnp.max(s, axis=0, keepdims=True))
    return m


def _lanes4(x):
    return jnp.concatenate([x] * Q_PER_KV, axis=1)


def _pattn_t_body(q_ref, gt_ref, kt_ref, vt_ref, kc_ref, vc_ref, vct_ref, rb_ref, bk0_ref, bk1_ref, bkc_ref, mft_ref,
                  gout_ref, o_ref, b0_s, b1_s, bc_s, bw4_s):
    i = pl.program_id(0)
    cols = Q_PER_KV * QB

    @pl.when(i == 0)
    def _tables():
        b_io = lax.broadcasted_iota(I32, (QB, QB), 0)
        a_io = lax.broadcasted_iota(I32, (QB, QB), 1)
        for hd in range(N_HEADS):
            t0 = _bias_lookup(bk0_ref[...], rb_ref, hd, shift=True) * LOG2E
            b0_s[hd] = jnp.where(a_io >= b_io, t0, NEG)
            b1_s[hd] = _bias_lookup(bk1_ref[...], rb_ref, hd, shift=True) * LOG2E
            bc_s[hd] = _bias_lookup(bkc_ref[...], rb_ref, hd, shift=True) * LOG2E
        bw4_s[...] = jnp.where(b_io >= a_io, 0.0, NEG)

    gt_t = gt_ref[...].T
    n_far = jnp.maximum(i - 1, 0) // FAR_TILES
    far_keys = FAR_TILES * QB

    def tab(ref, h):
        return jnp.concatenate([ref[Q_PER_KV * h + g] for g in range(Q_PER_KV)], axis=1)

    def v_aug(row0, start, n):
        return jnp.concatenate([vt_ref[row0:row0 + HEAD_DIM, pl.ds(start, n)], jnp.ones((ONES_ROWS, n), BF16)], axis=0)

    q_ts, qa_ts, o_cs, sel_st = [], [], [], []
    for h in range(N_KV):
        q_t = jnp.concatenate(
            [q_ref[:, (Q_PER_KV * h + g) * HEAD_DIM:(Q_PER_KV * h + g + 1) * HEAD_DIM].astype(F32).T.astype(BF16)
             for g in range(Q_PER_KV)], axis=1)
        hs = slice(h * HEAD_DIM, (h + 1) * HEAD_DIM)

        n_c = kc_ref.shape[0] - 2 * CMP_STRIDE
        near0 = pl.multiple_of(8 * i, 8)
        cf = lax.broadcasted_iota(I32, (n_c, QB), 0)
        ok_f = _lanes4((cf < 8 * i) & (cf >= CMP_STRIDE))
        s_f = jnp.where(ok_f, _dot(kc_ref[0:n_c, hs].astype(BF16), q_t), NEG)
        cn = lax.broadcasted_iota(I32, (32, cols), 0)
        a_n = lax.broadcasted_iota(I32, (32, cols), 1) % QB
        ok_n = (CMP_STRIDE * (cn - CMP_STRIDE) <= a_n - (CMP_LEN - 1)) & (cn + 8 * i >= CMP_STRIDE)
        s_n = jnp.where(ok_n, _dot(kc_ref[pl.ds(near0, 32), hs].astype(BF16), q_t) + tab(bc_s, h), NEG)
        m_c = _col_max([s_f, s_n])
        p_f = jnp.where(ok_f, jnp.exp2(s_f - m_c), 0.0)
        p_n = jnp.where(ok_n, jnp.exp2(s_n - m_c), 0.0)
        l_c = jnp.sum(p_f, axis=0, keepdims=True) + jnp.sum(p_n, axis=0, keepdims=True)
        inv_c = 1.0 / jnp.maximum(l_c, 1e-30)
        vc_near_t = vc_ref[pl.ds(near0, 32), hs].T.astype(BF16)
        o_c = (_dot(vct_ref[hs, 0:n_c].astype(BF16), p_f.astype(BF16)) + _dot(vc_near_t, p_n.astype(BF16))) * inv_c
        pn_f = p_f * inv_c
        pn_n = p_n * inv_c
        imp_f = pn_f[:, 0:QB] + pn_f[:, QB:2 * QB] + pn_f[:, 2 * QB:3 * QB] + pn_f[:, 3 * QB:4 * QB]
        imp_n = pn_n[:, 0:QB] + pn_n[:, QB:2 * QB] + pn_n[:, 2 * QB:3 * QB] + pn_n[:, 3 * QB:4 * QB]
        jn = lax.broadcasted_iota(I32, (QB, 32), 0)
        cc = lax.broadcasted_iota(I32, (QB, 32), 1) + 8 * i - CMP_STRIDE
        mnt = (((cc // CMP_PER_SEL) == jn) | (cc == CMP_PER_SEL * jn - 1)) & (cc >= 0)
        mnt = jnp.where(mnt, 1.0, 0.0).astype(BF16)
        imp_t = jnp.zeros((QB, QB), F32)
        for part in _split3(imp_f):
            imp_t = imp_t + _dot(mft_ref[...], part)
        for part in _split3(imp_n):
            imp_t = imp_t + _dot(mnt, part)
        j_io = lax.broadcasted_iota(I32, (QB, QB), 0)
        cur = 2 * i + lax.broadcasted_iota(I32, (QB, QB), 1) // SEL_BLOCK
        forced = (j_io == 0) | (j_io == cur) | (j_io == cur - 1)
        score_t = jnp.where(forced, 1e9, jnp.where(j_io <= cur, imp_t, -1.0))
        sel_t = _select_rounds(score_t)
        selneg = _lanes4(jnp.where(sel_t, 0.0, NEG).astype(BF16))
        qa_t = jnp.concatenate([q_t, selneg] if h == 0 else [selneg, q_t], axis=0)
        ka0 = h * HEAD_DIM

        near_s, near_v = [], []
        for back in range(FAR_TILES + 1):
            kt = i - back
            start = pl.multiple_of(jnp.maximum(kt, 0) * QB, QB)
            s = _dot(kt_ref[pl.ds(start, QB), ka0:ka0 + 2 * HEAD_DIM], qa_t)
            if back == 0:
                s = s + tab(b0_s, h)
            elif back == 1:
                s = s + tab(b1_s, h) + jnp.where(kt >= 0, 0.0, NEG)
            else:
                s = s + jnp.where((kt >= 0) & (kt >= FAR_TILES * n_far), 0.0, NEG)
            near_s.append(s)
            near_v.append(v_aug(h * HEAD_DIM, start, QB))
        m_s = _col_max(near_s)
        acc_s = jnp.zeros((HEAD_DIM + ONES_ROWS, cols), F32)
        for s, v in zip(near_s, near_v):
            acc_s = acc_s + _dot(v, jnp.exp2(s - m_s).astype(BF16))
        q_ts.append(q_t)
        qa_ts.append(qa_t)
        o_cs.append(o_c)
        sel_st += [m_s, acc_s]

    def far_step(k, st):
        start = pl.multiple_of(k * far_keys, far_keys)
        out = []
        for h in range(N_KV):
            m, acc = st[2 * h], st[2 * h + 1]
            s = _dot(kt_ref[pl.ds(start, far_keys), h * HEAD_DIM:(h + 2) * HEAD_DIM], qa_ts[h])
            m_new = jnp.maximum(m, jnp.max(s, axis=0, keepdims=True))
            p = jnp.exp2(s - m_new).astype(BF16)
            out += [m_new, jnp.exp2(m - m_new) * acc + _dot(v_aug(h * HEAD_DIM, start, far_keys), p)]
        return tuple(out)

    sel_st = lax.fori_loop(0, n_far, far_step, tuple(sel_st))

    attn = [None] * N_HEADS
    for h in range(N_KV):
        q_t = q_ts[h]
        acc_s = sel_st[2 * h + 1]
        o_s = acc_s[0:HEAD_DIM] / acc_s[HEAD_DIM:HEAD_DIM + 1]
        o_c = o_cs[h]

        kw0 = KT_KW + h * HEAD_DIM
        win_s, win_v = [], []
        for back in range(5):
            kt = i - back
            start = pl.multiple_of(jnp.maximum(kt, 0) * QB, QB)
            s = _dot(kt_ref[pl.ds(start, QB), kw0:kw0 + HEAD_DIM], q_t)
            if back == 0:
                s = s + tab(b0_s, h)
            elif back == 1:
                s = s + tab(b1_s, h)
            elif back == 4:
                s = s + _lanes4(bw4_s[...])
            if back > 0:
                s = s + jnp.where(kt >= 0, 0.0, NEG)
            win_s.append(s)
            win_v.append(v_aug((N_KV + h) * HEAD_DIM, start, QB))
        m_w = _col_max(win_s)
        acc_w = jnp.zeros((HEAD_DIM + ONES_ROWS, cols), F32)
        for s, v in zip(win_s, win_v):
            acc_w = acc_w + _dot(v, jnp.exp2(s - m_w).astype(BF16))
        o_w = acc_w[0:HEAD_DIM] / acc_w[HEAD_DIM:HEAD_DIM + 1]

        for g in range(Q_PER_KV):
            hd = Q_PER_KV * h + g
            cs = slice(g * QB, (g + 1) * QB)
            o_t = (gt_t[hd:hd + 1] * o_c[:, cs] + gt_t[N_HEADS + hd:N_HEADS + hd + 1] * o_s[:, cs]
                   + gt_t[2 * N_HEADS + hd:2 * N_HEADS + hd + 1] * o_w[:, cs])
            attn[hd] = o_t.T
    a = jnp.concatenate(attn, axis=1)
    o_ref[...] = _rms(a, gout_ref[...]).astype(BF16)


def _pattn_t_tables():
    b = np.arange(QB)[:, None]
    a = np.arange(QB)[None, :]
    bk0 = _bucket_np(a - b)
    bk1 = _bucket_np(a - b + QB)
    c = np.arange(32)[:, None] - CMP_STRIDE
    bkc = _bucket_np(a - CMP_STRIDE * c - (CMP_LEN - 1))
    cidx = np.arange(4 * QB)[None, :] - CMP_STRIDE
    j = np.arange(QB)[:, None]
    mft = (((cidx // CMP_PER_SEL) == j) | (cidx == CMP_PER_SEL * j - 1)) & (cidx >= 0)
    return (jnp.asarray(bk0), jnp.asarray(bk1), jnp.asarray(bkc), jnp.asarray(mft.astype(np.float32), dtype=BF16))


def _pattn_t(q, gates, kt, vt, kcp, vcp, vcpt, rel_bias, g_out):
    rows = q.shape[0]
    bk0, bk1, bkc, mft = _pattn_t_tables()
    full = lambda a: pl.BlockSpec(a.shape, lambda i: (0,) * a.ndim)
    once = lambda a: pl.BlockSpec(a.shape, lambda i: (0,) * a.ndim, pipeline_mode=pl.Buffered(1))
    return pl.pallas_call(
        _pattn_t_body,
        grid=(rows // QB,),
        in_specs=[pl.BlockSpec((QB, ATTN_W), lambda i: (i, 0)), pl.BlockSpec((QB, LANE), lambda i: (i, 0)),
                  once(kt), once(vt), full(kcp), full(vcp), full(vcpt), pl.BlockSpec(memory_space=pltpu.SMEM),
                  full(bk0), full(bk1), full(bkc), full(mft), full(g_out)],
        out_specs=pl.BlockSpec((QB, ATTN_W), lambda i: (i, 0)),
        out_shape=jax.ShapeDtypeStruct((rows, ATTN_W), BF16),
        scratch_shapes=[pltpu.VMEM((N_HEADS, QB, QB), F32), pltpu.VMEM((N_HEADS, QB, QB), F32),
                        pltpu.VMEM((N_HEADS, 32, QB), F32), pltpu.VMEM((QB, QB), F32)],
        compiler_params=_cparams(("arbitrary",)),
        name="pattn",
    )(q, gates, kt, vt, kcp, vcp, vcpt, rel_bias, bk0, bk1, bkc, mft, g_out)


def _ssm_body(u_ref, ire_ref, iim_ref, wb_ref, wc_ref, abre_ref, abim_ref, dsk_ref, wglu_ref, gout_ref,
              y_ref, fre_ref, fim_ref, u_s, xre_s, xim_s, y_s, sre_s, sim_s, *, n_seg, tc, emit):
    c = pl.program_id(0)
    sgl = SSM_LANES // SSM_SG

    @pl.when(c == 0)
    def _init():
        sre_s[...] = ire_ref[...]
        sim_s[...] = iim_ref[...]

    n_lt = SSM_W // LANE
    if tc == 1:
        u_cols = [u_ref[:, l * LANE:(l + 1) * LANE] for l in range(n_lt)]
    else:
        for s in range(n_seg):
            for l in range(n_lt):
                u_s[l, pl.ds(s, tc, stride=n_seg), :] = u_ref[:, s * SSM_W + l * LANE:s * SSM_W + (l + 1) * LANE]
        u_cols = [u_s[l] for l in range(n_lt)]
    for sg in range(SSM_SG):
        ls = slice(sg * sgl, (sg + 1) * sgl)
        bu = _dot(u_cols[sg].astype(BF16), wb_ref[sg])
        xre_s[:, ls] = bu[:, 0:sgl]
        xim_s[:, ls] = bu[:, sgl:2 * sgl]
        ar = abre_ref[:, ls]
        ai = abim_ref[:, ls]

        def step(t, carry):
            xr, xi = carry
            rows = pl.ds(pl.multiple_of(t * n_seg, n_seg), n_seg)
            nr = ar * xr - ai * xi + xre_s[rows, ls]
            ni = ar * xi + ai * xr + xim_s[rows, ls]
            xre_s[rows, ls] = nr
            xim_s[rows, ls] = ni
            return nr, ni

        xr, xi = lax.fori_loop(0, tc, step, (sre_s[:, ls], sim_s[:, ls]), unroll=min(tc, 4))
        sre_s[:, ls] = xr
        sim_s[:, ls] = xi
    fre_ref[...] = sre_s[...]
    fim_ref[...] = sim_s[...]
    if not emit:
        y_ref[...] = jnp.zeros(y_ref.shape, y_ref.dtype)
        return
    ys = []
    for sg in range(SSM_SG):
        ls = slice(sg * sgl, (sg + 1) * sgl)
        x2 = jnp.concatenate([xre_s[:, ls], xim_s[:, ls]], axis=1).astype(BF16)
        ys.append(_dot(x2, wc_ref[sg]))
    y = jnp.concatenate(ys, axis=1) + dsk_ref[...] * jnp.concatenate(u_cols, axis=1)
    z = _dot(_gelu(y).astype(BF16), wglu_ref[...])
    o = z[:, 0:SSM_W] * jax.nn.sigmoid(z[:, SSM_W:2 * SSM_W])
    yn = _rms(o, gout_ref[...])
    if tc == 1:
        y_ref[...] = yn.astype(BF16)
    else:
        for l in range(n_lt):
            y_s[l] = yn[:, l * LANE:(l + 1) * LANE]
        for s in range(n_seg):
            for l in range(n_lt):
                y_ref[:, s * SSM_W + l * LANE:s * SSM_W + (l + 1) * LANE] = (
                    y_s[l, pl.ds(s, tc, stride=n_seg), :].astype(BF16))


def _ssm(u2, n_seg, init_re, init_im, wb, wc, ab_re, ab_im, d_skip, w_glu, g_out, tc, emit):
    if tc == 1:
        assert u2.shape == (n_seg, SSM_W)
        t_len = 1
        blk = (n_seg, SSM_W)
    else:
        t_len = u2.shape[0]
        assert u2.shape[1] == n_seg * SSM_W and n_seg % 8 == 0
        blk = (tc, n_seg * SSM_W)
    rows = tc * n_seg
    full = lambda a: pl.BlockSpec(a.shape, lambda c: (0,) * a.ndim)
    body = functools.partial(_ssm_body, n_seg=n_seg, tc=tc, emit=emit)
    st = jax.ShapeDtypeStruct((n_seg, SSM_LANES), F32)
    y_shape = u2.shape if emit else blk
    y_map = (lambda c: (c, 0)) if emit else (lambda c: (0, 0))
    return pl.pallas_call(
        body,
        grid=(t_len // tc,),
        in_specs=[pl.BlockSpec(blk, lambda c: (c, 0)),
                  full(init_re), full(init_im), full(wb), full(wc),
                  full(ab_re), full(ab_im), full(d_skip), full(w_glu), full(g_out)],
        out_specs=[pl.BlockSpec(blk, y_map), full(init_re), full(init_im)],
        out_shape=[jax.ShapeDtypeStruct(y_shape, BF16), st, st],
        scratch_shapes=[pltpu.VMEM((SSM_W // LANE, rows, LANE), F32), pltpu.VMEM((rows, SSM_LANES), F32),
                        pltpu.VMEM((rows, SSM_LANES), F32), pltpu.VMEM((SSM_W // LANE, rows, LANE), F32),
                        pltpu.VMEM((n_seg, SSM_LANES), F32), pltpu.VMEM((n_seg, SSM_LANES), F32)],
        compiler_params=_cparams(("arbitrary",)),
        name="ssm_emit" if emit else "ssm_final",
    )(u2, init_re, init_im, wb, wc, ab_re, ab_im, d_skip, w_glu, g_out)


def _ssm_chain_body(fre_ref, fim_ref, abre_ref, abim_ref, ire_ref, iim_ref, *, n_seg, log2_len):
    pr = abre_ref[...]
    pi = abim_ref[...]
    for _ in range(log2_len):
        pr, pi = pr * pr - pi * pi, 2.0 * pr * pi
    cr = jnp.zeros((1, SSM_LANES), F32)
    ci = jnp.zeros((1, SSM_LANES), F32)
    for j in range(n_seg):
        ire_ref[j:j + 1, :] = cr
        iim_ref[j:j + 1, :] = ci
        fr = fre_ref[j:j + 1, :]
        fi = fim_ref[j:j + 1, :]
        cr, ci = fr + pr * cr - pi * ci, fi + pr * ci + pi * cr


def _ssm_chain(fre, fim, ab_re, ab_im, seg_len):
    n_seg = fre.shape[0]
    log2_len = int(math.log2(seg_len))
    assert 2 ** log2_len == seg_len
    st = jax.ShapeDtypeStruct((n_seg, SSM_LANES), F32)
    return pl.pallas_call(
        functools.partial(_ssm_chain_body, n_seg=n_seg, log2_len=log2_len),
        out_shape=[st, st],
        compiler_params=pltpu.CompilerParams(vmem_limit_bytes=VMEM_LIMIT),
        name="ssm_chain",
    )(fre, fim, ab_re, ab_im)


FFN_CHUNKS = 2


def _ffn_body(x_ref, a_ref, s_ref, wo_ref, gffn_ref, wa_ref, wg_ref, wd_ref, cw_ref, cb_ref, p2_ref, p1_ref,
              y_ref, cnew_ref, hn_s, car_s, *, seq, tm):
    r = pl.program_id(0)
    j = pl.program_id(1)

    @pl.when(j == 0)
    def _mix():
        h = x_ref[...] + _dot(a_ref[...], wo_ref[0:ATTN_W, :]) + _dot(s_ref[...], wo_ref[ATTN_W:D_MODEL, :])
        y_ref[...] = h
        hn_s[...] = _rms(h, gffn_ref[...]).astype(BF16)

    if seq:
        @pl.when(r == 0)
        def _first():
            car_s[j, 6:7, :] = p2_ref[...]
            car_s[j, 7:8, :] = p1_ref[...]

    hn = hn_s[...]
    tf = wa_ref.shape[1]
    tc = tf // FFN_CHUNKS
    part = None
    for k in range(FFN_CHUNKS):
        cs = slice(k * tc, (k + 1) * tc)
        a = _dot(hn, wa_ref[:, cs])
        g = _dot(hn, wg_ref[:, cs])
        if seq:
            p2 = car_s[j, 6:7, cs]
            p1 = car_s[j, 7:8, cs]
            row = lax.broadcasted_iota(I32, a.shape, 0)
            a1 = jnp.where(row == 0, p1, pltpu.roll(a, 1, 0))
            a2 = jnp.where(row == 0, p2, jnp.where(row == 1, p1, pltpu.roll(a, 2, 0)))
            car_s[j, :, cs] = a[tm - 8:tm, :]
            cnew_ref[:, pl.ds(pl.multiple_of(j * tf + k * tc, tc), tc)] = a[tm - 8:tm, :]
        else:
            a1 = p1_ref[:, cs]
            a2 = p2_ref[:, cs]
            cnew_ref[:, cs] = a
        c = cb_ref[:, cs] + cw_ref[0:1, cs] * a2 + cw_ref[1:2, cs] * a1 + cw_ref[2:3, cs] * a
        d = _dot((_gelu(c) * g).astype(BF16), wd_ref[cs, :])
        part = d if part is None else part + d
    y_ref[...] += part


def _ffn(x, a_n, s_n, w_out, g_ffn, w_up, wd, cw, cb, p2, p1, tm, tf, seq, n_seg=1):
    rows = x.shape[0]
    nj = D_FF_PAD // tf
    body = functools.partial(_ffn_body, seq=seq, tm=tm)
    tiles_per_seg = rows // n_seg // tm
    if seq:
        tap_spec = pl.BlockSpec((1, tf), lambda r, j: (0, j))
        cnew_spec = pl.BlockSpec((8, D_FF_PAD), lambda r, j: (0, 0))
        cnew_shape = jax.ShapeDtypeStruct((8, D_FF_PAD), F32)
    else:
        tap_spec = pl.BlockSpec((tm, tf), lambda r, j: (r, j))
        cnew_spec = pl.BlockSpec((tm, tf), lambda r, j: (r, j))
        cnew_shape = jax.ShapeDtypeStruct((rows, D_FF_PAD), F32)
    return pl.pallas_call(
        body,
        grid=(rows // tm, nj),
        in_specs=[pl.BlockSpec((tm, D_MODEL), lambda r, j: (r, 0)), pl.BlockSpec((tm, ATTN_W), lambda r, j: (r, 0)),
                  pl.BlockSpec((tm, SSM_W), lambda r, j: (r % tiles_per_seg, r // tiles_per_seg)),
                  pl.BlockSpec(w_out.shape, lambda r, j: (0, 0), pipeline_mode=pl.Buffered(1)),
                  pl.BlockSpec((1, D_MODEL), lambda r, j: (0, 0)),
                  pl.BlockSpec((None, D_MODEL, tf), lambda r, j: (0, 0, j)),
                  pl.BlockSpec((None, D_MODEL, tf), lambda r, j: (1, 0, j)),
                  pl.BlockSpec((tf, D_MODEL), lambda r, j: (j, 0)), pl.BlockSpec((CONV_W, tf), lambda r, j: (0, j)),
                  pl.BlockSpec((1, tf), lambda r, j: (0, j)), tap_spec, tap_spec],
        out_specs=[pl.BlockSpec((tm, D_MODEL), lambda r, j: (r, 0)), cnew_spec],
        out_shape=[jax.ShapeDtypeStruct((rows, D_MODEL), F32), cnew_shape],
        scratch_shapes=[pltpu.VMEM((tm, D_MODEL), BF16), pltpu.VMEM((nj, 8, tf), F32)],
        compiler_params=_cparams(("arbitrary", "arbitrary")),
        name="ffn_seq" if seq else "ffn_rows",
    )(x, a_n, s_n, w_out, g_ffn, w_up, w_up, wd, cw, cb, p2, p1)


def _s1_copies(pt_ref, cache_ref, x_s, sem, b, slot, n_pages):
    cps = []
    for pg in range(n_pages):
        page = pt_ref[b * n_pages + pg]
        for s in range(2 * N_KV):
            cps.append(pltpu.make_async_copy(cache_ref.at[page, :, s, :],
                                             x_s.at[slot, s, pl.ds(pg * PAGE, PAGE), :], sem.at[slot]))
    return cps


def _s1_body(pt_ref, cache_ref, q_ref, perm_ref, wcat2_ref, cpos_ref, w2_ref, gkc_ref, rb_ref, bkc_ref, mt_ref,
             oc_ref, idx_ref, val_ref, x_s, xp_s, bias_s, sem, *, n_pages, past):
    b = pl.program_id(0)
    nb = pl.num_programs(0)
    slot = b % 2

    @pl.when(b == 0)
    def _first():
        for cp in _s1_copies(pt_ref, cache_ref, x_s, sem, 0, 0, n_pages):
            cp.start()
        for hd in range(N_HEADS):
            bias_s[hd:hd + 1, :] = _bias_lookup(bkc_ref[...], rb_ref, hd)

    @pl.when(b + 1 < nb)
    def _next():
        for cp in _s1_copies(pt_ref, cache_ref, x_s, sem, b + 1, 1 - slot, n_pages):
            cp.start()

    for cp in _s1_copies(pt_ref, cache_ref, x_s, sem, b, slot, n_pages):
        cp.wait()

    n_c = past // CMP_STRIDE
    cio = lax.broadcasted_iota(I32, (N_HEADS, n_c), 1)
    hrow = lax.broadcasted_iota(I32, (N_HEADS, n_c), 0) // Q_PER_KV
    q = q_ref[0]
    s_all = jnp.zeros((N_HEADS, n_c), F32)
    kcs = _compress_pair((x_s.at[slot, 0], x_s.at[slot, 1]), perm_ref, xp_s, wcat2_ref.at[0], cpos_ref[0, 0:1, :], w2_ref[0])
    vcs = _compress_pair((x_s.at[slot, N_KV], x_s.at[slot, N_KV + 1]), perm_ref, xp_s, wcat2_ref.at[1], cpos_ref[1, 0:1, :],
                         w2_ref[1])
    vcs = [v.astype(BF16) for v in vcs]
    for h in range(N_KV):
        kc = _rms(kcs[h], gkc_ref[...])
        s_all = jnp.where(hrow == h, _dot_nt(q, kc.astype(BF16)), s_all)
    ok = cio < n_c - 1
    s_all = jnp.where(ok, s_all + bias_s[...], NEG)
    m = jnp.max(s_all, axis=-1, keepdims=True)
    p = jnp.where(ok, jnp.exp(s_all - m), 0.0)
    p = p / jnp.maximum(jnp.sum(p, axis=-1, keepdims=True), 1e-30)
    pb = p.astype(BF16)
    hrow_o = lax.broadcasted_iota(I32, (N_HEADS, HEAD_DIM), 0) // Q_PER_KV
    o_c = jnp.zeros((N_HEADS, HEAD_DIM), F32)
    for h in range(N_KV):
        o_c = jnp.where(hrow_o == h, _dot(pb, vcs[h]), o_c)
    oc_ref[0] = o_c
    rio = lax.broadcasted_iota(I32, (8, n_c), 0)
    imp = jnp.zeros((8, n_c), F32)
    for h in range(N_KV):
        ih = p[4 * h:4 * h + 1] + p[4 * h + 1:4 * h + 2] + p[4 * h + 2:4 * h + 3] + p[4 * h + 3:4 * h + 4]
        imp = jnp.where(rio == h, ih, imp)
    imp = jnp.concatenate([imp, jnp.zeros((LANE - 8, n_c), F32)], axis=0)
    n_j = mt_ref.shape[0]
    imp_t = jnp.zeros((n_j, LANE), F32)
    for part in _split3(imp):
        imp_t = imp_t + _dot_nt(mt_ref[...], part)
    j_io = lax.broadcasted_iota(I32, (n_j, LANE), 0)
    cur = past // SEL_BLOCK
    forced = (j_io == 0) | (j_io == cur) | (j_io == cur - 1)
    score_t = jnp.where(forced, 1e9, jnp.where(j_io <= cur, imp_t, -1.0))
    score_t = jnp.where(j_io <= cur, score_t, -jnp.inf)

    def on_pick(r, idx, okv):
        idx_ref[0, r:r + 1, :] = idx
        val_ref[0, r:r + 1, :] = jnp.where(okv, 1, 0)

    _select_rounds(score_t, on_pick)


def _s1(page_table, cache4, q3, wcat2, cpos, w2, g_kc, rel_bias):
    n_b, n_pages = page_table.shape
    past = n_pages * PAGE
    n_c = past // CMP_STRIDE
    perm = _perm_matrix()
    ns = past // SEL_BLOCK + 1
    n_j = -(-ns // 8) * 8
    c = np.arange(n_c)[None, :]
    bkc = _bucket_np(past - (CMP_STRIDE * c + CMP_LEN - 1))
    j = np.arange(n_j)[:, None]
    mt = (((c // CMP_PER_SEL) == j) | (c == CMP_PER_SEL * j - 1)) & (c < n_c - 1)
    mt = jnp.asarray(mt.astype(np.float32), dtype=BF16)
    full = lambda a: pl.BlockSpec(a.shape, lambda b, pt: (0,) * a.ndim)
    body = functools.partial(_s1_body, n_pages=n_pages, past=past)
    return pl.pallas_call(
        body,
        grid_spec=pltpu.PrefetchScalarGridSpec(
            num_scalar_prefetch=1,
            grid=(n_b,),
            in_specs=[pl.BlockSpec(memory_space=pl.ANY), pl.BlockSpec((1, N_HEADS, HEAD_DIM), lambda b, pt: (b, 0, 0)),
                      full(perm), full(wcat2), full(cpos), full(w2), full(g_kc), pl.BlockSpec(memory_space=pltpu.SMEM),
                      pl.BlockSpec((1, n_c), lambda b, pt: (0, 0)), full(mt)],
            out_specs=[pl.BlockSpec((1, N_HEADS, HEAD_DIM), lambda b, pt: (b, 0, 0)),
                       pl.BlockSpec((1, N_SEL, LANE), lambda b, pt: (b, 0, 0)),
                       pl.BlockSpec((1, N_SEL, LANE), lambda b, pt: (b, 0, 0))],
            scratch_shapes=[pltpu.VMEM((2, 2 * N_KV, past, HEAD_DIM), F32),
                            pltpu.VMEM((CMP_STRIDE, N_KV * n_c, HEAD_DIM), BF16), pltpu.VMEM((N_HEADS, n_c), F32),
                            pltpu.SemaphoreType.DMA((2,))],
        ),
        out_shape=[jax.ShapeDtypeStruct((n_b, N_HEADS, HEAD_DIM), F32), jax.ShapeDtypeStruct((n_b, N_SEL, LANE), I32),
                   jax.ShapeDtypeStruct((n_b, N_SEL, LANE), I32)],
        compiler_params=_cparams(("arbitrary",)),
        name="sample_cmp",
    )(page_table.reshape(-1), cache4, q3, perm, wcat2, cpos, w2, g_kc, rel_bias, jnp.asarray(bkc), mt)


def _s2_copies(idx_ref, pt_ref, cache_ref, cwin_ref, ks_s, vs_s, kw_s, vw_s, sem, b, slot, n_pages):
    cps = []
    n_blk = n_pages * (PAGE // SEL_BLOCK)
    for h in range(N_KV):
        for r in range(N_SEL):
            jb = jnp.minimum(idx_ref[(b * N_SEL + r) * N_KV + h], n_blk - 1)
            page = pt_ref[b * n_pages + jb // 2]
            row0 = pl.multiple_of((jb % 2) * SEL_BLOCK, SEL_BLOCK)
            cps.append(pltpu.make_async_copy(cache_ref.at[page, pl.ds(row0, SEL_BLOCK), 2 * N_KV + h, :],
                                             ks_s.at[slot, h, pl.ds(r * SEL_BLOCK, SEL_BLOCK), :], sem.at[slot]))
            cps.append(pltpu.make_async_copy(cache_ref.at[page, pl.ds(row0, SEL_BLOCK), 3 * N_KV + h, :],
                                             vs_s.at[slot, h, pl.ds(r * SEL_BLOCK, SEL_BLOCK), :], sem.at[slot]))
        cps.append(pltpu.make_async_copy(cwin_ref.at[b, :, h, :], kw_s.at[slot, h], sem.at[slot]))
        cps.append(pltpu.make_async_copy(cwin_ref.at[b, :, N_KV + h, :], vw_s.at[slot, h], sem.at[slot]))
    return cps


def _s2_body(idx_ref, val_ref, pt_ref, cache_ref, cwin_ref, q_ref, oc_ref, gt_ref, kns_ref, vns_ref, knw_ref, vnw_ref,
             rb_ref, bks_ref, bkw_ref, gout_ref, o_ref, ks_s, vs_s, kw_s, vw_s, bs_s, bw_s, sem, *, n_pages, past):
    b = pl.program_id(0)
    nb = pl.num_programs(0)
    slot = b % 2
    args = (idx_ref, pt_ref, cache_ref, cwin_ref, ks_s, vs_s, kw_s, vw_s, sem)

    @pl.when(b == 0)
    def _first():
        for cp in _s2_copies(*args, 0, 0, n_pages):
            cp.start()
        for hd in range(N_HEADS):
            bs_s[hd:hd + 1, :] = _bias_lookup(bks_ref[...], rb_ref, hd)
            bw_s[hd:hd + 1, :] = _bias_lookup(bkw_ref[...], rb_ref, hd)

    @pl.when(b + 1 < nb)
    def _next():
        for cp in _s2_copies(*args, b + 1, 1 - slot, n_pages):
            cp.start()

    for cp in _s2_copies(*args, b, slot, n_pages):
        cp.wait()

    n_blk = n_pages * (PAGE // SEL_BLOCK)
    q = q_ref[0]
    qf = q.astype(F32)
    hrow = lax.broadcasted_iota(I32, (N_HEADS, 1), 0) // Q_PER_KV
    lane = lax.broadcasted_iota(I32, (N_HEADS, LANE), 1)
    bias0 = jnp.concatenate([jnp.full((1, 1), rb_ref[0, hd], F32) for hd in range(N_HEADS)], axis=0)
    b31 = jnp.concatenate([jnp.full((1, 1), rb_ref[REL_BUCKETS - 1, hd], F32) for hd in range(N_HEADS)], axis=0)

    tiles = []
    new_sel = jnp.zeros((N_HEADS, 1), F32)
    for t in range(N_SEL // 2):
        s_t = jnp.zeros((N_HEADS, LANE), F32)
        for h in range(N_KV):
            s_h = _dot_nt(q, ks_s[slot, h, pl.ds(t * LANE, LANE), :].astype(BF16))
            halves = []
            for half in range(2):
                r = 2 * t + half
                jb = idx_ref[(b * N_SEL + r) * N_KV + h]
                okr = (val_ref[(b * N_SEL + r) * N_KV + h] > 0) & (jb < n_blk)
                near = bs_s[:, (half * 2) * LANE:(half * 2 + 1) * LANE]
                nearer = bs_s[:, (half * 2 + 1) * LANE:(half * 2 + 2) * LANE]
                bias = jnp.where(jb == n_blk - 1, nearer, jnp.where(jb == n_blk - 2, near, b31))
                halves.append(jnp.where(okr, s_h + bias, NEG))
                new_sel = jnp.where((hrow == h) & (val_ref[(b * N_SEL + r) * N_KV + h] > 0) & (jb == n_blk), 1.0, new_sel)
            s_h = jnp.where(lane < SEL_BLOCK, halves[0], halves[1])
            s_t = jnp.where(hrow == h, s_h, s_t)
        tiles.append(s_t)
    s_new = jnp.sum(qf * kns_ref[0], axis=-1, keepdims=True) + bias0
    s_new = jnp.where(new_sel > 0.5, s_new, NEG)
    m = s_new
    for s_t in tiles:
        m = jnp.maximum(m, jnp.max(s_t, axis=-1, keepdims=True))
    p_new = jnp.where(new_sel > 0.5, jnp.exp(s_new - m), 0.0)
    l = p_new
    acc = p_new * vns_ref[0]
    for t, s_t in enumerate(tiles):
        p = jnp.where(s_t > 0.5 * NEG, jnp.exp(s_t - m), 0.0)
        l = l + jnp.sum(p, axis=-1, keepdims=True)
        pb = p.astype(BF16)
        for h in range(N_KV):
            pv = _dot(pb, vs_s[slot, h, pl.ds(t * LANE, LANE), :].astype(BF16))
            acc = acc + jnp.where(hrow == h, pv, 0.0)
    o_s = acc / jnp.maximum(l, 1e-30)

    n_w = kw_s.shape[2]
    wt = []
    for t in range(n_w // LANE):
        s_t = jnp.zeros((N_HEADS, LANE), F32)
        for h in range(N_KV):
            s_h = _dot_nt(q, kw_s[slot, h, pl.ds(t * LANE, LANE), :].astype(BF16))
            s_t = jnp.where(hrow == h, s_h, s_t)
        wt.append(s_t + bw_s[:, t * LANE:(t + 1) * LANE])
    s_new = jnp.sum(qf * knw_ref[0], axis=-1, keepdims=True) + bias0
    m = s_new
    for s_t in wt:
        m = jnp.maximum(m, jnp.max(s_t, axis=-1, keepdims=True))
    p_new = jnp.exp(s_new - m)
    l = p_new
    acc = p_new * vnw_ref[0]
    for t, s_t in enumerate(wt):
        p = jnp.exp(s_t - m)
        l = l + jnp.sum(p, axis=-1, keepdims=True)
        pb = p.astype(BF16)
        for h in range(N_KV):
            pv = _dot(pb, vw_s[slot, h, pl.ds(t * LANE, LANE), :].astype(BF16))
            acc = acc + jnp.where(hrow == h, pv, 0.0)
    o_w = acc / l

    gt = gt_ref[0]
    a = gt[:, 0:1] * oc_ref[0] + gt[:, 1:2] * o_s + gt[:, 2:3] * o_w
    ms = jnp.sum(jnp.sum(a * a, axis=-1, keepdims=True), axis=0, keepdims=True) / (N_HEADS * HEAD_DIM)
    o_ref[0] = (a * lax.rsqrt(ms + EPS) * gout_ref[...]).astype(BF16)


def _s2(idx, val, page_table, cache4, cwin4, q3, o_c, gates3, kns, vns, knw, vnw, rel_bias, g_out3):
    n_b, n_pages = page_table.shape
    past = n_pages * PAGE
    n_w = cwin4.shape[1]
    s = np.arange(SEL_BLOCK)
    d_near = past - ((past // SEL_BLOCK - 2) * SEL_BLOCK + s)
    d_nearer = past - ((past // SEL_BLOCK - 1) * SEL_BLOCK + s)
    z = np.zeros(SEL_BLOCK, np.int64)
    bks = np.concatenate([d_near, z, d_nearer, z, z, d_near, z, d_nearer])[None, :]
    bkw = (past - (past - n_w + np.arange(n_w)))[None, :]
    full = lambda a: pl.BlockSpec(a.shape, lambda b, *_: (0,) * a.ndim)
    per_b = lambda a: pl.BlockSpec((1,) + a.shape[1:], lambda b, *_: (b,) + (0,) * (a.ndim - 1))
    body = functools.partial(_s2_body, n_pages=n_pages, past=past)
    return pl.pallas_call(
        body,
        grid_spec=pltpu.PrefetchScalarGridSpec(
            num_scalar_prefetch=3,
            grid=(n_b,),
            in_specs=[pl.BlockSpec(memory_space=pl.ANY), pl.BlockSpec(memory_space=pl.ANY),
                      per_b(q3), per_b(o_c), per_b(gates3), per_b(kns), per_b(vns), per_b(knw), per_b(vnw),
                      pl.BlockSpec(memory_space=pltpu.SMEM), pl.BlockSpec((1, 4 * LANE), lambda b, *_: (0, 0)),
                      pl.BlockSpec((1, n_w), lambda b, *_: (0, 0)), full(g_out3)],
            out_specs=pl.BlockSpec((1, N_HEADS, HEAD_DIM), lambda b, *_: (b, 0, 0)),
            scratch_shapes=[pltpu.VMEM((2, N_KV, N_SEL * SEL_BLOCK, HEAD_DIM), F32),
                            pltpu.VMEM((2, N_KV, N_SEL * SEL_BLOCK, HEAD_DIM), F32),
                            pltpu.VMEM((2, N_KV, n_w, HEAD_DIM), F32), pltpu.VMEM((2, N_KV, n_w, HEAD_DIM), F32),
                            pltpu.VMEM((N_HEADS, 4 * LANE), F32), pltpu.VMEM((N_HEADS, n_w), F32),
                            pltpu.SemaphoreType.DMA((2,))],
        ),
        out_shape=jax.ShapeDtypeStruct((n_b, N_HEADS, HEAD_DIM), BF16),
        compiler_params=_cparams(("arbitrary",)),
        name="sample_attn",
    )(idx, val, page_table.reshape(-1), cache4, cwin4, q3, o_c, gates3, kns, vns, knw, vnw, rel_bias,
      jnp.asarray(_bucket_np(bks)), jnp.asarray(_bucket_np(bkw)), g_out3)


def _winshift_body(cwin_ref, new_ref, o_ref):
    n_w = cwin_ref.shape[1]
    o_ref[0, 0:n_w - 1] = cwin_ref[0, 1:n_w]
    o_ref[0, n_w - 1] = new_ref[0]


def _winshift(cwin4, new3):
    n_b = cwin4.shape[0]
    return pl.pallas_call(
        _winshift_body,
        grid=(n_b,),
        in_specs=[pl.BlockSpec((1,) + cwin4.shape[1:], lambda b: (b, 0, 0, 0)),
                  pl.BlockSpec((1,) + new3.shape[1:], lambda b: (b, 0, 0))],
        out_specs=pl.BlockSpec((1,) + cwin4.shape[1:], lambda b: (b, 0, 0, 0)),
        out_shape=jax.ShapeDtypeStruct(cwin4.shape, F32),
        compiler_params=_cparams(("arbitrary",)),
        name="winshift",
    )(cwin4, new3)


def _block_diag_b(bb_re, bb_im):
    gl = SSM_G // SSM_SG
    eye = jnp.eye(gl, dtype=F32)

    def one(bb):
        t = jnp.transpose(bb, (1, 0, 2)).reshape(SSM_SG, gl, SSM_P, SSM_N)
        return jnp.einsum('sgpn,gh->sgphn', t, eye).reshape(SSM_SG, gl * SSM_P, gl * SSM_N)

    return jnp.concatenate([one(bb_re), one(bb_im)], axis=2).astype(BF16)


def _block_diag_c(c_re, c_im):
    gl = SSM_G // SSM_SG
    eye = jnp.eye(gl, dtype=F32)

    def one(c):
        t = jnp.transpose(c, (0, 2, 1)).reshape(SSM_SG, gl, SSM_N, SSM_P)
        return jnp.einsum('sgnp,gh->sgnhp', t, eye).reshape(SSM_SG, gl * SSM_N, gl * SSM_P)

    return jnp.concatenate([one(c_re), -one(c_im)], axis=1).astype(BF16)


def kernel(x_prompt, x_sample, cache_kv, page_table, cache_win, state_ssm, state_conv, rel_bias, g_mix, w_in, g_q, g_k, w_cmp1, pos_cmp, w_cmp2, lam_re, lam_im, log_dt, b_re, b_im, c_re, c_im, d_skip, w_glu, g_out_attn, g_out_ssm, w_out, g_ffn, w_up, conv_w, conv_b, w_down):
    depth = g_mix.shape[0]
    assert depth == 1 and x_prompt.shape[0] == 1 and x_sample.shape[1] == 1
    seq = x_prompt.shape[1]
    n_b = x_sample.shape[0]
    n_pages = page_table.shape[1]
    n_w = cache_win.shape[2]
    li = 0
    row = lambda v: v.reshape(1, -1)

    w_in_b = _cast_transposed(w_in[li].T, IN_COLS_PAD)
    w1 = w_cmp1[li]
    wcat = jnp.concatenate([w1[:, :CMP_STRIDE], w1[:, CMP_STRIDE:]], axis=-1).astype(BF16).reshape(
        2, CMP_STRIDE // 2, 2 * HEAD_DIM, 2 * HEAD_DIM)
    w2 = w_cmp2[li].astype(BF16)
    w_glu_b = w_glu[li].astype(BF16)
    w_out_b = w_out[li].astype(BF16)
    padc = lambda a: jnp.pad(a, ((0, 0), (0, D_FF_PAD - D_FF)))
    w_up_b = _cast_pad(w_up[li], 2, D_FF_PAD, 256)
    wd = _cast_pad_rows(w_down[li], D_FF_PAD, D_FF // 8)
    cw = padc(conv_w[li])
    cb = padc(row(conv_b[li]))

    ab_re, ab_im, bb_re, bb_im, cpos = _prep(lam_re[li], lam_im[li], log_dt[li], b_re[li], b_im[li], pos_cmp[li], w1)
    wb = _block_diag_b(bb_re, bb_im)
    wc = _block_diag_c(c_re[li], c_im[li])
    ab_re = row(ab_re)
    ab_im = row(ab_im)

    in_args = (row(g_mix[li]), w_in_b, row(g_q[li]), row(g_k[li, 1]), row(g_k[li, 2]))
    ssm_args = (wb, wc, ab_re, ab_im, row(d_skip[li]), w_glu_b, row(g_out_ssm[li]))

    xp = x_prompt[0]
    n_seg = 8
    q, kv, win, u2, gates, cmp_rows, kt, vt = _inproj(xp, *in_args, tm=512, q_scale=HEAD_DIM ** -0.5 * LOG2E, n_seg=n_seg,
                                            attn_operands=True)
    cmp_out = _pcompress(cmp_rows, wcat, cpos, w2, row(g_k[li, 0]))
    zpad = jnp.zeros((CMP_STRIDE, 2 * KV_W), F32)
    cmp_pad = jnp.concatenate([zpad, cmp_out, zpad], axis=0)
    attn_n = _pattn_t(q, gates, kt, vt, cmp_pad[:, :KV_W], cmp_pad[:, KV_W:], cmp_pad[:, KV_W:].T, rel_bias,
                      row(g_out_attn[li]))
    zst = jnp.zeros((n_seg, SSM_LANES), F32)
    _, fre, fim = _ssm(u2, n_seg, zst, zst, *ssm_args, tc=64, emit=False)
    ire, iim = _ssm_chain(fre, fim, ab_re, ab_im, seq // n_seg)
    ssm_n, hre, him = _ssm(u2, n_seg, ire, iim, *ssm_args, tc=64, emit=True)
    zrow = jnp.zeros((1, D_FF_PAD), F32)
    ffn_w = (w_out_b, row(g_ffn[li]), w_up_b, wd, cw, cb)
    y_p, cnew_p = _ffn(xp, attn_n, ssm_n, *ffn_w, zrow, zrow, tm=512, tf=512, seq=True, n_seg=n_seg)

    y_prompt = y_p[None]
    kv_prompt = kv.reshape(1, 1, seq, 4, N_KV, HEAD_DIM)
    win_prompt = win[seq - min(WINDOW, seq):].reshape(1, 1, min(WINDOW, seq), 2, N_KV, HEAD_DIM)
    ssm_prompt = jnp.stack([hre[n_seg - 1], him[n_seg - 1]], axis=-1).reshape(1, 1, SSM_G, SSM_N, 2)
    conv_prompt = cnew_p[6:8, :D_FF].reshape(1, 1, CONV_W - 1, D_FF)

    xs = x_sample[:, 0]
    q_s, kv_s, win_s, u_s, gates_s = _inproj(xs, *in_args, tm=n_b, q_scale=HEAD_DIM ** -0.5)
    cache4 = cache_kv[li].reshape(cache_kv.shape[1], PAGE, 4 * N_KV, HEAD_DIM)
    cwin4 = cache_win[li].reshape(n_b, n_w, 2 * N_KV, HEAD_DIM)
    q3 = q_s.reshape(n_b, N_HEADS, HEAD_DIM)
    o_c, idx, val = _s1(page_table, cache4, q3, wcat, cpos, w2, row(g_k[li, 0]), rel_bias)
    idx = idx[:, :, :N_KV].reshape(-1)
    val = val[:, :, :N_KV].reshape(-1)
    rep = lambda a: jnp.repeat(a.reshape(n_b, N_KV, HEAD_DIM), Q_PER_KV, axis=1)
    kns = rep(kv_s[:, 2 * N_KV:3 * N_KV])
    vns = rep(kv_s[:, 3 * N_KV:4 * N_KV])
    knw = rep(win_s[:, 0:N_KV])
    vnw = rep(win_s[:, N_KV:2 * N_KV])
    g3 = jnp.transpose(gates_s[:, :N_BRANCH * N_HEADS].reshape(n_b, N_BRANCH, N_HEADS), (0, 2, 1))
    g3 = jnp.pad(g3, ((0, 0), (0, 0), (0, LANE - N_BRANCH)))
    attn_s = _s2(idx, val, page_table, cache4, cwin4, q3, o_c, g3, kns, vns, knw, vnw, rel_bias,
                 g_out_attn[li].reshape(N_HEADS, HEAD_DIM))
    st = state_ssm[li].reshape(n_b, SSM_LANES, 2)
    ssm_s, sre, sim = _ssm(u_s, n_b, st[:, :, 0], st[:, :, 1], *ssm_args, tc=1, emit=True)
    sc = state_conv[li]
    y_s, a_s = _ffn(xs, attn_s.reshape(n_b, ATTN_W), ssm_s, *ffn_w, padc(sc[:, 0]), padc(sc[:, 1]),
                    tm=n_b, tf=512, seq=False)
    win_sample = _winshift(cwin4, win_s)

    y_sample = y_s[:, None]
    kv_sample = kv_s.reshape(1, n_b, 1, 4, N_KV, HEAD_DIM)
    win_sample = win_sample.reshape(1, n_b, n_w, 2, N_KV, HEAD_DIM)
    ssm_sample = jnp.stack([sre, sim], axis=-1).reshape(1, n_b, SSM_G, SSM_N, 2)
    conv_sample = jnp.stack([sc[:, 1], a_s[:, :D_FF]], axis=1)[None]
    return (y_prompt, y_sample, kv_prompt, kv_sample, win_prompt, win_sample,
            ssm_prompt, ssm_sample, conv_prompt, conv_sample)
```

```python
import functools
import math

import numpy as np
import jax
import jax.numpy as jnp
from jax import lax
from jax.experimental import pallas as pl
from jax.experimental.pallas import tpu as pltpu

F32 = jnp.float32
BF16 = jnp.bfloat16
I32 = jnp.int32

D_MODEL = 2048
HEAD_DIM = 128
N_HEADS = 8
N_KV = 2
Q_PER_KV = 4
ATTN_W = 1024
KV_W = 256
N_BRANCH = 3
CMP_LEN = 32
CMP_STRIDE = 16
SEL_BLOCK = 64
CMP_PER_SEL = 4
N_SEL = 16
WINDOW = 512
REL_BUCKETS = 32
REL_MAX_DIST = 128
PAGE = 128
SSM_W = 1024
SSM_G = 64
SSM_N = 64
SSM_P = 16
SSM_SG = 8
SSM_LANES = SSM_G * SSM_N
D_FF = 5504
D_FF_PAD = 5632
CONV_W = 3
EPS = 1e-6
NEG = -1e30
QB = 128
LANE = 128
VMEM_LIMIT = 56 * 1024 * 1024


def _cparams(sem):
    return pltpu.CompilerParams(dimension_semantics=sem, vmem_limit_bytes=VMEM_LIMIT)


def _rms(x, g):
    return x * lax.rsqrt(jnp.mean(x * x, axis=-1, keepdims=True) + EPS) * g


def _gelu(x):
    return jax.nn.gelu(x)


def _dot(a, b):
    return jnp.dot(a, b, preferred_element_type=F32)


def _dot_nt(a, b):
    return lax.dot_general(a, b, (((1,), (1,)), ((), ())), preferred_element_type=F32)


def _split3(x):
    hi = x.astype(BF16)
    r1 = x - hi.astype(F32)
    mid = r1.astype(BF16)
    lo = (r1 - mid.astype(F32)).astype(BF16)
    return hi, mid, lo


def _bucket_np(d):
    n = np.maximum(d, 0)
    exact = REL_BUCKETS // 2
    nf = np.maximum(n, 1).astype(np.float32)
    large = exact + (np.log(nf / np.float32(exact)) / np.float32(math.log(REL_MAX_DIST / exact))
                     * np.float32(REL_BUCKETS - exact)).astype(np.int32)
    return np.where(n < exact, n, np.minimum(large, REL_BUCKETS - 1)).astype(np.int32)


def _bias_lookup(bkt, rb_ref, head, shift=None):
    last = rb_ref[REL_BUCKETS - 1, head]
    acc = jnp.full(bkt.shape, last, F32)
    for b in range(REL_BUCKETS - 1):
        acc = jnp.where(bkt == b, rb_ref[b, head], acc)
    if shift:
        acc = acc - last
    return acc


def _cast_pad_body(w_ref, o_ref):
    cols = w_ref.shape[1]
    cols_pad = o_ref.shape[2]
    full = (cols // LANE) * LANE
    o_ref[0, :, 0:full] = w_ref[:, 0:full].astype(BF16)
    if full < cols:
        tail = w_ref[:, full:cols].astype(BF16)
        o_ref[0, :, full:full + LANE] = jnp.concatenate(
            [tail, jnp.zeros((tail.shape[0], full + LANE - cols), BF16)], axis=1)
        full += LANE
    if full < cols_pad:
        o_ref[0, :, full:cols_pad] = jnp.zeros((o_ref.shape[1], cols_pad - full), BF16)


def _cast_pad(w, n_parts, cols_pad, row_block):
    rows, width = w.shape
    cols = width // n_parts
    return pl.pallas_call(
        _cast_pad_body,
        grid=(n_parts, rows // row_block),
        in_specs=[pl.BlockSpec((row_block, cols), lambda p, r: (r, p))],
        out_specs=pl.BlockSpec((1, row_block, cols_pad), lambda p, r: (p, r, 0)),
        out_shape=jax.ShapeDtypeStruct((n_parts, rows, cols_pad), BF16),
        compiler_params=_cparams(("arbitrary", "arbitrary")),
        name="cast_pad",
    )(w)


def _cast_rows_body(w_ref, o_ref, *, n_src):
    i = pl.program_id(0)
    o_ref[...] = jnp.where(i < n_src, w_ref[...].astype(BF16), jnp.zeros(o_ref.shape, BF16))


def _cast_pad_rows(w, rows_pad, row_block):
    rows, cols = w.shape
    n_src = rows // row_block
    assert rows == n_src * row_block and row_block % 16 == 0 and rows_pad - rows <= row_block
    return pl.pallas_call(
        functools.partial(_cast_rows_body, n_src=n_src),
        grid=(n_src + 1,),
        in_specs=[pl.BlockSpec((row_block, cols), lambda i: (jnp.minimum(i, n_src - 1), 0))],
        out_specs=pl.BlockSpec((row_block, cols), lambda i: (i, 0)),
        out_shape=jax.ShapeDtypeStruct((rows_pad, cols), BF16),
        compiler_params=_cparams(("arbitrary",)),
        name="cast_pad_rows",
    )(w)


def _cast_t_body(w_ref, tail_ref, o_ref, *, n_src):
    i = pl.program_id(0)
    x = jnp.where(i < n_src, w_ref[...], tail_ref[...])
    o_ref[...] = x.T.astype(BF16)


def _cast_transposed(wt, cols_pad):
    cols, rows = wt.shape
    n_src = cols // LANE
    assert cols_pad == (n_src + 1) * LANE and cols > n_src * LANE
    tail = jnp.pad(wt[n_src * LANE:], ((0, cols_pad - cols), (0, 0)))
    return pl.pallas_call(
        functools.partial(_cast_t_body, n_src=n_src),
        grid=(n_src + 1,),
        in_specs=[pl.BlockSpec((LANE, rows), lambda i: (jnp.minimum(i, n_src - 1), 0)),
                  pl.BlockSpec((LANE, rows), lambda i: (0, 0))],
        out_specs=pl.BlockSpec((rows, LANE), lambda i: (0, i)),
        out_shape=jax.ShapeDtypeStruct((rows, cols_pad), BF16),
        compiler_params=_cparams(("arbitrary",)),
        name="cast_transposed",
    )(wt, tail)


def _prep_body(lre_ref, lim_ref, ldt_ref, bre_ref, bim_ref, pos_ref, w1_ref,
               abre_ref, abim_ref, bbre_ref, bbim_ref, cpos_ref):
    lr = lre_ref[...]
    li = lim_ref[...]
    dt = jnp.exp(ldt_ref[...])
    mag = jnp.exp(lr * dt)
    ab_re = mag * jnp.cos(li * dt)
    ab_im = mag * jnp.sin(li * dt)
    den = lr * lr + li * li
    nr = ab_re - 1.0
    ni = ab_im
    f_re = (nr * lr + ni * li) / den
    f_im = (ni * lr - nr * li) / den
    abre_ref[...] = ab_re
    abim_ref[...] = ab_im
    for p in range(SSM_P):
        br = bre_ref[p]
        bi = bim_ref[p]
        bbre_ref[p] = f_re * br - f_im * bi
        bbim_ref[p] = f_re * bi + f_im * br
    for kind in range(2):
        cpos_ref[kind] = jnp.dot(pos_ref[kind], w1_ref[kind], preferred_element_type=F32,
                                 precision=lax.Precision.HIGHEST)


def _prep(lam_re, lam_im, log_dt, b_re, b_im, pos_cmp, w_cmp1):
    bre_t = jnp.transpose(b_re, (2, 0, 1))
    bim_t = jnp.transpose(b_im, (2, 0, 1))
    pos = jnp.zeros((2, 8, CMP_LEN * HEAD_DIM), F32).at[:, 0, :].set(pos_cmp.reshape(2, CMP_LEN * HEAD_DIM))
    w1 = w_cmp1.reshape(2, CMP_LEN * HEAD_DIM, HEAD_DIM)
    return pl.pallas_call(
        _prep_body,
        out_shape=[jax.ShapeDtypeStruct((SSM_G, SSM_N), F32), jax.ShapeDtypeStruct((SSM_G, SSM_N), F32),
                   jax.ShapeDtypeStruct((SSM_P, SSM_G, SSM_N), F32), jax.ShapeDtypeStruct((SSM_P, SSM_G, SSM_N), F32),
                   jax.ShapeDtypeStruct((2, 8, HEAD_DIM), F32)],
        compiler_params=pltpu.CompilerParams(vmem_limit_bytes=VMEM_LIMIT),
        name="prep",
    )(lam_re, lam_im, log_dt.reshape(SSM_G, 1), bre_t, bim_t, pos, w1)


IN_COLS_PAD = ATTN_W + 4 * KV_W + 2 * KV_W + SSM_W + LANE
KT_KS = (0, 2 * HEAD_DIM)
KT_E = HEAD_DIM
KT_KW = 3 * HEAD_DIM
KT_COLS = 5 * HEAD_DIM
VT_ROWS = 4 * HEAD_DIM


def _inproj_body(x_ref, gmix_ref, w_ref, gq_ref, gks_ref, gkw_ref,
                 q_ref, kv_ref, win_ref, u_ref, gt_ref, *extra, q_scale):
    tm = x_ref.shape[0]
    xn = _rms(x_ref[...], gmix_ref[...]).astype(BF16)
    zq = _dot(xn, w_ref[:, 0:ATTN_W])
    for h in range(N_HEADS):
        sl = slice(h * HEAD_DIM, (h + 1) * HEAD_DIM)
        q_ref[:, sl] = (_rms(zq[:, sl], gq_ref[...]) * q_scale).astype(BF16)
    zkv = _dot(xn, w_ref[:, ATTN_W:ATTN_W + 4 * KV_W])
    ks, kw = [], []
    for s in range(4 * N_KV):
        col = zkv[:, s * HEAD_DIM:(s + 1) * HEAD_DIM]
        if s // N_KV == 2:
            col = _rms(col, gks_ref[...])
            ks.append(col)
        kv_ref[:, s, :] = col
    c0 = ATTN_W + 4 * KV_W
    zw = _dot(xn, w_ref[:, c0:c0 + 2 * KV_W])
    for s in range(2 * N_KV):
        col = zw[:, s * HEAD_DIM:(s + 1) * HEAD_DIM]
        if s // N_KV == 0:
            col = _rms(col, gkw_ref[...])
            kw.append(col)
        win_ref[:, s, :] = col
    if extra:
        cmp_ref, kt_ref, vt_ref = extra
        cmp_ref[...] = zkv[:, 0:2 * KV_W]
        for h in range(N_KV):
            kt_ref[:, KT_KS[h]:KT_KS[h] + HEAD_DIM] = ks[h].astype(BF16)
            kt_ref[:, KT_KW + h * HEAD_DIM:KT_KW + (h + 1) * HEAD_DIM] = kw[h].astype(BF16)
        blk = (lax.broadcasted_iota(I32, (tm, HEAD_DIM), 0) + pl.program_id(0) * tm) // SEL_BLOCK
        kt_ref[:, KT_E:KT_E + HEAD_DIM] = jnp.where(blk == lax.broadcasted_iota(I32, (tm, HEAD_DIM), 1), 1.0, 0.0).astype(BF16)
        vt_ref[0:KV_W, :] = zkv[:, 3 * KV_W:4 * KV_W].T.astype(BF16)
        vt_ref[KV_W:2 * KV_W, :] = zw[:, KV_W:2 * KV_W].T.astype(BF16)
    c1 = c0 + 2 * KV_W
    zt = _dot(xn, w_ref[:, c1:IN_COLS_PAD])
    gt_ref[...] = jax.nn.sigmoid(zt[:, 0:LANE])
    u_ref[...] = zt[:, N_BRANCH * N_HEADS:N_BRANCH * N_HEADS + SSM_W]


def _seg_spec(tm, rows, n_seg):
    tiles_per_seg = rows // n_seg // tm
    return pl.BlockSpec((tm, SSM_W), lambda i: (i % tiles_per_seg, i // tiles_per_seg))


def _inproj(x, g_mix, w, g_q, g_ks, g_kw, tm, q_scale, n_seg=1, attn_operands=False):
    rows = x.shape[0]
    row_spec = lambda n: pl.BlockSpec((tm, n), lambda i: (i, 0))
    full = lambda a: pl.BlockSpec(a.shape, lambda i: (0,) * a.ndim)
    out_specs = [row_spec(ATTN_W), pl.BlockSpec((tm, 4 * N_KV, HEAD_DIM), lambda i: (i, 0, 0)),
                 pl.BlockSpec((tm, 2 * N_KV, HEAD_DIM), lambda i: (i, 0, 0)), _seg_spec(tm, rows, n_seg), row_spec(LANE)]
    out_shape = [jax.ShapeDtypeStruct((rows, ATTN_W), BF16), jax.ShapeDtypeStruct((rows, 4 * N_KV, HEAD_DIM), F32),
                 jax.ShapeDtypeStruct((rows, 2 * N_KV, HEAD_DIM), F32),
                 jax.ShapeDtypeStruct((rows // n_seg, n_seg * SSM_W), F32), jax.ShapeDtypeStruct((rows, LANE), F32)]
    if attn_operands:
        out_specs += [row_spec(2 * KV_W), row_spec(KT_COLS), pl.BlockSpec((VT_ROWS, tm), lambda i: (0, i))]
        out_shape += [jax.ShapeDtypeStruct((rows, 2 * KV_W), F32), jax.ShapeDtypeStruct((rows, KT_COLS), BF16),
                      jax.ShapeDtypeStruct((VT_ROWS, rows), BF16)]
    return pl.pallas_call(
        functools.partial(_inproj_body, q_scale=q_scale),
        grid=(rows // tm,),
        in_specs=[row_spec(D_MODEL), full(g_mix), full(w), full(g_q), full(g_ks), full(g_kw)],
        out_specs=out_specs,
        out_shape=out_shape,
        compiler_params=_cparams(("arbitrary",)),
        name="inproj",
    )(x, g_mix, w, g_q, g_ks, g_kw)


PERM_ROWS = 2 * PAGE


def _perm_matrix():
    k = np.arange(PERM_ROWS // CMP_STRIDE)
    p = np.arange(CMP_STRIDE)
    m = np.zeros((PERM_ROWS, PERM_ROWS), np.float32)
    m[(p[:, None] * len(k) + k[None, :]).ravel(), (CMP_STRIDE * k[None, :] + p[:, None]).ravel()] = 1.0
    return jnp.asarray(m, dtype=BF16)


def _compress_pair(x_refs, perm_ref, xp_s, wcat2_ref, cpos, w2):
    n_rows = x_refs[0].shape[0]
    n_grp = n_rows // PERM_ROWS
    n_ch = n_rows // CMP_STRIDE
    ck = PERM_ROWS // CMP_STRIDE

    def perm(g, carry):
        rows = pl.ds(pl.multiple_of(g * PERM_ROWS, PERM_ROWS), PERM_ROWS)
        xcat = jnp.concatenate([x_refs[0][rows, :], x_refs[1][rows, :]], axis=1).astype(BF16)
        y = _dot(perm_ref[...], xcat).astype(BF16)
        for hd in range(N_KV):
            chunks = pl.ds(pl.multiple_of(hd * n_ch + g * ck, ck), ck)
            for p in range(CMP_STRIDE):
                xp_s[p, chunks, :] = y[p * ck:(p + 1) * ck, hd * HEAD_DIM:(hd + 1) * HEAD_DIM]
        return carry

    lax.fori_loop(0, n_grp, perm, 0, unroll=8)
    acc = jnp.zeros((N_KV * n_ch, 2 * HEAD_DIM), F32)
    for q in range(CMP_STRIDE // 2):
        acc = acc + _dot(jnp.concatenate([xp_s[2 * q], xp_s[2 * q + 1]], axis=-1), wcat2_ref[q])
    outs = []
    for hd in range(N_KV):
        a = acc[hd * n_ch:(hd + 1) * n_ch]
        e_hi_next = pltpu.roll(a[:, HEAD_DIM:2 * HEAD_DIM], n_ch - 1, 0)
        hid = _gelu(a[:, 0:HEAD_DIM] + e_hi_next + cpos)
        outs.append(_dot(hid.astype(BF16), w2))
    return outs


def _pcompress_body(x0_ref, x1_ref, perm_ref, wcat2_ref, cpos_ref, w2_ref, gkc_ref, o_ref, xp_s):
    kind = pl.program_id(0)
    outs = _compress_pair((x0_ref, x1_ref), perm_ref, xp_s, wcat2_ref.at[0], cpos_ref[0, 0:1, :], w2_ref[0])
    for hd in range(N_KV):
        o_ref[:, hd * HEAD_DIM:(hd + 1) * HEAD_DIM] = jnp.where(kind == 0, _rms(outs[hd], gkc_ref[...]), outs[hd])


def _pcompress(cmp_rows, wcat2, cpos, w2, g_kc):
    rows = cmp_rows.shape[0]
    n_ch = rows // CMP_STRIDE
    perm = _perm_matrix()
    return pl.pallas_call(
        _pcompress_body,
        grid=(2,),
        in_specs=[pl.BlockSpec((rows, HEAD_DIM), lambda kd: (0, 2 * kd)), pl.BlockSpec((rows, HEAD_DIM), lambda kd: (0, 2 * kd + 1)),
                  pl.BlockSpec(perm.shape, lambda kd: (0, 0)),
                  pl.BlockSpec((1,) + wcat2.shape[1:], lambda kd: (kd, 0, 0, 0)),
                  pl.BlockSpec((1, 8, HEAD_DIM), lambda kd: (kd, 0, 0)),
                  pl.BlockSpec((1, HEAD_DIM, HEAD_DIM), lambda kd: (kd, 0, 0)),
                  pl.BlockSpec((1, HEAD_DIM), lambda kd: (0, 0))],
        out_specs=pl.BlockSpec((n_ch, KV_W), lambda kd: (0, kd)),
        out_shape=jax.ShapeDtypeStruct((n_ch, 2 * KV_W), F32),
        scratch_shapes=[pltpu.VMEM((CMP_STRIDE, N_KV * n_ch, HEAD_DIM), BF16)],
        compiler_params=_cparams(("arbitrary",)),
        name="pcompress",
    )(cmp_rows, cmp_rows, perm, wcat2, cpos, w2, g_kc)


def _select_rounds(score_t, on_pick=None):
    n_j = score_t.shape[0]
    jio = lax.broadcasted_iota(I32, score_t.shape, 0)
    sc = score_t
    for r in range(N_SEL):
        m = jnp.max(sc, axis=0, keepdims=True)
        idx = jnp.min(jnp.where(sc == m, jio, n_j), axis=0, keepdims=True)
        sc = jnp.where(jio == idx, -jnp.inf, sc)
        if on_pick is not None:
            on_pick(r, idx, m >= 0.0)
    return (sc == -jnp.inf) & (score_t >= 0.0)


ONES_ROWS = 16
FAR_TILES = 8


def _col_max(tiles):
    m = jnp.max(tiles[0], axis=0, keepdims=True)
    for s in tiles[1:]:
        m = jnp.maximum(m, jnp.max(s, axis=0, keepdims=True))
    return m


def _lanes4(x):
    return jnp.concatenate([x] * Q_PER_KV, axis=1)


def _pattn_t_body(q_ref, gt_ref, kt_ref, vt_ref, kc_ref, vc_ref, vct_ref, rb_ref, bk0_ref, bk1_ref, bkc_ref, mft_ref,
                  gout_ref, o_ref, b0_s, b1_s, bc_s, bw4_s):
    i = pl.program_id(0)
    cols = Q_PER_KV * QB

    @pl.when(i == 0)
    def _tables():
        b_io = lax.broadcasted_iota(I32, (QB, QB), 0)
        a_io = lax.broadcasted_iota(I32, (QB, QB), 1)
        for hd in range(N_HEADS):
            t0 = _bias_lookup(bk0_ref[...], rb_ref, hd, shift=True)
            b0_s[hd] = jnp.where(a_io >= b_io, t0, NEG)
            b1_s[hd] = _bias_lookup(bk1_ref[...], rb_ref, hd, shift=True)
            bc_s[hd] = _bias_lookup(bkc_ref[...], rb_ref, hd, shift=True)
        bw4_s[...] = jnp.where(b_io >= a_io, 0.0, NEG)

    gt_t = gt_ref[...].T
    n_far = jnp.maximum(i - 1, 0) // FAR_TILES
    far_keys = FAR_TILES * QB

    def tab(ref, h):
        return jnp.concatenate([ref[Q_PER_KV * h + g] for g in range(Q_PER_KV)], axis=1)

    def v_aug(row0, start, n):
        return jnp.concatenate([vt_ref[row0:row0 + HEAD_DIM, pl.ds(start, n)], jnp.ones((ONES_ROWS, n), BF16)], axis=0)

    q_ts, qa_ts, o_cs, sel_st = [], [], [], []
    for h in range(N_KV):
        q_t = jnp.concatenate(
            [q_ref[:, (Q_PER_KV * h + g) * HEAD_DIM:(Q_PER_KV * h + g + 1) * HEAD_DIM].astype(F32).T.astype(BF16)
             for g in range(Q_PER_KV)], axis=1)
        hs = slice(h * HEAD_DIM, (h + 1) * HEAD_DIM)

        n_c = kc_ref.shape[0] - 2 * CMP_STRIDE
        near0 = pl.multiple_of(8 * i, 8)
        cf = lax.broadcasted_iota(I32, (n_c, QB), 0)
        ok_f = _lanes4((cf < 8 * i) & (cf >= CMP_STRIDE))
        s_f = jnp.where(ok_f, _dot(kc_ref[0:n_c, hs].astype(BF16), q_t), NEG)
        cn = lax.broadcasted_iota(I32, (32, cols), 0)
        a_n = lax.broadcasted_iota(I32, (32, cols), 1) % QB
        ok_n = (CMP_STRIDE * (cn - CMP_STRIDE) <= a_n - (CMP_LEN - 1)) & (cn + 8 * i >= CMP_STRIDE)
        s_n = jnp.where(ok_n, _dot(kc_ref[pl.ds(near0, 32), hs].astype(BF16), q_t) + tab(bc_s, h), NEG)
        m_c = _col_max([s_f, s_n])
        p_f = jnp.where(ok_f, jnp.exp(s_f - m_c), 0.0)
        p_n = jnp.where(ok_n, jnp.exp(s_n - m_c), 0.0)
        l_c = jnp.sum(p_f, axis=0, keepdims=True) + jnp.sum(p_n, axis=0, keepdims=True)
        inv_c = 1.0 / jnp.maximum(l_c, 1e-30)
        vc_near_t = vc_ref[pl.ds(near0, 32), hs].T.astype(BF16)
        o_c = (_dot(vct_ref[hs, 0:n_c].astype(BF16), p_f.astype(BF16)) + _dot(vc_near_t, p_n.astype(BF16))) * inv_c
        pn_f = p_f * inv_c
        pn_n = p_n * inv_c
        imp_f = pn_f[:, 0:QB] + pn_f[:, QB:2 * QB] + pn_f[:, 2 * QB:3 * QB] + pn_f[:, 3 * QB:4 * QB]
        imp_n = pn_n[:, 0:QB] + pn_n[:, QB:2 * QB] + pn_n[:, 2 * QB:3 * QB] + pn_n[:, 3 * QB:4 * QB]
        jn = lax.broadcasted_iota(I32, (QB, 32), 0)
        cc = lax.broadcasted_iota(I32, (QB, 32), 1) + 8 * i - CMP_STRIDE
        mnt = (((cc // CMP_PER_SEL) == jn) | (cc == CMP_PER_SEL * jn - 1)) & (cc >= 0)
        mnt = jnp.where(mnt, 1.0, 0.0).astype(BF16)
        imp_t = jnp.zeros((QB, QB), F32)
        for part in _split3(imp_f):
            imp_t = imp_t + _dot(mft_ref[...], part)
        for part in _split3(imp_n):
            imp_t = imp_t + _dot(mnt, part)
        j_io = lax.broadcasted_iota(I32, (QB, QB), 0)
        cur = 2 * i + lax.broadcasted_iota(I32, (QB, QB), 1) // SEL_BLOCK
        forced = (j_io == 0) | (j_io == cur) | (j_io == cur - 1)
        score_t = jnp.where(forced, 1e9, jnp.where(j_io <= cur, imp_t, -1.0))
        sel_t = _select_rounds(score_t)
        selneg = _lanes4(jnp.where(sel_t, 0.0, NEG).astype(BF16))
        qa_t = jnp.concatenate([q_t, selneg] if h == 0 else [selneg, q_t], axis=0)
        ka0 = h * HEAD_DIM

        near_s, near_v = [], []
        for back in range(FAR_TILES + 1):
            kt = i - back
            start = pl.multiple_of(jnp.maximum(kt, 0) * QB, QB)
            s = _dot(kt_ref[pl.ds(start, QB), ka0:ka0 + 2 * HEAD_DIM], qa_t)
            if back == 0:
                s = s + tab(b0_s, h)
            elif back == 1:
                s = s + tab(b1_s, h) + jnp.where(kt >= 0, 0.0, NEG)
            else:
                s = s + jnp.where((kt >= 0) & (kt >= FAR_TILES * n_far), 0.0, NEG)
            near_s.append(s)
            near_v.append(v_aug(h * HEAD_DIM, start, QB))
        m_s = _col_max(near_s)
        acc_s = jnp.zeros((HEAD_DIM + ONES_ROWS, cols), F32)
        for s, v in zip(near_s, near_v):
            acc_s = acc_s + _dot(v, jnp.exp(s - m_s).astype(BF16))
        q_ts.append(q_t)
        qa_ts.append(qa_t)
        o_cs.append(o_c)
        sel_st += [m_s, acc_s]

    def far_step(k, st):
        start = pl.multiple_of(k * far_keys, far_keys)
        out = []
        for h in range(N_KV):
            m, acc = st[2 * h], st[2 * h + 1]
            s = _dot(kt_ref[pl.ds(start, far_keys), h * HEAD_DIM:(h + 2) * HEAD_DIM], qa_ts[h])
            m_new = jnp.maximum(m, jnp.max(s, axis=0, keepdims=True))
            p = jnp.exp(s - m_new).astype(BF16)
            out += [m_new, jnp.exp(m - m_new) * acc + _dot(v_aug(h * HEAD_DIM, start, far_keys), p)]
        return tuple(out)

    sel_st = lax.fori_loop(0, n_far, far_step, tuple(sel_st))

    attn = [None] * N_HEADS
    for h in range(N_KV):
        q_t = q_ts[h]
        acc_s = sel_st[2 * h + 1]
        o_s = acc_s[0:HEAD_DIM] / acc_s[HEAD_DIM:HEAD_DIM + 1]
        o_c = o_cs[h]

        kw0 = KT_KW + h * HEAD_DIM
        win_s, win_v = [], []
        for back in range(5):
            kt = i - back
            start = pl.multiple_of(jnp.maximum(kt, 0) * QB, QB)
            s = _dot(kt_ref[pl.ds(start, QB), kw0:kw0 + HEAD_DIM], q_t)
            if back == 0:
                s = s + tab(b0_s, h)
            elif back == 1:
                s = s + tab(b1_s, h)
            elif back == 4:
                s = s + _lanes4(bw4_s[...])
            if back > 0:
                s = s + jnp.where(kt >= 0, 0.0, NEG)
            win_s.append(s)
            win_v.append(v_aug((N_KV + h) * HEAD_DIM, start, QB))
        m_w = _col_max(win_s)
        acc_w = jnp.zeros((HEAD_DIM + ONES_ROWS, cols), F32)
        for s, v in zip(win_s, win_v):
            acc_w = acc_w + _dot(v, jnp.exp(s - m_w).astype(BF16))
        o_w = acc_w[0:HEAD_DIM] / acc_w[HEAD_DIM:HEAD_DIM + 1]

        for g in range(Q_PER_KV):
            hd = Q_PER_KV * h + g
            cs = slice(g * QB, (g + 1) * QB)
            o_t = (gt_t[hd:hd + 1] * o_c[:, cs] + gt_t[N_HEADS + hd:N_HEADS + hd + 1] * o_s[:, cs]
                   + gt_t[2 * N_HEADS + hd:2 * N_HEADS + hd + 1] * o_w[:, cs])
            attn[hd] = o_t.T
    a = jnp.concatenate(attn, axis=1)
    o_ref[...] = _rms(a, gout_ref[...]).astype(BF16)


def _pattn_t_tables():
    b = np.arange(QB)[:, None]
    a = np.arange(QB)[None, :]
    bk0 = _bucket_np(a - b)
    bk1 = _bucket_np(a - b + QB)
    c = np.arange(32)[:, None] - CMP_STRIDE
    bkc = _bucket_np(a - CMP_STRIDE * c - (CMP_LEN - 1))
    cidx = np.arange(4 * QB)[None, :] - CMP_STRIDE
    j = np.arange(QB)[:, None]
    mft = (((cidx // CMP_PER_SEL) == j) | (cidx == CMP_PER_SEL * j - 1)) & (cidx >= 0)
    return (jnp.asarray(bk0), jnp.asarray(bk1), jnp.asarray(bkc), jnp.asarray(mft.astype(np.float32), dtype=BF16))


def _pattn_t(q, gates, kt, vt, kcp, vcp, vcpt, rel_bias, g_out):
    rows = q.shape[0]
    bk0, bk1, bkc, mft = _pattn_t_tables()
    full = lambda a: pl.BlockSpec(a.shape, lambda i: (0,) * a.ndim)
    once = lambda a: pl.BlockSpec(a.shape, lambda i: (0,) * a.ndim, pipeline_mode=pl.Buffered(1))
    return pl.pallas_call(
        _pattn_t_body,
        grid=(rows // QB,),
        in_specs=[pl.BlockSpec((QB, ATTN_W), lambda i: (i, 0)), pl.BlockSpec((QB, LANE), lambda i: (i, 0)),
                  once(kt), once(vt), full(kcp), full(vcp), full(vcpt), pl.BlockSpec(memory_space=pltpu.SMEM),
                  full(bk0), full(bk1), full(bkc), full(mft), full(g_out)],
        out_specs=pl.BlockSpec((QB, ATTN_W), lambda i: (i, 0)),
        out_shape=jax.ShapeDtypeStruct((rows, ATTN_W), BF16),
        scratch_shapes=[pltpu.VMEM((N_HEADS, QB, QB), F32), pltpu.VMEM((N_HEADS, QB, QB), F32),
                        pltpu.VMEM((N_HEADS, 32, QB), F32), pltpu.VMEM((QB, QB), F32)],
        compiler_params=_cparams(("arbitrary",)),
        name="pattn",
    )(q, gates, kt, vt, kcp, vcp, vcpt, rel_bias, bk0, bk1, bkc, mft, g_out)


def _ssm_body(u_ref, ire_ref, iim_ref, wb_ref, wc_ref, abre_ref, abim_ref, dsk_ref, wglu_ref, gout_ref,
              y_ref, fre_ref, fim_ref, u_s, xre_s, xim_s, y_s, sre_s, sim_s, *, n_seg, tc, emit):
    c = pl.program_id(0)
    sgl = SSM_LANES // SSM_SG

    @pl.when(c == 0)
    def _init():
        sre_s[...] = ire_ref[...]
        sim_s[...] = iim_ref[...]

    n_lt = SSM_W // LANE
    if tc == 1:
        u_cols = [u_ref[:, l * LANE:(l + 1) * LANE] for l in range(n_lt)]
    else:
        for s in range(n_seg):
            for l in range(n_lt):
                u_s[l, pl.ds(s, tc, stride=n_seg), :] = u_ref[:, s * SSM_W + l * LANE:s * SSM_W + (l + 1) * LANE]
        u_cols = [u_s[l] for l in range(n_lt)]
    for sg in range(SSM_SG):
        ls = slice(sg * sgl, (sg + 1) * sgl)
        bu = _dot(u_cols[sg].astype(BF16), wb_ref[sg])
        xre_s[:, ls] = bu[:, 0:sgl]
        xim_s[:, ls] = bu[:, sgl:2 * sgl]
        ar = abre_ref[:, ls]
        ai = abim_ref[:, ls]

        def step(t, carry):
            xr, xi = carry
            rows = pl.ds(pl.multiple_of(t * n_seg, n_seg), n_seg)
            nr = ar * xr - ai * xi + xre_s[rows, ls]
            ni = ar * xi + ai * xr + xim_s[rows, ls]
            xre_s[rows, ls] = nr
            xim_s[rows, ls] = ni
            return nr, ni

        xr, xi = lax.fori_loop(0, tc, step, (sre_s[:, ls], sim_s[:, ls]), unroll=min(tc, 4))
        sre_s[:, ls] = xr
        sim_s[:, ls] = xi
    fre_ref[...] = sre_s[...]
    fim_ref[...] = sim_s[...]
    if not emit:
        y_ref[...] = jnp.zeros(y_ref.shape, y_ref.dtype)
        return
    ys = []
    for sg in range(SSM_SG):
        ls = slice(sg * sgl, (sg + 1) * sgl)
        x2 = jnp.concatenate([xre_s[:, ls], xim_s[:, ls]], axis=1).astype(BF16)
        ys.append(_dot(x2, wc_ref[sg]))
    y = jnp.concatenate(ys, axis=1) + dsk_ref[...] * jnp.concatenate(u_cols, axis=1)
    z = _dot(_gelu(y).astype(BF16), wglu_ref[...])
    o = z[:, 0:SSM_W] * jax.nn.sigmoid(z[:, SSM_W:2 * SSM_W])
    yn = _rms(o, gout_ref[...])
    if tc == 1:
        y_ref[...] = yn.astype(BF16)
    else:
        for l in range(n_lt):
            y_s[l] = yn[:, l * LANE:(l + 1) * LANE]
        for s in range(n_seg):
            for l in range(n_lt):
                y_ref[:, s * SSM_W + l * LANE:s * SSM_W + (l + 1) * LANE] = (
                    y_s[l, pl.ds(s, tc, stride=n_seg), :].astype(BF16))


def _ssm(u2, n_seg, init_re, init_im, wb, wc, ab_re, ab_im, d_skip, w_glu, g_out, tc, emit):
    if tc == 1:
        assert u2.shape == (n_seg, SSM_W)
        t_len = 1
        blk = (n_seg, SSM_W)
    else:
        t_len = u2.shape[0]
        assert u2.shape[1] == n_seg * SSM_W and n_seg % 8 == 0
        blk = (tc, n_seg * SSM_W)
    rows = tc * n_seg
    full = lambda a: pl.BlockSpec(a.shape, lambda c: (0,) * a.ndim)
    body = functools.partial(_ssm_body, n_seg=n_seg, tc=tc, emit=emit)
    st = jax.ShapeDtypeStruct((n_seg, SSM_LANES), F32)
    y_shape = u2.shape if emit else blk
    y_map = (lambda c: (c, 0)) if emit else (lambda c: (0, 0))
    return pl.pallas_call(
        body,
        grid=(t_len // tc,),
        in_specs=[pl.BlockSpec(blk, lambda c: (c, 0)),
                  full(init_re), full(init_im), full(wb), full(wc),
                  full(ab_re), full(ab_im), full(d_skip), full(w_glu), full(g_out)],
        out_specs=[pl.BlockSpec(blk, y_map), full(init_re), full(init_im)],
        out_shape=[jax.ShapeDtypeStruct(y_shape, BF16), st, st],
        scratch_shapes=[pltpu.VMEM((SSM_W // LANE, rows, LANE), F32), pltpu.VMEM((rows, SSM_LANES), F32),
                        pltpu.VMEM((rows, SSM_LANES), F32), pltpu.VMEM((SSM_W // LANE, rows, LANE), F32),
                        pltpu.VMEM((n_seg, SSM_LANES), F32), pltpu.VMEM((n_seg, SSM_LANES), F32)],
        compiler_params=_cparams(("arbitrary",)),
        name="ssm_emit" if emit else "ssm_final",
    )(u2, init_re, init_im, wb, wc, ab_re, ab_im, d_skip, w_glu, g_out)


def _ssm_chain_body(fre_ref, fim_ref, abre_ref, abim_ref, ire_ref, iim_ref, *, n_seg, log2_len):
    pr = abre_ref[...]
    pi = abim_ref[...]
    for _ in range(log2_len):
        pr, pi = pr * pr - pi * pi, 2.0 * pr * pi
    cr = jnp.zeros((1, SSM_LANES), F32)
    ci = jnp.zeros((1, SSM_LANES), F32)
    for j in range(n_seg):
        ire_ref[j:j + 1, :] = cr
        iim_ref[j:j + 1, :] = ci
        fr = fre_ref[j:j + 1, :]
        fi = fim_ref[j:j + 1, :]
        cr, ci = fr + pr * cr - pi * ci, fi + pr * ci + pi * cr


def _ssm_chain(fre, fim, ab_re, ab_im, seg_len):
    n_seg = fre.shape[0]
    log2_len = int(math.log2(seg_len))
    assert 2 ** log2_len == seg_len
    st = jax.ShapeDtypeStruct((n_seg, SSM_LANES), F32)
    return pl.pallas_call(
        functools.partial(_ssm_chain_body, n_seg=n_seg, log2_len=log2_len),
        out_shape=[st, st],
        compiler_params=pltpu.CompilerParams(vmem_limit_bytes=VMEM_LIMIT),
        name="ssm_chain",
    )(fre, fim, ab_re, ab_im)


FFN_CHUNKS = 2


def _ffn_body(x_ref, a_ref, s_ref, wo_ref, gffn_ref, wa_ref, wg_ref, wd_ref, cw_ref, cb_ref, p2_ref, p1_ref,
              y_ref, cnew_ref, hn_s, car_s, *, seq, tm):
    r = pl.program_id(0)
    j = pl.program_id(1)

    @pl.when(j == 0)
    def _mix():
        h = x_ref[...] + _dot(a_ref[...], wo_ref[0:ATTN_W, :]) + _dot(s_ref[...], wo_ref[ATTN_W:D_MODEL, :])
        y_ref[...] = h
        hn_s[...] = _rms(h, gffn_ref[...]).astype(BF16)

    if seq:
        @pl.when(r == 0)
        def _first():
            car_s[j, 6:7, :] = p2_ref[...]
            car_s[j, 7:8, :] = p1_ref[...]

    hn = hn_s[...]
    tf = wa_ref.shape[1]
    tc = tf // FFN_CHUNKS
    part = None
    for k in range(FFN_CHUNKS):
        cs = slice(k * tc, (k + 1) * tc)
        a = _dot(hn, wa_ref[:, cs])
        g = _dot(hn, wg_ref[:, cs])
        if seq:
            p2 = car_s[j, 6:7, cs]
            p1 = car_s[j, 7:8, cs]
            row = lax.broadcasted_iota(I32, a.shape, 0)
            a1 = jnp.where(row == 0, p1, pltpu.roll(a, 1, 0))
            a2 = jnp.where(row == 0, p2, jnp.where(row == 1, p1, pltpu.roll(a, 2, 0)))
            car_s[j, :, cs] = a[tm - 8:tm, :]
            cnew_ref[:, pl.ds(pl.multiple_of(j * tf + k * tc, tc), tc)] = a[tm - 8:tm, :]
        else:
            a1 = p1_ref[:, cs]
            a2 = p2_ref[:, cs]
            cnew_ref[:, cs] = a
        c = cb_ref[:, cs] + cw_ref[0:1, cs] * a2 + cw_ref[1:2, cs] * a1 + cw_ref[2:3, cs] * a
        d = _dot((_gelu(c) * g).astype(BF16), wd_ref[cs, :])
        part = d if part is None else part + d
    y_ref[...] += part


def _ffn(x, a_n, s_n, w_out, g_ffn, w_up, wd, cw, cb, p2, p1, tm, tf, seq, n_seg=1):
    rows = x.shape[0]
    nj = D_FF_PAD // tf
    body = functools.partial(_ffn_body, seq=seq, tm=tm)
    tiles_per_seg = rows // n_seg // tm
    if seq:
        tap_spec = pl.BlockSpec((1, tf), lambda r, j: (0, j))
        cnew_spec = pl.BlockSpec((8, D_FF_PAD), lambda r, j: (0, 0))
        cnew_shape = jax.ShapeDtypeStruct((8, D_FF_PAD), F32)
    else:
        tap_spec = pl.BlockSpec((tm, tf), lambda r, j: (r, j))
        cnew_spec = pl.BlockSpec((tm, tf), lambda r, j: (r, j))
        cnew_shape = jax.ShapeDtypeStruct((rows, D_FF_PAD), F32)
    return pl.pallas_call(
        body,
        grid=(rows // tm, nj),
        in_specs=[pl.BlockSpec((tm, D_MODEL), lambda r, j: (r, 0)), pl.BlockSpec((tm, ATTN_W), lambda r, j: (r, 0)),
                  pl.BlockSpec((tm, SSM_W), lambda r, j: (r % tiles_per_seg, r // tiles_per_seg)),
                  pl.BlockSpec(w_out.shape, lambda r, j: (0, 0), pipeline_mode=pl.Buffered(1)),
                  pl.BlockSpec((1, D_MODEL), lambda r, j: (0, 0)),
                  pl.BlockSpec((None, D_MODEL, tf), lambda r, j: (0, 0, j)),
                  pl.BlockSpec((None, D_MODEL, tf), lambda r, j: (1, 0, j)),
                  pl.BlockSpec((tf, D_MODEL), lambda r, j: (j, 0)), pl.BlockSpec((CONV_W, tf), lambda r, j: (0, j)),
                  pl.BlockSpec((1, tf), lambda r, j: (0, j)), tap_spec, tap_spec],
        out_specs=[pl.BlockSpec((tm, D_MODEL), lambda r, j: (r, 0)), cnew_spec],
        out_shape=[jax.ShapeDtypeStruct((rows, D_MODEL), F32), cnew_shape],
        scratch_shapes=[pltpu.VMEM((tm, D_MODEL), BF16), pltpu.VMEM((nj, 8, tf), F32)],
        compiler_params=_cparams(("arbitrary", "arbitrary")),
        name="ffn_seq" if seq else "ffn_rows",
    )(x, a_n, s_n, w_out, g_ffn, w_up, w_up, wd, cw, cb, p2, p1)


def _s1_copies(pt_ref, cache_ref, x_s, sem, b, slot, n_pages):
    cps = []
    for pg in range(n_pages):
        page = pt_ref[b * n_pages + pg]
        for s in range(2 * N_KV):
            cps.append(pltpu.make_async_copy(cache_ref.at[page, :, s, :],
                                             x_s.at[slot, s, pl.ds(pg * PAGE, PAGE), :], sem.at[slot]))
    return cps


def _s1_body(pt_ref, cache_ref, q_ref, perm_ref, wcat2_ref, cpos_ref, w2_ref, gkc_ref, rb_ref, bkc_ref, mt_ref,
             oc_ref, idx_ref, val_ref, x_s, xp_s, bias_s, sem, *, n_pages, past):
    b = pl.program_id(0)
    nb = pl.num_programs(0)
    slot = b % 2

    @pl.when(b == 0)
    def _first():
        for cp in _s1_copies(pt_ref, cache_ref, x_s, sem, 0, 0, n_pages):
            cp.start()
        for hd in range(N_HEADS):
            bias_s[hd:hd + 1, :] = _bias_lookup(bkc_ref[...], rb_ref, hd)

    @pl.when(b + 1 < nb)
    def _next():
        for cp in _s1_copies(pt_ref, cache_ref, x_s, sem, b + 1, 1 - slot, n_pages):
            cp.start()

    for cp in _s1_copies(pt_ref, cache_ref, x_s, sem, b, slot, n_pages):
        cp.wait()

    n_c = past // CMP_STRIDE
    cio = lax.broadcasted_iota(I32, (N_HEADS, n_c), 1)
    hrow = lax.broadcasted_iota(I32, (N_HEADS, n_c), 0) // Q_PER_KV
    q = q_ref[0]
    s_all = jnp.zeros((N_HEADS, n_c), F32)
    kcs = _compress_pair((x_s.at[slot, 0], x_s.at[slot, 1]), perm_ref, xp_s, wcat2_ref.at[0], cpos_ref[0, 0:1, :], w2_ref[0])
    vcs = _compress_pair((x_s.at[slot, N_KV], x_s.at[slot, N_KV + 1]), perm_ref, xp_s, wcat2_ref.at[1], cpos_ref[1, 0:1, :],
                         w2_ref[1])
    vcs = [v.astype(BF16) for v in vcs]
    for h in range(N_KV):
        kc = _rms(kcs[h], gkc_ref[...])
        s_all = jnp.where(hrow == h, _dot_nt(q, kc.astype(BF16)), s_all)
    ok = cio < n_c - 1
    s_all = jnp.where(ok, s_all + bias_s[...], NEG)
    m = jnp.max(s_all, axis=-1, keepdims=True)
    p = jnp.where(ok, jnp.exp(s_all - m), 0.0)
    p = p / jnp.maximum(jnp.sum(p, axis=-1, keepdims=True), 1e-30)
    pb = p.astype(BF16)
    hrow_o = lax.broadcasted_iota(I32, (N_HEADS, HEAD_DIM), 0) // Q_PER_KV
    o_c = jnp.zeros((N_HEADS, HEAD_DIM), F32)
    for h in range(N_KV):
        o_c = jnp.where(hrow_o == h, _dot(pb, vcs[h]), o_c)
    oc_ref[0] = o_c
    rio = lax.broadcasted_iota(I32, (8, n_c), 0)
    imp = jnp.zeros((8, n_c), F32)
    for h in range(N_KV):
        ih = p[4 * h:4 * h + 1] + p[4 * h + 1:4 * h + 2] + p[4 * h + 2:4 * h + 3] + p[4 * h + 3:4 * h + 4]
        imp = jnp.where(rio == h, ih, imp)
    imp = jnp.concatenate([imp, jnp.zeros((LANE - 8, n_c), F32)], axis=0)
    n_j = mt_ref.shape[0]
    imp_t = jnp.zeros((n_j, LANE), F32)
    for part in _split3(imp):
        imp_t = imp_t + _dot_nt(mt_ref[...], part)
    j_io = lax.broadcasted_iota(I32, (n_j, LANE), 0)
    cur = past // SEL_BLOCK
    forced = (j_io == 0) | (j_io == cur) | (j_io == cur - 1)
    score_t = jnp.where(forced, 1e9, jnp.where(j_io <= cur, imp_t, -1.0))
    score_t = jnp.where(j_io <= cur, score_t, -jnp.inf)

    def on_pick(r, idx, okv):
        idx_ref[0, r:r + 1, :] = idx
        val_ref[0, r:r + 1, :] = jnp.where(okv, 1, 0)

    _select_rounds(score_t, on_pick)


def _s1(page_table, cache4, q3, wcat2, cpos, w2, g_kc, rel_bias):
    n_b, n_pages = page_table.shape
    past = n_pages * PAGE
    n_c = past // CMP_STRIDE
    perm = _perm_matrix()
    ns = past // SEL_BLOCK + 1
    n_j = -(-ns // 8) * 8
    c = np.arange(n_c)[None, :]
    bkc = _bucket_np(past - (CMP_STRIDE * c + CMP_LEN - 1))
    j = np.arange(n_j)[:, None]
    mt = (((c // CMP_PER_SEL) == j) | (c == CMP_PER_SEL * j - 1)) & (c < n_c - 1)
    mt = jnp.asarray(mt.astype(np.float32), dtype=BF16)
    full = lambda a: pl.BlockSpec(a.shape, lambda b, pt: (0,) * a.ndim)
    body = functools.partial(_s1_body, n_pages=n_pages, past=past)
    return pl.pallas_call(
        body,
        grid_spec=pltpu.PrefetchScalarGridSpec(
            num_scalar_prefetch=1,
            grid=(n_b,),
            in_specs=[pl.BlockSpec(memory_space=pl.ANY), pl.BlockSpec((1, N_HEADS, HEAD_DIM), lambda b, pt: (b, 0, 0)),
                      full(perm), full(wcat2), full(cpos), full(w2), full(g_kc), pl.BlockSpec(memory_space=pltpu.SMEM),
                      pl.BlockSpec((1, n_c), lambda b, pt: (0, 0)), full(mt)],
            out_specs=[pl.BlockSpec((1, N_HEADS, HEAD_DIM), lambda b, pt: (b, 0, 0)),
                       pl.BlockSpec((1, N_SEL, LANE), lambda b, pt: (b, 0, 0)),
                       pl.BlockSpec((1, N_SEL, LANE), lambda b, pt: (b, 0, 0))],
            scratch_shapes=[pltpu.VMEM((2, 2 * N_KV, past, HEAD_DIM), F32),
                            pltpu.VMEM((CMP_STRIDE, N_KV * n_c, HEAD_DIM), BF16), pltpu.VMEM((N_HEADS, n_c), F32),
                            pltpu.SemaphoreType.DMA((2,))],
        ),
        out_shape=[jax.ShapeDtypeStruct((n_b, N_HEADS, HEAD_DIM), F32), jax.ShapeDtypeStruct((n_b, N_SEL, LANE), I32),
                   jax.ShapeDtypeStruct((n_b, N_SEL, LANE), I32)],
        compiler_params=_cparams(("arbitrary",)),
        name="sample_cmp",
    )(page_table.reshape(-1), cache4, q3, perm, wcat2, cpos, w2, g_kc, rel_bias, jnp.asarray(bkc), mt)


def _s2_copies(idx_ref, pt_ref, cache_ref, cwin_ref, ks_s, vs_s, kw_s, vw_s, sem, b, slot, n_pages):
    cps = []
    n_blk = n_pages * (PAGE // SEL_BLOCK)
    for h in range(N_KV):
        for r in range(N_SEL):
            jb = jnp.minimum(idx_ref[(b * N_SEL + r) * N_KV + h], n_blk - 1)
            page = pt_ref[b * n_pages + jb // 2]
            row0 = pl.multiple_of((jb % 2) * SEL_BLOCK, SEL_BLOCK)
            cps.append(pltpu.make_async_copy(cache_ref.at[page, pl.ds(row0, SEL_BLOCK), 2 * N_KV + h, :],
                                             ks_s.at[slot, h, pl.ds(r * SEL_BLOCK, SEL_BLOCK), :], sem.at[slot]))
            cps.append(pltpu.make_async_copy(cache_ref.at[page, pl.ds(row0, SEL_BLOCK), 3 * N_KV + h, :],
                                             vs_s.at[slot, h, pl.ds(r * SEL_BLOCK, SEL_BLOCK), :], sem.at[slot]))
        cps.append(pltpu.make_async_copy(cwin_ref.at[b, :, h, :], kw_s.at[slot, h], sem.at[slot]))
        cps.append(pltpu.make_async_copy(cwin_ref.at[b, :, N_KV + h, :], vw_s.at[slot, h], sem.at[slot]))
    return cps


def _s2_body(idx_ref, val_ref, pt_ref, cache_ref, cwin_ref, q_ref, oc_ref, gt_ref, kns_ref, vns_ref, knw_ref, vnw_ref,
             rb_ref, bks_ref, bkw_ref, gout_ref, o_ref, ks_s, vs_s, kw_s, vw_s, bs_s, bw_s, sem, *, n_pages, past):
    b = pl.program_id(0)
    nb = pl.num_programs(0)
    slot = b % 2
    args = (idx_ref, pt_ref, cache_ref, cwin_ref, ks_s, vs_s, kw_s, vw_s, sem)

    @pl.when(b == 0)
    def _first():
        for cp in _s2_copies(*args, 0, 0, n_pages):
            cp.start()
        for hd in range(N_HEADS):
            bs_s[hd:hd + 1, :] = _bias_lookup(bks_ref[...], rb_ref, hd)
            bw_s[hd:hd + 1, :] = _bias_lookup(bkw_ref[...], rb_ref, hd)

    @pl.when(b + 1 < nb)
    def _next():
        for cp in _s2_copies(*args, b + 1, 1 - slot, n_pages):
            cp.start()

    for cp in _s2_copies(*args, b, slot, n_pages):
        cp.wait()

    n_blk = n_pages * (PAGE // SEL_BLOCK)
    q = q_ref[0]
    qf = q.astype(F32)
    hrow = lax.broadcasted_iota(I32, (N_HEADS, 1), 0) // Q_PER_KV
    lane = lax.broadcasted_iota(I32, (N_HEADS, LANE), 1)
    bias0 = jnp.concatenate([jnp.full((1, 1), rb_ref[0, hd], F32) for hd in range(N_HEADS)], axis=0)
    b31 = jnp.concatenate([jnp.full((1, 1), rb_ref[REL_BUCKETS - 1, hd], F32) for hd in range(N_HEADS)], axis=0)

    tiles = []
    new_sel = jnp.zeros((N_HEADS, 1), F32)
    for t in range(N_SEL // 2):
        s_t = jnp.zeros((N_HEADS, LANE), F32)
        for h in range(N_KV):
            s_h = _dot_nt(q, ks_s[slot, h, pl.ds(t * LANE, LANE), :].astype(BF16))
            halves = []
            for half in range(2):
                r = 2 * t + half
                jb = idx_ref[(b * N_SEL + r) * N_KV + h]
                okr = (val_ref[(b * N_SEL + r) * N_KV + h] > 0) & (jb < n_blk)
                near = bs_s[:, (half * 2) * LANE:(half * 2 + 1) * LANE]
                nearer = bs_s[:, (half * 2 + 1) * LANE:(half * 2 + 2) * LANE]
                bias = jnp.where(jb == n_blk - 1, nearer, jnp.where(jb == n_blk - 2, near, b31))
                halves.append(jnp.where(okr, s_h + bias, NEG))
                new_sel = jnp.where((hrow == h) & (val_ref[(b * N_SEL + r) * N_KV + h] > 0) & (jb == n_blk), 1.0, new_sel)
            s_h = jnp.where(lane < SEL_BLOCK, halves[0], halves[1])
            s_t = jnp.where(hrow == h, s_h, s_t)
        tiles.append(s_t)
    s_new = jnp.sum(qf * kns_ref[0], axis=-1, keepdims=True) + bias0
    s_new = jnp.where(new_sel > 0.5, s_new, NEG)
    m = s_new
    for s_t in tiles:
        m = jnp.maximum(m, jnp.max(s_t, axis=-1, keepdims=True))
    p_new = jnp.where(new_sel > 0.5, jnp.exp(s_new - m), 0.0)
    l = p_new
    acc = p_new * vns_ref[0]
    for t, s_t in enumerate(tiles):
        p = jnp.where(s_t > 0.5 * NEG, jnp.exp(s_t - m), 0.0)
        l = l + jnp.sum(p, axis=-1, keepdims=True)
        pb = p.astype(BF16)
        for h in range(N_KV):
            pv = _dot(pb, vs_s[slot, h, pl.ds(t * LANE, LANE), :].astype(BF16))
            acc = acc + jnp.where(hrow == h, pv, 0.0)
    o_s = acc / jnp.maximum(l, 1e-30)

    n_w = kw_s.shape[2]
    wt = []
    for t in range(n_w // LANE):
        s_t = jnp.zeros((N_HEADS, LANE), F32)
        for h in range(N_KV):
            s_h = _dot_nt(q, kw_s[slot, h, pl.ds(t * LANE, LANE), :].astype(BF16))
            s_t = jnp.where(hrow == h, s_h, s_t)
        wt.append(s_t + bw_s[:, t * LANE:(t + 1) * LANE])
    s_new = jnp.sum(qf * knw_ref[0], axis=-1, keepdims=True) + bias0
    m = s_new
    for s_t in wt:
        m = jnp.maximum(m, jnp.max(s_t, axis=-1, keepdims=True))
    p_new = jnp.exp(s_new - m)
    l = p_new
    acc = p_new * vnw_ref[0]
    for t, s_t in enumerate(wt):
        p = jnp.exp(s_t - m)
        l = l + jnp.sum(p, axis=-1, keepdims=True)
        pb = p.astype(BF16)
        for h in range(N_KV):
            pv = _dot(pb, vw_s[slot, h, pl.ds(t * LANE, LANE), :].astype(BF16))
            acc = acc + jnp.where(hrow == h, pv, 0.0)
    o_w = acc / l

    gt = gt_ref[0]
    a = gt[:, 0:1] * oc_ref[0] + gt[:, 1:2] * o_s + gt[:, 2:3] * o_w
    ms = jnp.sum(jnp.sum(a * a, axis=-1, keepdims=True), axis=0, keepdims=True) / (N_HEADS * HEAD_DIM)
    o_ref[0] = (a * lax.rsqrt(ms + EPS) * gout_ref[...]).astype(BF16)


def _s2(idx, val, page_table, cache4, cwin4, q3, o_c, gates3, kns, vns, knw, vnw, rel_bias, g_out3):
    n_b, n_pages = page_table.shape
    past = n_pages * PAGE
    n_w = cwin4.shape[1]
    s = np.arange(SEL_BLOCK)
    d_near = past - ((past // SEL_BLOCK - 2) * SEL_BLOCK + s)
    d_nearer = past - ((past // SEL_BLOCK - 1) * SEL_BLOCK + s)
    z = np.zeros(SEL_BLOCK, np.int64)
    bks = np.concatenate([d_near, z, d_nearer, z, z, d_near, z, d_nearer])[None, :]
    bkw = (past - (past - n_w + np.arange(n_w)))[None, :]
    full = lambda a: pl.BlockSpec(a.shape, lambda b, *_: (0,) * a.ndim)
    per_b = lambda a: pl.BlockSpec((1,) + a.shape[1:], lambda b, *_: (b,) + (0,) * (a.ndim - 1))
    body = functools.partial(_s2_body, n_pages=n_pages, past=past)
    return pl.pallas_call(
        body,
        grid_spec=pltpu.PrefetchScalarGridSpec(
            num_scalar_prefetch=3,
            grid=(n_b,),
            in_specs=[pl.BlockSpec(memory_space=pl.ANY), pl.BlockSpec(memory_space=pl.ANY),
                      per_b(q3), per_b(o_c), per_b(gates3), per_b(kns), per_b(vns), per_b(knw), per_b(vnw),
                      pl.BlockSpec(memory_space=pltpu.SMEM), pl.BlockSpec((1, 4 * LANE), lambda b, *_: (0, 0)),
                      pl.BlockSpec((1, n_w), lambda b, *_: (0, 0)), full(g_out3)],
            out_specs=pl.BlockSpec((1, N_HEADS, HEAD_DIM), lambda b, *_: (b, 0, 0)),
            scratch_shapes=[pltpu.VMEM((2, N_KV, N_SEL * SEL_BLOCK, HEAD_DIM), F32),
                            pltpu.VMEM((2, N_KV, N_SEL * SEL_BLOCK, HEAD_DIM), F32),
                            pltpu.VMEM((2, N_KV, n_w, HEAD_DIM), F32), pltpu.VMEM((2, N_KV, n_w, HEAD_DIM), F32),
                            pltpu.VMEM((N_HEADS, 4 * LANE), F32), pltpu.VMEM((N_HEADS, n_w), F32),
                            pltpu.SemaphoreType.DMA((2,))],
        ),
        out_shape=jax.ShapeDtypeStruct((n_b, N_HEADS, HEAD_DIM), BF16),
        compiler_params=_cparams(("arbitrary",)),
        name="sample_attn",
    )(idx, val, page_table.reshape(-1), cache4, cwin4, q3, o_c, gates3, kns, vns, knw, vnw, rel_bias,
      jnp.asarray(_bucket_np(bks)), jnp.asarray(_bucket_np(bkw)), g_out3)


def _winshift_body(cwin_ref, new_ref, o_ref):
    n_w = cwin_ref.shape[1]
    o_ref[0, 0:n_w - 1] = cwin_ref[0, 1:n_w]
    o_ref[0, n_w - 1] = new_ref[0]


def _winshift(cwin4, new3):
    n_b = cwin4.shape[0]
    return pl.pallas_call(
        _winshift_body,
        grid=(n_b,),
        in_specs=[pl.BlockSpec((1,) + cwin4.shape[1:], lambda b: (b, 0, 0, 0)),
                  pl.BlockSpec((1,) + new3.shape[1:], lambda b: (b, 0, 0))],
        out_specs=pl.BlockSpec((1,) + cwin4.shape[1:], lambda b: (b, 0, 0, 0)),
        out_shape=jax.ShapeDtypeStruct(cwin4.shape, F32),
        compiler_params=_cparams(("arbitrary",)),
        name="winshift",
    )(cwin4, new3)


def _block_diag_b(bb_re, bb_im):
    gl = SSM_G // SSM_SG
    eye = jnp.eye(gl, dtype=F32)

    def one(bb):
        t = jnp.transpose(bb, (1, 0, 2)).reshape(SSM_SG, gl, SSM_P, SSM_N)
        return jnp.einsum('sgpn,gh->sgphn', t, eye).reshape(SSM_SG, gl * SSM_P, gl * SSM_N)

    return jnp.concatenate([one(bb_re), one(bb_im)], axis=2).astype(BF16)


def _block_diag_c(c_re, c_im):
    gl = SSM_G // SSM_SG
    eye = jnp.eye(gl, dtype=F32)

    def one(c):
        t = jnp.transpose(c, (0, 2, 1)).reshape(SSM_SG, gl, SSM_N, SSM_P)
        return jnp.einsum('sgnp,gh->sgnhp', t, eye).reshape(SSM_SG, gl * SSM_N, gl * SSM_P)

    return jnp.concatenate([one(c_re), -one(c_im)], axis=1).astype(BF16)


def kernel(x_prompt, x_sample, cache_kv, page_table, cache_win, state_ssm, state_conv, rel_bias, g_mix, w_in, g_q, g_k, w_cmp1, pos_cmp, w_cmp2, lam_re, lam_im, log_dt, b_re, b_im, c_re, c_im, d_skip, w_glu, g_out_attn, g_out_ssm, w_out, g_ffn, w_up, conv_w, conv_b, w_down):
    depth = g_mix.shape[0]
    assert depth == 1 and x_prompt.shape[0] == 1 and x_sample.shape[1] == 1
    seq = x_prompt.shape[1]
    n_b = x_sample.shape[0]
    n_pages = page_table.shape[1]
    n_w = cache_win.shape[2]
    li = 0
    row = lambda v: v.reshape(1, -1)

    w_in_b = _cast_transposed(w_in[li].T, IN_COLS_PAD)
    w1 = w_cmp1[li]
    wcat = jnp.concatenate([w1[:, :CMP_STRIDE], w1[:, CMP_STRIDE:]], axis=-1).astype(BF16).reshape(
        2, CMP_STRIDE // 2, 2 * HEAD_DIM, 2 * HEAD_DIM)
    w2 = w_cmp2[li].astype(BF16)
    w_glu_b = w_glu[li].astype(BF16)
    w_out_b = w_out[li].astype(BF16)
    padc = lambda a: jnp.pad(a, ((0, 0), (0, D_FF_PAD - D_FF)))
    w_up_b = _cast_pad(w_up[li], 2, D_FF_PAD, 256)
    wd = _cast_pad_rows(w_down[li], D_FF_PAD, D_FF // 8)
    cw = padc(conv_w[li])
    cb = padc(row(conv_b[li]))

    ab_re, ab_im, bb_re, bb_im, cpos = _prep(lam_re[li], lam_im[li], log_dt[li], b_re[li], b_im[li], pos_cmp[li], w1)
    wb = _block_diag_b(bb_re, bb_im)
    wc = _block_diag_c(c_re[li], c_im[li])
    ab_re = row(ab_re)
    ab_im = row(ab_im)

    in_args = (row(g_mix[li]), w_in_b, row(g_q[li]), row(g_k[li, 1]), row(g_k[li, 2]))
    ssm_args = (wb, wc, ab_re, ab_im, row(d_skip[li]), w_glu_b, row(g_out_ssm[li]))

    xp = x_prompt[0]
    n_seg = 8
    q, kv, win, u2, gates, cmp_rows, kt, vt = _inproj(xp, *in_args, tm=512, q_scale=HEAD_DIM ** -0.5, n_seg=n_seg,
                                            attn_operands=True)
    cmp_out = _pcompress(cmp_rows, wcat, cpos, w2, row(g_k[li, 0]))
    zpad = jnp.zeros((CMP_STRIDE, 2 * KV_W), F32)
    cmp_pad = jnp.concatenate([zpad, cmp_out, zpad], axis=0)
    attn_n = _pattn_t(q, gates, kt, vt, cmp_pad[:, :KV_W], cmp_pad[:, KV_W:], cmp_pad[:, KV_W:].T, rel_bias,
                      row(g_out_attn[li]))
    zst = jnp.zeros((n_seg, SSM_LANES), F32)
    _, fre, fim = _ssm(u2, n_seg, zst, zst, *ssm_args, tc=64, emit=False)
    ire, iim = _ssm_chain(fre, fim, ab_re, ab_im, seq // n_seg)
    ssm_n, hre, him = _ssm(u2, n_seg, ire, iim, *ssm_args, tc=64, emit=True)
    zrow = jnp.zeros((1, D_FF_PAD), F32)
    ffn_w = (w_out_b, row(g_ffn[li]), w_up_b, wd, cw, cb)
    y_p, cnew_p = _ffn(xp, attn_n, ssm_n, *ffn_w, zrow, zrow, tm=512, tf=512, seq=True, n_seg=n_seg)

    y_prompt = y_p[None]
    kv_prompt = kv.reshape(1, 1, seq, 4, N_KV, HEAD_DIM)
    win_prompt = win[seq - min(WINDOW, seq):].reshape(1, 1, min(WINDOW, seq), 2, N_KV, HEAD_DIM)
    ssm_prompt = jnp.stack([hre[n_seg - 1], him[n_seg - 1]], axis=-1).reshape(1, 1, SSM_G, SSM_N, 2)
    conv_prompt = cnew_p[6:8, :D_FF].reshape(1, 1, CONV_W - 1, D_FF)

    xs = x_sample[:, 0]
    q_s, kv_s, win_s, u_s, gates_s = _inproj(xs, *in_args, tm=n_b, q_scale=HEAD_DIM ** -0.5)
    cache4 = cache_kv[li].reshape(cache_kv.shape[1], PAGE, 4 * N_KV, HEAD_DIM)
    cwin4 = cache_win[li].reshape(n_b, n_w, 2 * N_KV, HEAD_DIM)
    q3 = q_s.reshape(n_b, N_HEADS, HEAD_DIM)
    o_c, idx, val = _s1(page_table, cache4, q3, wcat, cpos, w2, row(g_k[li, 0]), rel_bias)
    idx = idx[:, :, :N_KV].reshape(-1)
    val = val[:, :, :N_KV].reshape(-1)
    rep = lambda a: jnp.repeat(a.reshape(n_b, N_KV, HEAD_DIM), Q_PER_KV, axis=1)
    kns = rep(kv_s[:, 2 * N_KV:3 * N_KV])
    vns = rep(kv_s[:, 3 * N_KV:4 * N_KV])
    knw = rep(win_s[:, 0:N_KV])
    vnw = rep(win_s[:, N_KV:2 * N_KV])
    g3 = jnp.transpose(gates_s[:, :N_BRANCH * N_HEADS].reshape(n_b, N_BRANCH, N_HEADS), (0, 2, 1))
    g3 = jnp.pad(g3, ((0, 0), (0, 0), (0, LANE - N_BRANCH)))
    attn_s = _s2(idx, val, page_table, cache4, cwin4, q3, o_c, g3, kns, vns, knw, vnw, rel_bias,
                 g_out_attn[li].reshape(N_HEADS, HEAD_DIM))
    st = state_ssm[li].reshape(n_b, SSM_LANES, 2)
    ssm_s, sre, sim = _ssm(u_s, n_b, st[:, :, 0], st[:, :, 1], *ssm_args, tc=1, emit=True)
    sc = state_conv[li]
    y_s, a_s = _ffn(xs, attn_s.reshape(n_b, ATTN_W), ssm_s, *ffn_w, padc(sc[:, 0]), padc(sc[:, 1]),
                    tm=n_b, tf=512, seq=False)
    win_sample = _winshift(cwin4, win_s)

    y_sample = y_s[:, None]
    kv_sample = kv_s.reshape(1, n_b, 1, 4, N_KV, HEAD_DIM)
    win_sample = win_sample.reshape(1, n_b, n_w, 2, N_KV, HEAD_DIM)
    ssm_sample = jnp.stack([sre, sim], axis=-1).reshape(1, n_b, SSM_G, SSM_N, 2)
    conv_sample = jnp.stack([sc[:, 1], a_s[:, :D_FF]], axis=1)[None]
    return (y_prompt, y_sample, kv_prompt, kv_sample, win_prompt, win_sample,
            ssm_prompt, ssm_sample, conv_prompt, conv_sample)
```

```python
import functools
import math

import numpy as np
import jax
import jax.numpy as jnp
from jax import lax
from jax.experimental import pallas as pl
from jax.experimental.pallas import tpu as pltpu

F32 = jnp.float32
BF16 = jnp.bfloat16
I32 = jnp.int32

D_MODEL = 2048
HEAD_DIM = 128
N_HEADS = 8
N_KV = 2
Q_PER_KV = 4
ATTN_W = 1024
KV_W = 256
N_BRANCH = 3
CMP_LEN = 32
CMP_STRIDE = 16
SEL_BLOCK = 64
CMP_PER_SEL = 4
N_SEL = 16
WINDOW = 512
REL_BUCKETS = 32
REL_MAX_DIST = 128
PAGE = 128
SSM_W = 1024
SSM_G = 64
SSM_N = 64
SSM_P = 16
SSM_SG = 8
SSM_LANES = SSM_G * SSM_N
D_FF = 5504
D_FF_PAD = 5632
CONV_W = 3
EPS = 1e-6
NEG = -1e30
QB = 128
LANE = 128
VMEM_LIMIT = 56 * 1024 * 1024


def _cparams(sem):
    return pltpu.CompilerParams(dimension_semantics=sem, vmem_limit_bytes=VMEM_LIMIT)


def _rms(x, g):
    return x * lax.rsqrt(jnp.mean(x * x, axis=-1, keepdims=True) + EPS) * g


def _gelu(x):
    return jax.nn.gelu(x)


def _dot(a, b):
    return jnp.dot(a, b, preferred_element_type=F32)


def _dot_nt(a, b):
    return lax.dot_general(a, b, (((1,), (1,)), ((), ())), preferred_element_type=F32)


def _split3(x):
    hi = x.astype(BF16)
    r1 = x - hi.astype(F32)
    mid = r1.astype(BF16)
    lo = (r1 - mid.astype(F32)).astype(BF16)
    return hi, mid, lo


def _bucket_np(d):
    n = np.maximum(d, 0)
    exact = REL_BUCKETS // 2
    nf = np.maximum(n, 1).astype(np.float32)
    large = exact + (np.log(nf / np.float32(exact)) / np.float32(math.log(REL_MAX_DIST / exact))
                     * np.float32(REL_BUCKETS - exact)).astype(np.int32)
    return np.where(n < exact, n, np.minimum(large, REL_BUCKETS - 1)).astype(np.int32)


def _bias_lookup(bkt, rb_ref, head, shift=None):
    last = rb_ref[REL_BUCKETS - 1, head]
    acc = jnp.full(bkt.shape, last, F32)
    for b in range(REL_BUCKETS - 1):
        acc = jnp.where(bkt == b, rb_ref[b, head], acc)
    if shift:
        acc = acc - last
    return acc


def _cast_pad_body(w_ref, o_ref):
    cols = w_ref.shape[1]
    cols_pad = o_ref.shape[2]
    full = (cols // LANE) * LANE
    o_ref[0, :, 0:full] = w_ref[:, 0:full].astype(BF16)
    if full < cols:
        tail = w_ref[:, full:cols].astype(BF16)
        o_ref[0, :, full:full + LANE] = jnp.concatenate(
            [tail, jnp.zeros((tail.shape[0], full + LANE - cols), BF16)], axis=1)
        full += LANE
    if full < cols_pad:
        o_ref[0, :, full:cols_pad] = jnp.zeros((o_ref.shape[1], cols_pad - full), BF16)


def _cast_pad(w, n_parts, cols_pad, row_block):
    rows, width = w.shape
    cols = width // n_parts
    return pl.pallas_call(
        _cast_pad_body,
        grid=(n_parts, rows // row_block),
        in_specs=[pl.BlockSpec((row_block, cols), lambda p, r: (r, p))],
        out_specs=pl.BlockSpec((1, row_block, cols_pad), lambda p, r: (p, r, 0)),
        out_shape=jax.ShapeDtypeStruct((n_parts, rows, cols_pad), BF16),
        compiler_params=_cparams(("arbitrary", "arbitrary")),
        name="cast_pad",
    )(w)


def _cast_rows_body(w_ref, o_ref, *, n_src):
    i = pl.program_id(0)
    o_ref[...] = jnp.where(i < n_src, w_ref[...].astype(BF16), jnp.zeros(o_ref.shape, BF16))


def _cast_pad_rows(w, rows_pad, row_block):
    rows, cols = w.shape
    n_src = rows // row_block
    assert rows == n_src * row_block and row_block % 16 == 0 and rows_pad - rows <= row_block
    return pl.pallas_call(
        functools.partial(_cast_rows_body, n_src=n_src),
        grid=(n_src + 1,),
        in_specs=[pl.BlockSpec((row_block, cols), lambda i: (jnp.minimum(i, n_src - 1), 0))],
        out_specs=pl.BlockSpec((row_block, cols), lambda i: (i, 0)),
        out_shape=jax.ShapeDtypeStruct((rows_pad, cols), BF16),
        compiler_params=_cparams(("arbitrary",)),
        name="cast_pad_rows",
    )(w)


def _cast_t_body(w_ref, tail_ref, o_ref, *, n_src):
    i = pl.program_id(0)
    x = jnp.where(i < n_src, w_ref[...], tail_ref[...])
    o_ref[...] = x.T.astype(BF16)


def _cast_transposed(wt, cols_pad):
    cols, rows = wt.shape
    n_src = cols // LANE
    assert cols_pad == (n_src + 1) * LANE and cols > n_src * LANE
    tail = jnp.pad(wt[n_src * LANE:], ((0, cols_pad - cols), (0, 0)))
    return pl.pallas_call(
        functools.partial(_cast_t_body, n_src=n_src),
        grid=(n_src + 1,),
        in_specs=[pl.BlockSpec((LANE, rows), lambda i: (jnp.minimum(i, n_src - 1), 0)),
                  pl.BlockSpec((LANE, rows), lambda i: (0, 0))],
        out_specs=pl.BlockSpec((rows, LANE), lambda i: (0, i)),
        out_shape=jax.ShapeDtypeStruct((rows, cols_pad), BF16),
        compiler_params=_cparams(("arbitrary",)),
        name="cast_transposed",
    )(wt, tail)


def _prep_body(lre_ref, lim_ref, ldt_ref, bre_ref, bim_ref, pos_ref, w1_ref,
               abre_ref, abim_ref, bbre_ref, bbim_ref, cpos_ref):
    lr = lre_ref[...]
    li = lim_ref[...]
    dt = jnp.exp(ldt_ref[...])
    mag = jnp.exp(lr * dt)
    ab_re = mag * jnp.cos(li * dt)
    ab_im = mag * jnp.sin(li * dt)
    den = lr * lr + li * li
    nr = ab_re - 1.0
    ni = ab_im
    f_re = (nr * lr + ni * li) / den
    f_im = (ni * lr - nr * li) / den
    abre_ref[...] = ab_re
    abim_ref[...] = ab_im
    for p in range(SSM_P):
        br = bre_ref[p]
        bi = bim_ref[p]
        bbre_ref[p] = f_re * br - f_im * bi
        bbim_ref[p] = f_re * bi + f_im * br
    for kind in range(2):
        cpos_ref[kind] = jnp.dot(pos_ref[kind], w1_ref[kind], preferred_element_type=F32,
                                 precision=lax.Precision.HIGHEST)


def _prep(lam_re, lam_im, log_dt, b_re, b_im, pos_cmp, w_cmp1):
    bre_t = jnp.transpose(b_re, (2, 0, 1))
    bim_t = jnp.transpose(b_im, (2, 0, 1))
    pos = jnp.zeros((2, 8, CMP_LEN * HEAD_DIM), F32).at[:, 0, :].set(pos_cmp.reshape(2, CMP_LEN * HEAD_DIM))
    w1 = w_cmp1.reshape(2, CMP_LEN * HEAD_DIM, HEAD_DIM)
    return pl.pallas_call(
        _prep_body,
        out_shape=[jax.ShapeDtypeStruct((SSM_G, SSM_N), F32), jax.ShapeDtypeStruct((SSM_G, SSM_N), F32),
                   jax.ShapeDtypeStruct((SSM_P, SSM_G, SSM_N), F32), jax.ShapeDtypeStruct((SSM_P, SSM_G, SSM_N), F32),
                   jax.ShapeDtypeStruct((2, 8, HEAD_DIM), F32)],
        compiler_params=pltpu.CompilerParams(vmem_limit_bytes=VMEM_LIMIT),
        name="prep",
    )(lam_re, lam_im, log_dt.reshape(SSM_G, 1), bre_t, bim_t, pos, w1)


IN_COLS_PAD = ATTN_W + 4 * KV_W + 2 * KV_W + SSM_W + LANE
KT_KS = (0, 2 * HEAD_DIM)
KT_E = HEAD_DIM
KT_KW = 3 * HEAD_DIM
KT_COLS = 5 * HEAD_DIM
VT_ROWS = 4 * HEAD_DIM


def _inproj_body(x_ref, gmix_ref, w_ref, gq_ref, gks_ref, gkw_ref,
                 q_ref, kv_ref, win_ref, u_ref, gt_ref, *extra, q_scale):
    tm = x_ref.shape[0]
    xn = _rms(x_ref[...], gmix_ref[...]).astype(BF16)
    zq = _dot(xn, w_ref[:, 0:ATTN_W])
    for h in range(N_HEADS):
        sl = slice(h * HEAD_DIM, (h + 1) * HEAD_DIM)
        q_ref[:, sl] = (_rms(zq[:, sl], gq_ref[...]) * q_scale).astype(BF16)
    zkv = _dot(xn, w_ref[:, ATTN_W:ATTN_W + 4 * KV_W])
    ks, kw = [], []
    for s in range(4 * N_KV):
        col = zkv[:, s * HEAD_DIM:(s + 1) * HEAD_DIM]
        if s // N_KV == 2:
            col = _rms(col, gks_ref[...])
            ks.append(col)
        kv_ref[:, s, :] = col
    c0 = ATTN_W + 4 * KV_W
    zw = _dot(xn, w_ref[:, c0:c0 + 2 * KV_W])
    for s in range(2 * N_KV):
        col = zw[:, s * HEAD_DIM:(s + 1) * HEAD_DIM]
        if s // N_KV == 0:
            col = _rms(col, gkw_ref[...])
            kw.append(col)
        win_ref[:, s, :] = col
    if extra:
        cmp_ref, kt_ref, vt_ref = extra
        cmp_ref[...] = zkv[:, 0:2 * KV_W]
        for h in range(N_KV):
            kt_ref[:, KT_KS[h]:KT_KS[h] + HEAD_DIM] = ks[h].astype(BF16)
            kt_ref[:, KT_KW + h * HEAD_DIM:KT_KW + (h + 1) * HEAD_DIM] = kw[h].astype(BF16)
        blk = (lax.broadcasted_iota(I32, (tm, HEAD_DIM), 0) + pl.program_id(0) * tm) // SEL_BLOCK
        kt_ref[:, KT_E:KT_E + HEAD_DIM] = jnp.where(blk == lax.broadcasted_iota(I32, (tm, HEAD_DIM), 1), 1.0, 0.0).astype(BF16)
        vt_ref[0:KV_W, :] = zkv[:, 3 * KV_W:4 * KV_W].T.astype(BF16)
        vt_ref[KV_W:2 * KV_W, :] = zw[:, KV_W:2 * KV_W].T.astype(BF16)
    c1 = c0 + 2 * KV_W
    zt = _dot(xn, w_ref[:, c1:IN_COLS_PAD])
    gt_ref[...] = jax.nn.sigmoid(zt[:, 0:LANE])
    u_ref[...] = zt[:, N_BRANCH * N_HEADS:N_BRANCH * N_HEADS + SSM_W]


def _seg_spec(tm, rows, n_seg):
    tiles_per_seg = rows // n_seg // tm
    return pl.BlockSpec((tm, SSM_W), lambda i: (i % tiles_per_seg, i // tiles_per_seg))


def _inproj(x, g_mix, w, g_q, g_ks, g_kw, tm, q_scale, n_seg=1, attn_operands=False):
    rows = x.shape[0]
    row_spec = lambda n: pl.BlockSpec((tm, n), lambda i: (i, 0))
    full = lambda a: pl.BlockSpec(a.shape, lambda i: (0,) * a.ndim)
    out_specs = [row_spec(ATTN_W), pl.BlockSpec((tm, 4 * N_KV, HEAD_DIM), lambda i: (i, 0, 0)),
                 pl.BlockSpec((tm, 2 * N_KV, HEAD_DIM), lambda i: (i, 0, 0)), _seg_spec(tm, rows, n_seg), row_spec(LANE)]
    out_shape = [jax.ShapeDtypeStruct((rows, ATTN_W), BF16), jax.ShapeDtypeStruct((rows, 4 * N_KV, HEAD_DIM), F32),
                 jax.ShapeDtypeStruct((rows, 2 * N_KV, HEAD_DIM), F32),
                 jax.ShapeDtypeStruct((rows // n_seg, n_seg * SSM_W), F32), jax.ShapeDtypeStruct((rows, LANE), F32)]
    if attn_operands:
        out_specs += [row_spec(2 * KV_W), row_spec(KT_COLS), pl.BlockSpec((VT_ROWS, tm), lambda i: (0, i))]
        out_shape += [jax.ShapeDtypeStruct((rows, 2 * KV_W), F32), jax.ShapeDtypeStruct((rows, KT_COLS), BF16),
                      jax.ShapeDtypeStruct((VT_ROWS, rows), BF16)]
    return pl.pallas_call(
        functools.partial(_inproj_body, q_scale=q_scale),
        grid=(rows // tm,),
        in_specs=[row_spec(D_MODEL), full(g_mix), full(w), full(g_q), full(g_ks), full(g_kw)],
        out_specs=out_specs,
        out_shape=out_shape,
        compiler_params=_cparams(("arbitrary",)),
        name="inproj",
    )(x, g_mix, w, g_q, g_ks, g_kw)


PERM_ROWS = 2 * PAGE


def _perm_matrix():
    k = np.arange(PERM_ROWS // CMP_STRIDE)
    p = np.arange(CMP_STRIDE)
    m = np.zeros((PERM_ROWS, PERM_ROWS), np.float32)
    m[(p[:, None] * len(k) + k[None, :]).ravel(), (CMP_STRIDE * k[None, :] + p[:, None]).ravel()] = 1.0
    return jnp.asarray(m, dtype=BF16)


def _compress_pair(x_refs, perm_ref, xp_s, wcat2_ref, cpos, w2):
    n_rows = x_refs[0].shape[0]
    n_grp = n_rows // PERM_ROWS
    n_ch = n_rows // CMP_STRIDE
    ck = PERM_ROWS // CMP_STRIDE

    def perm(g, carry):
        rows = pl.ds(pl.multiple_of(g * PERM_ROWS, PERM_ROWS), PERM_ROWS)
        xcat = jnp.concatenate([x_refs[0][rows, :], x_refs[1][rows, :]], axis=1).astype(BF16)
        y = _dot(perm_ref[...], xcat).astype(BF16)
        for hd in range(N_KV):
            chunks = pl.ds(pl.multiple_of(hd * n_ch + g * ck, ck), ck)
            for p in range(CMP_STRIDE):
                xp_s[p, chunks, :] = y[p * ck:(p + 1) * ck, hd * HEAD_DIM:(hd + 1) * HEAD_DIM]
        return carry

    lax.fori_loop(0, n_grp, perm, 0, unroll=8)
    acc = jnp.zeros((N_KV * n_ch, 2 * HEAD_DIM), F32)
    for q in range(CMP_STRIDE // 2):
        acc = acc + _dot(jnp.concatenate([xp_s[2 * q], xp_s[2 * q + 1]], axis=-1), wcat2_ref[q])
    outs = []
    for hd in range(N_KV):
        a = acc[hd * n_ch:(hd + 1) * n_ch]
        e_hi_next = pltpu.roll(a[:, HEAD_DIM:2 * HEAD_DIM], n_ch - 1, 0)
        hid = _gelu(a[:, 0:HEAD_DIM] + e_hi_next + cpos)
        outs.append(_dot(hid.astype(BF16), w2))
    return outs


def _pcompress_body(x0_ref, x1_ref, perm_ref, wcat2_ref, cpos_ref, w2_ref, gkc_ref, o_ref, xp_s):
    kind = pl.program_id(0)
    outs = _compress_pair((x0_ref, x1_ref), perm_ref, xp_s, wcat2_ref.at[0], cpos_ref[0, 0:1, :], w2_ref[0])
    for hd in range(N_KV):
        o_ref[:, hd * HEAD_DIM:(hd + 1) * HEAD_DIM] = jnp.where(kind == 0, _rms(outs[hd], gkc_ref[...]), outs[hd])


def _pcompress(cmp_rows, wcat2, cpos, w2, g_kc):
    rows = cmp_rows.shape[0]
    n_ch = rows // CMP_STRIDE
    perm = _perm_matrix()
    return pl.pallas_call(
        _pcompress_body,
        grid=(2,),
        in_specs=[pl.BlockSpec((rows, HEAD_DIM), lambda kd: (0, 2 * kd)), pl.BlockSpec((rows, HEAD_DIM), lambda kd: (0, 2 * kd + 1)),
                  pl.BlockSpec(perm.shape, lambda kd: (0, 0)),
                  pl.BlockSpec((1,) + wcat2.shape[1:], lambda kd: (kd, 0, 0, 0)),
                  pl.BlockSpec((1, 8, HEAD_DIM), lambda kd: (kd, 0, 0)),
                  pl.BlockSpec((1, HEAD_DIM, HEAD_DIM), lambda kd: (kd, 0, 0)),
                  pl.BlockSpec((1, HEAD_DIM), lambda kd: (0, 0))],
        out_specs=pl.BlockSpec((n_ch, KV_W), lambda kd: (0, kd)),
        out_shape=jax.ShapeDtypeStruct((n_ch, 2 * KV_W), F32),
        scratch_shapes=[pltpu.VMEM((CMP_STRIDE, N_KV * n_ch, HEAD_DIM), BF16)],
        compiler_params=_cparams(("arbitrary",)),
        name="pcompress",
    )(cmp_rows, cmp_rows, perm, wcat2, cpos, w2, g_kc)


def _select_rounds(score_t, on_pick=None):
    n_j = score_t.shape[0]
    jio = lax.broadcasted_iota(I32, score_t.shape, 0)
    sc = score_t
    for r in range(N_SEL):
        m = jnp.max(sc, axis=0, keepdims=True)
        idx = jnp.min(jnp.where(sc == m, jio, n_j), axis=0, keepdims=True)
        sc = jnp.where(jio == idx, -jnp.inf, sc)
        if on_pick is not None:
            on_pick(r, idx, m >= 0.0)
    return (sc == -jnp.inf) & (score_t >= 0.0)


ONES_ROWS = 16
FAR_TILES = 8
CMP_BAND = 2 * CMP_STRIDE
CMP_PER_QB = QB // CMP_STRIDE
SEL_PER_QB = QB // SEL_BLOCK


def _col_max(tiles):
    m = jnp.max(tiles[0], axis=0, keepdims=True)
    for s in tiles[1:]:
        m = jnp.maximum(m, jnp.max(s, axis=0, keepdims=True))
    return m


def _lanes4(x):
    return jnp.concatenate([x] * Q_PER_KV, axis=1)


def _pattn_t_body(q_ref, gt_ref, kt_ref, vt_ref, kc_ref, vc_ref, vct_ref, rb_ref, bk0_ref, bk1_ref, bkc_ref, mft_ref,
                  gout_ref, o_ref, b0_s, b1_s, bc_s, bw4_s):
    i = pl.program_id(0)
    cols = Q_PER_KV * QB

    @pl.when(i == 0)
    def _tables():
        b_io = lax.broadcasted_iota(I32, (QB, QB), 0)
        a_io = lax.broadcasted_iota(I32, (QB, QB), 1)
        for hd in range(N_HEADS):
            t0 = _bias_lookup(bk0_ref[...], rb_ref, hd, shift=True)
            b0_s[hd] = jnp.where(a_io >= b_io, t0, NEG)
            b1_s[hd] = _bias_lookup(bk1_ref[...], rb_ref, hd, shift=True)
            bc_s[hd] = _bias_lookup(bkc_ref[...], rb_ref, hd, shift=True)
        bw4_s[...] = jnp.where(b_io >= a_io, 0.0, NEG)

    gt_t = gt_ref[...].T
    n_far = jnp.maximum(i - 1, 0) // FAR_TILES
    far_keys = FAR_TILES * QB

    def tab(ref, h):
        return jnp.concatenate([ref[Q_PER_KV * h + g] for g in range(Q_PER_KV)], axis=1)

    def v_aug(row0, start, n):
        return jnp.concatenate([vt_ref[row0:row0 + HEAD_DIM, pl.ds(start, n)], jnp.ones((ONES_ROWS, n), BF16)], axis=0)

    q_ts, qa_ts, o_cs, sel_st = [], [], [], []
    for h in range(N_KV):
        q_t = jnp.concatenate(
            [q_ref[:, (Q_PER_KV * h + g) * HEAD_DIM:(Q_PER_KV * h + g + 1) * HEAD_DIM].astype(F32).T.astype(BF16)
             for g in range(Q_PER_KV)], axis=1)
        hs = slice(h * HEAD_DIM, (h + 1) * HEAD_DIM)

        n_c = kc_ref.shape[0] - CMP_BAND
        blk0 = CMP_PER_QB * i
        near0 = pl.multiple_of(blk0, CMP_PER_QB)
        cf = lax.broadcasted_iota(I32, (n_c, QB), 0)
        ok_f = _lanes4((cf < blk0) & (cf >= CMP_STRIDE))
        s_f = jnp.where(ok_f, _dot(kc_ref[0:n_c, hs].astype(BF16), q_t), NEG)
        cn = lax.broadcasted_iota(I32, (CMP_BAND, cols), 0)
        a_n = lax.broadcasted_iota(I32, (CMP_BAND, cols), 1) % QB
        ok_n = (CMP_STRIDE * (cn - CMP_STRIDE) <= a_n - (CMP_LEN - 1)) & (cn + blk0 >= CMP_STRIDE)
        s_n = jnp.where(ok_n, _dot(kc_ref[pl.ds(near0, CMP_BAND), hs].astype(BF16), q_t) + tab(bc_s, h), NEG)
        m_c = _col_max([s_f, s_n])
        p_f = jnp.where(ok_f, jnp.exp(s_f - m_c), 0.0)
        p_n = jnp.where(ok_n, jnp.exp(s_n - m_c), 0.0)
        l_c = jnp.sum(p_f, axis=0, keepdims=True) + jnp.sum(p_n, axis=0, keepdims=True)
        inv_c = 1.0 / jnp.maximum(l_c, 1e-30)
        vc_near_t = vc_ref[pl.ds(near0, CMP_BAND), hs].T.astype(BF16)
        o_c = (_dot(vct_ref[hs, 0:n_c].astype(BF16), p_f.astype(BF16)) + _dot(vc_near_t, p_n.astype(BF16))) * inv_c
        pn_f = p_f * inv_c
        pn_n = p_n * inv_c
        imp_f = pn_f[:, 0:QB] + pn_f[:, QB:2 * QB] + pn_f[:, 2 * QB:3 * QB] + pn_f[:, 3 * QB:4 * QB]
        imp_n = pn_n[:, 0:QB] + pn_n[:, QB:2 * QB] + pn_n[:, 2 * QB:3 * QB] + pn_n[:, 3 * QB:4 * QB]
        jn = lax.broadcasted_iota(I32, (QB, CMP_BAND), 0)
        cc = lax.broadcasted_iota(I32, (QB, CMP_BAND), 1) + blk0 - CMP_STRIDE
        mnt = (((cc // CMP_PER_SEL) == jn) | (cc == CMP_PER_SEL * jn - 1)) & (cc >= 0)
        mnt = jnp.where(mnt, 1.0, 0.0).astype(BF16)
        imp_t = jnp.zeros((QB, QB), F32)
        for part in _split3(imp_f):
            imp_t = imp_t + _dot(mft_ref[...], part)
        for part in _split3(imp_n):
            imp_t = imp_t + _dot(mnt, part)
        j_io = lax.broadcasted_iota(I32, (QB, QB), 0)
        cur = SEL_PER_QB * i + lax.broadcasted_iota(I32, (QB, QB), 1) // SEL_BLOCK
        forced = (j_io == 0) | (j_io == cur) | (j_io == cur - 1)
        score_t = jnp.where(forced, 1e9, jnp.where(j_io <= cur, imp_t, -1.0))
        sel_t = _select_rounds(score_t)
        selneg = _lanes4(jnp.where(sel_t, 0.0, NEG).astype(BF16))
        qa_t = jnp.concatenate([q_t, selneg] if h == 0 else [selneg, q_t], axis=0)
        ka0 = h * HEAD_DIM

        near_s, near_v = [], []
        for back in range(FAR_TILES + 1):
            kt = i - back
            start = pl.multiple_of(jnp.maximum(kt, 0) * QB, QB)
            s = _dot(kt_ref[pl.ds(start, QB), ka0:ka0 + 2 * HEAD_DIM], qa_t)
            if back == 0:
                s = s + tab(b0_s, h)
            elif back == 1:
                s = s + tab(b1_s, h) + jnp.where(kt >= 0, 0.0, NEG)
            else:
                s = s + jnp.where((kt >= 0) & (kt >= FAR_TILES * n_far), 0.0, NEG)
            near_s.append(s)
            near_v.append(v_aug(h * HEAD_DIM, start, QB))
        m_s = _col_max(near_s)
        acc_s = jnp.zeros((HEAD_DIM + ONES_ROWS, cols), F32)
        for s, v in zip(near_s, near_v):
            acc_s = acc_s + _dot(v, jnp.exp(s - m_s).astype(BF16))
        q_ts.append(q_t)
        qa_ts.append(qa_t)
        o_cs.append(o_c)
        sel_st += [m_s, acc_s]

    def far_step(k, st):
        start = pl.multiple_of(k * far_keys, far_keys)
        out = []
        for h in range(N_KV):
            m, acc = st[2 * h], st[2 * h + 1]
            s = _dot(kt_ref[pl.ds(start, far_keys), h * HEAD_DIM:(h + 2) * HEAD_DIM], qa_ts[h])
            m_new = jnp.maximum(m, jnp.max(s, axis=0, keepdims=True))
            p = jnp.exp(s - m_new).astype(BF16)
            out += [m_new, jnp.exp(m - m_new) * acc + _dot(v_aug(h * HEAD_DIM, start, far_keys), p)]
        return tuple(out)

    sel_st = lax.fori_loop(0, n_far, far_step, tuple(sel_st))

    attn = [None] * N_HEADS
    for h in range(N_KV):
        q_t = q_ts[h]
        acc_s = sel_st[2 * h + 1]
        o_s = acc_s[0:HEAD_DIM] / acc_s[HEAD_DIM:HEAD_DIM + 1]
        o_c = o_cs[h]

        kw0 = KT_KW + h * HEAD_DIM
        win_s, win_v = [], []
        for back in range(5):
            kt = i - back
            start = pl.multiple_of(jnp.maximum(kt, 0) * QB, QB)
            s = _dot(kt_ref[pl.ds(start, QB), kw0:kw0 + HEAD_DIM], q_t)
            if back == 0:
                s = s + tab(b0_s, h)
            elif back == 1:
                s = s + tab(b1_s, h)
            elif back == 4:
                s = s + _lanes4(bw4_s[...])
            if back > 0:
                s = s + jnp.where(kt >= 0, 0.0, NEG)
            win_s.append(s)
            win_v.append(v_aug((N_KV + h) * HEAD_DIM, start, QB))
        m_w = _col_max(win_s)
        acc_w = jnp.zeros((HEAD_DIM + ONES_ROWS, cols), F32)
        for s, v in zip(win_s, win_v):
            acc_w = acc_w + _dot(v, jnp.exp(s - m_w).astype(BF16))
        o_w = acc_w[0:HEAD_DIM] / acc_w[HEAD_DIM:HEAD_DIM + 1]

        for g in range(Q_PER_KV):
            hd = Q_PER_KV * h + g
            cs = slice(g * QB, (g + 1) * QB)
            o_t = (gt_t[hd:hd + 1] * o_c[:, cs] + gt_t[N_HEADS + hd:N_HEADS + hd + 1] * o_s[:, cs]
                   + gt_t[2 * N_HEADS + hd:2 * N_HEADS + hd + 1] * o_w[:, cs])
            attn[hd] = o_t.T
    a = jnp.concatenate(attn, axis=1)
    o_ref[...] = _rms(a, gout_ref[...]).astype(BF16)


def _pattn_t_tables():
    b = np.arange(QB)[:, None]
    a = np.arange(QB)[None, :]
    bk0 = _bucket_np(a - b)
    bk1 = _bucket_np(a - b + QB)
    c = np.arange(CMP_BAND)[:, None] - CMP_STRIDE
    bkc = _bucket_np(a - CMP_STRIDE * c - (CMP_LEN - 1))
    cidx = np.arange(4 * QB)[None, :] - CMP_STRIDE
    j = np.arange(QB)[:, None]
    mft = (((cidx // CMP_PER_SEL) == j) | (cidx == CMP_PER_SEL * j - 1)) & (cidx >= 0)
    return (jnp.asarray(bk0), jnp.asarray(bk1), jnp.asarray(bkc), jnp.asarray(mft.astype(np.float32), dtype=BF16))


def _pattn_t(q, gates, kt, vt, kcp, vcp, vcpt, rel_bias, g_out):
    rows = q.shape[0]
    bk0, bk1, bkc, mft = _pattn_t_tables()
    full = lambda a: pl.BlockSpec(a.shape, lambda i: (0,) * a.ndim)
    once = lambda a: pl.BlockSpec(a.shape, lambda i: (0,) * a.ndim, pipeline_mode=pl.Buffered(1))
    return pl.pallas_call(
        _pattn_t_body,
        grid=(rows // QB,),
        in_specs=[pl.BlockSpec((QB, ATTN_W), lambda i: (i, 0)), pl.BlockSpec((QB, LANE), lambda i: (i, 0)),
                  once(kt), once(vt), full(kcp), full(vcp), full(vcpt), pl.BlockSpec(memory_space=pltpu.SMEM),
                  full(bk0), full(bk1), full(bkc), full(mft), full(g_out)],
        out_specs=pl.BlockSpec((QB, ATTN_W), lambda i: (i, 0)),
        out_shape=jax.ShapeDtypeStruct((rows, ATTN_W), BF16),
        scratch_shapes=[pltpu.VMEM((N_HEADS, QB, QB), F32), pltpu.VMEM((N_HEADS, QB, QB), F32),
                        pltpu.VMEM((N_HEADS, CMP_BAND, QB), F32), pltpu.VMEM((QB, QB), F32)],
        compiler_params=_cparams(("arbitrary",)),
        name="pattn",
    )(q, gates, kt, vt, kcp, vcp, vcpt, rel_bias, bk0, bk1, bkc, mft, g_out)


def _ssm_body(u_ref, ire_ref, iim_ref, wb_ref, wc_ref, abre_ref, abim_ref, dsk_ref, wglu_ref, gout_ref,
              y_ref, fre_ref, fim_ref, u_s, xre_s, xim_s, y_s, sre_s, sim_s, *, n_seg, tc, emit):
    c = pl.program_id(0)
    sgl = SSM_LANES // SSM_SG

    @pl.when(c == 0)
    def _init():
        sre_s[...] = ire_ref[...]
        sim_s[...] = iim_ref[...]

    n_lt = SSM_W // LANE
    if tc == 1:
        u_cols = [u_ref[:, l * LANE:(l + 1) * LANE] for l in range(n_lt)]
    else:
        for s in range(n_seg):
            for l in range(n_lt):
                u_s[l, pl.ds(s, tc, stride=n_seg), :] = u_ref[:, s * SSM_W + l * LANE:s * SSM_W + (l + 1) * LANE]
        u_cols = [u_s[l] for l in range(n_lt)]
    for sg in range(SSM_SG):
        ls = slice(sg * sgl, (sg + 1) * sgl)
        bu = _dot(u_cols[sg].astype(BF16), wb_ref[sg])
        xre_s[:, ls] = bu[:, 0:sgl]
        xim_s[:, ls] = bu[:, sgl:2 * sgl]
        ar = abre_ref[:, ls]
        ai = abim_ref[:, ls]

        def step(t, carry):
            xr, xi = carry
            rows = pl.ds(pl.multiple_of(t * n_seg, n_seg), n_seg)
            nr = ar * xr - ai * xi + xre_s[rows, ls]
            ni = ar * xi + ai * xr + xim_s[rows, ls]
            xre_s[rows, ls] = nr
            xim_s[rows, ls] = ni
            return nr, ni

        xr, xi = lax.fori_loop(0, tc, step, (sre_s[:, ls], sim_s[:, ls]), unroll=min(tc, 4))
        sre_s[:, ls] = xr
        sim_s[:, ls] = xi
    fre_ref[...] = sre_s[...]
    fim_ref[...] = sim_s[...]
    if not emit:
        y_ref[...] = jnp.zeros(y_ref.shape, y_ref.dtype)
        return
    ys = []
    for sg in range(SSM_SG):
        ls = slice(sg * sgl, (sg + 1) * sgl)
        x2 = jnp.concatenate([xre_s[:, ls], xim_s[:, ls]], axis=1).astype(BF16)
        ys.append(_dot(x2, wc_ref[sg]))
    y = jnp.concatenate(ys, axis=1) + dsk_ref[...] * jnp.concatenate(u_cols, axis=1)
    z = _dot(_gelu(y).astype(BF16), wglu_ref[...])
    o = z[:, 0:SSM_W] * jax.nn.sigmoid(z[:, SSM_W:2 * SSM_W])
    yn = _rms(o, gout_ref[...])
    if tc == 1:
        y_ref[...] = yn.astype(BF16)
    else:
        for l in range(n_lt):
            y_s[l] = yn[:, l * LANE:(l + 1) * LANE]
        for s in range(n_seg):
            for l in range(n_lt):
                y_ref[:, s * SSM_W + l * LANE:s * SSM_W + (l + 1) * LANE] = (
                    y_s[l, pl.ds(s, tc, stride=n_seg), :].astype(BF16))


def _ssm(u2, n_seg, init_re, init_im, wb, wc, ab_re, ab_im, d_skip, w_glu, g_out, tc, emit):
    if tc == 1:
        assert u2.shape == (n_seg, SSM_W)
        t_len = 1
        blk = (n_seg, SSM_W)
    else:
        t_len = u2.shape[0]
        assert u2.shape[1] == n_seg * SSM_W and n_seg % 8 == 0
        blk = (tc, n_seg * SSM_W)
    rows = tc * n_seg
    full = lambda a: pl.BlockSpec(a.shape, lambda c: (0,) * a.ndim)
    body = functools.partial(_ssm_body, n_seg=n_seg, tc=tc, emit=emit)
    st = jax.ShapeDtypeStruct((n_seg, SSM_LANES), F32)
    y_shape = u2.shape if emit else blk
    y_map = (lambda c: (c, 0)) if emit else (lambda c: (0, 0))
    return pl.pallas_call(
        body,
        grid=(t_len // tc,),
        in_specs=[pl.BlockSpec(blk, lambda c: (c, 0)),
                  full(init_re), full(init_im), full(wb), full(wc),
                  full(ab_re), full(ab_im), full(d_skip), full(w_glu), full(g_out)],
        out_specs=[pl.BlockSpec(blk, y_map), full(init_re), full(init_im)],
        out_shape=[jax.ShapeDtypeStruct(y_shape, BF16), st, st],
        scratch_shapes=[pltpu.VMEM((SSM_W // LANE, rows, LANE), F32), pltpu.VMEM((rows, SSM_LANES), F32),
                        pltpu.VMEM((rows, SSM_LANES), F32), pltpu.VMEM((SSM_W // LANE, rows, LANE), F32),
                        pltpu.VMEM((n_seg, SSM_LANES), F32), pltpu.VMEM((n_seg, SSM_LANES), F32)],
        compiler_params=_cparams(("arbitrary",)),
        name="ssm_emit" if emit else "ssm_final",
    )(u2, init_re, init_im, wb, wc, ab_re, ab_im, d_skip, w_glu, g_out)


def _ssm_chain_body(fre_ref, fim_ref, abre_ref, abim_ref, ire_ref, iim_ref, *, n_seg, log2_len):
    pr = abre_ref[...]
    pi = abim_ref[...]
    for _ in range(log2_len):
        pr, pi = pr * pr - pi * pi, 2.0 * pr * pi
    cr = jnp.zeros((1, SSM_LANES), F32)
    ci = jnp.zeros((1, SSM_LANES), F32)
    for j in range(n_seg):
        ire_ref[j:j + 1, :] = cr
        iim_ref[j:j + 1, :] = ci
        fr = fre_ref[j:j + 1, :]
        fi = fim_ref[j:j + 1, :]
        cr, ci = fr + pr * cr - pi * ci, fi + pr * ci + pi * cr


def _ssm_chain(fre, fim, ab_re, ab_im, seg_len):
    n_seg = fre.shape[0]
    log2_len = int(math.log2(seg_len))
    assert 2 ** log2_len == seg_len
    st = jax.ShapeDtypeStruct((n_seg, SSM_LANES), F32)
    return pl.pallas_call(
        functools.partial(_ssm_chain_body, n_seg=n_seg, log2_len=log2_len),
        out_shape=[st, st],
        compiler_params=pltpu.CompilerParams(vmem_limit_bytes=VMEM_LIMIT),
        name="ssm_chain",
    )(fre, fim, ab_re, ab_im)


FFN_CHUNKS = 2


def _ffn_body(x_ref, a_ref, s_ref, wo_ref, gffn_ref, wa_ref, wg_ref, wd_ref, cw_ref, cb_ref, p2_ref, p1_ref,
              y_ref, cnew_ref, hn_s, car_s, *, seq, tm):
    r = pl.program_id(0)
    j = pl.program_id(1)

    @pl.when(j == 0)
    def _mix():
        h = x_ref[...] + _dot(a_ref[...], wo_ref[0:ATTN_W, :]) + _dot(s_ref[...], wo_ref[ATTN_W:D_MODEL, :])
        y_ref[...] = h
        hn_s[...] = _rms(h, gffn_ref[...]).astype(BF16)

    if seq:
        @pl.when(r == 0)
        def _first():
            car_s[j, 6:7, :] = p2_ref[...]
            car_s[j, 7:8, :] = p1_ref[...]

    hn = hn_s[...]
    tf = wa_ref.shape[1]
    tc = tf // FFN_CHUNKS
    part = None
    for k in range(FFN_CHUNKS):
        cs = slice(k * tc, (k + 1) * tc)
        a = _dot(hn, wa_ref[:, cs])
        g = _dot(hn, wg_ref[:, cs])
        if seq:
            p2 = car_s[j, 6:7, cs]
            p1 = car_s[j, 7:8, cs]
            row = lax.broadcasted_iota(I32, a.shape, 0)
            a1 = jnp.where(row == 0, p1, pltpu.roll(a, 1, 0))
            a2 = jnp.where(row == 0, p2, jnp.where(row == 1, p1, pltpu.roll(a, 2, 0)))
            car_s[j, :, cs] = a[tm - 8:tm, :]
            cnew_ref[:, pl.ds(pl.multiple_of(j * tf + k * tc, tc), tc)] = a[tm - 8:tm, :]
        else:
            a1 = p1_ref[:, cs]
            a2 = p2_ref[:, cs]
            cnew_ref[:, cs] = a
        c = cb_ref[:, cs] + cw_ref[0:1, cs] * a2 + cw_ref[1:2, cs] * a1 + cw_ref[2:3, cs] * a
        d = _dot((_gelu(c) * g).astype(BF16), wd_ref[cs, :])
        part = d if part is None else part + d
    y_ref[...] += part


def _ffn(x, a_n, s_n, w_out, g_ffn, w_up, wd, cw, cb, p2, p1, tm, tf, seq, n_seg=1):
    rows = x.shape[0]
    nj = D_FF_PAD // tf
    body = functools.partial(_ffn_body, seq=seq, tm=tm)
    tiles_per_seg = rows // n_seg // tm
    if seq:
        tap_spec = pl.BlockSpec((1, tf), lambda r, j: (0, j))
        cnew_spec = pl.BlockSpec((8, D_FF_PAD), lambda r, j: (0, 0))
        cnew_shape = jax.ShapeDtypeStruct((8, D_FF_PAD), F32)
    else:
        tap_spec = pl.BlockSpec((tm, tf), lambda r, j: (r, j))
        cnew_spec = pl.BlockSpec((tm, tf), lambda r, j: (r, j))
        cnew_shape = jax.ShapeDtypeStruct((rows, D_FF_PAD), F32)
    return pl.pallas_call(
        body,
        grid=(rows // tm, nj),
        in_specs=[pl.BlockSpec((tm, D_MODEL), lambda r, j: (r, 0)), pl.BlockSpec((tm, ATTN_W), lambda r, j: (r, 0)),
                  pl.BlockSpec((tm, SSM_W), lambda r, j: (r % tiles_per_seg, r // tiles_per_seg)),
                  pl.BlockSpec(w_out.shape, lambda r, j: (0, 0), pipeline_mode=pl.Buffered(1)),
                  pl.BlockSpec((1, D_MODEL), lambda r, j: (0, 0)),
                  pl.BlockSpec((None, D_MODEL, tf), lambda r, j: (0, 0, j)),
                  pl.BlockSpec((None, D_MODEL, tf), lambda r, j: (1, 0, j)),
                  pl.BlockSpec((tf, D_MODEL), lambda r, j: (j, 0)), pl.BlockSpec((CONV_W, tf), lambda r, j: (0, j)),
                  pl.BlockSpec((1, tf), lambda r, j: (0, j)), tap_spec, tap_spec],
        out_specs=[pl.BlockSpec((tm, D_MODEL), lambda r, j: (r, 0)), cnew_spec],
        out_shape=[jax.ShapeDtypeStruct((rows, D_MODEL), F32), cnew_shape],
        scratch_shapes=[pltpu.VMEM((tm, D_MODEL), BF16), pltpu.VMEM((nj, 8, tf), F32)],
        compiler_params=_cparams(("arbitrary", "arbitrary")),
        name="ffn_seq" if seq else "ffn_rows",
    )(x, a_n, s_n, w_out, g_ffn, w_up, w_up, wd, cw, cb, p2, p1)


def _s1_copies(pt_ref, cache_ref, x_s, sem, b, slot, n_pages):
    cps = []
    for pg in range(n_pages):
        page = pt_ref[b * n_pages + pg]
        for s in range(2 * N_KV):
            cps.append(pltpu.make_async_copy(cache_ref.at[page, :, s, :],
                                             x_s.at[slot, s, pl.ds(pg * PAGE, PAGE), :], sem.at[slot]))
    return cps


def _s1_body(pt_ref, cache_ref, q_ref, perm_ref, wcat2_ref, cpos_ref, w2_ref, gkc_ref, rb_ref, bkc_ref, mt_ref,
             oc_ref, idx_ref, val_ref, x_s, xp_s, bias_s, sem, *, n_pages, past):
    b = pl.program_id(0)
    nb = pl.num_programs(0)
    slot = b % 2

    @pl.when(b == 0)
    def _first():
        for cp in _s1_copies(pt_ref, cache_ref, x_s, sem, 0, 0, n_pages):
            cp.start()
        for hd in range(N_HEADS):
            bias_s[hd:hd + 1, :] = _bias_lookup(bkc_ref[...], rb_ref, hd)

    @pl.when(b + 1 < nb)
    def _next():
        for cp in _s1_copies(pt_ref, cache_ref, x_s, sem, b + 1, 1 - slot, n_pages):
            cp.start()

    for cp in _s1_copies(pt_ref, cache_ref, x_s, sem, b, slot, n_pages):
        cp.wait()

    n_c = past // CMP_STRIDE
    cio = lax.broadcasted_iota(I32, (N_HEADS, n_c), 1)
    hrow = lax.broadcasted_iota(I32, (N_HEADS, n_c), 0) // Q_PER_KV
    q = q_ref[0]
    s_all = jnp.zeros((N_HEADS, n_c), F32)
    kcs = _compress_pair((x_s.at[slot, 0], x_s.at[slot, 1]), perm_ref, xp_s, wcat2_ref.at[0], cpos_ref[0, 0:1, :], w2_ref[0])
    vcs = _compress_pair((x_s.at[slot, N_KV], x_s.at[slot, N_KV + 1]), perm_ref, xp_s, wcat2_ref.at[1], cpos_ref[1, 0:1, :],
                         w2_ref[1])
    vcs = [v.astype(BF16) for v in vcs]
    for h in range(N_KV):
        kc = _rms(kcs[h], gkc_ref[...])
        s_all = jnp.where(hrow == h, _dot_nt(q, kc.astype(BF16)), s_all)
    ok = cio < n_c - 1
    s_all = jnp.where(ok, s_all + bias_s[...], NEG)
    m = jnp.max(s_all, axis=-1, keepdims=True)
    p = jnp.where(ok, jnp.exp(s_all - m), 0.0)
    p = p / jnp.maximum(jnp.sum(p, axis=-1, keepdims=True), 1e-30)
    pb = p.astype(BF16)
    hrow_o = lax.broadcasted_iota(I32, (N_HEADS, HEAD_DIM), 0) // Q_PER_KV
    o_c = jnp.zeros((N_HEADS, HEAD_DIM), F32)
    for h in range(N_KV):
        o_c = jnp.where(hrow_o == h, _dot(pb, vcs[h]), o_c)
    oc_ref[0] = o_c
    rio = lax.broadcasted_iota(I32, (8, n_c), 0)
    imp = jnp.zeros((8, n_c), F32)
    for h in range(N_KV):
        ih = p[4 * h:4 * h + 1] + p[4 * h + 1:4 * h + 2] + p[4 * h + 2:4 * h + 3] + p[4 * h + 3:4 * h + 4]
        imp = jnp.where(rio == h, ih, imp)
    imp = jnp.concatenate([imp, jnp.zeros((LANE - 8, n_c), F32)], axis=0)
    n_j = mt_ref.shape[0]
    imp_t = jnp.zeros((n_j, LANE), F32)
    for part in _split3(imp):
        imp_t = imp_t + _dot_nt(mt_ref[...], part)
    j_io = lax.broadcasted_iota(I32, (n_j, LANE), 0)
    cur = past // SEL_BLOCK
    forced = (j_io == 0) | (j_io == cur) | (j_io == cur - 1)
    score_t = jnp.where(forced, 1e9, jnp.where(j_io <= cur, imp_t, -1.0))
    score_t = jnp.where(j_io <= cur, score_t, -jnp.inf)

    def on_pick(r, idx, okv):
        idx_ref[0, r:r + 1, :] = idx
        val_ref[0, r:r + 1, :] = jnp.where(okv, 1, 0)

    _select_rounds(score_t, on_pick)


def _s1(page_table, cache4, q3, wcat2, cpos, w2, g_kc, rel_bias):
    n_b, n_pages = page_table.shape
    past = n_pages * PAGE
    n_c = past // CMP_STRIDE
    perm = _perm_matrix()
    ns = past // SEL_BLOCK + 1
    n_j = -(-ns // 8) * 8
    c = np.arange(n_c)[None, :]
    bkc = _bucket_np(past - (CMP_STRIDE * c + CMP_LEN - 1))
    j = np.arange(n_j)[:, None]
    mt = (((c // CMP_PER_SEL) == j) | (c == CMP_PER_SEL * j - 1)) & (c < n_c - 1)
    mt = jnp.asarray(mt.astype(np.float32), dtype=BF16)
    full = lambda a: pl.BlockSpec(a.shape, lambda b, pt: (0,) * a.ndim)
    body = functools.partial(_s1_body, n_pages=n_pages, past=past)
    return pl.pallas_call(
        body,
        grid_spec=pltpu.PrefetchScalarGridSpec(
            num_scalar_prefetch=1,
            grid=(n_b,),
            in_specs=[pl.BlockSpec(memory_space=pl.ANY), pl.BlockSpec((1, N_HEADS, HEAD_DIM), lambda b, pt: (b, 0, 0)),
                      full(perm), full(wcat2), full(cpos), full(w2), full(g_kc), pl.BlockSpec(memory_space=pltpu.SMEM),
                      pl.BlockSpec((1, n_c), lambda b, pt: (0, 0)), full(mt)],
            out_specs=[pl.BlockSpec((1, N_HEADS, HEAD_DIM), lambda b, pt: (b, 0, 0)),
                       pl.BlockSpec((1, N_SEL, LANE), lambda b, pt: (b, 0, 0)),
                       pl.BlockSpec((1, N_SEL, LANE), lambda b, pt: (b, 0, 0))],
            scratch_shapes=[pltpu.VMEM((2, 2 * N_KV, past, HEAD_DIM), F32),
                            pltpu.VMEM((CMP_STRIDE, N_KV * n_c, HEAD_DIM), BF16), pltpu.VMEM((N_HEADS, n_c), F32),
                            pltpu.SemaphoreType.DMA((2,))],
        ),
        out_shape=[jax.ShapeDtypeStruct((n_b, N_HEADS, HEAD_DIM), F32), jax.ShapeDtypeStruct((n_b, N_SEL, LANE), I32),
                   jax.ShapeDtypeStruct((n_b, N_SEL, LANE), I32)],
        compiler_params=_cparams(("arbitrary",)),
        name="sample_cmp",
    )(page_table.reshape(-1), cache4, q3, perm, wcat2, cpos, w2, g_kc, rel_bias, jnp.asarray(bkc), mt)


def _s2_copies(idx_ref, pt_ref, cache_ref, cwin_ref, ks_s, vs_s, kw_s, vw_s, sem, b, slot, n_pages):
    cps = []
    n_blk = n_pages * (PAGE // SEL_BLOCK)
    for h in range(N_KV):
        for r in range(N_SEL):
            jb = jnp.minimum(idx_ref[(b * N_SEL + r) * N_KV + h], n_blk - 1)
            page = pt_ref[b * n_pages + jb // 2]
            row0 = pl.multiple_of((jb % 2) * SEL_BLOCK, SEL_BLOCK)
            cps.append(pltpu.make_async_copy(cache_ref.at[page, pl.ds(row0, SEL_BLOCK), 2 * N_KV + h, :],
                                             ks_s.at[slot, h, pl.ds(r * SEL_BLOCK, SEL_BLOCK), :], sem.at[slot]))
            cps.append(pltpu.make_async_copy(cache_ref.at[page, pl.ds(row0, SEL_BLOCK), 3 * N_KV + h, :],
                                             vs_s.at[slot, h, pl.ds(r * SEL_BLOCK, SEL_BLOCK), :], sem.at[slot]))
        cps.append(pltpu.make_async_copy(cwin_ref.at[b, :, h, :], kw_s.at[slot, h], sem.at[slot]))
        cps.append(pltpu.make_async_copy(cwin_ref.at[b, :, N_KV + h, :], vw_s.at[slot, h], sem.at[slot]))
    return cps


def _s2_body(idx_ref, val_ref, pt_ref, cache_ref, cwin_ref, q_ref, oc_ref, gt_ref, kns_ref, vns_ref, knw_ref, vnw_ref,
             rb_ref, bks_ref, bkw_ref, gout_ref, o_ref, ks_s, vs_s, kw_s, vw_s, bs_s, bw_s, sem, *, n_pages, past):
    b = pl.program_id(0)
    nb = pl.num_programs(0)
    slot = b % 2
    args = (idx_ref, pt_ref, cache_ref, cwin_ref, ks_s, vs_s, kw_s, vw_s, sem)

    @pl.when(b == 0)
    def _first():
        for cp in _s2_copies(*args, 0, 0, n_pages):
            cp.start()
        for hd in range(N_HEADS):
            bs_s[hd:hd + 1, :] = _bias_lookup(bks_ref[...], rb_ref, hd)
            bw_s[hd:hd + 1, :] = _bias_lookup(bkw_ref[...], rb_ref, hd)

    @pl.when(b + 1 < nb)
    def _next():
        for cp in _s2_copies(*args, b + 1, 1 - slot, n_pages):
            cp.start()

    for cp in _s2_copies(*args, b, slot, n_pages):
        cp.wait()

    n_blk = n_pages * (PAGE // SEL_BLOCK)
    q = q_ref[0]
    qf = q.astype(F32)
    hrow = lax.broadcasted_iota(I32, (N_HEADS, 1), 0) // Q_PER_KV
    lane = lax.broadcasted_iota(I32, (N_HEADS, LANE), 1)
    bias0 = jnp.concatenate([jnp.full((1, 1), rb_ref[0, hd], F32) for hd in range(N_HEADS)], axis=0)
    b31 = jnp.concatenate([jnp.full((1, 1), rb_ref[REL_BUCKETS - 1, hd], F32) for hd in range(N_HEADS)], axis=0)

    tiles = []
    new_sel = jnp.zeros((N_HEADS, 1), F32)
    for t in range(N_SEL // 2):
        s_t = jnp.zeros((N_HEADS, LANE), F32)
        for h in range(N_KV):
            s_h = _dot_nt(q, ks_s[slot, h, pl.ds(t * LANE, LANE), :].astype(BF16))
            halves = []
            for half in range(2):
                r = 2 * t + half
                jb = idx_ref[(b * N_SEL + r) * N_KV + h]
                okr = (val_ref[(b * N_SEL + r) * N_KV + h] > 0) & (jb < n_blk)
                near = bs_s[:, (half * 2) * LANE:(half * 2 + 1) * LANE]
                nearer = bs_s[:, (half * 2 + 1) * LANE:(half * 2 + 2) * LANE]
                bias = jnp.where(jb == n_blk - 1, nearer, jnp.where(jb == n_blk - 2, near, b31))
                halves.append(jnp.where(okr, s_h + bias, NEG))
                new_sel = jnp.where((hrow == h) & (val_ref[(b * N_SEL + r) * N_KV + h] > 0) & (jb == n_blk), 1.0, new_sel)
            s_h = jnp.where(lane < SEL_BLOCK, halves[0], halves[1])
            s_t = jnp.where(hrow == h, s_h, s_t)
        tiles.append(s_t)
    s_new = jnp.sum(qf * kns_ref[0], axis=-1, keepdims=True) + bias0
    s_new = jnp.where(new_sel > 0.5, s_new, NEG)
    m = s_new
    for s_t in tiles:
        m = jnp.maximum(m, jnp.max(s_t, axis=-1, keepdims=True))
    p_new = jnp.where(new_sel > 0.5, jnp.exp(s_new - m), 0.0)
    l = p_new
    acc = p_new * vns_ref[0]
    for t, s_t in enumerate(tiles):
        p = jnp.where(s_t > 0.5 * NEG, jnp.exp(s_t - m), 0.0)
        l = l + jnp.sum(p, axis=-1, keepdims=True)
        pb = p.astype(BF16)
        for h in range(N_KV):
            pv = _dot(pb, vs_s[slot, h, pl.ds(t * LANE, LANE), :].astype(BF16))
            acc = acc + jnp.where(hrow == h, pv, 0.0)
    o_s = acc / jnp.maximum(l, 1e-30)

    n_w = kw_s.shape[2]
    wt = []
    for t in range(n_w // LANE):
        s_t = jnp.zeros((N_HEADS, LANE), F32)
        for h in range(N_KV):
            s_h = _dot_nt(q, kw_s[slot, h, pl.ds(t * LANE, LANE), :].astype(BF16))
            s_t = jnp.where(hrow == h, s_h, s_t)
        wt.append(s_t + bw_s[:, t * LANE:(t + 1) * LANE])
    s_new = jnp.sum(qf * knw_ref[0], axis=-1, keepdims=True) + bias0
    m = s_new
    for s_t in wt:
        m = jnp.maximum(m, jnp.max(s_t, axis=-1, keepdims=True))
    p_new = jnp.exp(s_new - m)
    l = p_new
    acc = p_new * vnw_ref[0]
    for t, s_t in enumerate(wt):
        p = jnp.exp(s_t - m)
        l = l + jnp.sum(p, axis=-1, keepdims=True)
        pb = p.astype(BF16)
        for h in range(N_KV):
            pv = _dot(pb, vw_s[slot, h, pl.ds(t * LANE, LANE), :].astype(BF16))
            acc = acc + jnp.where(hrow == h, pv, 0.0)
    o_w = acc / l

    gt = gt_ref[0]
    a = gt[:, 0:1] * oc_ref[0] + gt[:, 1:2] * o_s + gt[:, 2:3] * o_w
    ms = jnp.sum(jnp.sum(a * a, axis=-1, keepdims=True), axis=0, keepdims=True) / (N_HEADS * HEAD_DIM)
    o_ref[0] = (a * lax.rsqrt(ms + EPS) * gout_ref[...]).astype(BF16)


def _s2(idx, val, page_table, cache4, cwin4, q3, o_c, gates3, kns, vns, knw, vnw, rel_bias, g_out3):
    n_b, n_pages = page_table.shape
    past = n_pages * PAGE
    n_w = cwin4.shape[1]
    s = np.arange(SEL_BLOCK)
    d_near = past - ((past // SEL_BLOCK - 2) * SEL_BLOCK + s)
    d_nearer = past - ((past // SEL_BLOCK - 1) * SEL_BLOCK + s)
    z = np.zeros(SEL_BLOCK, np.int64)
    bks = np.concatenate([d_near, z, d_nearer, z, z, d_near, z, d_nearer])[None, :]
    bkw = (past - (past - n_w + np.arange(n_w)))[None, :]
    full = lambda a: pl.BlockSpec(a.shape, lambda b, *_: (0,) * a.ndim)
    per_b = lambda a: pl.BlockSpec((1,) + a.shape[1:], lambda b, *_: (b,) + (0,) * (a.ndim - 1))
    body = functools.partial(_s2_body, n_pages=n_pages, past=past)
    return pl.pallas_call(
        body,
        grid_spec=pltpu.PrefetchScalarGridSpec(
            num_scalar_prefetch=3,
            grid=(n_b,),
            in_specs=[pl.BlockSpec(memory_space=pl.ANY), pl.BlockSpec(memory_space=pl.ANY),
                      per_b(q3), per_b(o_c), per_b(gates3), per_b(kns), per_b(vns), per_b(knw), per_b(vnw),
                      pl.BlockSpec(memory_space=pltpu.SMEM), pl.BlockSpec((1, 4 * LANE), lambda b, *_: (0, 0)),
                      pl.BlockSpec((1, n_w), lambda b, *_: (0, 0)), full(g_out3)],
            out_specs=pl.BlockSpec((1, N_HEADS, HEAD_DIM), lambda b, *_: (b, 0, 0)),
            scratch_shapes=[pltpu.VMEM((2, N_KV, N_SEL * SEL_BLOCK, HEAD_DIM), F32),
                            pltpu.VMEM((2, N_KV, N_SEL * SEL_BLOCK, HEAD_DIM), F32),
                            pltpu.VMEM((2, N_KV, n_w, HEAD_DIM), F32), pltpu.VMEM((2, N_KV, n_w, HEAD_DIM), F32),
                            pltpu.VMEM((N_HEADS, 4 * LANE), F32), pltpu.VMEM((N_HEADS, n_w), F32),
                            pltpu.SemaphoreType.DMA((2,))],
        ),
        out_shape=jax.ShapeDtypeStruct((n_b, N_HEADS, HEAD_DIM), BF16),
        compiler_params=_cparams(("arbitrary",)),
        name="sample_attn",
    )(idx, val, page_table.reshape(-1), cache4, cwin4, q3, o_c, gates3, kns, vns, knw, vnw, rel_bias,
      jnp.asarray(_bucket_np(bks)), jnp.asarray(_bucket_np(bkw)), g_out3)


def _winshift_body(cwin_ref, new_ref, o_ref):
    n_w = cwin_ref.shape[1]
    o_ref[0, 0:n_w - 1] = cwin_ref[0, 1:n_w]
    o_ref[0, n_w - 1] = new_ref[0]


def _winshift(cwin4, new3):
    n_b = cwin4.shape[0]
    return pl.pallas_call(
        _winshift_body,
        grid=(n_b,),
        in_specs=[pl.BlockSpec((1,) + cwin4.shape[1:], lambda b: (b, 0, 0, 0)),
                  pl.BlockSpec((1,) + new3.shape[1:], lambda b: (b, 0, 0))],
        out_specs=pl.BlockSpec((1,) + cwin4.shape[1:], lambda b: (b, 0, 0, 0)),
        out_shape=jax.ShapeDtypeStruct(cwin4.shape, F32),
        compiler_params=_cparams(("arbitrary",)),
        name="winshift",
    )(cwin4, new3)


def _block_diag_b(bb_re, bb_im):
    gl = SSM_G // SSM_SG
    eye = jnp.eye(gl, dtype=F32)

    def one(bb):
        t = jnp.transpose(bb, (1, 0, 2)).reshape(SSM_SG, gl, SSM_P, SSM_N)
        return jnp.einsum('sgpn,gh->sgphn', t, eye).reshape(SSM_SG, gl * SSM_P, gl * SSM_N)

    return jnp.concatenate([one(bb_re), one(bb_im)], axis=2).astype(BF16)


def _block_diag_c(c_re, c_im):
    gl = SSM_G // SSM_SG
    eye = jnp.eye(gl, dtype=F32)

    def one(c):
        t = jnp.transpose(c, (0, 2, 1)).reshape(SSM_SG, gl, SSM_N, SSM_P)
        return jnp.einsum('sgnp,gh->sgnhp', t, eye).reshape(SSM_SG, gl * SSM_N, gl * SSM_P)

    return jnp.concatenate([one(c_re), -one(c_im)], axis=1).astype(BF16)


def kernel(x_prompt, x_sample, cache_kv, page_table, cache_win, state_ssm, state_conv, rel_bias, g_mix, w_in, g_q, g_k, w_cmp1, pos_cmp, w_cmp2, lam_re, lam_im, log_dt, b_re, b_im, c_re, c_im, d_skip, w_glu, g_out_attn, g_out_ssm, w_out, g_ffn, w_up, conv_w, conv_b, w_down):
    depth = g_mix.shape[0]
    assert depth == 1 and x_prompt.shape[0] == 1 and x_sample.shape[1] == 1
    seq = x_prompt.shape[1]
    n_b = x_sample.shape[0]
    n_pages = page_table.shape[1]
    n_w = cache_win.shape[2]
    li = 0
    row = lambda v: v.reshape(1, -1)

    w_in_b = _cast_transposed(w_in[li].T, IN_COLS_PAD)
    w1 = w_cmp1[li]
    wcat = jnp.concatenate([w1[:, :CMP_STRIDE], w1[:, CMP_STRIDE:]], axis=-1).astype(BF16).reshape(
        2, CMP_STRIDE // 2, 2 * HEAD_DIM, 2 * HEAD_DIM)
    w2 = w_cmp2[li].astype(BF16)
    w_glu_b = w_glu[li].astype(BF16)
    w_out_b = w_out[li].astype(BF16)
    padc = lambda a: jnp.pad(a, ((0, 0), (0, D_FF_PAD - D_FF)))
    w_up_b = _cast_pad(w_up[li], 2, D_FF_PAD, 256)
    wd = _cast_pad_rows(w_down[li], D_FF_PAD, D_FF // 8)
    cw = padc(conv_w[li])
    cb = padc(row(conv_b[li]))

    ab_re, ab_im, bb_re, bb_im, cpos = _prep(lam_re[li], lam_im[li], log_dt[li], b_re[li], b_im[li], pos_cmp[li], w1)
    wb = _block_diag_b(bb_re, bb_im)
    wc = _block_diag_c(c_re[li], c_im[li])
    ab_re = row(ab_re)
    ab_im = row(ab_im)

    in_args = (row(g_mix[li]), w_in_b, row(g_q[li]), row(g_k[li, 1]), row(g_k[li, 2]))
    ssm_args = (wb, wc, ab_re, ab_im, row(d_skip[li]), w_glu_b, row(g_out_ssm[li]))

    xp = x_prompt[0]
    n_seg = 8
    q, kv, win, u2, gates, cmp_rows, kt, vt = _inproj(xp, *in_args, tm=512, q_scale=HEAD_DIM ** -0.5, n_seg=n_seg,
                                            attn_operands=True)
    cmp_out = _pcompress(cmp_rows, wcat, cpos, w2, row(g_k[li, 0]))
    zpad = jnp.zeros((CMP_STRIDE, 2 * KV_W), F32)
    cmp_pad = jnp.concatenate([zpad, cmp_out, zpad], axis=0)
    attn_n = _pattn_t(q, gates, kt, vt, cmp_pad[:, :KV_W], cmp_pad[:, KV_W:], cmp_pad[:, KV_W:].T, rel_bias,
                      row(g_out_attn[li]))
    zst = jnp.zeros((n_seg, SSM_LANES), F32)
    _, fre, fim = _ssm(u2, n_seg, zst, zst, *ssm_args, tc=64, emit=False)
    ire, iim = _ssm_chain(fre, fim, ab_re, ab_im, seq // n_seg)
    ssm_n, hre, him = _ssm(u2, n_seg, ire, iim, *ssm_args, tc=64, emit=True)
    zrow = jnp.zeros((1, D_FF_PAD), F32)
    ffn_w = (w_out_b, row(g_ffn[li]), w_up_b, wd, cw, cb)
    y_p, cnew_p = _ffn(xp, attn_n, ssm_n, *ffn_w, zrow, zrow, tm=512, tf=512, seq=True, n_seg=n_seg)

    y_prompt = y_p[None]
    kv_prompt = kv.reshape(1, 1, seq, 4, N_KV, HEAD_DIM)
    win_prompt = win[seq - min(WINDOW, seq):].reshape(1, 1, min(WINDOW, seq), 2, N_KV, HEAD_DIM)
    ssm_prompt = jnp.stack([hre[n_seg - 1], him[n_seg - 1]], axis=-1).reshape(1, 1, SSM_G, SSM_N, 2)
    conv_prompt = cnew_p[6:8, :D_FF].reshape(1, 1, CONV_W - 1, D_FF)

    xs = x_sample[:, 0]
    q_s, kv_s, win_s, u_s, gates_s = _inproj(xs, *in_args, tm=n_b, q_scale=HEAD_DIM ** -0.5)
    cache4 = cache_kv[li].reshape(cache_kv.shape[1], PAGE, 4 * N_KV, HEAD_DIM)
    cwin4 = cache_win[li].reshape(n_b, n_w, 2 * N_KV, HEAD_DIM)
    q3 = q_s.reshape(n_b, N_HEADS, HEAD_DIM)
    o_c, idx, val = _s1(page_table, cache4, q3, wcat, cpos, w2, row(g_k[li, 0]), rel_bias)
    idx = idx[:, :, :N_KV].reshape(-1)
    val = val[:, :, :N_KV].reshape(-1)
    rep = lambda a: jnp.repeat(a.reshape(n_b, N_KV, HEAD_DIM), Q_PER_KV, axis=1)
    kns = rep(kv_s[:, 2 * N_KV:3 * N_KV])
    vns = rep(kv_s[:, 3 * N_KV:4 * N_KV])
    knw = rep(win_s[:, 0:N_KV])
    vnw = rep(win_s[:, N_KV:2 * N_KV])
    g3 = jnp.transpose(gates_s[:, :N_BRANCH * N_HEADS].reshape(n_b, N_BRANCH, N_HEADS), (0, 2, 1))
    g3 = jnp.pad(g3, ((0, 0), (0, 0), (0, LANE - N_BRANCH)))
    attn_s = _s2(idx, val, page_table, cache4, cwin4, q3, o_c, g3, kns, vns, knw, vnw, rel_bias,
                 g_out_attn[li].reshape(N_HEADS, HEAD_DIM))
    st = state_ssm[li].reshape(n_b, SSM_LANES, 2)
    ssm_s, sre, sim = _ssm(u_s, n_b, st[:, :, 0], st[:, :, 1], *ssm_args, tc=1, emit=True)
    sc = state_conv[li]
    y_s, a_s = _ffn(xs, attn_s.reshape(n_b, ATTN_W), ssm_s, *ffn_w, padc(sc[:, 0]), padc(sc[:, 1]),
                    tm=n_b, tf=512, seq=False)
    win_sample = _winshift(cwin4, win_s)

    y_sample = y_s[:, None]
    kv_sample = kv_s.reshape(1, n_b, 1, 4, N_KV, HEAD_DIM)
    win_sample = win_sample.reshape(1, n_b, n_w, 2, N_KV, HEAD_DIM)
    ssm_sample = jnp.stack([sre, sim], axis=-1).reshape(1, n_b, SSM_G, SSM_N, 2)
    conv_sample = jnp.stack([sc[:, 1], a_s[:, :D_FF]], axis=1)[None]
    return (y_prompt, y_sample, kv_prompt, kv_sample, win_prompt, win_sample,
            ssm_prompt, ssm_sample, conv_prompt, conv_sample)
```

```python
import functools
import math

import numpy as np
import jax
import jax.numpy as jnp
from jax import lax
from jax.experimental import pallas as pl
from jax.experimental.pallas import tpu as pltpu

F32 = jnp.float32
BF16 = jnp.bfloat16
I32 = jnp.int32

D_MODEL = 2048
HEAD_DIM = 128
N_HEADS = 8
N_KV = 2
Q_PER_KV = 4
ATTN_W = 1024
KV_W = 256
N_BRANCH = 3
CMP_LEN = 32
CMP_STRIDE = 16
SEL_BLOCK = 64
CMP_PER_SEL = 4
N_SEL = 16
WINDOW = 512
REL_BUCKETS = 32
REL_MAX_DIST = 128
PAGE = 128
SSM_W = 1024
SSM_G = 64
SSM_N = 64
SSM_P = 16
SSM_SG = 8
SSM_LANES = SSM_G * SSM_N
D_FF = 5504
D_FF_PAD = 5632
CONV_W = 3
EPS = 1e-6
NEG = -1e30
QB = 128
LANE = 128
VMEM_LIMIT = 56 * 1024 * 1024


def _cparams(sem):
    return pltpu.CompilerParams(dimension_semantics=sem, vmem_limit_bytes=VMEM_LIMIT)


def _rms(x, g):
    return x * lax.rsqrt(jnp.mean(x * x, axis=-1, keepdims=True) + EPS) * g


def _gelu(x):
    return jax.nn.gelu(x)


def _dot(a, b):
    return jnp.dot(a, b, preferred_element_type=F32)


def _dot_nt(a, b):
    return lax.dot_general(a, b, (((1,), (1,)), ((), ())), preferred_element_type=F32)


def _split3(x):
    hi = x.astype(BF16)
    r1 = x - hi.astype(F32)
    mid = r1.astype(BF16)
    lo = (r1 - mid.astype(F32)).astype(BF16)
    return hi, mid, lo


def _bucket_np(d):
    n = np.maximum(d, 0)
    exact = REL_BUCKETS // 2
    nf = np.maximum(n, 1).astype(np.float32)
    large = exact + (np.log(nf / np.float32(exact)) / np.float32(math.log(REL_MAX_DIST / exact))
                     * np.float32(REL_BUCKETS - exact)).astype(np.int32)
    return np.where(n < exact, n, np.minimum(large, REL_BUCKETS - 1)).astype(np.int32)


def _bias_lookup(bkt, rb_ref, head, shift=None):
    last = rb_ref[REL_BUCKETS - 1, head]
    acc = jnp.full(bkt.shape, last, F32)
    for b in range(REL_BUCKETS - 1):
        acc = jnp.where(bkt == b, rb_ref[b, head], acc)
    if shift:
        acc = acc - last
    return acc


def _cast_pad_body(w_ref, o_ref):
    cols = w_ref.shape[1]
    cols_pad = o_ref.shape[2]
    full = (cols // LANE) * LANE
    o_ref[0, :, 0:full] = w_ref[:, 0:full].astype(BF16)
    if full < cols:
        tail = w_ref[:, full:cols].astype(BF16)
        o_ref[0, :, full:full + LANE] = jnp.concatenate(
            [tail, jnp.zeros((tail.shape[0], full + LANE - cols), BF16)], axis=1)
        full += LANE
    if full < cols_pad:
        o_ref[0, :, full:cols_pad] = jnp.zeros((o_ref.shape[1], cols_pad - full), BF16)


def _cast_pad(w, n_parts, cols_pad, row_block):
    rows, width = w.shape
    cols = width // n_parts
    return pl.pallas_call(
        _cast_pad_body,
        grid=(n_parts, rows // row_block),
        in_specs=[pl.BlockSpec((row_block, cols), lambda p, r: (r, p))],
        out_specs=pl.BlockSpec((1, row_block, cols_pad), lambda p, r: (p, r, 0)),
        out_shape=jax.ShapeDtypeStruct((n_parts, rows, cols_pad), BF16),
        compiler_params=_cparams(("arbitrary", "arbitrary")),
        name="cast_pad",
    )(w)


def _cast_rows_body(w_ref, o_ref, *, n_src):
    i = pl.program_id(0)
    o_ref[...] = jnp.where(i < n_src, w_ref[...].astype(BF16), jnp.zeros(o_ref.shape, BF16))


def _cast_pad_rows(w, rows_pad, row_block):
    rows, cols = w.shape
    n_src = rows // row_block
    assert rows == n_src * row_block and row_block % 16 == 0 and rows_pad - rows <= row_block
    return pl.pallas_call(
        functools.partial(_cast_rows_body, n_src=n_src),
        grid=(n_src + 1,),
        in_specs=[pl.BlockSpec((row_block, cols), lambda i: (jnp.minimum(i, n_src - 1), 0))],
        out_specs=pl.BlockSpec((row_block, cols), lambda i: (i, 0)),
        out_shape=jax.ShapeDtypeStruct((rows_pad, cols), BF16),
        compiler_params=_cparams(("arbitrary",)),
        name="cast_pad_rows",
    )(w)


def _cast_t_body(w_ref, tail_ref, o_ref, *, n_src):
    i = pl.program_id(0)
    x = jnp.where(i < n_src, w_ref[...], tail_ref[...])
    o_ref[...] = x.T.astype(BF16)


def _cast_transposed(wt, cols_pad):
    cols, rows = wt.shape
    n_src = cols // LANE
    assert cols_pad == (n_src + 1) * LANE and cols > n_src * LANE
    tail = jnp.pad(wt[n_src * LANE:], ((0, cols_pad - cols), (0, 0)))
    return pl.pallas_call(
        functools.partial(_cast_t_body, n_src=n_src),
        grid=(n_src + 1,),
        in_specs=[pl.BlockSpec((LANE, rows), lambda i: (jnp.minimum(i, n_src - 1), 0)),
                  pl.BlockSpec((LANE, rows), lambda i: (0, 0))],
        out_specs=pl.BlockSpec((rows, LANE), lambda i: (0, i)),
        out_shape=jax.ShapeDtypeStruct((rows, cols_pad), BF16),
        compiler_params=_cparams(("arbitrary",)),
        name="cast_transposed",
    )(wt, tail)


def _prep_body(lre_ref, lim_ref, ldt_ref, bre_ref, bim_ref, pos_ref, w1_ref,
               abre_ref, abim_ref, bbre_ref, bbim_ref, cpos_ref):
    lr = lre_ref[...]
    li = lim_ref[...]
    dt = jnp.exp(ldt_ref[...])
    mag = jnp.exp(lr * dt)
    ab_re = mag * jnp.cos(li * dt)
    ab_im = mag * jnp.sin(li * dt)
    den = lr * lr + li * li
    nr = ab_re - 1.0
    ni = ab_im
    f_re = (nr * lr + ni * li) / den
    f_im = (ni * lr - nr * li) / den
    abre_ref[...] = ab_re
    abim_ref[...] = ab_im
    for p in range(SSM_P):
        br = bre_ref[p]
        bi = bim_ref[p]
        bbre_ref[p] = f_re * br - f_im * bi
        bbim_ref[p] = f_re * bi + f_im * br
    for kind in range(2):
        cpos_ref[kind] = jnp.dot(pos_ref[kind], w1_ref[kind], preferred_element_type=F32,
                                 precision=lax.Precision.HIGHEST)


def _prep(lam_re, lam_im, log_dt, b_re, b_im, pos_cmp, w_cmp1):
    bre_t = jnp.transpose(b_re, (2, 0, 1))
    bim_t = jnp.transpose(b_im, (2, 0, 1))
    pos = jnp.zeros((2, 8, CMP_LEN * HEAD_DIM), F32).at[:, 0, :].set(pos_cmp.reshape(2, CMP_LEN * HEAD_DIM))
    w1 = w_cmp1.reshape(2, CMP_LEN * HEAD_DIM, HEAD_DIM)
    return pl.pallas_call(
        _prep_body,
        out_shape=[jax.ShapeDtypeStruct((SSM_G, SSM_N), F32), jax.ShapeDtypeStruct((SSM_G, SSM_N), F32),
                   jax.ShapeDtypeStruct((SSM_P, SSM_G, SSM_N), F32), jax.ShapeDtypeStruct((SSM_P, SSM_G, SSM_N), F32),
                   jax.ShapeDtypeStruct((2, 8, HEAD_DIM), F32)],
        compiler_params=pltpu.CompilerParams(vmem_limit_bytes=VMEM_LIMIT),
        name="prep",
    )(lam_re, lam_im, log_dt.reshape(SSM_G, 1), bre_t, bim_t, pos, w1)


IN_COLS_PAD = ATTN_W + 4 * KV_W + 2 * KV_W + SSM_W + LANE
KT_KS = (0, 2 * HEAD_DIM)
KT_E = HEAD_DIM
KT_KW = 3 * HEAD_DIM
KT_COLS = 5 * HEAD_DIM
VT_ROWS = 4 * HEAD_DIM


def _inproj_body(x_ref, gmix_ref, w_ref, gq_ref, gks_ref, gkw_ref,
                 q_ref, kv_ref, win_ref, u_ref, gt_ref, *extra, q_scale):
    tm = x_ref.shape[0]
    xn = _rms(x_ref[...], gmix_ref[...]).astype(BF16)
    zq = _dot(xn, w_ref[:, 0:ATTN_W])
    for h in range(N_HEADS):
        sl = slice(h * HEAD_DIM, (h + 1) * HEAD_DIM)
        q_ref[:, sl] = (_rms(zq[:, sl], gq_ref[...]) * q_scale).astype(BF16)
    zkv = _dot(xn, w_ref[:, ATTN_W:ATTN_W + 4 * KV_W])
    ks, kw = [], []
    for s in range(4 * N_KV):
        col = zkv[:, s * HEAD_DIM:(s + 1) * HEAD_DIM]
        if s // N_KV == 2:
            col = _rms(col, gks_ref[...])
            ks.append(col)
        kv_ref[:, s, :] = col
    c0 = ATTN_W + 4 * KV_W
    zw = _dot(xn, w_ref[:, c0:c0 + 2 * KV_W])
    for s in range(2 * N_KV):
        col = zw[:, s * HEAD_DIM:(s + 1) * HEAD_DIM]
        if s // N_KV == 0:
            col = _rms(col, gkw_ref[...])
            kw.append(col)
        win_ref[:, s, :] = col
    if extra:
        cmp_ref, kt_ref, vt_ref = extra
        cmp_ref[...] = zkv[:, 0:2 * KV_W]
        for h in range(N_KV):
            kt_ref[:, KT_KS[h]:KT_KS[h] + HEAD_DIM] = ks[h].astype(BF16)
            kt_ref[:, KT_KW + h * HEAD_DIM:KT_KW + (h + 1) * HEAD_DIM] = kw[h].astype(BF16)
        blk = (lax.broadcasted_iota(I32, (tm, HEAD_DIM), 0) + pl.program_id(0) * tm) // SEL_BLOCK
        kt_ref[:, KT_E:KT_E + HEAD_DIM] = jnp.where(blk == lax.broadcasted_iota(I32, (tm, HEAD_DIM), 1), 1.0, 0.0).astype(BF16)
        vt_ref[0:KV_W, :] = zkv[:, 3 * KV_W:4 * KV_W].T.astype(BF16)
        vt_ref[KV_W:2 * KV_W, :] = zw[:, KV_W:2 * KV_W].T.astype(BF16)
    c1 = c0 + 2 * KV_W
    zt = _dot(xn, w_ref[:, c1:IN_COLS_PAD])
    gt_ref[...] = jax.nn.sigmoid(zt[:, 0:LANE])
    u_ref[...] = zt[:, N_BRANCH * N_HEADS:N_BRANCH * N_HEADS + SSM_W]


def _seg_spec(tm, rows, n_seg):
    tiles_per_seg = rows // n_seg // tm
    return pl.BlockSpec((tm, SSM_W), lambda i: (i % tiles_per_seg, i // tiles_per_seg))


def _inproj(x, g_mix, w, g_q, g_ks, g_kw, tm, q_scale, n_seg=1, attn_operands=False):
    rows = x.shape[0]
    row_spec = lambda n: pl.BlockSpec((tm, n), lambda i: (i, 0))
    full = lambda a: pl.BlockSpec(a.shape, lambda i: (0,) * a.ndim)
    out_specs = [row_spec(ATTN_W), pl.BlockSpec((tm, 4 * N_KV, HEAD_DIM), lambda i: (i, 0, 0)),
                 pl.BlockSpec((tm, 2 * N_KV, HEAD_DIM), lambda i: (i, 0, 0)), _seg_spec(tm, rows, n_seg), row_spec(LANE)]
    out_shape = [jax.ShapeDtypeStruct((rows, ATTN_W), BF16), jax.ShapeDtypeStruct((rows, 4 * N_KV, HEAD_DIM), F32),
                 jax.ShapeDtypeStruct((rows, 2 * N_KV, HEAD_DIM), F32),
                 jax.ShapeDtypeStruct((rows // n_seg, n_seg * SSM_W), F32), jax.ShapeDtypeStruct((rows, LANE), F32)]
    if attn_operands:
        out_specs += [row_spec(2 * KV_W), row_spec(KT_COLS), pl.BlockSpec((VT_ROWS, tm), lambda i: (0, i))]
        out_shape += [jax.ShapeDtypeStruct((rows, 2 * KV_W), F32), jax.ShapeDtypeStruct((rows, KT_COLS), BF16),
                      jax.ShapeDtypeStruct((VT_ROWS, rows), BF16)]
    return pl.pallas_call(
        functools.partial(_inproj_body, q_scale=q_scale),
        grid=(rows // tm,),
        in_specs=[row_spec(D_MODEL), full(g_mix), full(w), full(g_q), full(g_ks), full(g_kw)],
        out_specs=out_specs,
        out_shape=out_shape,
        compiler_params=_cparams(("arbitrary",)),
        name="inproj",
    )(x, g_mix, w, g_q, g_ks, g_kw)


PERM_ROWS = 2 * PAGE


def _perm_matrix():
    k = np.arange(PERM_ROWS // CMP_STRIDE)
    p = np.arange(CMP_STRIDE)
    m = np.zeros((PERM_ROWS, PERM_ROWS), np.float32)
    m[(p[:, None] * len(k) + k[None, :]).ravel(), (CMP_STRIDE * k[None, :] + p[:, None]).ravel()] = 1.0
    return jnp.asarray(m, dtype=BF16)


def _compress_pair(x_refs, perm_ref, xp_s, wcat2_ref, cpos, w2):
    n_rows = x_refs[0].shape[0]
    n_grp = n_rows // PERM_ROWS
    n_ch = n_rows // CMP_STRIDE
    ck = PERM_ROWS // CMP_STRIDE

    def perm(g, carry):
        rows = pl.ds(pl.multiple_of(g * PERM_ROWS, PERM_ROWS), PERM_ROWS)
        xcat = jnp.concatenate([x_refs[0][rows, :], x_refs[1][rows, :]], axis=1).astype(BF16)
        y = _dot(perm_ref[...], xcat).astype(BF16)
        for hd in range(N_KV):
            chunks = pl.ds(pl.multiple_of(hd * n_ch + g * ck, ck), ck)
            for p in range(CMP_STRIDE):
                xp_s[p, chunks, :] = y[p * ck:(p + 1) * ck, hd * HEAD_DIM:(hd + 1) * HEAD_DIM]
        return carry

    lax.fori_loop(0, n_grp, perm, 0, unroll=16)
    acc = jnp.zeros((N_KV * n_ch, 2 * HEAD_DIM), F32)
    for q in range(CMP_STRIDE // 2):
        acc = acc + _dot(jnp.concatenate([xp_s[2 * q], xp_s[2 * q + 1]], axis=-1), wcat2_ref[q])
    outs = []
    for hd in range(N_KV):
        a = acc[hd * n_ch:(hd + 1) * n_ch]
        e_hi_next = pltpu.roll(a[:, HEAD_DIM:2 * HEAD_DIM], n_ch - 1, 0)
        hid = _gelu(a[:, 0:HEAD_DIM] + e_hi_next + cpos)
        outs.append(_dot(hid.astype(BF16), w2))
    return outs


def _pcompress_body(x0_ref, x1_ref, perm_ref, wcat2_ref, cpos_ref, w2_ref, gkc_ref, o_ref, xp_s):
    kind = pl.program_id(0)
    outs = _compress_pair((x0_ref, x1_ref), perm_ref, xp_s, wcat2_ref.at[0], cpos_ref[0, 0:1, :], w2_ref[0])
    for hd in range(N_KV):
        o_ref[:, hd * HEAD_DIM:(hd + 1) * HEAD_DIM] = jnp.where(kind == 0, _rms(outs[hd], gkc_ref[...]), outs[hd])


def _pcompress(cmp_rows, wcat2, cpos, w2, g_kc):
    rows = cmp_rows.shape[0]
    n_ch = rows // CMP_STRIDE
    perm = _perm_matrix()
    return pl.pallas_call(
        _pcompress_body,
        grid=(2,),
        in_specs=[pl.BlockSpec((rows, HEAD_DIM), lambda kd: (0, 2 * kd)), pl.BlockSpec((rows, HEAD_DIM), lambda kd: (0, 2 * kd + 1)),
                  pl.BlockSpec(perm.shape, lambda kd: (0, 0)),
                  pl.BlockSpec((1,) + wcat2.shape[1:], lambda kd: (kd, 0, 0, 0)),
                  pl.BlockSpec((1, 8, HEAD_DIM), lambda kd: (kd, 0, 0)),
                  pl.BlockSpec((1, HEAD_DIM, HEAD_DIM), lambda kd: (kd, 0, 0)),
                  pl.BlockSpec((1, HEAD_DIM), lambda kd: (0, 0))],
        out_specs=pl.BlockSpec((n_ch, KV_W), lambda kd: (0, kd)),
        out_shape=jax.ShapeDtypeStruct((n_ch, 2 * KV_W), F32),
        scratch_shapes=[pltpu.VMEM((CMP_STRIDE, N_KV * n_ch, HEAD_DIM), BF16)],
        compiler_params=_cparams(("arbitrary",)),
        name="pcompress",
    )(cmp_rows, cmp_rows, perm, wcat2, cpos, w2, g_kc)


def _select_rounds(score_t, on_pick=None):
    n_j = score_t.shape[0]
    jio = lax.broadcasted_iota(I32, score_t.shape, 0)
    sc = score_t
    for r in range(N_SEL):
        m = jnp.max(sc, axis=0, keepdims=True)
        idx = jnp.min(jnp.where(sc == m, jio, n_j), axis=0, keepdims=True)
        sc = jnp.where(jio == idx, -jnp.inf, sc)
        if on_pick is not None:
            on_pick(r, idx, m >= 0.0)
    return (sc == -jnp.inf) & (score_t >= 0.0)


ONES_ROWS = 16
FAR_TILES = 8
CMP_BAND = 2 * CMP_STRIDE
CMP_PER_QB = QB // CMP_STRIDE
SEL_PER_QB = QB // SEL_BLOCK


def _col_max(tiles):
    m = jnp.max(tiles[0], axis=0, keepdims=True)
    for s in tiles[1:]:
        m = jnp.maximum(m, jnp.max(s, axis=0, keepdims=True))
    return m


def _lanes4(x):
    return jnp.concatenate([x] * Q_PER_KV, axis=1)


def _pattn_t_body(q_ref, gt_ref, kt_ref, vt_ref, kc_ref, vc_ref, vct_ref, rb_ref, bk0_ref, bk1_ref, bkc_ref, mft_ref,
                  gout_ref, o_ref, b0_s, b1_s, bc_s, bw4_s):
    i = pl.program_id(0)
    cols = Q_PER_KV * QB

    @pl.when(i == 0)
    def _tables():
        b_io = lax.broadcasted_iota(I32, (QB, QB), 0)
        a_io = lax.broadcasted_iota(I32, (QB, QB), 1)
        for hd in range(N_HEADS):
            t0 = _bias_lookup(bk0_ref[...], rb_ref, hd, shift=True)
            b0_s[hd] = jnp.where(a_io >= b_io, t0, NEG)
            b1_s[hd] = _bias_lookup(bk1_ref[...], rb_ref, hd, shift=True)
            bc_s[hd] = _bias_lookup(bkc_ref[...], rb_ref, hd, shift=True)
        bw4_s[...] = jnp.where(b_io >= a_io, 0.0, NEG)

    gt_t = gt_ref[...].T
    n_far = jnp.maximum(i - 1, 0) // FAR_TILES
    far_keys = FAR_TILES * QB

    def tab(ref, h):
        return jnp.concatenate([ref[Q_PER_KV * h + g] for g in range(Q_PER_KV)], axis=1)

    def v_aug(row0, start, n):
        return jnp.concatenate([vt_ref[row0:row0 + HEAD_DIM, pl.ds(start, n)], jnp.ones((ONES_ROWS, n), BF16)], axis=0)

    q_ts, qa_ts, o_cs, sel_st = [], [], [], []
    for h in range(N_KV):
        q_t = jnp.concatenate(
            [q_ref[:, (Q_PER_KV * h + g) * HEAD_DIM:(Q_PER_KV * h + g + 1) * HEAD_DIM].astype(F32).T.astype(BF16)
             for g in range(Q_PER_KV)], axis=1)
        hs = slice(h * HEAD_DIM, (h + 1) * HEAD_DIM)

        n_c = kc_ref.shape[0] - CMP_BAND
        blk0 = CMP_PER_QB * i
        near0 = pl.multiple_of(blk0, CMP_PER_QB)
        cf = lax.broadcasted_iota(I32, (n_c, QB), 0)
        ok_f = _lanes4((cf < blk0) & (cf >= CMP_STRIDE))
        s_f = jnp.where(ok_f, _dot(kc_ref[0:n_c, hs].astype(BF16), q_t), NEG)
        cn = lax.broadcasted_iota(I32, (CMP_BAND, cols), 0)
        a_n = lax.broadcasted_iota(I32, (CMP_BAND, cols), 1) % QB
        ok_n = (CMP_STRIDE * (cn - CMP_STRIDE) <= a_n - (CMP_LEN - 1)) & (cn + blk0 >= CMP_STRIDE)
        s_n = jnp.where(ok_n, _dot(kc_ref[pl.ds(near0, CMP_BAND), hs].astype(BF16), q_t) + tab(bc_s, h), NEG)
        m_c = _col_max([s_f, s_n])
        p_f = jnp.where(ok_f, jnp.exp(s_f - m_c), 0.0)
        p_n = jnp.where(ok_n, jnp.exp(s_n - m_c), 0.0)
        l_c = jnp.sum(p_f, axis=0, keepdims=True) + jnp.sum(p_n, axis=0, keepdims=True)
        inv_c = 1.0 / jnp.maximum(l_c, 1e-30)
        vc_near_t = vc_ref[pl.ds(near0, CMP_BAND), hs].T.astype(BF16)
        o_c = (_dot(vct_ref[hs, 0:n_c].astype(BF16), p_f.astype(BF16)) + _dot(vc_near_t, p_n.astype(BF16))) * inv_c
        pn_f = p_f * inv_c
        pn_n = p_n * inv_c
        imp_f = pn_f[:, 0:QB] + pn_f[:, QB:2 * QB] + pn_f[:, 2 * QB:3 * QB] + pn_f[:, 3 * QB:4 * QB]
        imp_n = pn_n[:, 0:QB] + pn_n[:, QB:2 * QB] + pn_n[:, 2 * QB:3 * QB] + pn_n[:, 3 * QB:4 * QB]
        jn = lax.broadcasted_iota(I32, (QB, CMP_BAND), 0)
        cc = lax.broadcasted_iota(I32, (QB, CMP_BAND), 1) + blk0 - CMP_STRIDE
        mnt = (((cc // CMP_PER_SEL) == jn) | (cc == CMP_PER_SEL * jn - 1)) & (cc >= 0)
        mnt = jnp.where(mnt, 1.0, 0.0).astype(BF16)
        imp_t = jnp.zeros((QB, QB), F32)
        for part in _split3(imp_f):
            imp_t = imp_t + _dot(mft_ref[...], part)
        for part in _split3(imp_n):
            imp_t = imp_t + _dot(mnt, part)
        j_io = lax.broadcasted_iota(I32, (QB, QB), 0)
        cur = SEL_PER_QB * i + lax.broadcasted_iota(I32, (QB, QB), 1) // SEL_BLOCK
        forced = (j_io == 0) | (j_io == cur) | (j_io == cur - 1)
        score_t = jnp.where(forced, 1e9, jnp.where(j_io <= cur, imp_t, -1.0))
        sel_t = _select_rounds(score_t)
        selneg = _lanes4(jnp.where(sel_t, 0.0, NEG).astype(BF16))
        qa_t = jnp.concatenate([q_t, selneg] if h == 0 else [selneg, q_t], axis=0)
        ka0 = h * HEAD_DIM

        near_s, near_v = [], []
        for back in range(FAR_TILES + 1):
            kt = i - back
            start = pl.multiple_of(jnp.maximum(kt, 0) * QB, QB)
            s = _dot(kt_ref[pl.ds(start, QB), ka0:ka0 + 2 * HEAD_DIM], qa_t)
            if back == 0:
                s = s + tab(b0_s, h)
            elif back == 1:
                s = s + tab(b1_s, h) + jnp.where(kt >= 0, 0.0, NEG)
            else:
                s = s + jnp.where((kt >= 0) & (kt >= FAR_TILES * n_far), 0.0, NEG)
            near_s.append(s)
            near_v.append(v_aug(h * HEAD_DIM, start, QB))
        m_s = _col_max(near_s)
        acc_s = jnp.zeros((HEAD_DIM + ONES_ROWS, cols), F32)
        for s, v in zip(near_s, near_v):
            acc_s = acc_s + _dot(v, jnp.exp(s - m_s).astype(BF16))
        q_ts.append(q_t)
        qa_ts.append(qa_t)
        o_cs.append(o_c)
        sel_st += [m_s, acc_s]

    def far_step(k, st):
        start = pl.multiple_of(k * far_keys, far_keys)
        out = []
        for h in range(N_KV):
            m, acc = st[2 * h], st[2 * h + 1]
            s = _dot(kt_ref[pl.ds(start, far_keys), h * HEAD_DIM:(h + 2) * HEAD_DIM], qa_ts[h])
            m_new = jnp.maximum(m, jnp.max(s, axis=0, keepdims=True))
            p = jnp.exp(s - m_new).astype(BF16)
            out += [m_new, jnp.exp(m - m_new) * acc + _dot(v_aug(h * HEAD_DIM, start, far_keys), p)]
        return tuple(out)

    sel_st = lax.fori_loop(0, n_far, far_step, tuple(sel_st))

    attn = [None] * N_HEADS
    for h in range(N_KV):
        q_t = q_ts[h]
        acc_s = sel_st[2 * h + 1]
        o_s = acc_s[0:HEAD_DIM] / acc_s[HEAD_DIM:HEAD_DIM + 1]
        o_c = o_cs[h]

        kw0 = KT_KW + h * HEAD_DIM
        win_s, win_v = [], []
        for back in range(5):
            kt = i - back
            start = pl.multiple_of(jnp.maximum(kt, 0) * QB, QB)
            s = _dot(kt_ref[pl.ds(start, QB), kw0:kw0 + HEAD_DIM], q_t)
            if back == 0:
                s = s + tab(b0_s, h)
            elif back == 1:
                s = s + tab(b1_s, h)
            elif back == 4:
                s = s + _lanes4(bw4_s[...])
            if back > 0:
                s = s + jnp.where(kt >= 0, 0.0, NEG)
            win_s.append(s)
            win_v.append(v_aug((N_KV + h) * HEAD_DIM, start, QB))
        m_w = _col_max(win_s)
        acc_w = jnp.zeros((HEAD_DIM + ONES_ROWS, cols), F32)
        for s, v in zip(win_s, win_v):
            acc_w = acc_w + _dot(v, jnp.exp(s - m_w).astype(BF16))
        o_w = acc_w[0:HEAD_DIM] / acc_w[HEAD_DIM:HEAD_DIM + 1]

        for g in range(Q_PER_KV):
            hd = Q_PER_KV * h + g
            cs = slice(g * QB, (g + 1) * QB)
            o_t = (gt_t[hd:hd + 1] * o_c[:, cs] + gt_t[N_HEADS + hd:N_HEADS + hd + 1] * o_s[:, cs]
                   + gt_t[2 * N_HEADS + hd:2 * N_HEADS + hd + 1] * o_w[:, cs])
            attn[hd] = o_t.T
    a = jnp.concatenate(attn, axis=1)
    o_ref[...] = _rms(a, gout_ref[...]).astype(BF16)


def _pattn_t_tables():
    b = np.arange(QB)[:, None]
    a = np.arange(QB)[None, :]
    bk0 = _bucket_np(a - b)
    bk1 = _bucket_np(a - b + QB)
    c = np.arange(CMP_BAND)[:, None] - CMP_STRIDE
    bkc = _bucket_np(a - CMP_STRIDE * c - (CMP_LEN - 1))
    cidx = np.arange(4 * QB)[None, :] - CMP_STRIDE
    j = np.arange(QB)[:, None]
    mft = (((cidx // CMP_PER_SEL) == j) | (cidx == CMP_PER_SEL * j - 1)) & (cidx >= 0)
    return (jnp.asarray(bk0), jnp.asarray(bk1), jnp.asarray(bkc), jnp.asarray(mft.astype(np.float32), dtype=BF16))


def _pattn_t(q, gates, kt, vt, kcp, vcp, vcpt, rel_bias, g_out):
    rows = q.shape[0]
    bk0, bk1, bkc, mft = _pattn_t_tables()
    full = lambda a: pl.BlockSpec(a.shape, lambda i: (0,) * a.ndim)
    once = lambda a: pl.BlockSpec(a.shape, lambda i: (0,) * a.ndim, pipeline_mode=pl.Buffered(1))
    return pl.pallas_call(
        _pattn_t_body,
        grid=(rows // QB,),
        in_specs=[pl.BlockSpec((QB, ATTN_W), lambda i: (i, 0)), pl.BlockSpec((QB, LANE), lambda i: (i, 0)),
                  once(kt), once(vt), full(kcp), full(vcp), full(vcpt), pl.BlockSpec(memory_space=pltpu.SMEM),
                  full(bk0), full(bk1), full(bkc), full(mft), full(g_out)],
        out_specs=pl.BlockSpec((QB, ATTN_W), lambda i: (i, 0)),
        out_shape=jax.ShapeDtypeStruct((rows, ATTN_W), BF16),
        scratch_shapes=[pltpu.VMEM((N_HEADS, QB, QB), F32), pltpu.VMEM((N_HEADS, QB, QB), F32),
                        pltpu.VMEM((N_HEADS, CMP_BAND, QB), F32), pltpu.VMEM((QB, QB), F32)],
        compiler_params=_cparams(("arbitrary",)),
        name="pattn",
    )(q, gates, kt, vt, kcp, vcp, vcpt, rel_bias, bk0, bk1, bkc, mft, g_out)


def _ssm_body(u_ref, ire_ref, iim_ref, wb_ref, wc_ref, abre_ref, abim_ref, dsk_ref, wglu_ref, gout_ref,
              y_ref, fre_ref, fim_ref, u_s, xre_s, xim_s, y_s, sre_s, sim_s, *, n_seg, tc, emit):
    c = pl.program_id(0)
    sgl = SSM_LANES // SSM_SG

    @pl.when(c == 0)
    def _init():
        sre_s[...] = ire_ref[...]
        sim_s[...] = iim_ref[...]

    n_lt = SSM_W // LANE
    if tc == 1:
        u_cols = [u_ref[:, l * LANE:(l + 1) * LANE] for l in range(n_lt)]
    else:
        for s in range(n_seg):
            for l in range(n_lt):
                u_s[l, pl.ds(s, tc, stride=n_seg), :] = u_ref[:, s * SSM_W + l * LANE:s * SSM_W + (l + 1) * LANE]
        u_cols = [u_s[l] for l in range(n_lt)]
    for sg in range(SSM_SG):
        ls = slice(sg * sgl, (sg + 1) * sgl)
        bu = _dot(u_cols[sg].astype(BF16), wb_ref[sg])
        xre_s[:, ls] = bu[:, 0:sgl]
        xim_s[:, ls] = bu[:, sgl:2 * sgl]
        ar = abre_ref[:, ls]
        ai = abim_ref[:, ls]

        def step(t, carry):
            xr, xi = carry
            rows = pl.ds(pl.multiple_of(t * n_seg, n_seg), n_seg)
            nr = ar * xr - ai * xi + xre_s[rows, ls]
            ni = ar * xi + ai * xr + xim_s[rows, ls]
            if emit:
                xre_s[rows, ls] = nr
                xim_s[rows, ls] = ni
            return nr, ni

        xr, xi = lax.fori_loop(0, tc, step, (sre_s[:, ls], sim_s[:, ls]), unroll=min(tc, 8))
        sre_s[:, ls] = xr
        sim_s[:, ls] = xi
    fre_ref[...] = sre_s[...]
    fim_ref[...] = sim_s[...]
    if not emit:
        y_ref[...] = jnp.zeros(y_ref.shape, y_ref.dtype)
        return
    ys = []
    for sg in range(SSM_SG):
        ls = slice(sg * sgl, (sg + 1) * sgl)
        x2 = jnp.concatenate([xre_s[:, ls], xim_s[:, ls]], axis=1).astype(BF16)
        ys.append(_dot(x2, wc_ref[sg]))
    y = jnp.concatenate(ys, axis=1) + dsk_ref[...] * jnp.concatenate(u_cols, axis=1)
    z = _dot(_gelu(y).astype(BF16), wglu_ref[...])
    o = z[:, 0:SSM_W] * jax.nn.sigmoid(z[:, SSM_W:2 * SSM_W])
    yn = _rms(o, gout_ref[...])
    if tc == 1:
        y_ref[...] = yn.astype(BF16)
    else:
        for l in range(n_lt):
            y_s[l] = yn[:, l * LANE:(l + 1) * LANE]
        for s in range(n_seg):
            for l in range(n_lt):
                y_ref[:, s * SSM_W + l * LANE:s * SSM_W + (l + 1) * LANE] = (
                    y_s[l, pl.ds(s, tc, stride=n_seg), :].astype(BF16))


def _ssm(u2, n_seg, init_re, init_im, wb, wc, ab_re, ab_im, d_skip, w_glu, g_out, tc, emit):
    if tc == 1:
        assert u2.shape == (n_seg, SSM_W)
        t_len = 1
        blk = (n_seg, SSM_W)
    else:
        t_len = u2.shape[0]
        assert u2.shape[1] == n_seg * SSM_W and n_seg % 8 == 0
        blk = (tc, n_seg * SSM_W)
    rows = tc * n_seg
    full = lambda a: pl.BlockSpec(a.shape, lambda c: (0,) * a.ndim)
    body = functools.partial(_ssm_body, n_seg=n_seg, tc=tc, emit=emit)
    st = jax.ShapeDtypeStruct((n_seg, SSM_LANES), F32)
    y_shape = u2.shape if emit else blk
    y_map = (lambda c: (c, 0)) if emit else (lambda c: (0, 0))
    return pl.pallas_call(
        body,
        grid=(t_len // tc,),
        in_specs=[pl.BlockSpec(blk, lambda c: (c, 0)),
                  full(init_re), full(init_im), full(wb), full(wc),
                  full(ab_re), full(ab_im), full(d_skip), full(w_glu), full(g_out)],
        out_specs=[pl.BlockSpec(blk, y_map), full(init_re), full(init_im)],
        out_shape=[jax.ShapeDtypeStruct(y_shape, BF16), st, st],
        scratch_shapes=[pltpu.VMEM((SSM_W // LANE, rows, LANE), F32), pltpu.VMEM((rows, SSM_LANES), F32),
                        pltpu.VMEM((rows, SSM_LANES), F32), pltpu.VMEM((SSM_W // LANE, rows, LANE), F32),
                        pltpu.VMEM((n_seg, SSM_LANES), F32), pltpu.VMEM((n_seg, SSM_LANES), F32)],
        compiler_params=_cparams(("arbitrary",)),
        name="ssm_emit" if emit else "ssm_final",
    )(u2, init_re, init_im, wb, wc, ab_re, ab_im, d_skip, w_glu, g_out)


def _ssm_chain_body(fre_ref, fim_ref, abre_ref, abim_ref, ire_ref, iim_ref, *, n_seg, log2_len):
    pr = abre_ref[...]
    pi = abim_ref[...]
    for _ in range(log2_len):
        pr, pi = pr * pr - pi * pi, 2.0 * pr * pi
    cr = jnp.zeros((1, SSM_LANES), F32)
    ci = jnp.zeros((1, SSM_LANES), F32)
    for j in range(n_seg):
        ire_ref[j:j + 1, :] = cr
        iim_ref[j:j + 1, :] = ci
        fr = fre_ref[j:j + 1, :]
        fi = fim_ref[j:j + 1, :]
        cr, ci = fr + pr * cr - pi * ci, fi + pr * ci + pi * cr


def _ssm_chain(fre, fim, ab_re, ab_im, seg_len):
    n_seg = fre.shape[0]
    log2_len = int(math.log2(seg_len))
    assert 2 ** log2_len == seg_len
    st = jax.ShapeDtypeStruct((n_seg, SSM_LANES), F32)
    return pl.pallas_call(
        functools.partial(_ssm_chain_body, n_seg=n_seg, log2_len=log2_len),
        out_shape=[st, st],
        compiler_params=pltpu.CompilerParams(vmem_limit_bytes=VMEM_LIMIT),
        name="ssm_chain",
    )(fre, fim, ab_re, ab_im)


FFN_CHUNKS = 2


def _ffn_body(x_ref, a_ref, s_ref, wo_ref, gffn_ref, wa_ref, wg_ref, wd_ref, cw_ref, cb_ref, p2_ref, p1_ref,
              y_ref, cnew_ref, hn_s, car_s, *, seq, tm):
    r = pl.program_id(0)
    j = pl.program_id(1)

    @pl.when(j == 0)
    def _mix():
        h = x_ref[...] + _dot(a_ref[...], wo_ref[0:ATTN_W, :]) + _dot(s_ref[...], wo_ref[ATTN_W:D_MODEL, :])
        y_ref[...] = h
        hn_s[...] = _rms(h, gffn_ref[...]).astype(BF16)

    if seq:
        @pl.when(r == 0)
        def _first():
            car_s[j, 6:7, :] = p2_ref[...]
            car_s[j, 7:8, :] = p1_ref[...]

    hn = hn_s[...]
    tf = wa_ref.shape[1]
    tc = tf // FFN_CHUNKS
    part = None
    for k in range(FFN_CHUNKS):
        cs = slice(k * tc, (k + 1) * tc)
        a = _dot(hn, wa_ref[:, cs])
        g = _dot(hn, wg_ref[:, cs])
        if seq:
            p2 = car_s[j, 6:7, cs]
            p1 = car_s[j, 7:8, cs]
            row = lax.broadcasted_iota(I32, a.shape, 0)
            a1 = jnp.where(row == 0, p1, pltpu.roll(a, 1, 0))
            a2 = jnp.where(row == 0, p2, jnp.where(row == 1, p1, pltpu.roll(a, 2, 0)))
            car_s[j, :, cs] = a[tm - 8:tm, :]
            cnew_ref[:, pl.ds(pl.multiple_of(j * tf + k * tc, tc), tc)] = a[tm - 8:tm, :]
        else:
            a1 = p1_ref[:, cs]
            a2 = p2_ref[:, cs]
            cnew_ref[:, cs] = a
        c = cb_ref[:, cs] + cw_ref[0:1, cs] * a2 + cw_ref[1:2, cs] * a1 + cw_ref[2:3, cs] * a
        d = _dot((_gelu(c) * g).astype(BF16), wd_ref[cs, :])
        part = d if part is None else part + d
    y_ref[...] += part


def _ffn(x, a_n, s_n, w_out, g_ffn, w_up, wd, cw, cb, p2, p1, tm, tf, seq, n_seg=1):
    rows = x.shape[0]
    nj = D_FF_PAD // tf
    body = functools.partial(_ffn_body, seq=seq, tm=tm)
    tiles_per_seg = rows // n_seg // tm
    if seq:
        tap_spec = pl.BlockSpec((1, tf), lambda r, j: (0, j))
        cnew_spec = pl.BlockSpec((8, D_FF_PAD), lambda r, j: (0, 0))
        cnew_shape = jax.ShapeDtypeStruct((8, D_FF_PAD), F32)
    else:
        tap_spec = pl.BlockSpec((tm, tf), lambda r, j: (r, j))
        cnew_spec = pl.BlockSpec((tm, tf), lambda r, j: (r, j))
        cnew_shape = jax.ShapeDtypeStruct((rows, D_FF_PAD), F32)
    return pl.pallas_call(
        body,
        grid=(rows // tm, nj),
        in_specs=[pl.BlockSpec((tm, D_MODEL), lambda r, j: (r, 0)), pl.BlockSpec((tm, ATTN_W), lambda r, j: (r, 0)),
                  pl.BlockSpec((tm, SSM_W), lambda r, j: (r % tiles_per_seg, r // tiles_per_seg)),
                  pl.BlockSpec(w_out.shape, lambda r, j: (0, 0), pipeline_mode=pl.Buffered(1)),
                  pl.BlockSpec((1, D_MODEL), lambda r, j: (0, 0)),
                  pl.BlockSpec((None, D_MODEL, tf), lambda r, j: (0, 0, j)),
                  pl.BlockSpec((None, D_MODEL, tf), lambda r, j: (1, 0, j)),
                  pl.BlockSpec((tf, D_MODEL), lambda r, j: (j, 0)), pl.BlockSpec((CONV_W, tf), lambda r, j: (0, j)),
                  pl.BlockSpec((1, tf), lambda r, j: (0, j)), tap_spec, tap_spec],
        out_specs=[pl.BlockSpec((tm, D_MODEL), lambda r, j: (r, 0)), cnew_spec],
        out_shape=[jax.ShapeDtypeStruct((rows, D_MODEL), F32), cnew_shape],
        scratch_shapes=[pltpu.VMEM((tm, D_MODEL), BF16), pltpu.VMEM((nj, 8, tf), F32)],
        compiler_params=_cparams(("arbitrary", "arbitrary")),
        name="ffn_seq" if seq else "ffn_rows",
    )(x, a_n, s_n, w_out, g_ffn, w_up, w_up, wd, cw, cb, p2, p1)


def _s1_copies(pt_ref, cache_ref, x_s, sem, b, slot, n_pages):
    cps = []
    for pg in range(n_pages):
        page = pt_ref[b * n_pages + pg]
        for s in range(2 * N_KV):
            cps.append(pltpu.make_async_copy(cache_ref.at[page, :, s, :],
                                             x_s.at[slot, s, pl.ds(pg * PAGE, PAGE), :], sem.at[slot]))
    return cps


def _s1_body(pt_ref, cache_ref, q_ref, perm_ref, wcat2_ref, cpos_ref, w2_ref, gkc_ref, rb_ref, bkc_ref, mt_ref,
             oc_ref, idx_ref, val_ref, x_s, xp_s, bias_s, sem, *, n_pages, past):
    b = pl.program_id(0)
    nb = pl.num_programs(0)
    slot = b % 2

    @pl.when(b == 0)
    def _first():
        for cp in _s1_copies(pt_ref, cache_ref, x_s, sem, 0, 0, n_pages):
            cp.start()
        for hd in range(N_HEADS):
            bias_s[hd:hd + 1, :] = _bias_lookup(bkc_ref[...], rb_ref, hd)

    @pl.when(b + 1 < nb)
    def _next():
        for cp in _s1_copies(pt_ref, cache_ref, x_s, sem, b + 1, 1 - slot, n_pages):
            cp.start()

    for cp in _s1_copies(pt_ref, cache_ref, x_s, sem, b, slot, n_pages):
        cp.wait()

    n_c = past // CMP_STRIDE
    cio = lax.broadcasted_iota(I32, (N_HEADS, n_c), 1)
    hrow = lax.broadcasted_iota(I32, (N_HEADS, n_c), 0) // Q_PER_KV
    q = q_ref[0]
    s_all = jnp.zeros((N_HEADS, n_c), F32)
    kcs = _compress_pair((x_s.at[slot, 0], x_s.at[slot, 1]), perm_ref, xp_s, wcat2_ref.at[0], cpos_ref[0, 0:1, :], w2_ref[0])
    vcs = _compress_pair((x_s.at[slot, N_KV], x_s.at[slot, N_KV + 1]), perm_ref, xp_s, wcat2_ref.at[1], cpos_ref[1, 0:1, :],
                         w2_ref[1])
    vcs = [v.astype(BF16) for v in vcs]
    for h in range(N_KV):
        kc = _rms(kcs[h], gkc_ref[...])
        s_all = jnp.where(hrow == h, _dot_nt(q, kc.astype(BF16)), s_all)
    ok = cio < n_c - 1
    s_all = jnp.where(ok, s_all + bias_s[...], NEG)
    m = jnp.max(s_all, axis=-1, keepdims=True)
    p = jnp.where(ok, jnp.exp(s_all - m), 0.0)
    p = p / jnp.maximum(jnp.sum(p, axis=-1, keepdims=True), 1e-30)
    pb = p.astype(BF16)
    hrow_o = lax.broadcasted_iota(I32, (N_HEADS, HEAD_DIM), 0) // Q_PER_KV
    o_c = jnp.zeros((N_HEADS, HEAD_DIM), F32)
    for h in range(N_KV):
        o_c = jnp.where(hrow_o == h, _dot(pb, vcs[h]), o_c)
    oc_ref[0] = o_c
    rio = lax.broadcasted_iota(I32, (8, n_c), 0)
    imp = jnp.zeros((8, n_c), F32)
    for h in range(N_KV):
        ih = p[4 * h:4 * h + 1] + p[4 * h + 1:4 * h + 2] + p[4 * h + 2:4 * h + 3] + p[4 * h + 3:4 * h + 4]
        imp = jnp.where(rio == h, ih, imp)
    imp = jnp.concatenate([imp, jnp.zeros((LANE - 8, n_c), F32)], axis=0)
    n_j = mt_ref.shape[0]
    imp_t = jnp.zeros((n_j, LANE), F32)
    for part in _split3(imp):
        imp_t = imp_t + _dot_nt(mt_ref[...], part)
    j_io = lax.broadcasted_iota(I32, (n_j, LANE), 0)
    cur = past // SEL_BLOCK
    forced = (j_io == 0) | (j_io == cur) | (j_io == cur - 1)
    score_t = jnp.where(forced, 1e9, jnp.where(j_io <= cur, imp_t, -1.0))
    score_t = jnp.where(j_io <= cur, score_t, -jnp.inf)

    def on_pick(r, idx, okv):
        idx_ref[0, r:r + 1, :] = idx
        val_ref[0, r:r + 1, :] = jnp.where(okv, 1, 0)

    _select_rounds(score_t, on_pick)


def _s1(page_table, cache4, q3, wcat2, cpos, w2, g_kc, rel_bias):
    n_b, n_pages = page_table.shape
    past = n_pages * PAGE
    n_c = past // CMP_STRIDE
    perm = _perm_matrix()
    ns = past // SEL_BLOCK + 1
    n_j = -(-ns // 8) * 8
    c = np.arange(n_c)[None, :]
    bkc = _bucket_np(past - (CMP_STRIDE * c + CMP_LEN - 1))
    j = np.arange(n_j)[:, None]
    mt = (((c // CMP_PER_SEL) == j) | (c == CMP_PER_SEL * j - 1)) & (c < n_c - 1)
    mt = jnp.asarray(mt.astype(np.float32), dtype=BF16)
    full = lambda a: pl.BlockSpec(a.shape, lambda b, pt: (0,) * a.ndim)
    body = functools.partial(_s1_body, n_pages=n_pages, past=past)
    return pl.pallas_call(
        body,
        grid_spec=pltpu.PrefetchScalarGridSpec(
            num_scalar_prefetch=1,
            grid=(n_b,),
            in_specs=[pl.BlockSpec(memory_space=pl.ANY), pl.BlockSpec((1, N_HEADS, HEAD_DIM), lambda b, pt: (b, 0, 0)),
                      full(perm), full(wcat2), full(cpos), full(w2), full(g_kc), pl.BlockSpec(memory_space=pltpu.SMEM),
                      pl.BlockSpec((1, n_c), lambda b, pt: (0, 0)), full(mt)],
            out_specs=[pl.BlockSpec((1, N_HEADS, HEAD_DIM), lambda b, pt: (b, 0, 0)),
                       pl.BlockSpec((1, N_SEL, LANE), lambda b, pt: (b, 0, 0)),
                       pl.BlockSpec((1, N_SEL, LANE), lambda b, pt: (b, 0, 0))],
            scratch_shapes=[pltpu.VMEM((2, 2 * N_KV, past, HEAD_DIM), F32),
                            pltpu.VMEM((CMP_STRIDE, N_KV * n_c, HEAD_DIM), BF16), pltpu.VMEM((N_HEADS, n_c), F32),
                            pltpu.SemaphoreType.DMA((2,))],
        ),
        out_shape=[jax.ShapeDtypeStruct((n_b, N_HEADS, HEAD_DIM), F32), jax.ShapeDtypeStruct((n_b, N_SEL, LANE), I32),
                   jax.ShapeDtypeStruct((n_b, N_SEL, LANE), I32)],
        compiler_params=_cparams(("arbitrary",)),
        name="sample_cmp",
    )(page_table.reshape(-1), cache4, q3, perm, wcat2, cpos, w2, g_kc, rel_bias, jnp.asarray(bkc), mt)


def _s2_copies(idx_ref, pt_ref, cache_ref, cwin_ref, ks_s, vs_s, kw_s, vw_s, sem, b, slot, n_pages):
    cps = []
    n_blk = n_pages * (PAGE // SEL_BLOCK)
    for h in range(N_KV):
        for r in range(N_SEL):
            jb = jnp.minimum(idx_ref[(b * N_SEL + r) * N_KV + h], n_blk - 1)
            page = pt_ref[b * n_pages + jb // 2]
            row0 = pl.multiple_of((jb % 2) * SEL_BLOCK, SEL_BLOCK)
            cps.append(pltpu.make_async_copy(cache_ref.at[page, pl.ds(row0, SEL_BLOCK), 2 * N_KV + h, :],
                                             ks_s.at[slot, h, pl.ds(r * SEL_BLOCK, SEL_BLOCK), :], sem.at[slot]))
            cps.append(pltpu.make_async_copy(cache_ref.at[page, pl.ds(row0, SEL_BLOCK), 3 * N_KV + h, :],
                                             vs_s.at[slot, h, pl.ds(r * SEL_BLOCK, SEL_BLOCK), :], sem.at[slot]))
        cps.append(pltpu.make_async_copy(cwin_ref.at[b, :, h, :], kw_s.at[slot, h], sem.at[slot]))
        cps.append(pltpu.make_async_copy(cwin_ref.at[b, :, N_KV + h, :], vw_s.at[slot, h], sem.at[slot]))
    return cps


def _s2_body(idx_ref, val_ref, pt_ref, cache_ref, cwin_ref, q_ref, oc_ref, gt_ref, kns_ref, vns_ref, knw_ref, vnw_ref,
             rb_ref, bks_ref, bkw_ref, gout_ref, o_ref, ks_s, vs_s, kw_s, vw_s, bs_s, bw_s, sem, *, n_pages, past):
    b = pl.program_id(0)
    nb = pl.num_programs(0)
    slot = b % 2
    args = (idx_ref, pt_ref, cache_ref, cwin_ref, ks_s, vs_s, kw_s, vw_s, sem)

    @pl.when(b == 0)
    def _first():
        for cp in _s2_copies(*args, 0, 0, n_pages):
            cp.start()
        for hd in range(N_HEADS):
            bs_s[hd:hd + 1, :] = _bias_lookup(bks_ref[...], rb_ref, hd)
            bw_s[hd:hd + 1, :] = _bias_lookup(bkw_ref[...], rb_ref, hd)

    @pl.when(b + 1 < nb)
    def _next():
        for cp in _s2_copies(*args, b + 1, 1 - slot, n_pages):
            cp.start()

    for cp in _s2_copies(*args, b, slot, n_pages):
        cp.wait()

    n_blk = n_pages * (PAGE // SEL_BLOCK)
    q = q_ref[0]
    qf = q.astype(F32)
    hrow = lax.broadcasted_iota(I32, (N_HEADS, 1), 0) // Q_PER_KV
    lane = lax.broadcasted_iota(I32, (N_HEADS, LANE), 1)
    bias0 = jnp.concatenate([jnp.full((1, 1), rb_ref[0, hd], F32) for hd in range(N_HEADS)], axis=0)
    b31 = jnp.concatenate([jnp.full((1, 1), rb_ref[REL_BUCKETS - 1, hd], F32) for hd in range(N_HEADS)], axis=0)

    tiles = []
    new_sel = jnp.zeros((N_HEADS, 1), F32)
    for t in range(N_SEL // 2):
        s_t = jnp.zeros((N_HEADS, LANE), F32)
        for h in range(N_KV):
            s_h = _dot_nt(q, ks_s[slot, h, pl.ds(t * LANE, LANE), :].astype(BF16))
            halves = []
            for half in range(2):
                r = 2 * t + half
                jb = idx_ref[(b * N_SEL + r) * N_KV + h]
                okr = (val_ref[(b * N_SEL + r) * N_KV + h] > 0) & (jb < n_blk)
                near = bs_s[:, (half * 2) * LANE:(half * 2 + 1) * LANE]
                nearer = bs_s[:, (half * 2 + 1) * LANE:(half * 2 + 2) * LANE]
                bias = jnp.where(jb == n_blk - 1, nearer, jnp.where(jb == n_blk - 2, near, b31))
                halves.append(jnp.where(okr, s_h + bias, NEG))
                new_sel = jnp.where((hrow == h) & (val_ref[(b * N_SEL + r) * N_KV + h] > 0) & (jb == n_blk), 1.0, new_sel)
            s_h = jnp.where(lane < SEL_BLOCK, halves[0], halves[1])
            s_t = jnp.where(hrow == h, s_h, s_t)
        tiles.append(s_t)
    s_new = jnp.sum(qf * kns_ref[0], axis=-1, keepdims=True) + bias0
    s_new = jnp.where(new_sel > 0.5, s_new, NEG)
    m = s_new
    for s_t in tiles:
        m = jnp.maximum(m, jnp.max(s_t, axis=-1, keepdims=True))
    p_new = jnp.where(new_sel > 0.5, jnp.exp(s_new - m), 0.0)
    l = p_new
    acc = p_new * vns_ref[0]
    for t, s_t in enumerate(tiles):
        p = jnp.where(s_t > 0.5 * NEG, jnp.exp(s_t - m), 0.0)
        l = l + jnp.sum(p, axis=-1, keepdims=True)
        pb = p.astype(BF16)
        for h in range(N_KV):
            pv = _dot(pb, vs_s[slot, h, pl.ds(t * LANE, LANE), :].astype(BF16))
            acc = acc + jnp.where(hrow == h, pv, 0.0)
    o_s = acc / jnp.maximum(l, 1e-30)

    n_w = kw_s.shape[2]
    wt = []
    for t in range(n_w // LANE):
        s_t = jnp.zeros((N_HEADS, LANE), F32)
        for h in range(N_KV):
            s_h = _dot_nt(q, kw_s[slot, h, pl.ds(t * LANE, LANE), :].astype(BF16))
            s_t = jnp.where(hrow == h, s_h, s_t)
        wt.append(s_t + bw_s[:, t * LANE:(t + 1) * LANE])
    s_new = jnp.sum(qf * knw_ref[0], axis=-1, keepdims=True) + bias0
    m = s_new
    for s_t in wt:
        m = jnp.maximum(m, jnp.max(s_t, axis=-1, keepdims=True))
    p_new = jnp.exp(s_new - m)
    l = p_new
    acc = p_new * vnw_ref[0]
    for t, s_t in enumerate(wt):
        p = jnp.exp(s_t - m)
        l = l + jnp.sum(p, axis=-1, keepdims=True)
        pb = p.astype(BF16)
        for h in range(N_KV):
            pv = _dot(pb, vw_s[slot, h, pl.ds(t * LANE, LANE), :].astype(BF16))
            acc = acc + jnp.where(hrow == h, pv, 0.0)
    o_w = acc / l

    gt = gt_ref[0]
    a = gt[:, 0:1] * oc_ref[0] + gt[:, 1:2] * o_s + gt[:, 2:3] * o_w
    ms = jnp.sum(jnp.sum(a * a, axis=-1, keepdims=True), axis=0, keepdims=True) / (N_HEADS * HEAD_DIM)
    o_ref[0] = (a * lax.rsqrt(ms + EPS) * gout_ref[...]).astype(BF16)


def _s2(idx, val, page_table, cache4, cwin4, q3, o_c, gates3, kns, vns, knw, vnw, rel_bias, g_out3):
    n_b, n_pages = page_table.shape
    past = n_pages * PAGE
    n_w = cwin4.shape[1]
    s = np.arange(SEL_BLOCK)
    d_near = past - ((past // SEL_BLOCK - 2) * SEL_BLOCK + s)
    d_nearer = past - ((past // SEL_BLOCK - 1) * SEL_BLOCK + s)
    z = np.zeros(SEL_BLOCK, np.int64)
    bks = np.concatenate([d_near, z, d_nearer, z, z, d_near, z, d_nearer])[None, :]
    bkw = (past - (past - n_w + np.arange(n_w)))[None, :]
    full = lambda a: pl.BlockSpec(a.shape, lambda b, *_: (0,) * a.ndim)
    per_b = lambda a: pl.BlockSpec((1,) + a.shape[1:], lambda b, *_: (b,) + (0,) * (a.ndim - 1))
    body = functools.partial(_s2_body, n_pages=n_pages, past=past)
    return pl.pallas_call(
        body,
        grid_spec=pltpu.PrefetchScalarGridSpec(
            num_scalar_prefetch=3,
            grid=(n_b,),
            in_specs=[pl.BlockSpec(memory_space=pl.ANY), pl.BlockSpec(memory_space=pl.ANY),
                      per_b(q3), per_b(o_c), per_b(gates3), per_b(kns), per_b(vns), per_b(knw), per_b(vnw),
                      pl.BlockSpec(memory_space=pltpu.SMEM), pl.BlockSpec((1, 4 * LANE), lambda b, *_: (0, 0)),
                      pl.BlockSpec((1, n_w), lambda b, *_: (0, 0)), full(g_out3)],
            out_specs=pl.BlockSpec((1, N_HEADS, HEAD_DIM), lambda b, *_: (b, 0, 0)),
            scratch_shapes=[pltpu.VMEM((2, N_KV, N_SEL * SEL_BLOCK, HEAD_DIM), F32),
                            pltpu.VMEM((2, N_KV, N_SEL * SEL_BLOCK, HEAD_DIM), F32),
                            pltpu.VMEM((2, N_KV, n_w, HEAD_DIM), F32), pltpu.VMEM((2, N_KV, n_w, HEAD_DIM), F32),
                            pltpu.VMEM((N_HEADS, 4 * LANE), F32), pltpu.VMEM((N_HEADS, n_w), F32),
                            pltpu.SemaphoreType.DMA((2,))],
        ),
        out_shape=jax.ShapeDtypeStruct((n_b, N_HEADS, HEAD_DIM), BF16),
        compiler_params=_cparams(("arbitrary",)),
        name="sample_attn",
    )(idx, val, page_table.reshape(-1), cache4, cwin4, q3, o_c, gates3, kns, vns, knw, vnw, rel_bias,
      jnp.asarray(_bucket_np(bks)), jnp.asarray(_bucket_np(bkw)), g_out3)


def _winshift_body(cwin_ref, new_ref, o_ref):
    n_w = cwin_ref.shape[1]
    o_ref[0, 0:n_w - 1] = cwin_ref[0, 1:n_w]
    o_ref[0, n_w - 1] = new_ref[0]


def _winshift(cwin4, new3):
    n_b = cwin4.shape[0]
    return pl.pallas_call(
        _winshift_body,
        grid=(n_b,),
        in_specs=[pl.BlockSpec((1,) + cwin4.shape[1:], lambda b: (b, 0, 0, 0)),
                  pl.BlockSpec((1,) + new3.shape[1:], lambda b: (b, 0, 0))],
        out_specs=pl.BlockSpec((1,) + cwin4.shape[1:], lambda b: (b, 0, 0, 0)),
        out_shape=jax.ShapeDtypeStruct(cwin4.shape, F32),
        compiler_params=_cparams(("arbitrary",)),
        name="winshift",
    )(cwin4, new3)


def _block_diag_b(bb_re, bb_im):
    gl = SSM_G // SSM_SG
    eye = jnp.eye(gl, dtype=F32)

    def one(bb):
        t = jnp.transpose(bb, (1, 0, 2)).reshape(SSM_SG, gl, SSM_P, SSM_N)
        return jnp.einsum('sgpn,gh->sgphn', t, eye).reshape(SSM_SG, gl * SSM_P, gl * SSM_N)

    return jnp.concatenate([one(bb_re), one(bb_im)], axis=2).astype(BF16)


def _block_diag_c(c_re, c_im):
    gl = SSM_G // SSM_SG
    eye = jnp.eye(gl, dtype=F32)

    def one(c):
        t = jnp.transpose(c, (0, 2, 1)).reshape(SSM_SG, gl, SSM_N, SSM_P)
        return jnp.einsum('sgnp,gh->sgnhp', t, eye).reshape(SSM_SG, gl * SSM_N, gl * SSM_P)

    return jnp.concatenate([one(c_re), -one(c_im)], axis=1).astype(BF16)


def kernel(x_prompt, x_sample, cache_kv, page_table, cache_win, state_ssm, state_conv, rel_bias, g_mix, w_in, g_q, g_k, w_cmp1, pos_cmp, w_cmp2, lam_re, lam_im, log_dt, b_re, b_im, c_re, c_im, d_skip, w_glu, g_out_attn, g_out_ssm, w_out, g_ffn, w_up, conv_w, conv_b, w_down):
    depth = g_mix.shape[0]
    assert depth == 1 and x_prompt.shape[0] == 1 and x_sample.shape[1] == 1
    seq = x_prompt.shape[1]
    n_b = x_sample.shape[0]
    n_pages = page_table.shape[1]
    n_w = cache_win.shape[2]
    li = 0
    row = lambda v: v.reshape(1, -1)

    w_in_b = _cast_transposed(w_in[li].T, IN_COLS_PAD)
    w1 = w_cmp1[li]
    wcat = jnp.concatenate([w1[:, :CMP_STRIDE], w1[:, CMP_STRIDE:]], axis=-1).astype(BF16).reshape(
        2, CMP_STRIDE // 2, 2 * HEAD_DIM, 2 * HEAD_DIM)
    w2 = w_cmp2[li].astype(BF16)
    w_glu_b = w_glu[li].astype(BF16)
    w_out_b = w_out[li].astype(BF16)
    padc = lambda a: jnp.pad(a, ((0, 0), (0, D_FF_PAD - D_FF)))
    w_up_b = _cast_pad(w_up[li], 2, D_FF_PAD, 256)
    wd = _cast_pad_rows(w_down[li], D_FF_PAD, D_FF // 8)
    cw = padc(conv_w[li])
    cb = padc(row(conv_b[li]))

    ab_re, ab_im, bb_re, bb_im, cpos = _prep(lam_re[li], lam_im[li], log_dt[li], b_re[li], b_im[li], pos_cmp[li], w1)
    wb = _block_diag_b(bb_re, bb_im)
    wc = _block_diag_c(c_re[li], c_im[li])
    ab_re = row(ab_re)
    ab_im = row(ab_im)

    in_args = (row(g_mix[li]), w_in_b, row(g_q[li]), row(g_k[li, 1]), row(g_k[li, 2]))
    ssm_args = (wb, wc, ab_re, ab_im, row(d_skip[li]), w_glu_b, row(g_out_ssm[li]))

    xp = x_prompt[0]
    n_seg = 8
    q, kv, win, u2, gates, cmp_rows, kt, vt = _inproj(xp, *in_args, tm=512, q_scale=HEAD_DIM ** -0.5, n_seg=n_seg,
                                            attn_operands=True)
    cmp_out = _pcompress(cmp_rows, wcat, cpos, w2, row(g_k[li, 0]))
    zpad = jnp.zeros((CMP_STRIDE, 2 * KV_W), F32)
    cmp_pad = jnp.concatenate([zpad, cmp_out, zpad], axis=0)
    attn_n = _pattn_t(q, gates, kt, vt, cmp_pad[:, :KV_W], cmp_pad[:, KV_W:], cmp_pad[:, KV_W:].T, rel_bias,
                      row(g_out_attn[li]))
    zst = jnp.zeros((n_seg, SSM_LANES), F32)
    _, fre, fim = _ssm(u2, n_seg, zst, zst, *ssm_args, tc=64, emit=False)
    ire, iim = _ssm_chain(fre, fim, ab_re, ab_im, seq // n_seg)
    ssm_n, hre, him = _ssm(u2, n_seg, ire, iim, *ssm_args, tc=64, emit=True)
    zrow = jnp.zeros((1, D_FF_PAD), F32)
    ffn_w = (w_out_b, row(g_ffn[li]), w_up_b, wd, cw, cb)
    y_p, cnew_p = _ffn(xp, attn_n, ssm_n, *ffn_w, zrow, zrow, tm=512, tf=512, seq=True, n_seg=n_seg)

    y_prompt = y_p[None]
    kv_prompt = kv.reshape(1, 1, seq, 4, N_KV, HEAD_DIM)
    win_prompt = win[seq - min(WINDOW, seq):].reshape(1, 1, min(WINDOW, seq), 2, N_KV, HEAD_DIM)
    ssm_prompt = jnp.stack([hre[n_seg - 1], him[n_seg - 1]], axis=-1).reshape(1, 1, SSM_G, SSM_N, 2)
    conv_prompt = cnew_p[6:8, :D_FF].reshape(1, 1, CONV_W - 1, D_FF)

    xs = x_sample[:, 0]
    q_s, kv_s, win_s, u_s, gates_s = _inproj(xs, *in_args, tm=n_b, q_scale=HEAD_DIM ** -0.5)
    cache4 = cache_kv[li].reshape(cache_kv.shape[1], PAGE, 4 * N_KV, HEAD_DIM)
    cwin4 = cache_win[li].reshape(n_b, n_w, 2 * N_KV, HEAD_DIM)
    q3 = q_s.reshape(n_b, N_HEADS, HEAD_DIM)
    o_c, idx, val = _s1(page_table, cache4, q3, wcat, cpos, w2, row(g_k[li, 0]), rel_bias)
    idx = idx[:, :, :N_KV].reshape(-1)
    val = val[:, :, :N_KV].reshape(-1)
    rep = lambda a: jnp.repeat(a.reshape(n_b, N_KV, HEAD_DIM), Q_PER_KV, axis=1)
    kns = rep(kv_s[:, 2 * N_KV:3 * N_KV])
    vns = rep(kv_s[:, 3 * N_KV:4 * N_KV])
    knw = rep(win_s[:, 0:N_KV])
    vnw = rep(win_s[:, N_KV:2 * N_KV])
    g3 = jnp.transpose(gates_s[:, :N_BRANCH * N_HEADS].reshape(n_b, N_BRANCH, N_HEADS), (0, 2, 1))
    g3 = jnp.pad(g3, ((0, 0), (0, 0), (0, LANE - N_BRANCH)))
    attn_s = _s2(idx, val, page_table, cache4, cwin4, q3, o_c, g3, kns, vns, knw, vnw, rel_bias,
                 g_out_attn[li].reshape(N_HEADS, HEAD_DIM))
    st = state_ssm[li].reshape(n_b, SSM_LANES, 2)
    ssm_s, sre, sim = _ssm(u_s, n_b, st[:, :, 0], st[:, :, 1], *ssm_args, tc=1, emit=True)
    sc = state_conv[li]
    y_s, a_s = _ffn(xs, attn_s.reshape(n_b, ATTN_W), ssm_s, *ffn_w, padc(sc[:, 0]), padc(sc[:, 1]),
                    tm=n_b, tf=512, seq=False)
    win_sample = _winshift(cwin4, win_s)

    y_sample = y_s[:, None]
    kv_sample = kv_s.reshape(1, n_b, 1, 4, N_KV, HEAD_DIM)
    win_sample = win_sample.reshape(1, n_b, n_w, 2, N_KV, HEAD_DIM)
    ssm_sample = jnp.stack([sre, sim], axis=-1).reshape(1, n_b, SSM_G, SSM_N, 2)
    conv_sample = jnp.stack([sc[:, 1], a_s[:, :D_FF]], axis=1)[None]
    return (y_prompt, y_sample, kv_prompt, kv_sample, win_prompt, win_sample,
            ssm_prompt, ssm_sample, conv_prompt, conv_sample)
```

```python
import functools
import math

import numpy as np
import jax
import jax.numpy as jnp
from jax import lax
from jax.experimental import pallas as pl
from jax.experimental.pallas import tpu as pltpu

F32 = jnp.float32
BF16 = jnp.bfloat16
I32 = jnp.int32

D_MODEL = 2048
HEAD_DIM = 128
N_HEADS = 8
N_KV = 2
Q_PER_KV = 4
ATTN_W = 1024
KV_W = 256
N_BRANCH = 3
CMP_LEN = 32
CMP_STRIDE = 16
SEL_BLOCK = 64
CMP_PER_SEL = 4
N_SEL = 16
WINDOW = 512
REL_BUCKETS = 32
REL_MAX_DIST = 128
PAGE = 128
SSM_W = 1024
SSM_G = 64
SSM_N = 64
SSM_P = 16
SSM_SG = 8
SSM_LANES = SSM_G * SSM_N
D_FF = 5504
D_FF_PAD = 5632
CONV_W = 3
EPS = 1e-6
NEG = -1e30
QB = 128
LANE = 128
VMEM_LIMIT = 56 * 1024 * 1024


def _cparams(sem):
    return pltpu.CompilerParams(dimension_semantics=sem, vmem_limit_bytes=VMEM_LIMIT)


def _rms(x, g):
    return x * lax.rsqrt(jnp.mean(x * x, axis=-1, keepdims=True) + EPS) * g


def _gelu(x):
    return jax.nn.gelu(x)


def _dot(a, b):
    return jnp.dot(a, b, preferred_element_type=F32)


def _dot_nt(a, b):
    return lax.dot_general(a, b, (((1,), (1,)), ((), ())), preferred_element_type=F32)


def _split3(x):
    hi = x.astype(BF16)
    r1 = x - hi.astype(F32)
    mid = r1.astype(BF16)
    lo = (r1 - mid.astype(F32)).astype(BF16)
    return hi, mid, lo


def _bucket_np(d):
    n = np.maximum(d, 0)
    exact = REL_BUCKETS // 2
    nf = np.maximum(n, 1).astype(np.float32)
    large = exact + (np.log(nf / np.float32(exact)) / np.float32(math.log(REL_MAX_DIST / exact))
                     * np.float32(REL_BUCKETS - exact)).astype(np.int32)
    return np.where(n < exact, n, np.minimum(large, REL_BUCKETS - 1)).astype(np.int32)


def _bias_lookup(bkt, rb_ref, head, shift=None):
    last = rb_ref[REL_BUCKETS - 1, head]
    acc = jnp.full(bkt.shape, last, F32)
    for b in range(REL_BUCKETS - 1):
        acc = jnp.where(bkt == b, rb_ref[b, head], acc)
    if shift:
        acc = acc - last
    return acc


def _cast_pad_body(w_ref, o_ref):
    cols = w_ref.shape[1]
    cols_pad = o_ref.shape[2]
    full = (cols // LANE) * LANE
    o_ref[0, :, 0:full] = w_ref[:, 0:full].astype(BF16)
    if full < cols:
        tail = w_ref[:, full:cols].astype(BF16)
        o_ref[0, :, full:full + LANE] = jnp.concatenate(
            [tail, jnp.zeros((tail.shape[0], full + LANE - cols), BF16)], axis=1)
        full += LANE
    if full < cols_pad:
        o_ref[0, :, full:cols_pad] = jnp.zeros((o_ref.shape[1], cols_pad - full), BF16)


def _cast_pad(w, n_parts, cols_pad, row_block):
    rows, width = w.shape
    cols = width // n_parts
    return pl.pallas_call(
        _cast_pad_body,
        grid=(n_parts, rows // row_block),
        in_specs=[pl.BlockSpec((row_block, cols), lambda p, r: (r, p))],
        out_specs=pl.BlockSpec((1, row_block, cols_pad), lambda p, r: (p, r, 0)),
        out_shape=jax.ShapeDtypeStruct((n_parts, rows, cols_pad), BF16),
        compiler_params=_cparams(("arbitrary", "arbitrary")),
        name="cast_pad",
    )(w)


def _cast_rows_body(w_ref, o_ref, *, n_src):
    i = pl.program_id(0)
    o_ref[...] = jnp.where(i < n_src, w_ref[...].astype(BF16), jnp.zeros(o_ref.shape, BF16))


def _cast_pad_rows(w, rows_pad, row_block):
    rows, cols = w.shape
    n_src = rows // row_block
    assert rows == n_src * row_block and row_block % 16 == 0 and rows_pad - rows <= row_block
    return pl.pallas_call(
        functools.partial(_cast_rows_body, n_src=n_src),
        grid=(n_src + 1,),
        in_specs=[pl.BlockSpec((row_block, cols), lambda i: (jnp.minimum(i, n_src - 1), 0))],
        out_specs=pl.BlockSpec((row_block, cols), lambda i: (i, 0)),
        out_shape=jax.ShapeDtypeStruct((rows_pad, cols), BF16),
        compiler_params=_cparams(("arbitrary",)),
        name="cast_pad_rows",
    )(w)


def _cast_t_body(w_ref, tail_ref, o_ref, *, n_src):
    i = pl.program_id(0)
    x = jnp.where(i < n_src, w_ref[...], tail_ref[...])
    o_ref[...] = x.T.astype(BF16)


def _cast_transposed(wt, cols_pad):
    cols, rows = wt.shape
    n_src = cols // LANE
    assert cols_pad == (n_src + 1) * LANE and cols > n_src * LANE
    tail = jnp.pad(wt[n_src * LANE:], ((0, cols_pad - cols), (0, 0)))
    return pl.pallas_call(
        functools.partial(_cast_t_body, n_src=n_src),
        grid=(n_src + 1,),
        in_specs=[pl.BlockSpec((LANE, rows), lambda i: (jnp.minimum(i, n_src - 1), 0)),
                  pl.BlockSpec((LANE, rows), lambda i: (0, 0))],
        out_specs=pl.BlockSpec((rows, LANE), lambda i: (0, i)),
        out_shape=jax.ShapeDtypeStruct((rows, cols_pad), BF16),
        compiler_params=_cparams(("arbitrary",)),
        name="cast_transposed",
    )(wt, tail)


def _prep_body(lre_ref, lim_ref, ldt_ref, bre_ref, bim_ref, pos_ref, w1_ref,
               abre_ref, abim_ref, bbre_ref, bbim_ref, cpos_ref):
    lr = lre_ref[...]
    li = lim_ref[...]
    dt = jnp.exp(ldt_ref[...])
    mag = jnp.exp(lr * dt)
    ab_re = mag * jnp.cos(li * dt)
    ab_im = mag * jnp.sin(li * dt)
    den = lr * lr + li * li
    nr = ab_re - 1.0
    ni = ab_im
    f_re = (nr * lr + ni * li) / den
    f_im = (ni * lr - nr * li) / den
    abre_ref[...] = ab_re
    abim_ref[...] = ab_im
    for p in range(SSM_P):
        br = bre_ref[p]
        bi = bim_ref[p]
        bbre_ref[p] = f_re * br - f_im * bi
        bbim_ref[p] = f_re * bi + f_im * br
    for kind in range(2):
        cpos_ref[kind] = jnp.dot(pos_ref[kind], w1_ref[kind], preferred_element_type=F32,
                                 precision=lax.Precision.HIGHEST)


def _prep(lam_re, lam_im, log_dt, b_re, b_im, pos_cmp, w_cmp1):
    bre_t = jnp.transpose(b_re, (2, 0, 1))
    bim_t = jnp.transpose(b_im, (2, 0, 1))
    pos = jnp.zeros((2, 8, CMP_LEN * HEAD_DIM), F32).at[:, 0, :].set(pos_cmp.reshape(2, CMP_LEN * HEAD_DIM))
    w1 = w_cmp1.reshape(2, CMP_LEN * HEAD_DIM, HEAD_DIM)
    return pl.pallas_call(
        _prep_body,
        out_shape=[jax.ShapeDtypeStruct((SSM_G, SSM_N), F32), jax.ShapeDtypeStruct((SSM_G, SSM_N), F32),
                   jax.ShapeDtypeStruct((SSM_P, SSM_G, SSM_N), F32), jax.ShapeDtypeStruct((SSM_P, SSM_G, SSM_N), F32),
                   jax.ShapeDtypeStruct((2, 8, HEAD_DIM), F32)],
        compiler_params=pltpu.CompilerParams(vmem_limit_bytes=VMEM_LIMIT),
        name="prep",
    )(lam_re, lam_im, log_dt.reshape(SSM_G, 1), bre_t, bim_t, pos, w1)


IN_COLS_PAD = ATTN_W + 4 * KV_W + 2 * KV_W + SSM_W + LANE
KT_KS = (0, 2 * HEAD_DIM)
KT_E = HEAD_DIM
KT_KW = 3 * HEAD_DIM
KT_COLS = 5 * HEAD_DIM
VT_ROWS = 4 * HEAD_DIM


def _inproj_body(x_ref, gmix_ref, w_ref, gq_ref, gks_ref, gkw_ref,
                 q_ref, kv_ref, win_ref, u_ref, gt_ref, *extra, q_scale):
    tm = x_ref.shape[0]
    xn = _rms(x_ref[...], gmix_ref[...]).astype(BF16)
    zq = _dot(xn, w_ref[:, 0:ATTN_W])
    for h in range(N_HEADS):
        sl = slice(h * HEAD_DIM, (h + 1) * HEAD_DIM)
        q_ref[:, sl] = (_rms(zq[:, sl], gq_ref[...]) * q_scale).astype(BF16)
    zkv = _dot(xn, w_ref[:, ATTN_W:ATTN_W + 4 * KV_W])
    ks, kw = [], []
    for s in range(4 * N_KV):
        col = zkv[:, s * HEAD_DIM:(s + 1) * HEAD_DIM]
        if s // N_KV == 2:
            col = _rms(col, gks_ref[...])
            ks.append(col)
        kv_ref[:, s, :] = col
    c0 = ATTN_W + 4 * KV_W
    zw = _dot(xn, w_ref[:, c0:c0 + 2 * KV_W])
    for s in range(2 * N_KV):
        col = zw[:, s * HEAD_DIM:(s + 1) * HEAD_DIM]
        if s // N_KV == 0:
            col = _rms(col, gkw_ref[...])
            kw.append(col)
        win_ref[:, s, :] = col
    if extra:
        cmp_ref, kt_ref, vt_ref = extra
        cmp_ref[...] = zkv[:, 0:2 * KV_W]
        for h in range(N_KV):
            kt_ref[:, KT_KS[h]:KT_KS[h] + HEAD_DIM] = ks[h].astype(BF16)
            kt_ref[:, KT_KW + h * HEAD_DIM:KT_KW + (h + 1) * HEAD_DIM] = kw[h].astype(BF16)
        blk = (lax.broadcasted_iota(I32, (tm, HEAD_DIM), 0) + pl.program_id(0) * tm) // SEL_BLOCK
        kt_ref[:, KT_E:KT_E + HEAD_DIM] = jnp.where(blk == lax.broadcasted_iota(I32, (tm, HEAD_DIM), 1), 1.0, 0.0).astype(BF16)
        vt_ref[0:KV_W, :] = zkv[:, 3 * KV_W:4 * KV_W].T.astype(BF16)
        vt_ref[KV_W:2 * KV_W, :] = zw[:, KV_W:2 * KV_W].T.astype(BF16)
    c1 = c0 + 2 * KV_W
    zt = _dot(xn, w_ref[:, c1:IN_COLS_PAD])
    gt_ref[...] = jax.nn.sigmoid(zt[:, 0:LANE])
    u_ref[...] = zt[:, N_BRANCH * N_HEADS:N_BRANCH * N_HEADS + SSM_W]


def _seg_spec(tm, rows, n_seg):
    tiles_per_seg = rows // n_seg // tm
    return pl.BlockSpec((tm, SSM_W), lambda i: (i % tiles_per_seg, i // tiles_per_seg))


def _inproj(x, g_mix, w, g_q, g_ks, g_kw, tm, q_scale, n_seg=1, attn_operands=False):
    rows = x.shape[0]
    row_spec = lambda n: pl.BlockSpec((tm, n), lambda i: (i, 0))
    full = lambda a: pl.BlockSpec(a.shape, lambda i: (0,) * a.ndim)
    out_specs = [row_spec(ATTN_W), pl.BlockSpec((tm, 4 * N_KV, HEAD_DIM), lambda i: (i, 0, 0)),
                 pl.BlockSpec((tm, 2 * N_KV, HEAD_DIM), lambda i: (i, 0, 0)), _seg_spec(tm, rows, n_seg), row_spec(LANE)]
    out_shape = [jax.ShapeDtypeStruct((rows, ATTN_W), BF16), jax.ShapeDtypeStruct((rows, 4 * N_KV, HEAD_DIM), F32),
                 jax.ShapeDtypeStruct((rows, 2 * N_KV, HEAD_DIM), F32),
                 jax.ShapeDtypeStruct((rows // n_seg, n_seg * SSM_W), F32), jax.ShapeDtypeStruct((rows, LANE), F32)]
    if attn_operands:
        out_specs += [row_spec(2 * KV_W), row_spec(KT_COLS), pl.BlockSpec((VT_ROWS, tm), lambda i: (0, i))]
        out_shape += [jax.ShapeDtypeStruct((rows, 2 * KV_W), F32), jax.ShapeDtypeStruct((rows, KT_COLS), BF16),
                      jax.ShapeDtypeStruct((VT_ROWS, rows), BF16)]
    return pl.pallas_call(
        functools.partial(_inproj_body, q_scale=q_scale),
        grid=(rows // tm,),
        in_specs=[row_spec(D_MODEL), full(g_mix), full(w), full(g_q), full(g_ks), full(g_kw)],
        out_specs=out_specs,
        out_shape=out_shape,
        compiler_params=_cparams(("arbitrary",)),
        name="inproj",
    )(x, g_mix, w, g_q, g_ks, g_kw)


PERM_ROWS = 2 * PAGE


def _perm_matrix():
    k = np.arange(PERM_ROWS // CMP_STRIDE)
    p = np.arange(CMP_STRIDE)
    m = np.zeros((PERM_ROWS, PERM_ROWS), np.float32)
    m[(p[:, None] * len(k) + k[None, :]).ravel(), (CMP_STRIDE * k[None, :] + p[:, None]).ravel()] = 1.0
    return jnp.asarray(m, dtype=BF16)


def _compress_pair(x_refs, perm_ref, xp_s, wcat2_ref, cpos, w2):
    n_rows = x_refs[0].shape[0]
    n_grp = n_rows // PERM_ROWS
    n_ch = n_rows // CMP_STRIDE
    ck = PERM_ROWS // CMP_STRIDE

    def perm(g, carry):
        rows = pl.ds(pl.multiple_of(g * PERM_ROWS, PERM_ROWS), PERM_ROWS)
        xcat = jnp.concatenate([x_refs[0][rows, :], x_refs[1][rows, :]], axis=1).astype(BF16)
        y = _dot(perm_ref[...], xcat).astype(BF16)
        for hd in range(N_KV):
            chunks = pl.ds(pl.multiple_of(hd * n_ch + g * ck, ck), ck)
            for p in range(CMP_STRIDE):
                xp_s[p, chunks, :] = y[p * ck:(p + 1) * ck, hd * HEAD_DIM:(hd + 1) * HEAD_DIM]
        return carry

    lax.fori_loop(0, n_grp, perm, 0, unroll=16)
    acc = jnp.zeros((N_KV * n_ch, 2 * HEAD_DIM), F32)
    for q in range(CMP_STRIDE // 2):
        acc = acc + _dot(jnp.concatenate([xp_s[2 * q], xp_s[2 * q + 1]], axis=-1), wcat2_ref[q])
    outs = []
    for hd in range(N_KV):
        a = acc[hd * n_ch:(hd + 1) * n_ch]
        e_hi_next = pltpu.roll(a[:, HEAD_DIM:2 * HEAD_DIM], n_ch - 1, 0)
        hid = _gelu(a[:, 0:HEAD_DIM] + e_hi_next + cpos)
        outs.append(_dot(hid.astype(BF16), w2))
    return outs


def _pcompress_body(x0_ref, x1_ref, perm_ref, wcat2_ref, cpos_ref, w2_ref, gkc_ref, o_ref, xp_s):
    kind = pl.program_id(0)
    outs = _compress_pair((x0_ref, x1_ref), perm_ref, xp_s, wcat2_ref.at[0], cpos_ref[0, 0:1, :], w2_ref[0])
    for hd in range(N_KV):
        o_ref[:, hd * HEAD_DIM:(hd + 1) * HEAD_DIM] = jnp.where(kind == 0, _rms(outs[hd], gkc_ref[...]), outs[hd])


def _pcompress(cmp_rows, wcat2, cpos, w2, g_kc):
    rows = cmp_rows.shape[0]
    n_ch = rows // CMP_STRIDE
    perm = _perm_matrix()
    return pl.pallas_call(
        _pcompress_body,
        grid=(2,),
        in_specs=[pl.BlockSpec((rows, HEAD_DIM), lambda kd: (0, 2 * kd)), pl.BlockSpec((rows, HEAD_DIM), lambda kd: (0, 2 * kd + 1)),
                  pl.BlockSpec(perm.shape, lambda kd: (0, 0)),
                  pl.BlockSpec((1,) + wcat2.shape[1:], lambda kd: (kd, 0, 0, 0)),
                  pl.BlockSpec((1, 8, HEAD_DIM), lambda kd: (kd, 0, 0)),
                  pl.BlockSpec((1, HEAD_DIM, HEAD_DIM), lambda kd: (kd, 0, 0)),
                  pl.BlockSpec((1, HEAD_DIM), lambda kd: (0, 0))],
        out_specs=pl.BlockSpec((n_ch, KV_W), lambda kd: (0, kd)),
        out_shape=jax.ShapeDtypeStruct((n_ch, 2 * KV_W), F32),
        scratch_shapes=[pltpu.VMEM((CMP_STRIDE, N_KV * n_ch, HEAD_DIM), BF16)],
        compiler_params=_cparams(("arbitrary",)),
        name="pcompress",
    )(cmp_rows, cmp_rows, perm, wcat2, cpos, w2, g_kc)


def _select_rounds(score_t, on_pick=None):
    n_j = score_t.shape[0]
    jio = lax.broadcasted_iota(I32, score_t.shape, 0)
    sc = score_t
    for r in range(N_SEL):
        m = jnp.max(sc, axis=0, keepdims=True)
        idx = jnp.min(jnp.where(sc == m, jio, n_j), axis=0, keepdims=True)
        sc = jnp.where(jio == idx, -jnp.inf, sc)
        if on_pick is not None:
            on_pick(r, idx, m >= 0.0)
    return (sc == -jnp.inf) & (score_t >= 0.0)


ONES_ROWS = 16
FAR_TILES = 12
CMP_BAND = 2 * CMP_STRIDE
CMP_PER_QB = QB // CMP_STRIDE
SEL_PER_QB = QB // SEL_BLOCK


def _col_max(tiles):
    m = jnp.max(tiles[0], axis=0, keepdims=True)
    for s in tiles[1:]:
        m = jnp.maximum(m, jnp.max(s, axis=0, keepdims=True))
    return m


def _lanes4(x):
    return jnp.concatenate([x] * Q_PER_KV, axis=1)


def _pattn_t_body(q_ref, gt_ref, kt_ref, vt_ref, kc_ref, vc_ref, vct_ref, rb_ref, bk0_ref, bk1_ref, bkc_ref, mft_ref,
                  gout_ref, o_ref, b0_s, b1_s, bc_s, bw4_s):
    i = pl.program_id(0)
    cols = Q_PER_KV * QB

    @pl.when(i == 0)
    def _tables():
        b_io = lax.broadcasted_iota(I32, (QB, QB), 0)
        a_io = lax.broadcasted_iota(I32, (QB, QB), 1)
        for hd in range(N_HEADS):
            t0 = _bias_lookup(bk0_ref[...], rb_ref, hd, shift=True)
            b0_s[hd] = jnp.where(a_io >= b_io, t0, NEG)
            b1_s[hd] = _bias_lookup(bk1_ref[...], rb_ref, hd, shift=True)
            bc_s[hd] = _bias_lookup(bkc_ref[...], rb_ref, hd, shift=True)
        bw4_s[...] = jnp.where(b_io >= a_io, 0.0, NEG)

    gt_t = gt_ref[...].T
    n_far = jnp.maximum(i - 1, 0) // FAR_TILES
    far_keys = FAR_TILES * QB

    def tab(ref, h):
        return jnp.concatenate([ref[Q_PER_KV * h + g] for g in range(Q_PER_KV)], axis=1)

    def v_aug(row0, start, n):
        return jnp.concatenate([vt_ref[row0:row0 + HEAD_DIM, pl.ds(start, n)], jnp.ones((ONES_ROWS, n), BF16)], axis=0)

    q_ts, qa_ts, o_cs, sel_st = [], [], [], []
    for h in range(N_KV):
        q_t = jnp.concatenate(
            [q_ref[:, (Q_PER_KV * h + g) * HEAD_DIM:(Q_PER_KV * h + g + 1) * HEAD_DIM].astype(F32).T.astype(BF16)
             for g in range(Q_PER_KV)], axis=1)
        hs = slice(h * HEAD_DIM, (h + 1) * HEAD_DIM)

        n_c = kc_ref.shape[0] - CMP_BAND
        blk0 = CMP_PER_QB * i
        near0 = pl.multiple_of(blk0, CMP_PER_QB)
        cf = lax.broadcasted_iota(I32, (n_c, QB), 0)
        ok_f = _lanes4((cf < blk0) & (cf >= CMP_STRIDE))
        s_f = jnp.where(ok_f, _dot(kc_ref[0:n_c, hs].astype(BF16), q_t), NEG)
        cn = lax.broadcasted_iota(I32, (CMP_BAND, cols), 0)
        a_n = lax.broadcasted_iota(I32, (CMP_BAND, cols), 1) % QB
        ok_n = (CMP_STRIDE * (cn - CMP_STRIDE) <= a_n - (CMP_LEN - 1)) & (cn + blk0 >= CMP_STRIDE)
        s_n = jnp.where(ok_n, _dot(kc_ref[pl.ds(near0, CMP_BAND), hs].astype(BF16), q_t) + tab(bc_s, h), NEG)
        m_c = _col_max([s_f, s_n])
        p_f = jnp.where(ok_f, jnp.exp(s_f - m_c), 0.0)
        p_n = jnp.where(ok_n, jnp.exp(s_n - m_c), 0.0)
        l_c = jnp.sum(p_f, axis=0, keepdims=True) + jnp.sum(p_n, axis=0, keepdims=True)
        inv_c = 1.0 / jnp.maximum(l_c, 1e-30)
        vc_near_t = vc_ref[pl.ds(near0, CMP_BAND), hs].T.astype(BF16)
        o_c = (_dot(vct_ref[hs, 0:n_c].astype(BF16), p_f.astype(BF16)) + _dot(vc_near_t, p_n.astype(BF16))) * inv_c
        pn_f = p_f * inv_c
        pn_n = p_n * inv_c
        imp_f = pn_f[:, 0:QB] + pn_f[:, QB:2 * QB] + pn_f[:, 2 * QB:3 * QB] + pn_f[:, 3 * QB:4 * QB]
        imp_n = pn_n[:, 0:QB] + pn_n[:, QB:2 * QB] + pn_n[:, 2 * QB:3 * QB] + pn_n[:, 3 * QB:4 * QB]
        jn = lax.broadcasted_iota(I32, (QB, CMP_BAND), 0)
        cc = lax.broadcasted_iota(I32, (QB, CMP_BAND), 1) + blk0 - CMP_STRIDE
        mnt = (((cc // CMP_PER_SEL) == jn) | (cc == CMP_PER_SEL * jn - 1)) & (cc >= 0)
        mnt = jnp.where(mnt, 1.0, 0.0).astype(BF16)
        imp_t = jnp.zeros((QB, QB), F32)
        for part in _split3(imp_f):
            imp_t = imp_t + _dot(mft_ref[...], part)
        for part in _split3(imp_n):
            imp_t = imp_t + _dot(mnt, part)
        j_io = lax.broadcasted_iota(I32, (QB, QB), 0)
        cur = SEL_PER_QB * i + lax.broadcasted_iota(I32, (QB, QB), 1) // SEL_BLOCK
        forced = (j_io == 0) | (j_io == cur) | (j_io == cur - 1)
        score_t = jnp.where(forced, 1e9, jnp.where(j_io <= cur, imp_t, -1.0))
        sel_t = _select_rounds(score_t)
        selneg = _lanes4(jnp.where(sel_t, 0.0, NEG).astype(BF16))
        qa_t = jnp.concatenate([q_t, selneg] if h == 0 else [selneg, q_t], axis=0)
        ka0 = h * HEAD_DIM

        near_s, near_v = [], []
        for back in range(FAR_TILES + 1):
            kt = i - back
            start = pl.multiple_of(jnp.maximum(kt, 0) * QB, QB)
            s = _dot(kt_ref[pl.ds(start, QB), ka0:ka0 + 2 * HEAD_DIM], qa_t)
            if back == 0:
                s = s + tab(b0_s, h)
            elif back == 1:
                s = s + tab(b1_s, h) + jnp.where(kt >= 0, 0.0, NEG)
            else:
                s = s + jnp.where((kt >= 0) & (kt >= FAR_TILES * n_far), 0.0, NEG)
            near_s.append(s)
            near_v.append(v_aug(h * HEAD_DIM, start, QB))
        m_s = _col_max(near_s)
        acc_s = jnp.zeros((HEAD_DIM + ONES_ROWS, cols), F32)
        for s, v in zip(near_s, near_v):
            acc_s = acc_s + _dot(v, jnp.exp(s - m_s).astype(BF16))
        q_ts.append(q_t)
        qa_ts.append(qa_t)
        o_cs.append(o_c)
        sel_st += [m_s, acc_s]

    def far_step(k, st):
        start = pl.multiple_of(k * far_keys, far_keys)
        out = []
        for h in range(N_KV):
            m, acc = st[2 * h], st[2 * h + 1]
            s = _dot(kt_ref[pl.ds(start, far_keys), h * HEAD_DIM:(h + 2) * HEAD_DIM], qa_ts[h])
            m_new = jnp.maximum(m, jnp.max(s, axis=0, keepdims=True))
            p = jnp.exp(s - m_new).astype(BF16)
            out += [m_new, jnp.exp(m - m_new) * acc + _dot(v_aug(h * HEAD_DIM, start, far_keys), p)]
        return tuple(out)

    sel_st = lax.fori_loop(0, n_far, far_step, tuple(sel_st))

    attn = [None] * N_HEADS
    for h in range(N_KV):
        q_t = q_ts[h]
        acc_s = sel_st[2 * h + 1]
        o_s = acc_s[0:HEAD_DIM] / acc_s[HEAD_DIM:HEAD_DIM + 1]
        o_c = o_cs[h]

        kw0 = KT_KW + h * HEAD_DIM
        win_s, win_v = [], []
        for back in range(5):
            kt = i - back
            start = pl.multiple_of(jnp.maximum(kt, 0) * QB, QB)
            s = _dot(kt_ref[pl.ds(start, QB), kw0:kw0 + HEAD_DIM], q_t)
            if back == 0:
                s = s + tab(b0_s, h)
            elif back == 1:
                s = s + tab(b1_s, h)
            elif back == 4:
                s = s + _lanes4(bw4_s[...])
            if back > 0:
                s = s + jnp.where(kt >= 0, 0.0, NEG)
            win_s.append(s)
            win_v.append(v_aug((N_KV + h) * HEAD_DIM, start, QB))
        m_w = _col_max(win_s)
        acc_w = jnp.zeros((HEAD_DIM + ONES_ROWS, cols), F32)
        for s, v in zip(win_s, win_v):
            acc_w = acc_w + _dot(v, jnp.exp(s - m_w).astype(BF16))
        o_w = acc_w[0:HEAD_DIM] / acc_w[HEAD_DIM:HEAD_DIM + 1]

        for g in range(Q_PER_KV):
            hd = Q_PER_KV * h + g
            cs = slice(g * QB, (g + 1) * QB)
            o_t = (gt_t[hd:hd + 1] * o_c[:, cs] + gt_t[N_HEADS + hd:N_HEADS + hd + 1] * o_s[:, cs]
                   + gt_t[2 * N_HEADS + hd:2 * N_HEADS + hd + 1] * o_w[:, cs])
            attn[hd] = o_t.T
    a = jnp.concatenate(attn, axis=1)
    o_ref[...] = _rms(a, gout_ref[...]).astype(BF16)


def _pattn_t_tables():
    b = np.arange(QB)[:, None]
    a = np.arange(QB)[None, :]
    bk0 = _bucket_np(a - b)
    bk1 = _bucket_np(a - b + QB)
    c = np.arange(CMP_BAND)[:, None] - CMP_STRIDE
    bkc = _bucket_np(a - CMP_STRIDE * c - (CMP_LEN - 1))
    cidx = np.arange(4 * QB)[None, :] - CMP_STRIDE
    j = np.arange(QB)[:, None]
    mft = (((cidx // CMP_PER_SEL) == j) | (cidx == CMP_PER_SEL * j - 1)) & (cidx >= 0)
    return (jnp.asarray(bk0), jnp.asarray(bk1), jnp.asarray(bkc), jnp.asarray(mft.astype(np.float32), dtype=BF16))


def _pattn_t(q, gates, kt, vt, kcp, vcp, vcpt, rel_bias, g_out):
    rows = q.shape[0]
    bk0, bk1, bkc, mft = _pattn_t_tables()
    full = lambda a: pl.BlockSpec(a.shape, lambda i: (0,) * a.ndim)
    once = lambda a: pl.BlockSpec(a.shape, lambda i: (0,) * a.ndim, pipeline_mode=pl.Buffered(1))
    return pl.pallas_call(
        _pattn_t_body,
        grid=(rows // QB,),
        in_specs=[pl.BlockSpec((QB, ATTN_W), lambda i: (i, 0)), pl.BlockSpec((QB, LANE), lambda i: (i, 0)),
                  once(kt), once(vt), full(kcp), full(vcp), full(vcpt), pl.BlockSpec(memory_space=pltpu.SMEM),
                  full(bk0), full(bk1), full(bkc), full(mft), full(g_out)],
        out_specs=pl.BlockSpec((QB, ATTN_W), lambda i: (i, 0)),
        out_shape=jax.ShapeDtypeStruct((rows, ATTN_W), BF16),
        scratch_shapes=[pltpu.VMEM((N_HEADS, QB, QB), F32), pltpu.VMEM((N_HEADS, QB, QB), F32),
                        pltpu.VMEM((N_HEADS, CMP_BAND, QB), F32), pltpu.VMEM((QB, QB), F32)],
        compiler_params=_cparams(("arbitrary",)),
        name="pattn",
    )(q, gates, kt, vt, kcp, vcp, vcpt, rel_bias, bk0, bk1, bkc, mft, g_out)


def _ssm_body(u_ref, ire_ref, iim_ref, wb_ref, wc_ref, abre_ref, abim_ref, dsk_ref, wglu_ref, gout_ref,
              y_ref, fre_ref, fim_ref, u_s, xre_s, xim_s, y_s, sre_s, sim_s, *, n_seg, tc, emit):
    c = pl.program_id(0)
    sgl = SSM_LANES // SSM_SG

    @pl.when(c == 0)
    def _init():
        sre_s[...] = ire_ref[...]
        sim_s[...] = iim_ref[...]

    n_lt = SSM_W // LANE
    if tc == 1:
        u_cols = [u_ref[:, l * LANE:(l + 1) * LANE] for l in range(n_lt)]
    else:
        for s in range(n_seg):
            for l in range(n_lt):
                u_s[l, pl.ds(s, tc, stride=n_seg), :] = u_ref[:, s * SSM_W + l * LANE:s * SSM_W + (l + 1) * LANE]
        u_cols = [u_s[l] for l in range(n_lt)]
    for sg in range(SSM_SG):
        ls = slice(sg * sgl, (sg + 1) * sgl)
        bu = _dot(u_cols[sg].astype(BF16), wb_ref[sg])
        xre_s[:, ls] = bu[:, 0:sgl]
        xim_s[:, ls] = bu[:, sgl:2 * sgl]
        ar = abre_ref[:, ls]
        ai = abim_ref[:, ls]

        def step(t, carry):
            xr, xi = carry
            rows = pl.ds(pl.multiple_of(t * n_seg, n_seg), n_seg)
            nr = ar * xr - ai * xi + xre_s[rows, ls]
            ni = ar * xi + ai * xr + xim_s[rows, ls]
            if emit:
                xre_s[rows, ls] = nr
                xim_s[rows, ls] = ni
            return nr, ni

        xr, xi = lax.fori_loop(0, tc, step, (sre_s[:, ls], sim_s[:, ls]), unroll=min(tc, 8))
        sre_s[:, ls] = xr
        sim_s[:, ls] = xi
    fre_ref[...] = sre_s[...]
    fim_ref[...] = sim_s[...]
    if not emit:
        y_ref[...] = jnp.zeros(y_ref.shape, y_ref.dtype)
        return
    ys = []
    for sg in range(SSM_SG):
        ls = slice(sg * sgl, (sg + 1) * sgl)
        x2 = jnp.concatenate([xre_s[:, ls], xim_s[:, ls]], axis=1).astype(BF16)
        ys.append(_dot(x2, wc_ref[sg]))
    y = jnp.concatenate(ys, axis=1) + dsk_ref[...] * jnp.concatenate(u_cols, axis=1)
    z = _dot(_gelu(y).astype(BF16), wglu_ref[...])
    o = z[:, 0:SSM_W] * jax.nn.sigmoid(z[:, SSM_W:2 * SSM_W])
    yn = _rms(o, gout_ref[...])
    if tc == 1:
        y_ref[...] = yn.astype(BF16)
    else:
        for l in range(n_lt):
            y_s[l] = yn[:, l * LANE:(l + 1) * LANE]
        for s in range(n_seg):
            for l in range(n_lt):
                y_ref[:, s * SSM_W + l * LANE:s * SSM_W + (l + 1) * LANE] = (
                    y_s[l, pl.ds(s, tc, stride=n_seg), :].astype(BF16))


def _ssm(u2, n_seg, init_re, init_im, wb, wc, ab_re, ab_im, d_skip, w_glu, g_out, tc, emit):
    if tc == 1:
        assert u2.shape == (n_seg, SSM_W)
        t_len = 1
        blk = (n_seg, SSM_W)
    else:
        t_len = u2.shape[0]
        assert u2.shape[1] == n_seg * SSM_W and n_seg % 8 == 0
        blk = (tc, n_seg * SSM_W)
    rows = tc * n_seg
    full = lambda a: pl.BlockSpec(a.shape, lambda c: (0,) * a.ndim)
    body = functools.partial(_ssm_body, n_seg=n_seg, tc=tc, emit=emit)
    st = jax.ShapeDtypeStruct((n_seg, SSM_LANES), F32)
    y_shape = u2.shape if emit else blk
    y_map = (lambda c: (c, 0)) if emit else (lambda c: (0, 0))
    return pl.pallas_call(
        body,
        grid=(t_len // tc,),
        in_specs=[pl.BlockSpec(blk, lambda c: (c, 0)),
                  full(init_re), full(init_im), full(wb), full(wc),
                  full(ab_re), full(ab_im), full(d_skip), full(w_glu), full(g_out)],
        out_specs=[pl.BlockSpec(blk, y_map), full(init_re), full(init_im)],
        out_shape=[jax.ShapeDtypeStruct(y_shape, BF16), st, st],
        scratch_shapes=[pltpu.VMEM((SSM_W // LANE, rows, LANE), F32), pltpu.VMEM((rows, SSM_LANES), F32),
                        pltpu.VMEM((rows, SSM_LANES), F32), pltpu.VMEM((SSM_W // LANE, rows, LANE), F32),
                        pltpu.VMEM((n_seg, SSM_LANES), F32), pltpu.VMEM((n_seg, SSM_LANES), F32)],
        compiler_params=_cparams(("arbitrary",)),
        name="ssm_emit" if emit else "ssm_final",
    )(u2, init_re, init_im, wb, wc, ab_re, ab_im, d_skip, w_glu, g_out)


def _ssm_chain_body(fre_ref, fim_ref, abre_ref, abim_ref, ire_ref, iim_ref, *, n_seg, log2_len):
    pr = abre_ref[...]
    pi = abim_ref[...]
    for _ in range(log2_len):
        pr, pi = pr * pr - pi * pi, 2.0 * pr * pi
    cr = jnp.zeros((1, SSM_LANES), F32)
    ci = jnp.zeros((1, SSM_LANES), F32)
    for j in range(n_seg):
        ire_ref[j:j + 1, :] = cr
        iim_ref[j:j + 1, :] = ci
        fr = fre_ref[j:j + 1, :]
        fi = fim_ref[j:j + 1, :]
        cr, ci = fr + pr * cr - pi * ci, fi + pr * ci + pi * cr


def _ssm_chain(fre, fim, ab_re, ab_im, seg_len):
    n_seg = fre.shape[0]
    log2_len = int(math.log2(seg_len))
    assert 2 ** log2_len == seg_len
    st = jax.ShapeDtypeStruct((n_seg, SSM_LANES), F32)
    return pl.pallas_call(
        functools.partial(_ssm_chain_body, n_seg=n_seg, log2_len=log2_len),
        out_shape=[st, st],
        compiler_params=pltpu.CompilerParams(vmem_limit_bytes=VMEM_LIMIT),
        name="ssm_chain",
    )(fre, fim, ab_re, ab_im)


FFN_CHUNKS = 2


def _ffn_body(x_ref, a_ref, s_ref, wo_ref, gffn_ref, wa_ref, wg_ref, wd_ref, cw_ref, cb_ref, p2_ref, p1_ref,
              y_ref, cnew_ref, hn_s, car_s, *, seq, tm):
    r = pl.program_id(0)
    j = pl.program_id(1)

    @pl.when(j == 0)
    def _mix():
        h = x_ref[...] + _dot(a_ref[...], wo_ref[0:ATTN_W, :]) + _dot(s_ref[...], wo_ref[ATTN_W:D_MODEL, :])
        y_ref[...] = h
        hn_s[...] = _rms(h, gffn_ref[...]).astype(BF16)

    if seq:
        @pl.when(r == 0)
        def _first():
            car_s[j, 6:7, :] = p2_ref[...]
            car_s[j, 7:8, :] = p1_ref[...]

    hn = hn_s[...]
    tf = wa_ref.shape[1]
    tc = tf // FFN_CHUNKS
    part = None
    for k in range(FFN_CHUNKS):
        cs = slice(k * tc, (k + 1) * tc)
        a = _dot(hn, wa_ref[:, cs])
        g = _dot(hn, wg_ref[:, cs])
        if seq:
            p2 = car_s[j, 6:7, cs]
            p1 = car_s[j, 7:8, cs]
            row = lax.broadcasted_iota(I32, a.shape, 0)
            a1 = jnp.where(row == 0, p1, pltpu.roll(a, 1, 0))
            a2 = jnp.where(row == 0, p2, jnp.where(row == 1, p1, pltpu.roll(a, 2, 0)))
            car_s[j, :, cs] = a[tm - 8:tm, :]
            cnew_ref[:, pl.ds(pl.multiple_of(j * tf + k * tc, tc), tc)] = a[tm - 8:tm, :]
        else:
            a1 = p1_ref[:, cs]
            a2 = p2_ref[:, cs]
            cnew_ref[:, cs] = a
        c = cb_ref[:, cs] + cw_ref[0:1, cs] * a2 + cw_ref[1:2, cs] * a1 + cw_ref[2:3, cs] * a
        d = _dot((_gelu(c) * g).astype(BF16), wd_ref[cs, :])
        part = d if part is None else part + d
    y_ref[...] += part


def _ffn(x, a_n, s_n, w_out, g_ffn, w_up, wd, cw, cb, p2, p1, tm, tf, seq, n_seg=1):
    rows = x.shape[0]
    nj = D_FF_PAD // tf
    body = functools.partial(_ffn_body, seq=seq, tm=tm)
    tiles_per_seg = rows // n_seg // tm
    if seq:
        tap_spec = pl.BlockSpec((1, tf), lambda r, j: (0, j))
        cnew_spec = pl.BlockSpec((8, D_FF_PAD), lambda r, j: (0, 0))
        cnew_shape = jax.ShapeDtypeStruct((8, D_FF_PAD), F32)
    else:
        tap_spec = pl.BlockSpec((tm, tf), lambda r, j: (r, j))
        cnew_spec = pl.BlockSpec((tm, tf), lambda r, j: (r, j))
        cnew_shape = jax.ShapeDtypeStruct((rows, D_FF_PAD), F32)
    return pl.pallas_call(
        body,
        grid=(rows // tm, nj),
        in_specs=[pl.BlockSpec((tm, D_MODEL), lambda r, j: (r, 0)), pl.BlockSpec((tm, ATTN_W), lambda r, j: (r, 0)),
                  pl.BlockSpec((tm, SSM_W), lambda r, j: (r % tiles_per_seg, r // tiles_per_seg)),
                  pl.BlockSpec(w_out.shape, lambda r, j: (0, 0), pipeline_mode=pl.Buffered(1)),
                  pl.BlockSpec((1, D_MODEL), lambda r, j: (0, 0)),
                  pl.BlockSpec((None, D_MODEL, tf), lambda r, j: (0, 0, j)),
                  pl.BlockSpec((None, D_MODEL, tf), lambda r, j: (1, 0, j)),
                  pl.BlockSpec((tf, D_MODEL), lambda r, j: (j, 0)), pl.BlockSpec((CONV_W, tf), lambda r, j: (0, j)),
                  pl.BlockSpec((1, tf), lambda r, j: (0, j)), tap_spec, tap_spec],
        out_specs=[pl.BlockSpec((tm, D_MODEL), lambda r, j: (r, 0)), cnew_spec],
        out_shape=[jax.ShapeDtypeStruct((rows, D_MODEL), F32), cnew_shape],
        scratch_shapes=[pltpu.VMEM((tm, D_MODEL), BF16), pltpu.VMEM((nj, 8, tf), F32)],
        compiler_params=_cparams(("arbitrary", "arbitrary")),
        name="ffn_seq" if seq else "ffn_rows",
    )(x, a_n, s_n, w_out, g_ffn, w_up, w_up, wd, cw, cb, p2, p1)


def _s1_copies(pt_ref, cache_ref, x_s, sem, b, slot, n_pages):
    cps = []
    for pg in range(n_pages):
        page = pt_ref[b * n_pages + pg]
        for s in range(2 * N_KV):
            cps.append(pltpu.make_async_copy(cache_ref.at[page, :, s, :],
                                             x_s.at[slot, s, pl.ds(pg * PAGE, PAGE), :], sem.at[slot]))
    return cps


def _s1_body(pt_ref, cache_ref, q_ref, perm_ref, wcat2_ref, cpos_ref, w2_ref, gkc_ref, rb_ref, bkc_ref, mt_ref,
             oc_ref, idx_ref, val_ref, x_s, xp_s, bias_s, sem, *, n_pages, past):
    b = pl.program_id(0)
    nb = pl.num_programs(0)
    slot = b % 2

    @pl.when(b == 0)
    def _first():
        for cp in _s1_copies(pt_ref, cache_ref, x_s, sem, 0, 0, n_pages):
            cp.start()
        for hd in range(N_HEADS):
            bias_s[hd:hd + 1, :] = _bias_lookup(bkc_ref[...], rb_ref, hd)

    @pl.when(b + 1 < nb)
    def _next():
        for cp in _s1_copies(pt_ref, cache_ref, x_s, sem, b + 1, 1 - slot, n_pages):
            cp.start()

    for cp in _s1_copies(pt_ref, cache_ref, x_s, sem, b, slot, n_pages):
        cp.wait()

    n_c = past // CMP_STRIDE
    cio = lax.broadcasted_iota(I32, (N_HEADS, n_c), 1)
    hrow = lax.broadcasted_iota(I32, (N_HEADS, n_c), 0) // Q_PER_KV
    q = q_ref[0]
    s_all = jnp.zeros((N_HEADS, n_c), F32)
    kcs = _compress_pair((x_s.at[slot, 0], x_s.at[slot, 1]), perm_ref, xp_s, wcat2_ref.at[0], cpos_ref[0, 0:1, :], w2_ref[0])
    vcs = _compress_pair((x_s.at[slot, N_KV], x_s.at[slot, N_KV + 1]), perm_ref, xp_s, wcat2_ref.at[1], cpos_ref[1, 0:1, :],
                         w2_ref[1])
    vcs = [v.astype(BF16) for v in vcs]
    for h in range(N_KV):
        kc = _rms(kcs[h], gkc_ref[...])
        s_all = jnp.where(hrow == h, _dot_nt(q, kc.astype(BF16)), s_all)
    ok = cio < n_c - 1
    s_all = jnp.where(ok, s_all + bias_s[...], NEG)
    m = jnp.max(s_all, axis=-1, keepdims=True)
    p = jnp.where(ok, jnp.exp(s_all - m), 0.0)
    p = p / jnp.maximum(jnp.sum(p, axis=-1, keepdims=True), 1e-30)
    pb = p.astype(BF16)
    hrow_o = lax.broadcasted_iota(I32, (N_HEADS, HEAD_DIM), 0) // Q_PER_KV
    o_c = jnp.zeros((N_HEADS, HEAD_DIM), F32)
    for h in range(N_KV):
        o_c = jnp.where(hrow_o == h, _dot(pb, vcs[h]), o_c)
    oc_ref[0] = o_c
    rio = lax.broadcasted_iota(I32, (8, n_c), 0)
    imp = jnp.zeros((8, n_c), F32)
    for h in range(N_KV):
        ih = p[4 * h:4 * h + 1] + p[4 * h + 1:4 * h + 2] + p[4 * h + 2:4 * h + 3] + p[4 * h + 3:4 * h + 4]
        imp = jnp.where(rio == h, ih, imp)
    imp = jnp.concatenate([imp, jnp.zeros((LANE - 8, n_c), F32)], axis=0)
    n_j = mt_ref.shape[0]
    imp_t = jnp.zeros((n_j, LANE), F32)
    for part in _split3(imp):
        imp_t = imp_t + _dot_nt(mt_ref[...], part)
    j_io = lax.broadcasted_iota(I32, (n_j, LANE), 0)
    cur = past // SEL_BLOCK
    forced = (j_io == 0) | (j_io == cur) | (j_io == cur - 1)
    score_t = jnp.where(forced, 1e9, jnp.where(j_io <= cur, imp_t, -1.0))
    score_t = jnp.where(j_io <= cur, score_t, -jnp.inf)

    def on_pick(r, idx, okv):
        idx_ref[0, r:r + 1, :] = idx
        val_ref[0, r:r + 1, :] = jnp.where(okv, 1, 0)

    _select_rounds(score_t, on_pick)


def _s1(page_table, cache4, q3, wcat2, cpos, w2, g_kc, rel_bias):
    n_b, n_pages = page_table.shape
    past = n_pages * PAGE
    n_c = past // CMP_STRIDE
    perm = _perm_matrix()
    ns = past // SEL_BLOCK + 1
    n_j = -(-ns // 8) * 8
    c = np.arange(n_c)[None, :]
    bkc = _bucket_np(past - (CMP_STRIDE * c + CMP_LEN - 1))
    j = np.arange(n_j)[:, None]
    mt = (((c // CMP_PER_SEL) == j) | (c == CMP_PER_SEL * j - 1)) & (c < n_c - 1)
    mt = jnp.asarray(mt.astype(np.float32), dtype=BF16)
    full = lambda a: pl.BlockSpec(a.shape, lambda b, pt: (0,) * a.ndim)
    body = functools.partial(_s1_body, n_pages=n_pages, past=past)
    return pl.pallas_call(
        body,
        grid_spec=pltpu.PrefetchScalarGridSpec(
            num_scalar_prefetch=1,
            grid=(n_b,),
            in_specs=[pl.BlockSpec(memory_space=pl.ANY), pl.BlockSpec((1, N_HEADS, HEAD_DIM), lambda b, pt: (b, 0, 0)),
                      full(perm), full(wcat2), full(cpos), full(w2), full(g_kc), pl.BlockSpec(memory_space=pltpu.SMEM),
                      pl.BlockSpec((1, n_c), lambda b, pt: (0, 0)), full(mt)],
            out_specs=[pl.BlockSpec((1, N_HEADS, HEAD_DIM), lambda b, pt: (b, 0, 0)),
                       pl.BlockSpec((1, N_SEL, LANE), lambda b, pt: (b, 0, 0)),
                       pl.BlockSpec((1, N_SEL, LANE), lambda b, pt: (b, 0, 0))],
            scratch_shapes=[pltpu.VMEM((2, 2 * N_KV, past, HEAD_DIM), F32),
                            pltpu.VMEM((CMP_STRIDE, N_KV * n_c, HEAD_DIM), BF16), pltpu.VMEM((N_HEADS, n_c), F32),
                            pltpu.SemaphoreType.DMA((2,))],
        ),
        out_shape=[jax.ShapeDtypeStruct((n_b, N_HEADS, HEAD_DIM), F32), jax.ShapeDtypeStruct((n_b, N_SEL, LANE), I32),
                   jax.ShapeDtypeStruct((n_b, N_SEL, LANE), I32)],
        compiler_params=_cparams(("arbitrary",)),
        name="sample_cmp",
    )(page_table.reshape(-1), cache4, q3, perm, wcat2, cpos, w2, g_kc, rel_bias, jnp.asarray(bkc), mt)


def _s2_copies(idx_ref, pt_ref, cache_ref, cwin_ref, ks_s, vs_s, kw_s, vw_s, sem, b, slot, n_pages):
    cps = []
    n_blk = n_pages * (PAGE // SEL_BLOCK)
    for h in range(N_KV):
        for r in range(N_SEL):
            jb = jnp.minimum(idx_ref[(b * N_SEL + r) * N_KV + h], n_blk - 1)
            page = pt_ref[b * n_pages + jb // 2]
            row0 = pl.multiple_of((jb % 2) * SEL_BLOCK, SEL_BLOCK)
            cps.append(pltpu.make_async_copy(cache_ref.at[page, pl.ds(row0, SEL_BLOCK), 2 * N_KV + h, :],
                                             ks_s.at[slot, h, pl.ds(r * SEL_BLOCK, SEL_BLOCK), :], sem.at[slot]))
            cps.append(pltpu.make_async_copy(cache_ref.at[page, pl.ds(row0, SEL_BLOCK), 3 * N_KV + h, :],
                                             vs_s.at[slot, h, pl.ds(r * SEL_BLOCK, SEL_BLOCK), :], sem.at[slot]))
        cps.append(pltpu.make_async_copy(cwin_ref.at[b, :, h, :], kw_s.at[slot, h], sem.at[slot]))
        cps.append(pltpu.make_async_copy(cwin_ref.at[b, :, N_KV + h, :], vw_s.at[slot, h], sem.at[slot]))
    return cps


def _s2_body(idx_ref, val_ref, pt_ref, cache_ref, cwin_ref, q_ref, oc_ref, gt_ref, kns_ref, vns_ref, knw_ref, vnw_ref,
             rb_ref, bks_ref, bkw_ref, gout_ref, o_ref, ks_s, vs_s, kw_s, vw_s, bs_s, bw_s, sem, *, n_pages, past):
    b = pl.program_id(0)
    nb = pl.num_programs(0)
    slot = b % 2
    args = (idx_ref, pt_ref, cache_ref, cwin_ref, ks_s, vs_s, kw_s, vw_s, sem)

    @pl.when(b == 0)
    def _first():
        for cp in _s2_copies(*args, 0, 0, n_pages):
            cp.start()
        for hd in range(N_HEADS):
            bs_s[hd:hd + 1, :] = _bias_lookup(bks_ref[...], rb_ref, hd)
            bw_s[hd:hd + 1, :] = _bias_lookup(bkw_ref[...], rb_ref, hd)

    @pl.when(b + 1 < nb)
    def _next():
        for cp in _s2_copies(*args, b + 1, 1 - slot, n_pages):
            cp.start()

    for cp in _s2_copies(*args, b, slot, n_pages):
        cp.wait()

    n_blk = n_pages * (PAGE // SEL_BLOCK)
    q = q_ref[0]
    qf = q.astype(F32)
    hrow = lax.broadcasted_iota(I32, (N_HEADS, 1), 0) // Q_PER_KV
    lane = lax.broadcasted_iota(I32, (N_HEADS, LANE), 1)
    bias0 = jnp.concatenate([jnp.full((1, 1), rb_ref[0, hd], F32) for hd in range(N_HEADS)], axis=0)
    b31 = jnp.concatenate([jnp.full((1, 1), rb_ref[REL_BUCKETS - 1, hd], F32) for hd in range(N_HEADS)], axis=0)

    tiles = []
    new_sel = jnp.zeros((N_HEADS, 1), F32)
    for t in range(N_SEL // 2):
        s_t = jnp.zeros((N_HEADS, LANE), F32)
        for h in range(N_KV):
            s_h = _dot_nt(q, ks_s[slot, h, pl.ds(t * LANE, LANE), :].astype(BF16))
            halves = []
            for half in range(2):
                r = 2 * t + half
                jb = idx_ref[(b * N_SEL + r) * N_KV + h]
                okr = (val_ref[(b * N_SEL + r) * N_KV + h] > 0) & (jb < n_blk)
                near = bs_s[:, (half * 2) * LANE:(half * 2 + 1) * LANE]
                nearer = bs_s[:, (half * 2 + 1) * LANE:(half * 2 + 2) * LANE]
                bias = jnp.where(jb == n_blk - 1, nearer, jnp.where(jb == n_blk - 2, near, b31))
                halves.append(jnp.where(okr, s_h + bias, NEG))
                new_sel = jnp.where((hrow == h) & (val_ref[(b * N_SEL + r) * N_KV + h] > 0) & (jb == n_blk), 1.0, new_sel)
            s_h = jnp.where(lane < SEL_BLOCK, halves[0], halves[1])
            s_t = jnp.where(hrow == h, s_h, s_t)
        tiles.append(s_t)
    s_new = jnp.sum(qf * kns_ref[0], axis=-1, keepdims=True) + bias0
    s_new = jnp.where(new_sel > 0.5, s_new, NEG)
    m = s_new
    for s_t in tiles:
        m = jnp.maximum(m, jnp.max(s_t, axis=-1, keepdims=True))
    p_new = jnp.where(new_sel > 0.5, jnp.exp(s_new - m), 0.0)
    l = p_new
    acc = p_new * vns_ref[0]
    for t, s_t in enumerate(tiles):
        p = jnp.where(s_t > 0.5 * NEG, jnp.exp(s_t - m), 0.0)
        l = l + jnp.sum(p, axis=-1, keepdims=True)
        pb = p.astype(BF16)
        for h in range(N_KV):
            pv = _dot(pb, vs_s[slot, h, pl.ds(t * LANE, LANE), :].astype(BF16))
            acc = acc + jnp.where(hrow == h, pv, 0.0)
    o_s = acc / jnp.maximum(l, 1e-30)

    n_w = kw_s.shape[2]
    wt = []
    for t in range(n_w // LANE):
        s_t = jnp.zeros((N_HEADS, LANE), F32)
        for h in range(N_KV):
            s_h = _dot_nt(q, kw_s[slot, h, pl.ds(t * LANE, LANE), :].astype(BF16))
            s_t = jnp.where(hrow == h, s_h, s_t)
        wt.append(s_t + bw_s[:, t * LANE:(t + 1) * LANE])
    s_new = jnp.sum(qf * knw_ref[0], axis=-1, keepdims=True) + bias0
    m = s_new
    for s_t in wt:
        m = jnp.maximum(m, jnp.max(s_t, axis=-1, keepdims=True))
    p_new = jnp.exp(s_new - m)
    l = p_new
    acc = p_new * vnw_ref[0]
    for t, s_t in enumerate(wt):
        p = jnp.exp(s_t - m)
        l = l + jnp.sum(p, axis=-1, keepdims=True)
        pb = p.astype(BF16)
        for h in range(N_KV):
            pv = _dot(pb, vw_s[slot, h, pl.ds(t * LANE, LANE), :].astype(BF16))
            acc = acc + jnp.where(hrow == h, pv, 0.0)
    o_w = acc / l

    gt = gt_ref[0]
    a = gt[:, 0:1] * oc_ref[0] + gt[:, 1:2] * o_s + gt[:, 2:3] * o_w
    ms = jnp.sum(jnp.sum(a * a, axis=-1, keepdims=True), axis=0, keepdims=True) / (N_HEADS * HEAD_DIM)
    o_ref[0] = (a * lax.rsqrt(ms + EPS) * gout_ref[...]).astype(BF16)


def _s2(idx, val, page_table, cache4, cwin4, q3, o_c, gates3, kns, vns, knw, vnw, rel_bias, g_out3):
    n_b, n_pages = page_table.shape
    past = n_pages * PAGE
    n_w = cwin4.shape[1]
    s = np.arange(SEL_BLOCK)
    d_near = past - ((past // SEL_BLOCK - 2) * SEL_BLOCK + s)
    d_nearer = past - ((past // SEL_BLOCK - 1) * SEL_BLOCK + s)
    z = np.zeros(SEL_BLOCK, np.int64)
    bks = np.concatenate([d_near, z, d_nearer, z, z, d_near, z, d_nearer])[None, :]
    bkw = (past - (past - n_w + np.arange(n_w)))[None, :]
    full = lambda a: pl.BlockSpec(a.shape, lambda b, *_: (0,) * a.ndim)
    per_b = lambda a: pl.BlockSpec((1,) + a.shape[1:], lambda b, *_: (b,) + (0,) * (a.ndim - 1))
    body = functools.partial(_s2_body, n_pages=n_pages, past=past)
    return pl.pallas_call(
        body,
        grid_spec=pltpu.PrefetchScalarGridSpec(
            num_scalar_prefetch=3,
            grid=(n_b,),
            in_specs=[pl.BlockSpec(memory_space=pl.ANY), pl.BlockSpec(memory_space=pl.ANY),
                      per_b(q3), per_b(o_c), per_b(gates3), per_b(kns), per_b(vns), per_b(knw), per_b(vnw),
                      pl.BlockSpec(memory_space=pltpu.SMEM), pl.BlockSpec((1, 4 * LANE), lambda b, *_: (0, 0)),
                      pl.BlockSpec((1, n_w), lambda b, *_: (0, 0)), full(g_out3)],
            out_specs=pl.BlockSpec((1, N_HEADS, HEAD_DIM), lambda b, *_: (b, 0, 0)),
            scratch_shapes=[pltpu.VMEM((2, N_KV, N_SEL * SEL_BLOCK, HEAD_DIM), F32),
                            pltpu.VMEM((2, N_KV, N_SEL * SEL_BLOCK, HEAD_DIM), F32),
                            pltpu.VMEM((2, N_KV, n_w, HEAD_DIM), F32), pltpu.VMEM((2, N_KV, n_w, HEAD_DIM), F32),
                            pltpu.VMEM((N_HEADS, 4 * LANE), F32), pltpu.VMEM((N_HEADS, n_w), F32),
                            pltpu.SemaphoreType.DMA((2,))],
        ),
        out_shape=jax.ShapeDtypeStruct((n_b, N_HEADS, HEAD_DIM), BF16),
        compiler_params=_cparams(("arbitrary",)),
        name="sample_attn",
    )(idx, val, page_table.reshape(-1), cache4, cwin4, q3, o_c, gates3, kns, vns, knw, vnw, rel_bias,
      jnp.asarray(_bucket_np(bks)), jnp.asarray(_bucket_np(bkw)), g_out3)


def _winshift_body(cwin_ref, new_ref, o_ref):
    n_w = cwin_ref.shape[1]
    o_ref[0, 0:n_w - 1] = cwin_ref[0, 1:n_w]
    o_ref[0, n_w - 1] = new_ref[0]


def _winshift(cwin4, new3):
    n_b = cwin4.shape[0]
    return pl.pallas_call(
        _winshift_body,
        grid=(n_b,),
        in_specs=[pl.BlockSpec((1,) + cwin4.shape[1:], lambda b: (b, 0, 0, 0)),
                  pl.BlockSpec((1,) + new3.shape[1:], lambda b: (b, 0, 0))],
        out_specs=pl.BlockSpec((1,) + cwin4.shape[1:], lambda b: (b, 0, 0, 0)),
        out_shape=jax.ShapeDtypeStruct(cwin4.shape, F32),
        compiler_params=_cparams(("arbitrary",)),
        name="winshift",
    )(cwin4, new3)


def _block_diag_b(bb_re, bb_im):
    gl = SSM_G // SSM_SG
    eye = jnp.eye(gl, dtype=F32)

    def one(bb):
        t = jnp.transpose(bb, (1, 0, 2)).reshape(SSM_SG, gl, SSM_P, SSM_N)
        return jnp.einsum('sgpn,gh->sgphn', t, eye).reshape(SSM_SG, gl * SSM_P, gl * SSM_N)

    return jnp.concatenate([one(bb_re), one(bb_im)], axis=2).astype(BF16)


def _block_diag_c(c_re, c_im):
    gl = SSM_G // SSM_SG
    eye = jnp.eye(gl, dtype=F32)

    def one(c):
        t = jnp.transpose(c, (0, 2, 1)).reshape(SSM_SG, gl, SSM_N, SSM_P)
        return jnp.einsum('sgnp,gh->sgnhp', t, eye).reshape(SSM_SG, gl * SSM_N, gl * SSM_P)

    return jnp.concatenate([one(c_re), -one(c_im)], axis=1).astype(BF16)


def kernel(x_prompt, x_sample, cache_kv, page_table, cache_win, state_ssm, state_conv, rel_bias, g_mix, w_in, g_q, g_k, w_cmp1, pos_cmp, w_cmp2, lam_re, lam_im, log_dt, b_re, b_im, c_re, c_im, d_skip, w_glu, g_out_attn, g_out_ssm, w_out, g_ffn, w_up, conv_w, conv_b, w_down):
    depth = g_mix.shape[0]
    assert depth == 1 and x_prompt.shape[0] == 1 and x_sample.shape[1] == 1
    seq = x_prompt.shape[1]
    n_b = x_sample.shape[0]
    n_pages = page_table.shape[1]
    n_w = cache_win.shape[2]
    li = 0
    row = lambda v: v.reshape(1, -1)

    w_in_b = _cast_transposed(w_in[li].T, IN_COLS_PAD)
    w1 = w_cmp1[li]
    wcat = jnp.concatenate([w1[:, :CMP_STRIDE], w1[:, CMP_STRIDE:]], axis=-1).astype(BF16).reshape(
        2, CMP_STRIDE // 2, 2 * HEAD_DIM, 2 * HEAD_DIM)
    w2 = w_cmp2[li].astype(BF16)
    w_glu_b = w_glu[li].astype(BF16)
    w_out_b = w_out[li].astype(BF16)
    padc = lambda a: jnp.pad(a, ((0, 0), (0, D_FF_PAD - D_FF)))
    w_up_b = _cast_pad(w_up[li], 2, D_FF_PAD, 256)
    wd = _cast_pad_rows(w_down[li], D_FF_PAD, D_FF // 8)
    cw = padc(conv_w[li])
    cb = padc(row(conv_b[li]))

    ab_re, ab_im, bb_re, bb_im, cpos = _prep(lam_re[li], lam_im[li], log_dt[li], b_re[li], b_im[li], pos_cmp[li], w1)
    wb = _block_diag_b(bb_re, bb_im)
    wc = _block_diag_c(c_re[li], c_im[li])
    ab_re = row(ab_re)
    ab_im = row(ab_im)

    in_args = (row(g_mix[li]), w_in_b, row(g_q[li]), row(g_k[li, 1]), row(g_k[li, 2]))
    ssm_args = (wb, wc, ab_re, ab_im, row(d_skip[li]), w_glu_b, row(g_out_ssm[li]))

    xp = x_prompt[0]
    n_seg = 8
    q, kv, win, u2, gates, cmp_rows, kt, vt = _inproj(xp, *in_args, tm=512, q_scale=HEAD_DIM ** -0.5, n_seg=n_seg,
                                            attn_operands=True)
    cmp_out = _pcompress(cmp_rows, wcat, cpos, w2, row(g_k[li, 0]))
    zpad = jnp.zeros((CMP_STRIDE, 2 * KV_W), F32)
    cmp_pad = jnp.concatenate([zpad, cmp_out, zpad], axis=0)
    attn_n = _pattn_t(q, gates, kt, vt, cmp_pad[:, :KV_W], cmp_pad[:, KV_W:], cmp_pad[:, KV_W:].T, rel_bias,
                      row(g_out_attn[li]))
    zst = jnp.zeros((n_seg, SSM_LANES), F32)
    _, fre, fim = _ssm(u2, n_seg, zst, zst, *ssm_args, tc=64, emit=False)
    ire, iim = _ssm_chain(fre, fim, ab_re, ab_im, seq // n_seg)
    ssm_n, hre, him = _ssm(u2, n_seg, ire, iim, *ssm_args, tc=64, emit=True)
    zrow = jnp.zeros((1, D_FF_PAD), F32)
    ffn_w = (w_out_b, row(g_ffn[li]), w_up_b, wd, cw, cb)
    y_p, cnew_p = _ffn(xp, attn_n, ssm_n, *ffn_w, zrow, zrow, tm=512, tf=512, seq=True, n_seg=n_seg)

    y_prompt = y_p[None]
    kv_prompt = kv.reshape(1, 1, seq, 4, N_KV, HEAD_DIM)
    win_prompt = win[seq - min(WINDOW, seq):].reshape(1, 1, min(WINDOW, seq), 2, N_KV, HEAD_DIM)
    ssm_prompt = jnp.stack([hre[n_seg - 1], him[n_seg - 1]], axis=-1).reshape(1, 1, SSM_G, SSM_N, 2)
    conv_prompt = cnew_p[6:8, :D_FF].reshape(1, 1, CONV_W - 1, D_FF)

    xs = x_sample[:, 0]
    q_s, kv_s, win_s, u_s, gates_s = _inproj(xs, *in_args, tm=n_b, q_scale=HEAD_DIM ** -0.5)
    cache4 = cache_kv[li].reshape(cache_kv.shape[1], PAGE, 4 * N_KV, HEAD_DIM)
    cwin4 = cache_win[li].reshape(n_b, n_w, 2 * N_KV, HEAD_DIM)
    q3 = q_s.reshape(n_b, N_HEADS, HEAD_DIM)
    o_c, idx, val = _s1(page_table, cache4, q3, wcat, cpos, w2, row(g_k[li, 0]), rel_bias)
    idx = idx[:, :, :N_KV].reshape(-1)
    val = val[:, :, :N_KV].reshape(-1)
    rep = lambda a: jnp.repeat(a.reshape(n_b, N_KV, HEAD_DIM), Q_PER_KV, axis=1)
    kns = rep(kv_s[:, 2 * N_KV:3 * N_KV])
    vns = rep(kv_s[:, 3 * N_KV:4 * N_KV])
    knw = rep(win_s[:, 0:N_KV])
    vnw = rep(win_s[:, N_KV:2 * N_KV])
    g3 = jnp.transpose(gates_s[:, :N_BRANCH * N_HEADS].reshape(n_b, N_BRANCH, N_HEADS), (0, 2, 1))
    g3 = jnp.pad(g3, ((0, 0), (0, 0), (0, LANE - N_BRANCH)))
    attn_s = _s2(idx, val, page_table, cache4, cwin4, q3, o_c, g3, kns, vns, knw, vnw, rel_bias,
                 g_out_attn[li].reshape(N_HEADS, HEAD_DIM))
    st = state_ssm[li].reshape(n_b, SSM_LANES, 2)
    ssm_s, sre, sim = _ssm(u_s, n_b, st[:, :, 0], st[:, :, 1], *ssm_args, tc=1, emit=True)
    sc = state_conv[li]
    y_s, a_s = _ffn(xs, attn_s.reshape(n_b, ATTN_W), ssm_s, *ffn_w, padc(sc[:, 0]), padc(sc[:, 1]),
                    tm=n_b, tf=512, seq=False)
    win_sample = _winshift(cwin4, win_s)

    y_sample = y_s[:, None]
    kv_sample = kv_s.reshape(1, n_b, 1, 4, N_KV, HEAD_DIM)
    win_sample = win_sample.reshape(1, n_b, n_w, 2, N_KV, HEAD_DIM)
    ssm_sample = jnp.stack([sre, sim], axis=-1).reshape(1, n_b, SSM_G, SSM_N, 2)
    conv_sample = jnp.stack([sc[:, 1], a_s[:, :D_FF]], axis=1)[None]
    return (y_prompt, y_sample, kv_prompt, kv_sample, win_prompt, win_sample,
            ssm_prompt, ssm_sample, conv_prompt, conv_sample)
```
